```python
import math
import jax, jax.numpy as jnp
from jax import lax
import numpy as np

D_MODEL = 1024
BATCH = 8
SEQ = 8192
DEPTH = 4

N_META = 16
POOL_WIDTH = D_MODEL // 2
POOL_WINDOWS = (2, 4, 8, 16)
POOL_GROUPS = len(POOL_WINDOWS)
POOL_GROUP_DIM = POOL_WIDTH // POOL_GROUPS
N_HEADS = 8
QK_NOPE_DIM = 64
QK_ROPE_DIM = 32
QK_HEAD_DIM = QK_NOPE_DIM + QK_ROPE_DIM
V_HEAD_DIM = 64
MLA_WIDTH = N_HEADS * V_HEAD_DIM
KV_LORA_RANK = 256
Q_LORA_RANK = 768
ROPE_THETA = 10000.0
NORM_EPS = 1e-6
Q_BLOCK = 128
MASK_VALUE = -1e30

IN_SPLITS = (POOL_WIDTH, POOL_WIDTH, Q_LORA_RANK, KV_LORA_RANK, QK_ROPE_DIM, MLA_WIDTH, D_MODEL, D_MODEL)
D_IN = sum(IN_SPLITS)
IN_SPLIT_POINTS = tuple(int(v) for v in np.cumsum(IN_SPLITS)[:-1])

kernel_name = "hybrid_pool_mla_gated_trunk"


def rmsnorm(x, gain):
    xf = x.astype(jnp.float32)
    inv = lax.rsqrt(jnp.mean(xf * xf, axis=-1, keepdims=True) + NORM_EPS)
    return (xf * inv * gain.astype(jnp.float32)).astype(x.dtype)


def apply_rope(x, pos):
    half = x.shape[-1] // 2
    inv_freq = ROPE_THETA ** (-jnp.arange(half, dtype=jnp.float32) / half)
    ang = pos.astype(jnp.float32)[..., None] * inv_freq
    cos = jnp.cos(ang)[:, :, None, :]
    sin = jnp.sin(ang)[:, :, None, :]
    xf = x.astype(jnp.float32)
    x1, x2 = xf[..., :half], xf[..., half:]
    return jnp.concatenate([x1 * cos - x2 * sin, x2 * cos + x1 * sin], axis=-1).astype(x.dtype)


def pool_mix(u, w_group, scale):
    B, L, _ = u.shape
    ug = u.reshape(B, L, POOL_GROUPS, POOL_GROUP_DIM).astype(jnp.float32)
    csum = jnp.cumsum(ug, axis=1)
    t1 = jnp.arange(1, L + 1, dtype=jnp.float32)
    means = []
    for g, w in enumerate(POOL_WINDOWS):
        s = csum[:, :, g]
        lag = jnp.pad(s, ((0, 0), (w, 0), (0, 0)))[:, :L]
        cnt = jnp.minimum(t1, float(w))[None, :, None]
        means.append((s - lag) / cnt)
    mixed = (jnp.stack(means, axis=2) - ug).astype(u.dtype)
    y = jnp.einsum('blgc,gcd->blgd', mixed, w_group)
    return y.reshape(B, L, POOL_WIDTH) * scale


def causal_block_attention(q, k, v):
    B, L = q.shape[0], q.shape[1]
    pad_front = (-N_META) % Q_BLOCK
    pad_back = (-(L + pad_front)) % Q_BLOCK
    padw = ((0, 0), (pad_front, pad_back), (0, 0), (0, 0))
    q, k, v = jnp.pad(q, padw), jnp.pad(k, padw), jnp.pad(v, padw)
    n_blocks = q.shape[1] // Q_BLOCK
    scale = 1.0 / math.sqrt(QK_HEAD_DIM)
    outs = []
    for i in range(n_blocks):
        q0 = i * Q_BLOCK
        kend = q0 + Q_BLOCK
        s = jnp.einsum('bqhd,bkhd->bhqk', q[:, q0:kend], k[:, :kend]).astype(jnp.float32) * scale
        qi = jnp.arange(q0, kend)[:, None]
        ki = jnp.arange(kend)[None, :]
        valid = (ki <= qi) & (ki >= pad_front)
        s = jnp.where(valid, s, MASK_VALUE)
        p = jax.nn.softmax(s, axis=-1).astype(v.dtype)
        outs.append(jnp.einsum('bhqk,bkhd->bqhd', p, v[:, :kend]))
    o = jnp.concatenate(outs, axis=1)
    return o[:, pad_front:pad_front + L]


def mla(c_q_raw, c_kv_raw, k_rope_raw, pos, g_qa, g_kva, w_q_b, w_kv_b, g_qn, g_kn):
    B, L, _ = c_q_raw.shape
    c_q = rmsnorm(c_q_raw, g_qa)
    c_kv = rmsnorm(c_kv_raw, g_kva)
    q = (c_q @ w_q_b).reshape(B, L, N_HEADS, QK_HEAD_DIM)
    kv = (c_kv @ w_kv_b).reshape(B, L, N_HEADS, QK_NOPE_DIM + V_HEAD_DIM)
    k_nope, v = kv[..., :QK_NOPE_DIM], kv[..., QK_NOPE_DIM:]
    k_pe = jnp.broadcast_to(k_rope_raw[:, :, None, :], (B, L, N_HEADS, QK_ROPE_DIM))
    k = jnp.concatenate([k_nope, k_pe], axis=-1)
    q = rmsnorm(q, g_qn)
    k = rmsnorm(k, g_kn)
    q = jnp.concatenate([q[..., :QK_NOPE_DIM], apply_rope(q[..., QK_NOPE_DIM:], pos)], axis=-1)
    k = jnp.concatenate([k[..., :QK_NOPE_DIM], apply_rope(k[..., QK_NOPE_DIM:], pos)], axis=-1)
    o = causal_block_attention(q, k, v)
    return o.reshape(B, L, MLA_WIDTH)


def _fwd_setup_inputs(seed: int = 0) -> dict:
    key = jax.random.key(seed)
    ks = jax.random.split(key, 18)
    f32 = jnp.float32

    def nrm(k, shape, scale):
        return jax.random.normal(k, shape, f32) * scale

    def gain(k, shape):
        return 1.0 + 0.02 * jax.random.normal(k, shape, f32)

    x = jax.random.normal(ks[0], (BATCH, SEQ, D_MODEL), f32)
    offset = jax.random.randint(ks[1], (BATCH, 1), 0, 4096, dtype=jnp.int32)
    positions = offset + jnp.arange(SEQ, dtype=jnp.int32)[None, :]
    return {
        "x": x,
        "positions": positions,
        "meta_tokens": nrm(ks[2], (N_META, D_MODEL), 1.0),
        "norm_gain": gain(ks[3], (DEPTH, D_MODEL)),
        "w_in": nrm(ks[4], (DEPTH, D_MODEL, D_IN), D_MODEL ** -0.5),
        "pool_w_group": nrm(ks[5], (DEPTH, POOL_GROUPS, POOL_GROUP_DIM, POOL_GROUP_DIM), POOL_GROUP_DIM ** -0.5),
        "pool_scale": gain(ks[6], (DEPTH, POOL_WIDTH)),
        "pool_w_up": nrm(ks[7], (DEPTH, POOL_WIDTH, D_MODEL), POOL_WIDTH ** -0.5),
        "q_a_norm_gain": gain(ks[8], (DEPTH, Q_LORA_RANK)),
        "kv_a_norm_gain": gain(ks[9], (DEPTH, KV_LORA_RANK)),
        "w_q_b": nrm(ks[10], (DEPTH, Q_LORA_RANK, N_HEADS * QK_HEAD_DIM), Q_LORA_RANK ** -0.5),
        "w_kv_b": nrm(ks[11], (DEPTH, KV_LORA_RANK, N_HEADS * (QK_NOPE_DIM + V_HEAD_DIM)), KV_LORA_RANK ** -0.5),
        "q_norm_gain": gain(ks[12], (DEPTH, QK_HEAD_DIM)),
        "k_norm_gain": gain(ks[13], (DEPTH, QK_HEAD_DIM)),
        "mla_w_up": nrm(ks[14], (DEPTH, MLA_WIDTH, D_MODEL), MLA_WIDTH ** -0.5),
        "w_out": nrm(ks[15], (DEPTH, D_MODEL, D_MODEL), (D_MODEL * 2 * DEPTH) ** -0.5),
    }


def _fwd_reference(x, positions, meta_tokens, norm_gain, w_in, pool_w_group, pool_scale, pool_w_up,
              q_a_norm_gain, kv_a_norm_gain, w_q_b, w_kv_b, q_norm_gain, k_norm_gain, mla_w_up, w_out):
    B = x.shape[0]
    meta = jnp.broadcast_to(meta_tokens[None].astype(x.dtype), (B, N_META, D_MODEL))
    h_res = jnp.concatenate([meta, x], axis=1)
    meta_pos = jnp.broadcast_to(jnp.arange(N_META, dtype=jnp.int32)[None], (B, N_META))
    pos = jnp.concatenate([meta_pos, positions + N_META], axis=1)

    for l in range(DEPTH):
        h = rmsnorm(h_res, norm_gain[l])
        proj = h @ w_in[l]
        u_pool, z_pool, c_q, c_kv, k_rope, z_mla, g_pool, g_mla = jnp.split(proj, IN_SPLIT_POINTS, axis=-1)
        y_pool = (pool_mix(u_pool, pool_w_group[l], pool_scale[l]) * jax.nn.silu(z_pool)) @ pool_w_up[l]
        o_mla = mla(c_q, c_kv, k_rope, pos, q_a_norm_gain[l], kv_a_norm_gain[l], w_q_b[l], w_kv_b[l],
                    q_norm_gain[l], k_norm_gain[l])
        y_mla = (o_mla * jax.nn.silu(z_mla)) @ mla_w_up[l]
        merged = jax.nn.sigmoid(g_pool) * y_pool + jax.nn.sigmoid(g_mla) * y_mla
        h_res = h_res + merged @ w_out[l]

    return h_res[:, N_META:]


import jax as _jax
import jax.numpy as _jnp

TWIN_FORMAT = 'train_step'
FWD_PARAMS = ['x', 'positions', 'meta_tokens', 'norm_gain', 'w_in', 'pool_w_group', 'pool_scale', 'pool_w_up', 'q_a_norm_gain', 'kv_a_norm_gain', 'w_q_b', 'w_kv_b', 'q_norm_gain', 'k_norm_gain', 'mla_w_up', 'w_out']
TWIN_WEIGHTS = ['meta_tokens', 'norm_gain', 'w_in', 'pool_w_group', 'pool_scale', 'pool_w_up', 'q_a_norm_gain', 'kv_a_norm_gain', 'w_q_b', 'w_kv_b', 'q_norm_gain', 'k_norm_gain', 'mla_w_up', 'w_out']
TWIN_DIFF_INPUT = 'x'
TWIN_INPUTS = ['x', 'positions', 'meta_tokens', 'norm_gain', 'w_in', 'pool_w_group', 'pool_scale', 'pool_w_up', 'q_a_norm_gain', 'kv_a_norm_gain', 'w_q_b', 'w_kv_b', 'q_norm_gain', 'k_norm_gain', 'mla_w_up', 'w_out', 'loss_target', 'm_meta_tokens', 'm_norm_gain', 'm_w_in', 'm_pool_w_group', 'm_pool_scale', 'm_pool_w_up', 'm_q_a_norm_gain', 'm_kv_a_norm_gain', 'm_w_q_b', 'm_w_kv_b', 'm_q_norm_gain', 'm_k_norm_gain', 'm_mla_w_up', 'm_w_out', 'v_meta_tokens', 'v_norm_gain', 'v_w_in', 'v_pool_w_group', 'v_pool_scale', 'v_pool_w_up', 'v_q_a_norm_gain', 'v_kv_a_norm_gain', 'v_w_q_b', 'v_w_kv_b', 'v_q_norm_gain', 'v_k_norm_gain', 'v_mla_w_up', 'v_w_out']
TWIN_OUTPUTS = ['loss', 'grad_x', 'grad_meta_tokens', 'grad_norm_gain', 'grad_w_in', 'grad_pool_w_group', 'grad_pool_scale', 'grad_pool_w_up', 'grad_q_a_norm_gain', 'grad_kv_a_norm_gain', 'grad_w_q_b', 'grad_w_kv_b', 'grad_q_norm_gain', 'grad_k_norm_gain', 'grad_mla_w_up', 'grad_w_out', 'delta_meta_tokens', 'delta_norm_gain', 'delta_w_in', 'delta_pool_w_group', 'delta_pool_scale', 'delta_pool_w_up', 'delta_q_a_norm_gain', 'delta_kv_a_norm_gain', 'delta_w_q_b', 'delta_w_kv_b', 'delta_q_norm_gain', 'delta_k_norm_gain', 'delta_mla_w_up', 'delta_w_out', 'new_m_meta_tokens', 'new_m_norm_gain', 'new_m_w_in', 'new_m_pool_w_group', 'new_m_pool_scale', 'new_m_pool_w_up', 'new_m_q_a_norm_gain', 'new_m_kv_a_norm_gain', 'new_m_w_q_b', 'new_m_w_kv_b', 'new_m_q_norm_gain', 'new_m_k_norm_gain', 'new_m_mla_w_up', 'new_m_w_out', 'new_v_meta_tokens', 'new_v_norm_gain', 'new_v_w_in', 'new_v_pool_w_group', 'new_v_pool_scale', 'new_v_pool_w_up', 'new_v_q_a_norm_gain', 'new_v_kv_a_norm_gain', 'new_v_w_q_b', 'new_v_w_kv_b', 'new_v_q_norm_gain', 'new_v_k_norm_gain', 'new_v_mla_w_up', 'new_v_w_out']
TWIN_LEAF_KINDS = {'loss': 'loss', 'grad_x': 'grad_x', 'grad_meta_tokens': 'grad_w', 'grad_norm_gain': 'grad_w', 'grad_w_in': 'grad_w', 'grad_pool_w_group': 'grad_w', 'grad_pool_scale': 'grad_w', 'grad_pool_w_up': 'grad_w', 'grad_q_a_norm_gain': 'grad_w', 'grad_kv_a_norm_gain': 'grad_w', 'grad_w_q_b': 'grad_w', 'grad_w_kv_b': 'grad_w', 'grad_q_norm_gain': 'grad_w', 'grad_k_norm_gain': 'grad_w', 'grad_mla_w_up': 'grad_w', 'grad_w_out': 'grad_w', 'delta_meta_tokens': 'delta_w', 'delta_norm_gain': 'delta_w', 'delta_w_in': 'delta_w', 'delta_pool_w_group': 'delta_w', 'delta_pool_scale': 'delta_w', 'delta_pool_w_up': 'delta_w', 'delta_q_a_norm_gain': 'delta_w', 'delta_kv_a_norm_gain': 'delta_w', 'delta_w_q_b': 'delta_w', 'delta_w_kv_b': 'delta_w', 'delta_q_norm_gain': 'delta_w', 'delta_k_norm_gain': 'delta_w', 'delta_mla_w_up': 'delta_w', 'delta_w_out': 'delta_w', 'new_m_meta_tokens': 'new_m', 'new_m_norm_gain': 'new_m', 'new_m_w_in': 'new_m', 'new_m_pool_w_group': 'new_m', 'new_m_pool_scale': 'new_m', 'new_m_pool_w_up': 'new_m', 'new_m_q_a_norm_gain': 'new_m', 'new_m_kv_a_norm_gain': 'new_m', 'new_m_w_q_b': 'new_m', 'new_m_w_kv_b': 'new_m', 'new_m_q_norm_gain': 'new_m', 'new_m_k_norm_gain': 'new_m', 'new_m_mla_w_up': 'new_m', 'new_m_w_out': 'new_m', 'new_v_meta_tokens': 'new_v', 'new_v_norm_gain': 'new_v', 'new_v_w_in': 'new_v', 'new_v_pool_w_group': 'new_v', 'new_v_pool_scale': 'new_v', 'new_v_pool_w_up': 'new_v', 'new_v_q_a_norm_gain': 'new_v', 'new_v_kv_a_norm_gain': 'new_v', 'new_v_w_q_b': 'new_v', 'new_v_w_kv_b': 'new_v', 'new_v_q_norm_gain': 'new_v', 'new_v_k_norm_gain': 'new_v', 'new_v_mla_w_up': 'new_v', 'new_v_w_out': 'new_v'}


def _forward(args):
    return _fwd_reference(*[args[k] for k in FWD_PARAMS])


def _output_shape():
    def fwd():
        inp = _fwd_setup_inputs(0)
        return _fwd_reference(*[inp[k] for k in FWD_PARAMS])
    out = _jax.eval_shape(fwd)
    return out.shape, out.dtype

N_MICROBATCH = 1
ADAM_LR = 0.001
ADAM_B1 = 0.9
ADAM_B2 = 0.999
ADAM_EPS = 1e-08
ADAM_WD = 0.01
ADAM_STEP = 10
PER_EXAMPLE_BATCH_AXIS = {'x': 0, 'positions': 0, 'loss_target': 0}
SHARED_INPUTS = []
_WEIGHT_DTYPES = {'meta_tokens': _jnp.float32, 'norm_gain': _jnp.float32, 'w_in': _jnp.float32, 'pool_w_group': _jnp.float32, 'pool_scale': _jnp.float32, 'pool_w_up': _jnp.float32, 'q_a_norm_gain': _jnp.float32, 'kv_a_norm_gain': _jnp.float32, 'w_q_b': _jnp.float32, 'w_kv_b': _jnp.float32, 'q_norm_gain': _jnp.float32, 'k_norm_gain': _jnp.float32, 'mla_w_up': _jnp.float32, 'w_out': _jnp.float32}
MOMENT_SCALE = {'meta_tokens': 2.999403e-03, 'norm_gain': 1.404485e+00, 'w_in': 3.279390e-02, 'pool_w_group': 1.056070e-01, 'pool_scale': 1.286663e+00, 'pool_w_up': 5.358436e-02, 'q_a_norm_gain': 7.848464e-03, 'kv_a_norm_gain': 2.773629e-02, 'w_q_b': 7.851746e-03, 'w_kv_b': 9.515580e-03, 'q_norm_gain': 4.233869e-02, 'k_norm_gain': 4.239885e-02, 'mla_w_up': 7.508215e-03, 'w_out': 1.456147e-01}


def _to_microbatches(a, axis):
    t = _jnp.moveaxis(a, axis, 0)
    t = t.reshape((N_MICROBATCH, t.shape[0] // N_MICROBATCH) + t.shape[1:])
    return _jnp.moveaxis(t, 1, axis + 1)


def setup_inputs(seed: int = 0) -> dict:
    inp = _fwd_setup_inputs(seed)
    key = _jax.random.fold_in(_jax.random.key(seed), 7919)
    shape, _ = _output_shape()
    out = dict(inp)
    out["loss_target"] = _jax.random.normal(_jax.random.fold_in(key, 0), shape, _jnp.float32)
    for i, name in enumerate(TWIN_WEIGHTS):
        w = inp[name].astype(_jnp.float32)
        if MOMENT_SCALE is None:
            s = _jnp.sqrt(_jnp.mean(_jnp.square(w)) + 1e-30)
        else:
            s = MOMENT_SCALE[name]
        km, kv = _jax.random.split(_jax.random.fold_in(key, i + 1))
        out[name] = w
        out["m_" + name] = s * _jax.random.normal(km, w.shape, _jnp.float32)
        out["v_" + name] = (s * s) * _jax.random.uniform(kv, w.shape, _jnp.float32, 0.5, 1.5)
    if N_MICROBATCH > 1:
        for name, axis in PER_EXAMPLE_BATCH_AXIS.items():
            out[name] = _to_microbatches(out[name], axis)
    return {'x': out['x'], 'positions': out['positions'], 'meta_tokens': out['meta_tokens'], 'norm_gain': out['norm_gain'], 'w_in': out['w_in'], 'pool_w_group': out['pool_w_group'], 'pool_scale': out['pool_scale'], 'pool_w_up': out['pool_w_up'], 'q_a_norm_gain': out['q_a_norm_gain'], 'kv_a_norm_gain': out['kv_a_norm_gain'], 'w_q_b': out['w_q_b'], 'w_kv_b': out['w_kv_b'], 'q_norm_gain': out['q_norm_gain'], 'k_norm_gain': out['k_norm_gain'], 'mla_w_up': out['mla_w_up'], 'w_out': out['w_out'], 'loss_target': out['loss_target'], 'm_meta_tokens': out['m_meta_tokens'], 'm_norm_gain': out['m_norm_gain'], 'm_w_in': out['m_w_in'], 'm_pool_w_group': out['m_pool_w_group'], 'm_pool_scale': out['m_pool_scale'], 'm_pool_w_up': out['m_pool_w_up'], 'm_q_a_norm_gain': out['m_q_a_norm_gain'], 'm_kv_a_norm_gain': out['m_kv_a_norm_gain'], 'm_w_q_b': out['m_w_q_b'], 'm_w_kv_b': out['m_w_kv_b'], 'm_q_norm_gain': out['m_q_norm_gain'], 'm_k_norm_gain': out['m_k_norm_gain'], 'm_mla_w_up': out['m_mla_w_up'], 'm_w_out': out['m_w_out'], 'v_meta_tokens': out['v_meta_tokens'], 'v_norm_gain': out['v_norm_gain'], 'v_w_in': out['v_w_in'], 'v_pool_w_group': out['v_pool_w_group'], 'v_pool_scale': out['v_pool_scale'], 'v_pool_w_up': out['v_pool_w_up'], 'v_q_a_norm_gain': out['v_q_a_norm_gain'], 'v_kv_a_norm_gain': out['v_kv_a_norm_gain'], 'v_w_q_b': out['v_w_q_b'], 'v_w_kv_b': out['v_w_kv_b'], 'v_q_norm_gain': out['v_q_norm_gain'], 'v_k_norm_gain': out['v_k_norm_gain'], 'v_mla_w_up': out['v_mla_w_up'], 'v_w_out': out['v_w_out']}


def _loss(weights, diff, rest, loss_target):
    with _jax.named_scope("forward"):
        args = {**rest, TWIN_DIFF_INPUT: diff, **{k: w.astype(_WEIGHT_DTYPES[k]) for k, w in weights.items()}}
        y = _forward(args)
    with _jax.named_scope("loss_head"):
        err = _jnp.square(y.astype(_jnp.float32) - loss_target)
        return 0.5 * _jnp.sum(_jnp.mean(err, axis=-1)) if err.ndim else 0.5 * err


def _adamw(w, g, m, v):
    m = ADAM_B1 * m + (1.0 - ADAM_B1) * g
    v = ADAM_B2 * v + (1.0 - ADAM_B2) * _jnp.square(g)
    m_hat = m / (1.0 - ADAM_B1 ** ADAM_STEP)
    v_hat = v / (1.0 - ADAM_B2 ** ADAM_STEP)
    delta = -ADAM_LR * (m_hat / (_jnp.sqrt(v_hat) + ADAM_EPS) + ADAM_WD * w)
    return delta, m, v


def reference(x, positions, meta_tokens, norm_gain, w_in, pool_w_group, pool_scale, pool_w_up, q_a_norm_gain, kv_a_norm_gain, w_q_b, w_kv_b, q_norm_gain, k_norm_gain, mla_w_up, w_out, loss_target, m_meta_tokens, m_norm_gain, m_w_in, m_pool_w_group, m_pool_scale, m_pool_w_up, m_q_a_norm_gain, m_kv_a_norm_gain, m_w_q_b, m_w_kv_b, m_q_norm_gain, m_k_norm_gain, m_mla_w_up, m_w_out, v_meta_tokens, v_norm_gain, v_w_in, v_pool_w_group, v_pool_scale, v_pool_w_up, v_q_a_norm_gain, v_kv_a_norm_gain, v_w_q_b, v_w_kv_b, v_q_norm_gain, v_k_norm_gain, v_mla_w_up, v_w_out):
    given = dict(x=x, positions=positions, meta_tokens=meta_tokens, norm_gain=norm_gain, w_in=w_in, pool_w_group=pool_w_group, pool_scale=pool_scale, pool_w_up=pool_w_up, q_a_norm_gain=q_a_norm_gain, kv_a_norm_gain=kv_a_norm_gain, w_q_b=w_q_b, w_kv_b=w_kv_b, q_norm_gain=q_norm_gain, k_norm_gain=k_norm_gain, mla_w_up=mla_w_up, w_out=w_out, loss_target=loss_target, m_meta_tokens=m_meta_tokens, m_norm_gain=m_norm_gain, m_w_in=m_w_in, m_pool_w_group=m_pool_w_group, m_pool_scale=m_pool_scale, m_pool_w_up=m_pool_w_up, m_q_a_norm_gain=m_q_a_norm_gain, m_kv_a_norm_gain=m_kv_a_norm_gain, m_w_q_b=m_w_q_b, m_w_kv_b=m_w_kv_b, m_q_norm_gain=m_q_norm_gain, m_k_norm_gain=m_k_norm_gain, m_mla_w_up=m_mla_w_up, m_w_out=m_w_out, v_meta_tokens=v_meta_tokens, v_norm_gain=v_norm_gain, v_w_in=v_w_in, v_pool_w_group=v_pool_w_group, v_pool_scale=v_pool_scale, v_pool_w_up=v_pool_w_up, v_q_a_norm_gain=v_q_a_norm_gain, v_kv_a_norm_gain=v_kv_a_norm_gain, v_w_q_b=v_w_q_b, v_w_kv_b=v_w_kv_b, v_q_norm_gain=v_q_norm_gain, v_k_norm_gain=v_k_norm_gain, v_mla_w_up=v_mla_w_up, v_w_out=v_w_out)
    weights = {n: given[n] for n in TWIN_WEIGHTS}
    shared = {n: given[n] for n in SHARED_INPUTS}
    per_example = {n: given[n] for n in ['x', 'positions']}
    grad_fn = _jax.value_and_grad(_loss, argnums=(0, 1))

    def one_microbatch(ex, loss_target):
        ex = dict(ex)
        diff = ex.pop(TWIN_DIFF_INPUT)
        return grad_fn(weights, diff, {**shared, **ex}, loss_target)

    if N_MICROBATCH == 1:
        loss, (grad_w, grad_x) = one_microbatch(per_example, given["loss_target"])
    else:
        def body(carry, xs):
            loss_sum, grad_sum = carry
            l_k, (gw_k, gx_k) = one_microbatch(xs[0], xs[1])
            with _jax.named_scope("update"):
                return (loss_sum + l_k, _jax.tree.map(_jnp.add, grad_sum, gw_k)), gx_k

        init = (_jnp.zeros((), _jnp.float32), _jax.tree.map(_jnp.zeros_like, weights))
        (loss, grad_w), grad_x = _jax.lax.scan(body, init, (per_example, given["loss_target"]))
    with _jax.named_scope("update"):
        delta_w, new_m, new_v = {}, {}, {}
        for n in TWIN_WEIGHTS:
            delta_w[n], new_m[n], new_v[n] = _adamw(weights[n], grad_w[n], given["m_" + n], given["v_" + n])
    return (loss, grad_x, *[grad_w[n] for n in TWIN_WEIGHTS], *[delta_w[n] for n in TWIN_WEIGHTS],
            *[new_m[n] for n in TWIN_WEIGHTS], *[new_v[n] for n in TWIN_WEIGHTS])
```

```python
import math

import jax
import jax.numpy as jnp
from jax import lax
from jax.experimental import pallas as pl
from jax.experimental.pallas import tpu as pltpu

F32 = jnp.float32
BF16 = jnp.bfloat16

D_MODEL = 1024
DEPTH = 4
N_META = 16
PAD_FRONT = 112
HEAD_ROWS = PAD_FRONT + N_META
POOL_WIDTH = 512
POOL_WINDOWS = (2, 4, 8, 16)
POOL_GROUP_DIM = 128
HALO = 16
N_HEADS = 8
QK_NOPE = 64
QK_ROPE = 32
QK_DIM = 96
V_DIM = 64
MLA_WIDTH = 512
Q_RANK = 768
KV_RANK = 256
ROPE_THETA = 10000.0
NORM_EPS = 1e-6
MASK_VALUE = -1e30
D_IN = 4640
N_DEV = 8
IN_SHARD = D_IN // N_DEV

P_IN = 4736
IN_SEGMENTS = ((0, 2048, 0), (2048, 2080, 4608), (2080, 2592, 4096), (2592, 3616, 2048), (3616, 4640, 3072))
COL_UZ, COL_CQKV, COL_GATES, COL_ZMLA, COL_KR = 0, 1024, 2048, 4096, 4608

ADAM_LR = 0.001
ADAM_B1 = 0.9
ADAM_B2 = 0.999
ADAM_EPS = 1e-08
ADAM_WD = 0.01
ADAM_STEP = 10

VMEM_LIMIT = 56 * 1024 * 1024
MESH_AXES = ("x", "y", "c")


def _cp(*sem):
    return pltpu.CompilerParams(dimension_semantics=sem, vmem_limit_bytes=VMEM_LIMIT)


def _row_tile(n_rows):
    return 320 if n_rows % 320 == 0 else 128


def _attn_tile(n_rows):
    return 640 if (n_rows % 640 == 0 and n_rows > 640) else 128


def _mm(a, b):
    return jnp.dot(a.astype(BF16), b.astype(BF16), preferred_element_type=F32)


def _mm_nt(a, b):
    return lax.dot_general(a.astype(BF16), b.astype(BF16), (((1,), (1,)), ((), ())), preferred_element_type=F32)


def _mm_tn(a, b):
    return lax.dot_general(a.astype(BF16), b.astype(BF16), (((0,), (0,)), ((), ())), preferred_element_type=F32)


def _sigmoid(z):
    return 1.0 / (1.0 + jnp.exp(-z))


def _rms(x, g):
    inv = lax.rsqrt(jnp.mean(x * x, axis=-1, keepdims=True) + NORM_EPS)
    xh = x * inv
    return xh * g, xh, inv


def _rms_bwd(dy, xh, inv, g):
    dg = jnp.sum(dy * xh, axis=0, keepdims=True)
    dxh = dy * g
    dx = inv * (dxh - xh * jnp.mean(dxh * xh, axis=-1, keepdims=True))
    return dx, dg


def _rope(r, cos, sin):
    rot = jnp.concatenate([-r[:, 16:32], r[:, 0:16]], axis=1)
    return r * cos + rot * sin


def _rope_bwd(dy, cos, sin):
    ys = dy * sin
    return dy * cos + jnp.concatenate([ys[:, 16:32], -ys[:, 0:16]], axis=1)


def _exchange(name, gather_list, scatter_list):
    arrays = list(gather_list) + list(scatter_list)
    n_g, n = len(gather_list), len(arrays)
    out_shape = [jax.ShapeDtypeStruct((N_DEV,) + a.shape, a.dtype) for a in gather_list]
    out_shape += [jax.ShapeDtypeStruct(a.shape, a.dtype) for a in scatter_list]

    def body(*refs):
        ins, outs = refs[:n], refs[n:2 * n]
        send_sems, recv_sems, local_sems = refs[2 * n:]
        x, y, c = lax.axis_index("x"), lax.axis_index("y"), lax.axis_index("c")
        me = 4 * x + 2 * y + c

        def src(a, slot):
            return ins[a] if a < n_g else ins[a].at[slot]

        local = []
        for a in range(n):
            cp = pltpu.make_async_copy(src(a, me), outs[a].at[me], local_sems.at[a])
            cp.start()
            local.append(cp)
        remote = []
        for k in range(1, N_DEV):
            px = 1 - x if k & 4 else x
            py = 1 - y if k & 2 else y
            pc = 1 - c if k & 1 else c
            peer = 4 * px + 2 * py + pc
            for a in range(n):
                cp = pltpu.make_async_remote_copy(
                    src_ref=src(a, peer), dst_ref=outs[a].at[me],
                    send_sem=send_sems.at[a * 7 + k - 1], recv_sem=recv_sems.at[a * 7 + k - 1],
                    device_id=(px, py, pc), device_id_type=pl.DeviceIdType.MESH)
                cp.start()
                remote.append(cp)
        for cp in remote:
            cp.wait()
        for cp in local:
            cp.wait()

    any_spec = pl.BlockSpec(memory_space=pl.ANY)
    return pl.pallas_call(
        body, name=name, out_shape=out_shape,
        in_specs=[any_spec] * n, out_specs=[any_spec] * n,
        scratch_shapes=[pltpu.SemaphoreType.DMA((7 * n,)), pltpu.SemaphoreType.DMA((7 * n,)),
                        pltpu.SemaphoreType.DMA((n,))],
        compiler_params=pltpu.CompilerParams(has_side_effects=True),
    )(*arrays)


def _in_pieces():
    runs = []
    for lo, hi, dst in sorted(IN_SEGMENTS, key=lambda s: s[2]):
        col = lo
        while col < hi:
            d = col // IN_SHARD
            end = min(hi, (d + 1) * IN_SHARD)
            runs.append((d, col - d * IN_SHARD, end - d * IN_SHARD))
            col = end
    return runs


def _repack_w_in(gathered):
    tr = 256
    runs = _in_pieces()

    def body(x_ref, o_ref):
        parts = [x_ref[d, 0, :, a:b] for d, a, b in runs]
        parts.append(jnp.zeros((tr, P_IN - D_IN), F32))
        o_ref[0] = jnp.concatenate(parts, axis=1).astype(BF16)

    return pl.pallas_call(
        body, name="repack_w_in", out_shape=jax.ShapeDtypeStruct((DEPTH, D_MODEL, P_IN), BF16),
        grid=(DEPTH, D_MODEL // tr),
        in_specs=[pl.BlockSpec((N_DEV, 1, tr, IN_SHARD), lambda l, i: (0, l, i, 0))],
        out_specs=pl.BlockSpec((1, tr, P_IN), lambda l, i: (l, i, 0)),
        compiler_params=_cp("parallel", "parallel"),
    )(gathered)


def _unpack_w_in(parts):
    tr = 256
    runs = _in_pieces()
    widths = [p.shape[-1] for p in parts]

    def body(a_ref, b_ref, c_ref, d_ref, o_ref):
        full = jnp.concatenate([a_ref[0], b_ref[0], c_ref[0], d_ref[0]], axis=1)
        col = 0
        per_dev = [[] for _ in range(N_DEV)]
        for d, a, b in runs:
            per_dev[d].append((a, full[:, col:col + (b - a)]))
            col += b - a
        for d in range(N_DEV):
            pieces = [v for _, v in sorted(per_dev[d], key=lambda av: av[0])]
            o_ref[d, 0] = jnp.concatenate(pieces, axis=1) if len(pieces) > 1 else pieces[0]

    return pl.pallas_call(
        body, name="unpack_w_in", out_shape=jax.ShapeDtypeStruct((N_DEV, DEPTH, D_MODEL, IN_SHARD), F32),
        grid=(DEPTH, D_MODEL // tr),
        in_specs=[pl.BlockSpec((1, tr, w), lambda l, i: (l, i, 0)) for w in widths],
        out_specs=pl.BlockSpec((N_DEV, 1, tr, IN_SHARD), lambda l, i: (0, l, i, 0)),
        compiler_params=_cp("parallel", "parallel"),
    )(*parts)


def _repack_cols(name, gathered, dtype):
    _, depth, k, n = gathered.shape

    def body(x_ref, o_ref):
        o_ref[0] = jnp.concatenate([x_ref[d, 0] for d in range(N_DEV)], axis=1).astype(dtype)

    return pl.pallas_call(
        body, name=name, out_shape=jax.ShapeDtypeStruct((depth, k, N_DEV * n), dtype), grid=(depth,),
        in_specs=[pl.BlockSpec((N_DEV, 1, k, n), lambda l: (0, l, 0, 0))],
        out_specs=pl.BlockSpec((1, k, N_DEV * n), lambda l: (l, 0, 0)),
        compiler_params=_cp("parallel"),
    )(gathered)


def _unpack_cols(name, full, n):
    depth, k, _ = full.shape

    def body(x_ref, o_ref):
        for d in range(N_DEV):
            o_ref[d, 0] = x_ref[0, :, d * n:(d + 1) * n]

    return pl.pallas_call(
        body, name=name, out_shape=jax.ShapeDtypeStruct((N_DEV, depth, k, n), F32), grid=(depth,),
        in_specs=[pl.BlockSpec((1, k, N_DEV * n), lambda l: (l, 0, 0))],
        out_specs=pl.BlockSpec((N_DEV, 1, k, n), lambda l: (0, l, 0, 0)),
        compiler_params=_cp("parallel"),
    )(full)


def _repack_rows(name, gathered):
    _, depth, r, n = gathered.shape

    def body(x_ref, o_ref):
        for d in range(N_DEV):
            o_ref[0, d * r:(d + 1) * r, :] = x_ref[d, 0].astype(BF16)

    return pl.pallas_call(
        body, name=name, out_shape=jax.ShapeDtypeStruct((depth, N_DEV * r, n), BF16), grid=(depth,),
        in_specs=[pl.BlockSpec((N_DEV, 1, r, n), lambda l: (0, l, 0, 0))],
        out_specs=pl.BlockSpec((1, N_DEV * r, n), lambda l: (l, 0, 0)),
        compiler_params=_cp("parallel"),
    )(gathered)


def _unpack_rows(name, full, r):
    depth, _, n = full.shape

    def body(x_ref, o_ref):
        for d in range(N_DEV):
            o_ref[d, 0] = x_ref[0, d * r:(d + 1) * r, :]

    return pl.pallas_call(
        body, name=name, out_shape=jax.ShapeDtypeStruct((N_DEV, depth, r, n), F32), grid=(depth,),
        in_specs=[pl.BlockSpec((1, N_DEV * r, n), lambda l: (l, 0, 0))],
        out_specs=pl.BlockSpec((N_DEV, 1, r, n), lambda l: (0, l, 0, 0)),
        compiler_params=_cp("parallel"),
    )(full)


def _embed(x, meta_g):
    seq = x.shape[0]
    n_blk = seq // HEAD_ROWS + 1

    def body(x_ref, m_ref, o_ref):
        i = pl.program_id(0)

        @pl.when(i == 0)
        def _():
            meta = jnp.concatenate([m_ref[d] for d in range(N_DEV)], axis=1)
            o_ref[...] = jnp.concatenate([jnp.zeros((PAD_FRONT, D_MODEL), F32), meta], axis=0)

        @pl.when(i > 0)
        def _():
            o_ref[...] = x_ref[...]

    return pl.pallas_call(
        body, name="embed", out_shape=jax.ShapeDtypeStruct((seq + HEAD_ROWS, D_MODEL), F32), grid=(n_blk,),
        in_specs=[pl.BlockSpec((HEAD_ROWS, D_MODEL), lambda i: (jnp.maximum(i - 1, 0), 0)),
                  pl.BlockSpec((N_DEV, N_META, 128), lambda i: (0, 0, 0))],
        out_specs=pl.BlockSpec((HEAD_ROWS, D_MODEL), lambda i: (i, 0)),
        compiler_params=_cp("parallel"),
    )(x, meta_g)


def _rope_table(pos_col, freq_row):
    n_rows = pos_col.shape[0]
    tr = 128

    def body(p_ref, f_ref, o_ref):
        ang = p_ref[...] * f_ref[...]
        lane = lax.broadcasted_iota(jnp.int32, ang.shape, 1)
        o_ref[...] = jnp.where(lane < QK_ROPE, jnp.cos(ang), jnp.sin(ang))

    return pl.pallas_call(
        body, name="rope_table", out_shape=jax.ShapeDtypeStruct((n_rows, 128), F32), grid=(n_rows // tr,),
        in_specs=[pl.BlockSpec((tr, 1), lambda i: (i, 0)), pl.BlockSpec((1, 128), lambda i: (0, 0))],
        out_specs=pl.BlockSpec((tr, 128), lambda i: (i, 0)),
        compiler_params=_cp("parallel"),
    )(pos_col, freq_row)


def _loss_head(h, target):
    n_rows = h.shape[0]
    n_blk = n_rows // HEAD_ROWS

    def body(h_ref, t_ref, loss_ref, dh_ref):
        i = pl.program_id(0)

        @pl.when(i == 0)
        def _():
            loss_ref[...] = jnp.zeros(loss_ref.shape, F32)
            dh_ref[...] = jnp.zeros(dh_ref.shape, F32)

        @pl.when(i > 0)
        def _():
            err = h_ref[...] - t_ref[...]
            dh_ref[...] = err * (1.0 / D_MODEL)
            loss_ref[...] += 0.5 * jnp.sum(jnp.mean(err * err, axis=-1, keepdims=True))

    return pl.pallas_call(
        body, name="loss_head",
        out_shape=(jax.ShapeDtypeStruct((8, 128), F32), jax.ShapeDtypeStruct((n_rows, D_MODEL), F32)),
        grid=(n_blk,),
        in_specs=[pl.BlockSpec((HEAD_ROWS, D_MODEL), lambda i: (i, 0)),
                  pl.BlockSpec((HEAD_ROWS, D_MODEL), lambda i: (jnp.maximum(i - 1, 0), 0))],
        out_specs=(pl.BlockSpec((8, 128), lambda i: (0, 0)), pl.BlockSpec((HEAD_ROWS, D_MODEL), lambda i: (i, 0))),
        compiler_params=_cp("arbitrary"),
    )(h, target)


PROJ_CHUNKS = (0, 1024, 2048, 3072, 4096, P_IN)


def _norm_proj(h_res, gain, w_in):
    n_rows = h_res.shape[0]
    tm = _row_tile(n_rows)

    def body(x_ref, g_ref, w_ref, hb_ref, p_ref):
        h, _, _ = _rms(x_ref[...], g_ref[...])
        hb = h.astype(BF16)
        hb_ref[...] = hb
        for c0, c1 in zip(PROJ_CHUNKS[:-1], PROJ_CHUNKS[1:]):
            p_ref[:, c0:c1] = jnp.dot(hb, w_ref[:, c0:c1], preferred_element_type=F32)

    return pl.pallas_call(
        body, name="norm_proj",
        out_shape=(jax.ShapeDtypeStruct((n_rows, D_MODEL), BF16), jax.ShapeDtypeStruct((n_rows, P_IN), F32)),
        grid=(n_rows // tm,),
        in_specs=[pl.BlockSpec((tm, D_MODEL), lambda i: (i, 0)), pl.BlockSpec((1, D_MODEL), lambda i: (0, 0)),
                  pl.BlockSpec((D_MODEL, P_IN), lambda i: (0, 0))],
        out_specs=(pl.BlockSpec((tm, D_MODEL), lambda i: (i, 0)), pl.BlockSpec((tm, P_IN), lambda i: (i, 0))),
        compiler_params=_cp("parallel"),
    )(h_res, gain, w_in)


def _pool_math(u, halo, z, wg_ref, scale, row0):
    tm = u.shape[0]
    ext = jnp.concatenate([halo, u], axis=0)
    t1 = (row0 + lax.broadcasted_iota(jnp.int32, (tm, 1), 0) - (PAD_FRONT - 1)).astype(F32)
    mixed, yg = [], []
    for g, w in enumerate(POOL_WINDOWS):
        a = ext[:, g * 128:(g + 1) * 128]
        k = 1
        while k < w:
            a = a + pltpu.roll(a, k, 0)
            k *= 2
        cnt = jnp.clip(t1, 1.0, float(w))
        mg = a[HALO:, :] / cnt - u[:, g * 128:(g + 1) * 128]
        mixed.append(mg)
        yg.append(_mm(mg, wg_ref[g]))
    mixed = jnp.concatenate(mixed, axis=1)
    yg = jnp.concatenate(yg, axis=1)
    ys = yg * scale
    sig = _sigmoid(z)
    return mixed, yg, ys, sig


def _halo_above(tm):
    return lambda i: (jnp.maximum(i * (tm // HALO) - 1, 0), 0)


def _pool_fwd(proj, wg, scale, w_up):
    n_rows = proj.shape[0]
    tm = _row_tile(n_rows)

    def body(uz_ref, halo_ref, wg_ref, sc_ref, wup_ref, y_ref):
        uz = uz_ref[...]
        u, z = uz[:, :POOL_WIDTH], uz[:, POOL_WIDTH:]
        _, _, ys, sig = _pool_math(u, halo_ref[...], z, wg_ref, sc_ref[...], pl.program_id(0) * tm)
        y_ref[...] = _mm(ys * (z * sig), wup_ref[...])

    return pl.pallas_call(
        body, name="pool_fwd", out_shape=jax.ShapeDtypeStruct((n_rows, D_MODEL), F32), grid=(n_rows // tm,),
        in_specs=[pl.BlockSpec((tm, 1024), lambda i: (i, 0)), pl.BlockSpec((HALO, POOL_WIDTH), _halo_above(tm)),
                  pl.BlockSpec((4, 128, 128), lambda i: (0, 0, 0)), pl.BlockSpec((1, POOL_WIDTH), lambda i: (0, 0)),
                  pl.BlockSpec((POOL_WIDTH, D_MODEL), lambda i: (0, 0))],
        out_specs=pl.BlockSpec((tm, D_MODEL), lambda i: (i, 0)),
        compiler_params=_cp("parallel"),
    )(proj, proj, wg, scale, w_up)


def _mla_heads(cq, ckv, kr, gqa, gkva, wq_ref, wkv_ref):
    cqn, cq_h, cq_inv = _rms(cq, gqa)
    ckvn, ckv_h, ckv_inv = _rms(ckv, gkva)
    qraw = _mm(cqn, wq_ref[...])
    kvraw = _mm(ckvn, wkv_ref[...])
    return cqn, cq_h, cq_inv, ckvn, ckv_h, ckv_inv, qraw, kvraw


def _mla_pre_fwd(proj, tab, gqa, gkva, wq, wkv, gqn, gkn):
    n_rows = proj.shape[0]
    tm = _row_tile(n_rows)

    def body(c_ref, kr_ref, tab_ref, gqa_ref, gkva_ref, wq_ref, wkv_ref, gqn_ref, gkn_ref, q_ref, k_ref, v_ref):
        c = c_ref[...]
        kr = kr_ref[:, 0:QK_ROPE]
        cos, sin = tab_ref[:, 0:QK_ROPE], tab_ref[:, QK_ROPE:2 * QK_ROPE]
        _, _, _, _, _, _, qraw, kvraw = _mla_heads(c[:, :Q_RANK], c[:, Q_RANK:], kr, gqa_ref[...], gkva_ref[...],
                                                  wq_ref, wkv_ref)
        zpad = jnp.zeros((tm, 128 - QK_DIM), F32)
        for h in range(N_HEADS):
            qn, _, _ = _rms(qraw[:, QK_DIM * h:QK_DIM * (h + 1)], gqn_ref[...])
            q_ref[h] = jnp.concatenate([qn[:, :QK_NOPE], _rope(qn[:, QK_NOPE:], cos, sin), zpad], axis=1).astype(BF16)
            kin = jnp.concatenate([kvraw[:, 128 * h:128 * h + QK_NOPE], kr], axis=1)
            kn, _, _ = _rms(kin, gkn_ref[...])
            k_ref[h] = jnp.concatenate([kn[:, :QK_NOPE], _rope(kn[:, QK_NOPE:], cos, sin), zpad], axis=1).astype(BF16)
            v_ref[h] = kvraw[:, 128 * h + QK_NOPE:128 * (h + 1)].astype(BF16)

    full = lambda *s: pl.BlockSpec(s, lambda i: (0,) * len(s))
    return pl.pallas_call(
        body, name="mla_pre_fwd",
        out_shape=(jax.ShapeDtypeStruct((N_HEADS, n_rows, 128), BF16), jax.ShapeDtypeStruct((N_HEADS, n_rows, 128), BF16),
                   jax.ShapeDtypeStruct((N_HEADS, n_rows, V_DIM), BF16)),
        grid=(n_rows // tm,),
        in_specs=[pl.BlockSpec((tm, 1024), lambda i: (i, COL_CQKV // 1024)),
                  pl.BlockSpec((tm, 128), lambda i: (i, COL_KR // 128)),
                  pl.BlockSpec((tm, 128), lambda i: (i, 0)),
                  full(1, Q_RANK), full(1, KV_RANK), full(Q_RANK, Q_RANK), full(KV_RANK, 1024), full(1, QK_DIM), full(1, QK_DIM)],
        out_specs=(pl.BlockSpec((N_HEADS, tm, 128), lambda i: (0, i, 0)), pl.BlockSpec((N_HEADS, tm, 128), lambda i: (0, i, 0)),
                   pl.BlockSpec((N_HEADS, tm, V_DIM), lambda i: (0, i, 0))),
        compiler_params=_cp("parallel"),
    )(proj, proj, tab, gqa, gkva, wq, wkv, gqn, gkn)


def _attn_mask(qi, ki, t):
    row = qi * t + lax.broadcasted_iota(jnp.int32, (t, t), 0)
    col = ki * t + lax.broadcasted_iota(jnp.int32, (t, t), 1)
    return (col <= row) & (col >= PAD_FRONT)


ATTN_SCALE = 1.0 / math.sqrt(QK_DIM)


def _attn_fwd(q, k, v):
    n_rows = q.shape[1]
    t = _attn_tile(n_rows)
    nb = n_rows // t

    def body(q_ref, k_ref, v_ref, o_ref, lse_ref, m_sc, l_sc, acc_sc):
        qi, ki = pl.program_id(1), pl.program_id(2)

        @pl.when(ki == 0)
        def _():
            m_sc[...] = jnp.full(m_sc.shape, -jnp.inf, F32)
            l_sc[...] = jnp.zeros(l_sc.shape, F32)
            acc_sc[...] = jnp.zeros(acc_sc.shape, F32)

        @pl.when(ki <= qi)
        def _():
            s = _mm_nt(q_ref[0], k_ref[0]) * ATTN_SCALE
            s = jnp.where(_attn_mask(qi, ki, t), s, MASK_VALUE)
            m_prev = m_sc[...]
            m_new = jnp.maximum(m_prev, jnp.max(s, axis=1, keepdims=True))
            alpha = jnp.exp(m_prev - m_new)
            p = jnp.exp(s - m_new)
            l_sc[...] = alpha * l_sc[...] + jnp.sum(p, axis=1, keepdims=True)
            acc_sc[...] = alpha * acc_sc[...] + _mm(p, v_ref[0])
            m_sc[...] = m_new

        @pl.when(ki == nb - 1)
        def _():
            o_ref[0] = acc_sc[...] / l_sc[...]
            lse_ref[0] = m_sc[...] + jnp.log(l_sc[...])

    kv_idx = lambda h, qi, ki: (h, jnp.minimum(ki, qi), 0)
    return pl.pallas_call(
        body, name="attn_fwd",
        out_shape=(jax.ShapeDtypeStruct((N_HEADS, n_rows, V_DIM), F32), jax.ShapeDtypeStruct((N_HEADS, n_rows, 1), F32)),
        grid=(N_HEADS, nb, nb),
        in_specs=[pl.BlockSpec((1, t, 128), lambda h, qi, ki: (h, qi, 0)), pl.BlockSpec((1, t, 128), kv_idx),
                  pl.BlockSpec((1, t, V_DIM), kv_idx)],
        out_specs=(pl.BlockSpec((1, t, V_DIM), lambda h, qi, ki: (h, qi, 0)), pl.BlockSpec((1, t, 1), lambda h, qi, ki: (h, qi, 0))),
        scratch_shapes=[pltpu.VMEM((t, 1), F32), pltpu.VMEM((t, 1), F32), pltpu.VMEM((t, V_DIM), F32)],
        compiler_params=_cp("parallel", "parallel", "arbitrary"),
    )(q, k, v)


def _merge_math(o_ref, z, gates, y_pool, y_mla_fn):
    o_cat = jnp.concatenate([o_ref[h] for h in range(N_HEADS)], axis=1)
    sig_z = _sigmoid(z)
    a_mla = o_cat * (z * sig_z)
    y_mla = y_mla_fn(a_mla)
    sgp, sgm = _sigmoid(gates[:, :D_MODEL]), _sigmoid(gates[:, D_MODEL:])
    merged = sgp * y_pool + sgm * y_mla
    return o_cat, sig_z, a_mla, y_mla, sgp, sgm, merged


def _mla_post_fwd(h_res, proj, o, y_pool, w_mla_up, w_out):
    n_rows = h_res.shape[0]
    tm = _row_tile(n_rows)

    def body(h_ref, g_ref, z_ref, o_ref, yp_ref, wup_ref, wout_ref, ymla_ref, hn_ref):
        _, _, _, y_mla, _, _, merged = _merge_math(o_ref, z_ref[...], g_ref[...], yp_ref[...],
                                                   lambda a: _mm(a, wup_ref[...]))
        ymla_ref[...] = y_mla
        hn_ref[...] = h_ref[...] + _mm(merged, wout_ref[...])

    row = lambda w, c: pl.BlockSpec((tm, w), lambda i: (i, c))
    return pl.pallas_call(
        body, name="mla_post_fwd",
        out_shape=(jax.ShapeDtypeStruct((n_rows, D_MODEL), F32), jax.ShapeDtypeStruct((n_rows, D_MODEL), F32)),
        grid=(n_rows // tm,),
        in_specs=[row(D_MODEL, 0), row(2048, COL_GATES // 2048), row(MLA_WIDTH, COL_ZMLA // MLA_WIDTH),
                  pl.BlockSpec((N_HEADS, tm, V_DIM), lambda i: (0, i, 0)), row(D_MODEL, 0),
                  pl.BlockSpec((MLA_WIDTH, D_MODEL), lambda i: (0, 0)), pl.BlockSpec((D_MODEL, D_MODEL), lambda i: (0, 0))],
        out_specs=(row(D_MODEL, 0), row(D_MODEL, 0)),
        compiler_params=_cp("parallel"),
    )(h_res, proj, proj, o, y_pool, w_mla_up, w_out)


def _dsilu(z, sig):
    return sig * (1.0 + z * (1.0 - sig))


def _acc(ref, val, first):
    @pl.when(first)
    def _():
        ref[...] = val

    @pl.when(jnp.logical_not(first))
    def _():
        ref[...] += val


def _mla_post_bwd(dh, proj, o, y_pool, y_mla, w_mla_up, w_out):
    n_rows = dh.shape[0]
    tm = _row_tile(n_rows)

    def body(dh_ref, g_ref, z_ref, o_ref, yp_ref, ym_ref, wup_ref, wout_ref,
             dgz_ref, dyp_ref, do_ref, delta_ref, dwout_ref, dwup_ref):
        first = pl.program_id(0) == 0
        z, y_pool, y_mla = z_ref[...], yp_ref[...], ym_ref[...]
        o_cat, sig_z, a_mla, _, sgp, sgm, merged = _merge_math(o_ref, z, g_ref[...], y_pool, lambda a: y_mla)
        dhv = dh_ref[...]
        dmerged = _mm_nt(dhv, wout_ref[...])
        _acc(dwout_ref, _mm_tn(merged, dhv), first)
        dyp_ref[...] = dmerged * sgp
        dy_mla = dmerged * sgm
        da = _mm_nt(dy_mla, wup_ref[...])
        _acc(dwup_ref, _mm_tn(a_mla, dy_mla), first)
        do_cat = da * (z * sig_z)
        dgz_ref[:, 0:D_MODEL] = dmerged * y_pool * (sgp * (1.0 - sgp))
        dgz_ref[:, D_MODEL:2 * D_MODEL] = dmerged * y_mla * (sgm * (1.0 - sgm))
        dgz_ref[:, 2 * D_MODEL:] = da * o_cat * _dsilu(z, sig_z)
        for h in range(N_HEADS):
            doh = do_cat[:, V_DIM * h:V_DIM * (h + 1)]
            do_ref[h] = doh
            delta_ref[h] = jnp.sum(doh * o_cat[:, V_DIM * h:V_DIM * (h + 1)], axis=1, keepdims=True)

    row = lambda w, c: pl.BlockSpec((tm, w), lambda i: (i, c))
    head = lambda w: pl.BlockSpec((N_HEADS, tm, w), lambda i: (0, i, 0))
    const = lambda r, c: pl.BlockSpec((r, c), lambda i: (0, 0))
    return pl.pallas_call(
        body, name="mla_post_bwd",
        out_shape=(jax.ShapeDtypeStruct((n_rows, 2560), F32), jax.ShapeDtypeStruct((n_rows, D_MODEL), F32),
                   jax.ShapeDtypeStruct((N_HEADS, n_rows, V_DIM), F32), jax.ShapeDtypeStruct((N_HEADS, n_rows, 1), F32),
                   jax.ShapeDtypeStruct((D_MODEL, D_MODEL), F32), jax.ShapeDtypeStruct((MLA_WIDTH, D_MODEL), F32)),
        grid=(n_rows // tm,),
        in_specs=[row(D_MODEL, 0), row(2048, COL_GATES // 2048), row(MLA_WIDTH, COL_ZMLA // MLA_WIDTH), head(V_DIM),
                  row(D_MODEL, 0), row(D_MODEL, 0), const(MLA_WIDTH, D_MODEL), const(D_MODEL, D_MODEL)],
        out_specs=(row(2560, 0), row(D_MODEL, 0), head(V_DIM), head(1), const(D_MODEL, D_MODEL), const(MLA_WIDTH, D_MODEL)),
        compiler_params=_cp("arbitrary"),
    )(dh, proj, proj, o, y_pool, y_mla, w_mla_up, w_out)


def _attn_bwd(q, k, v, do, lse, delta):
    n_rows = q.shape[1]
    t = _attn_tile(n_rows)
    nb = n_rows // t

    def body(q_ref, k_ref, v_ref, do_ref, lse_ref, delta_ref, dq_ref, dk_ref, dv_ref, dk_sc, dv_sc):
        ki, qi = pl.program_id(1), pl.program_id(2)

        @pl.when((ki == 0) & (qi == 0))
        def _():
            dq_ref[...] = jnp.zeros(dq_ref.shape, F32)

        @pl.when(qi == 0)
        def _():
            dk_sc[...] = jnp.zeros(dk_sc.shape, F32)
            dv_sc[...] = jnp.zeros(dv_sc.shape, F32)

        @pl.when(qi >= ki)
        def _():
            qv, kv, dov = q_ref[0], k_ref[0], do_ref[0]
            s = _mm_nt(qv, kv) * ATTN_SCALE
            s = jnp.where(_attn_mask(qi, ki, t), s, MASK_VALUE)
            p = jnp.exp(s - lse_ref[0])
            dv_sc[...] += _mm_tn(p, dov)
            dp = _mm_nt(dov, v_ref[0])
            ds = p * (dp - delta_ref[0]) * ATTN_SCALE
            rows = pl.ds(pl.multiple_of(qi * t, t), t)
            dq_ref[0, rows, :] += _mm(ds, kv)
            dk_sc[...] += _mm_tn(ds, qv)

        @pl.when(qi == nb - 1)
        def _():
            dk_ref[0] = dk_sc[...]
            dv_ref[0] = dv_sc[...]

    q_idx = lambda h, ki, qi: (h, jnp.maximum(qi, ki), 0)
    k_idx = lambda h, ki, qi: (h, ki, 0)
    return pl.pallas_call(
        body, name="attn_bwd",
        out_shape=(jax.ShapeDtypeStruct((N_HEADS, n_rows, 128), F32), jax.ShapeDtypeStruct((N_HEADS, n_rows, 128), F32),
                   jax.ShapeDtypeStruct((N_HEADS, n_rows, V_DIM), F32)),
        grid=(N_HEADS, nb, nb),
        in_specs=[pl.BlockSpec((1, t, 128), q_idx), pl.BlockSpec((1, t, 128), k_idx), pl.BlockSpec((1, t, V_DIM), k_idx),
                  pl.BlockSpec((1, t, V_DIM), q_idx), pl.BlockSpec((1, t, 1), q_idx), pl.BlockSpec((1, t, 1), q_idx)],
        out_specs=(pl.BlockSpec((1, n_rows, 128), lambda h, ki, qi: (h, 0, 0)), pl.BlockSpec((1, t, 128), k_idx),
                   pl.BlockSpec((1, t, V_DIM), k_idx)),
        scratch_shapes=[pltpu.VMEM((t, 128), F32), pltpu.VMEM((t, V_DIM), F32)],
        compiler_params=_cp("parallel", "arbitrary", "arbitrary"),
    )(q, k, v, do, lse, delta)


def _mla_pre_bwd(proj, tab, gqa, gkva, wq, wkv, gqn, gkn, dq, dk, dv):
    n_rows = proj.shape[0]
    tm = _row_tile(n_rows)

    def body(c_ref, kr_ref, tab_ref, gqa_ref, gkva_ref, wq_ref, wkv_ref, gqn_ref, gkn_ref, dq_ref, dk_ref, dv_ref,
             dc_ref, dkr_ref, dwq_ref, dwkv_ref, dgqa_ref, dgkva_ref, dgqn_ref, dgkn_ref):
        first = pl.program_id(0) == 0
        c = c_ref[...]
        kr = kr_ref[:, 0:QK_ROPE]
        cos, sin = tab_ref[:, 0:QK_ROPE], tab_ref[:, QK_ROPE:2 * QK_ROPE]
        gqa, gkva, gqn, gkn = gqa_ref[...], gkva_ref[...], gqn_ref[...], gkn_ref[...]
        cqn, cq_h, cq_inv, ckvn, ckv_h, ckv_inv, qraw, kvraw = _mla_heads(c[:, :Q_RANK], c[:, Q_RANK:], kr, gqa, gkva,
                                                                          wq_ref, wkv_ref)
        dq_parts, dkv_parts = [], []
        dkr = jnp.zeros((tm, QK_ROPE), F32)
        dgqn = jnp.zeros((1, QK_DIM), F32)
        dgkn = jnp.zeros((1, QK_DIM), F32)
        for h in range(N_HEADS):
            _, xh, inv = _rms(qraw[:, QK_DIM * h:QK_DIM * (h + 1)], gqn)
            dqo = dq_ref[h]
            dqn = jnp.concatenate([dqo[:, :QK_NOPE], _rope_bwd(dqo[:, QK_NOPE:QK_DIM], cos, sin)], axis=1)
            dx, dg = _rms_bwd(dqn, xh, inv, gqn)
            dq_parts.append(dx)
            dgqn = dgqn + dg
            kin = jnp.concatenate([kvraw[:, 128 * h:128 * h + QK_NOPE], kr], axis=1)
            _, xh, inv = _rms(kin, gkn)
            dko = dk_ref[h]
            dkn = jnp.concatenate([dko[:, :QK_NOPE], _rope_bwd(dko[:, QK_NOPE:QK_DIM], cos, sin)], axis=1)
            dx, dg = _rms_bwd(dkn, xh, inv, gkn)
            dgkn = dgkn + dg
            dkv_parts += [dx[:, :QK_NOPE], dv_ref[h]]
            dkr = dkr + dx[:, QK_NOPE:]
        dqraw = jnp.concatenate(dq_parts, axis=1)
        dkvraw = jnp.concatenate(dkv_parts, axis=1)
        _acc(dwq_ref, _mm_tn(cqn, dqraw), first)
        _acc(dwkv_ref, _mm_tn(ckvn, dkvraw), first)
        dcq, dg1 = _rms_bwd(_mm_nt(dqraw, wq_ref[...]), cq_h, cq_inv, gqa)
        dckv, dg2 = _rms_bwd(_mm_nt(dkvraw, wkv_ref[...]), ckv_h, ckv_inv, gkva)
        _acc(dgqa_ref, dg1, first)
        _acc(dgkva_ref, dg2, first)
        _acc(dgqn_ref, dgqn, first)
        _acc(dgkn_ref, dgkn, first)
        dc_ref[...] = jnp.concatenate([dcq, dckv], axis=1)
        dkr_ref[...] = jnp.concatenate([dkr, jnp.zeros((tm, 128 - QK_ROPE), F32)], axis=1)

    full = lambda *s: pl.BlockSpec(s, lambda i: (0,) * len(s))
    head = lambda w: pl.BlockSpec((N_HEADS, tm, w), lambda i: (0, i, 0))
    return pl.pallas_call(
        body, name="mla_pre_bwd",
        out_shape=(jax.ShapeDtypeStruct((n_rows, 1024), F32), jax.ShapeDtypeStruct((n_rows, 128), F32),
                   jax.ShapeDtypeStruct((Q_RANK, Q_RANK), F32), jax.ShapeDtypeStruct((KV_RANK, 1024), F32),
                   jax.ShapeDtypeStruct((1, Q_RANK), F32), jax.ShapeDtypeStruct((1, KV_RANK), F32),
                   jax.ShapeDtypeStruct((1, QK_DIM), F32), jax.ShapeDtypeStruct((1, QK_DIM), F32)),
        grid=(n_rows // tm,),
        in_specs=[pl.BlockSpec((tm, 1024), lambda i: (i, COL_CQKV // 1024)),
                  pl.BlockSpec((tm, 128), lambda i: (i, COL_KR // 128)),
                  pl.BlockSpec((tm, 128), lambda i: (i, 0)),
                  full(1, Q_RANK), full(1, KV_RANK), full(Q_RANK, Q_RANK), full(KV_RANK, 1024), full(1, QK_DIM), full(1, QK_DIM),
                  head(128), head(128), head(V_DIM)],
        out_specs=(pl.BlockSpec((tm, 1024), lambda i: (i, 0)), pl.BlockSpec((tm, 128), lambda i: (i, 0)),
                   full(Q_RANK, Q_RANK), full(KV_RANK, 1024), full(1, Q_RANK), full(1, KV_RANK), full(1, QK_DIM), full(1, QK_DIM)),
        compiler_params=_cp("arbitrary"),
    )(proj, proj, tab, gqa, gkva, wq, wkv, gqn, gkn, dq, dk, dv)


def _pool_bwd_a(proj, dy_pool, wg, scale, w_up):
    n_rows = proj.shape[0]
    tm = _row_tile(n_rows)

    def body(uz_ref, halo_ref, dy_ref, wg_ref, sc_ref, wup_ref, dmz_ref, dwg_ref, dsc_ref, dwup_ref):
        first = pl.program_id(0) == 0
        uz = uz_ref[...]
        u, z = uz[:, :POOL_WIDTH], uz[:, POOL_WIDTH:]
        scale_v = sc_ref[...]
        mixed, yg, ys, sig = _pool_math(u, halo_ref[...], z, wg_ref, scale_v, pl.program_id(0) * tm)
        sp = z * sig
        dy = dy_ref[...]
        da = _mm_nt(dy, wup_ref[...])
        _acc(dwup_ref, _mm_tn(ys * sp, dy), first)
        dys = da * sp
        _acc(dsc_ref, jnp.sum(dys * yg, axis=0, keepdims=True), first)
        dyg = dys * scale_v
        for g in range(4):
            cols = slice(g * 128, (g + 1) * 128)
            dmz_ref[:, cols] = _mm_nt(dyg[:, cols], wg_ref[g])
            _acc(dwg_ref.at[g], _mm_tn(mixed[:, cols], dyg[:, cols]), first)
        dmz_ref[:, POOL_WIDTH:] = da * ys * _dsilu(z, sig)

    return pl.pallas_call(
        body, name="pool_bwd_a",
        out_shape=(jax.ShapeDtypeStruct((n_rows, 1024), F32), jax.ShapeDtypeStruct((4, 128, 128), F32),
                   jax.ShapeDtypeStruct((1, POOL_WIDTH), F32), jax.ShapeDtypeStruct((POOL_WIDTH, D_MODEL), F32)),
        grid=(n_rows // tm,),
        in_specs=[pl.BlockSpec((tm, 1024), lambda i: (i, 0)), pl.BlockSpec((HALO, POOL_WIDTH), _halo_above(tm)),
                  pl.BlockSpec((tm, D_MODEL), lambda i: (i, 0)),
                  pl.BlockSpec((4, 128, 128), lambda i: (0, 0, 0)), pl.BlockSpec((1, POOL_WIDTH), lambda i: (0, 0)),
                  pl.BlockSpec((POOL_WIDTH, D_MODEL), lambda i: (0, 0))],
        out_specs=(pl.BlockSpec((tm, 1024), lambda i: (i, 0)), pl.BlockSpec((4, 128, 128), lambda i: (0, 0, 0)),
                   pl.BlockSpec((1, POOL_WIDTH), lambda i: (0, 0)), pl.BlockSpec((POOL_WIDTH, D_MODEL), lambda i: (0, 0))),
        compiler_params=_cp("arbitrary"),
    )(proj, proj, dy_pool, wg, scale, w_up)


def _pool_bwd_b(dmz):
    n_rows = dmz.shape[0]
    tm = _row_tile(n_rows)
    n_tiles = n_rows // tm
    n_ext = tm + HALO

    def body(dmz_ref, halo_ref, o_ref):
        i = pl.program_id(0)
        v = dmz_ref[...]
        dm = v[:, :POOL_WIDTH]
        halo = jnp.where(i == n_tiles - 1, 0.0, halo_ref[...])
        ext = jnp.concatenate([dm, halo], axis=0)
        t1 = (i * tm + lax.broadcasted_iota(jnp.int32, (n_ext, 1), 0) - (PAD_FRONT - 1)).astype(F32)
        du = []
        for g, w in enumerate(POOL_WINDOWS):
            cols = slice(g * 128, (g + 1) * 128)
            a = ext[:, cols] / jnp.clip(t1, 1.0, float(w))
            k = 1
            while k < w:
                a = a + pltpu.roll(a, n_ext - k, 0)
                k *= 2
            du.append(a[:tm, :] - dm[:, cols])
        o_ref[...] = jnp.concatenate(du + [v[:, POOL_WIDTH:]], axis=1)

    last_halo = n_rows // HALO - 1
    return pl.pallas_call(
        body, name="pool_bwd_b", out_shape=jax.ShapeDtypeStruct((n_rows, 1024), F32), grid=(n_tiles,),
        in_specs=[pl.BlockSpec((tm, 1024), lambda i: (i, 0)),
                  pl.BlockSpec((HALO, POOL_WIDTH), lambda i: (jnp.minimum((i + 1) * (tm // HALO), last_halo), 0))],
        out_specs=pl.BlockSpec((tm, 1024), lambda i: (i, 0)),
        compiler_params=_cp("parallel"),
    )(dmz, dmz)


def _norm_proj_bwd(d_uz, d_cqkv, d_gz, d_kr, w_in, h_res, gain, dh_out):
    n_rows = h_res.shape[0]
    tm = _row_tile(n_rows)
    pieces = ((COL_UZ, 1024), (COL_CQKV, 1024), (COL_GATES, 2560), (COL_KR, 128))

    def body(a_ref, b_ref, c_ref, d_ref, w_ref, x_ref, g_ref, dho_ref, dhi_ref, dg_ref):
        i = pl.program_id(0)
        dh = None
        for ref, (c0, wd) in zip((a_ref, b_ref, c_ref, d_ref), pieces):
            part = _mm_nt(ref[...], w_ref[:, c0:c0 + wd])
            dh = part if dh is None else dh + part
        g = g_ref[...]
        _, xh, inv = _rms(x_ref[...], g)
        dx, dg = _rms_bwd(dh, xh, inv, g)
        _acc(dg_ref, dg, i == 0)
        row = i * tm + lax.broadcasted_iota(jnp.int32, (tm, 1), 0)
        dhi_ref[...] = jnp.where(row >= PAD_FRONT, dho_ref[...] + dx, 0.0)

    row = lambda w: pl.BlockSpec((tm, w), lambda i: (i, 0))
    return pl.pallas_call(
        body, name="norm_proj_bwd",
        out_shape=(jax.ShapeDtypeStruct((n_rows, D_MODEL), F32), jax.ShapeDtypeStruct((1, D_MODEL), F32)),
        grid=(n_rows // tm,),
        in_specs=[row(1024), row(1024), row(2560), row(128), pl.BlockSpec((D_MODEL, P_IN), lambda i: (0, 0)),
                  row(D_MODEL), pl.BlockSpec((1, D_MODEL), lambda i: (0, 0)), row(D_MODEL)],
        out_specs=(row(D_MODEL), pl.BlockSpec((1, D_MODEL), lambda i: (0, 0))),
        compiler_params=_cp("arbitrary"),
    )(d_uz, d_cqkv, d_gz, d_kr, w_in, h_res, gain, dh_out)


def _weight_grad(name, a, b):
    k_rows, m = a.shape
    n = b.shape[1]
    tk = 640 if k_rows % 640 == 0 else 128
    tn = 512 if n % 512 == 0 else 128

    def body(a_ref, b_ref, o_ref):
        _acc(o_ref, _mm_tn(a_ref[...], b_ref[...]), pl.program_id(1) == 0)

    return pl.pallas_call(
        body, name=name, out_shape=jax.ShapeDtypeStruct((m, n), F32), grid=(n // tn, k_rows // tk),
        in_specs=[pl.BlockSpec((tk, m), lambda j, k: (k, 0)), pl.BlockSpec((tk, tn), lambda j, k: (k, j))],
        out_specs=pl.BlockSpec((m, tn), lambda j, k: (0, j)),
        compiler_params=_cp("parallel", "arbitrary"),
    )(a, b)


def _adamw(name, w, m, v, gbuf):
    r, c = w.shape
    tr = r
    if r * c * 4 > (1 << 20):
        tr = 256 if r % 256 == 0 else 128
    assert r % tr == 0, (name, r, tr)

    def body(w_ref, m_ref, v_ref, g_ref, go_ref, d_ref, mo_ref, vo_ref):
        g = g_ref[0]
        for s in range(1, N_DEV):
            g = g + g_ref[s]
        go_ref[...] = g
        m_new = ADAM_B1 * m_ref[...] + (1.0 - ADAM_B1) * g
        v_new = ADAM_B2 * v_ref[...] + (1.0 - ADAM_B2) * (g * g)
        mo_ref[...] = m_new
        vo_ref[...] = v_new
        m_hat = m_new / (1.0 - ADAM_B1 ** ADAM_STEP)
        v_hat = v_new / (1.0 - ADAM_B2 ** ADAM_STEP)
        d_ref[...] = -ADAM_LR * (m_hat / (jnp.sqrt(v_hat) + ADAM_EPS) + ADAM_WD * w_ref[...])

    spec = pl.BlockSpec((tr, c), lambda i: (i, 0))
    sds = jax.ShapeDtypeStruct((r, c), F32)
    return pl.pallas_call(
        body, name=name, out_shape=(sds, sds, sds, sds), grid=(r // tr,),
        in_specs=[spec, spec, spec, pl.BlockSpec((N_DEV, tr, c), lambda i: (0, i, 0))],
        out_specs=(spec, spec, spec, spec),
        compiler_params=_cp("parallel"),
    )(w, m, v, gbuf)


SMALL = ("norm_gain", "pool_w_group", "pool_scale", "q_a_norm_gain", "kv_a_norm_gain", "q_norm_gain", "k_norm_gain")


def _pack_small(parts):
    rows = []
    for p in parts:
        flat = p.reshape(-1)
        pad = (-flat.shape[0]) % 1024
        if pad:
            flat = jnp.concatenate([flat, jnp.zeros((pad,), F32)])
        rows.append(flat.reshape(-1, 128))
    n_rows = sum(r.shape[0] for r in rows)
    tail = (-n_rows) % 256
    if tail:
        rows.append(jnp.zeros((tail, 128), F32))
    return jnp.concatenate(rows, axis=0)


def _unpack_small(packed, shapes):
    out, r0 = [], 0
    for shp in shapes:
        n = math.prod(shp)
        n_rows = (n + 1023) // 1024 * 8
        out.append(packed[r0:r0 + n_rows].reshape(-1)[:n].reshape(shp))
        r0 += n_rows
    return out


def kernel(x, positions, meta_tokens, norm_gain, w_in, pool_w_group, pool_scale, pool_w_up, q_a_norm_gain, kv_a_norm_gain, w_q_b, w_kv_b, q_norm_gain, k_norm_gain, mla_w_up, w_out, loss_target, m_meta_tokens, m_norm_gain, m_w_in, m_pool_w_group, m_pool_scale, m_pool_w_up, m_q_a_norm_gain, m_kv_a_norm_gain, m_w_q_b, m_w_kv_b, m_q_norm_gain, m_k_norm_gain, m_mla_w_up, m_w_out, v_meta_tokens, v_norm_gain, v_w_in, v_pool_w_group, v_pool_scale, v_pool_w_up, v_q_a_norm_gain, v_kv_a_norm_gain, v_w_q_b, v_w_kv_b, v_q_norm_gain, v_k_norm_gain, v_mla_w_up, v_w_out):
    seq = x.shape[1]
    x2, target = x[0], loss_target[0]

    g_meta, g_in, g_pup, g_qb, g_kvb, g_mup, g_out = _exchange(
        "gather_weights", [meta_tokens, w_in, pool_w_up, w_q_b, w_kv_b, mla_w_up, w_out], [])
    w_in_p = _repack_w_in(g_in)
    w_pup = _repack_cols("repack_pool_up", g_pup, BF16)
    w_kvb = _repack_cols("repack_kv_b", g_kvb, BF16)
    w_mup = _repack_cols("repack_mla_up", g_mup, BF16)
    w_qb = _repack_rows("repack_q_b", g_qb)
    w_o = _repack_rows("repack_out", g_out)

    h = _embed(x2, g_meta)
    pos = jnp.concatenate([jnp.zeros((PAD_FRONT,), jnp.int32), jnp.arange(N_META, dtype=jnp.int32),
                           positions[0] + N_META]).astype(F32).reshape(-1, 1)
    half = QK_ROPE // 2
    inv_freq = ROPE_THETA ** (-jnp.arange(half, dtype=F32) / half)
    tab = _rope_table(pos, jnp.tile(inv_freq, 128 // half).reshape(1, 128))

    row = lambda a, l: a[l].reshape(1, -1)
    saved = []
    for l in range(DEPTH):
        hb, proj = _norm_proj(h, row(norm_gain, l), w_in_p[l])
        y_pool = _pool_fwd(proj, pool_w_group[l], row(pool_scale, l), w_pup[l])
        q, k, v = _mla_pre_fwd(proj, tab, row(q_a_norm_gain, l), row(kv_a_norm_gain, l), w_qb[l], w_kvb[l],
                               row(q_norm_gain, l), row(k_norm_gain, l))
        o, lse = _attn_fwd(q, k, v)
        y_mla, h_next = _mla_post_fwd(h, proj, o, y_pool, w_mup[l], w_o[l])
        saved.append((h, hb, proj, y_pool, q, k, v, o, lse, y_mla))
        h = h_next

    loss_part, dh = _loss_head(h, target)
    loss = lax.psum(loss_part[0, 0], MESH_AXES)

    grads = {n: [None] * DEPTH for n in ("norm_gain", "w_in", "pool_w_group", "pool_scale", "pool_w_up", "q_a_norm_gain",
                                         "kv_a_norm_gain", "w_q_b", "w_kv_b", "q_norm_gain", "k_norm_gain", "mla_w_up", "w_out")}
    for l in reversed(range(DEPTH)):
        h_l, hb, proj, y_pool, q, k, v, o, lse, y_mla = saved[l]
        d_gz, dy_pool, do, delta, grads["w_out"][l], grads["mla_w_up"][l] = _mla_post_bwd(
            dh, proj, o, y_pool, y_mla, w_mup[l], w_o[l])
        dq, dk, dv = _attn_bwd(q, k, v, do, lse, delta)
        (d_cqkv, d_kr, grads["w_q_b"][l], grads["w_kv_b"][l], grads["q_a_norm_gain"][l], grads["kv_a_norm_gain"][l],
         grads["q_norm_gain"][l], grads["k_norm_gain"][l]) = _mla_pre_bwd(
            proj, tab, row(q_a_norm_gain, l), row(kv_a_norm_gain, l), w_qb[l], w_kvb[l], row(q_norm_gain, l),
            row(k_norm_gain, l), dq, dk, dv)
        dmz, grads["pool_w_group"][l], grads["pool_scale"][l], grads["pool_w_up"][l] = _pool_bwd_a(
            proj, dy_pool, pool_w_group[l], row(pool_scale, l), w_pup[l])
        d_uz = _pool_bwd_b(dmz)
        grads["w_in"][l] = tuple(_weight_grad("dw_in", hb, d) for d in (d_uz, d_cqkv, d_gz, d_kr))
        dh, grads["norm_gain"][l] = _norm_proj_bwd(d_uz, d_cqkv, d_gz, d_kr, w_in_p[l], h_l, row(norm_gain, l), dh)

    grad_x = dh[HEAD_ROWS:][None]
    d_meta = dh[PAD_FRONT:HEAD_ROWS]

    stack = lambda n: jnp.stack(grads[n], axis=0)
    dw_in_parts = [jnp.stack([grads["w_in"][l][j] for l in range(DEPTH)], axis=0) for j in range(4)]
    scat = {
        "meta_tokens": _unpack_cols("unpack_meta", d_meta[None], 128)[:, 0],
        "w_in": _unpack_w_in(dw_in_parts),
        "pool_w_up": _unpack_cols("unpack_pool_up", stack("pool_w_up"), 128),
        "w_q_b": _unpack_rows("unpack_q_b", stack("w_q_b"), Q_RANK // N_DEV),
        "w_kv_b": _unpack_cols("unpack_kv_b", stack("w_kv_b"), 128),
        "mla_w_up": _unpack_cols("unpack_mla_up", stack("mla_w_up"), 128),
        "w_out": _unpack_rows("unpack_out", stack("w_out"), D_MODEL // N_DEV),
    }
    small_shapes = [a.shape for a in (norm_gain, pool_w_group, pool_scale, q_a_norm_gain, kv_a_norm_gain, q_norm_gain, k_norm_gain)]
    small_grad = _pack_small([stack(n).reshape(s) for n, s in zip(SMALL, small_shapes)])
    sharded = list(scat)
    received = _exchange("exchange_grads", [small_grad], [scat[n] for n in sharded])
    small_buf, shard_bufs = received[0], dict(zip(sharded, received[1:]))

    given = dict(meta_tokens=(meta_tokens, m_meta_tokens, v_meta_tokens), w_in=(w_in, m_w_in, v_w_in),
                 pool_w_up=(pool_w_up, m_pool_w_up, v_pool_w_up), w_q_b=(w_q_b, m_w_q_b, v_w_q_b),
                 w_kv_b=(w_kv_b, m_w_kv_b, v_w_kv_b), mla_w_up=(mla_w_up, m_mla_w_up, v_mla_w_up),
                 w_out=(w_out, m_w_out, v_w_out))
    result = {}
    for n in sharded:
        w, m, v = given[n]
        cols = w.shape[-1]
        res = _adamw("adamw_" + n, w.reshape(-1, cols), m.reshape(-1, cols), v.reshape(-1, cols),
                     shard_bufs[n].reshape(N_DEV, -1, cols))
        result[n] = [r.reshape(w.shape) for r in res]
    small_w = (norm_gain, pool_w_group, pool_scale, q_a_norm_gain, kv_a_norm_gain, q_norm_gain, k_norm_gain)
    small_m = (m_norm_gain, m_pool_w_group, m_pool_scale, m_q_a_norm_gain, m_kv_a_norm_gain, m_q_norm_gain, m_k_norm_gain)
    small_v = (v_norm_gain, v_pool_w_group, v_pool_scale, v_q_a_norm_gain, v_kv_a_norm_gain, v_q_norm_gain, v_k_norm_gain)
    res = _adamw("adamw_small", _pack_small(small_w), _pack_small(small_m), _pack_small(small_v), small_buf)
    small_res = [_unpack_small(r, small_shapes) for r in res]
    for j, n in enumerate(SMALL):
        result[n] = [small_res[kind][j] for kind in range(4)]

    order = ("meta_tokens", "norm_gain", "w_in", "pool_w_group", "pool_scale", "pool_w_up", "q_a_norm_gain",
             "kv_a_norm_gain", "w_q_b", "w_kv_b", "q_norm_gain", "k_norm_gain", "mla_w_up", "w_out")
    outs = [loss, grad_x]
    for kind in range(4):
        outs += [result[n][kind] for n in order]
    return tuple(outs)
```

```python
import math

import jax
import jax.numpy as jnp
from jax import lax
from jax.experimental import pallas as pl
from jax.experimental.pallas import tpu as pltpu

F32 = jnp.float32
BF16 = jnp.bfloat16

D_MODEL = 1024
DEPTH = 4
N_META = 16
PAD_FRONT = 112
HEAD_ROWS = PAD_FRONT + N_META
POOL_WIDTH = 512
POOL_WINDOWS = (2, 4, 8, 16)
POOL_GROUP_DIM = 128
HALO = 16
N_HEADS = 8
HEADS_PER_STEP = 2
QK_NOPE = 64
QK_ROPE = 32
QK_DIM = 96
V_DIM = 64
MLA_WIDTH = 512
Q_RANK = 768
KV_RANK = 256
ROPE_THETA = 10000.0
NORM_EPS = 1e-6
MASK_VALUE = -1e30
D_IN = 4640
N_DEV = 8
IN_SHARD = D_IN // N_DEV

P_IN = 4736
IN_SEGMENTS = ((0, 2048, 0), (2048, 2080, 4608), (2080, 2592, 4096), (2592, 3616, 2048), (3616, 4640, 3072))
COL_UZ, COL_CQKV, COL_GATES, COL_ZMLA, COL_KR = 0, 1024, 2048, 4096, 4608

ADAM_LR = 0.001
ADAM_B1 = 0.9
ADAM_B2 = 0.999
ADAM_EPS = 1e-08
ADAM_WD = 0.01
ADAM_STEP = 10

VMEM_LIMIT = 56 * 1024 * 1024
MESH_AXES = ("x", "y", "c")


def _cp(*sem):
    return pltpu.CompilerParams(dimension_semantics=sem, vmem_limit_bytes=VMEM_LIMIT)


def _row_tile(n_rows):
    return 320 if n_rows % 320 == 0 else 128


def _attn_tile(n_rows):
    return 640 if (n_rows % 640 == 0 and n_rows > 640) else 128


def _mm(a, b):
    return jnp.dot(a.astype(BF16), b.astype(BF16), preferred_element_type=F32)


def _mm_nt(a, b):
    return lax.dot_general(a.astype(BF16), b.astype(BF16), (((1,), (1,)), ((), ())), preferred_element_type=F32)


def _mm_tn(a, b):
    return lax.dot_general(a.astype(BF16), b.astype(BF16), (((0,), (0,)), ((), ())), preferred_element_type=F32)


def _sigmoid(z):
    return 1.0 / (1.0 + jnp.exp(-z))


def _rms(x, g):
    inv = lax.rsqrt(jnp.mean(x * x, axis=-1, keepdims=True) + NORM_EPS)
    xh = x * inv
    return xh * g, xh, inv


def _rms_bwd(dy, xh, inv, g):
    dg = jnp.sum(dy * xh, axis=0, keepdims=True)
    dxh = dy * g
    dx = inv * (dxh - xh * jnp.mean(dxh * xh, axis=-1, keepdims=True))
    return dx, dg


def _rope(r, cos, sin):
    rot = jnp.concatenate([-r[:, 16:32], r[:, 0:16]], axis=1)
    return r * cos + rot * sin


def _rope_bwd(dy, cos, sin):
    ys = dy * sin
    return dy * cos + jnp.concatenate([ys[:, 16:32], -ys[:, 0:16]], axis=1)


def _exchange(name, gather_list, scatter_list):
    arrays = list(gather_list) + list(scatter_list)
    n_g, n = len(gather_list), len(arrays)
    out_shape = [jax.ShapeDtypeStruct((N_DEV,) + a.shape, a.dtype) for a in gather_list]
    out_shape += [jax.ShapeDtypeStruct(a.shape, a.dtype) for a in scatter_list]

    def body(*refs):
        ins, outs = refs[:n], refs[n:2 * n]
        send_sems, recv_sems, local_sems = refs[2 * n:]
        x, y, c = lax.axis_index("x"), lax.axis_index("y"), lax.axis_index("c")
        me = 4 * x + 2 * y + c

        def src(a, slot):
            return ins[a] if a < n_g else ins[a].at[slot]

        local = []
        for a in range(n):
            cp = pltpu.make_async_copy(src(a, me), outs[a].at[me], local_sems.at[a])
            cp.start()
            local.append(cp)
        remote = []
        for k in range(1, N_DEV):
            px = 1 - x if k & 4 else x
            py = 1 - y if k & 2 else y
            pc = 1 - c if k & 1 else c
            peer = 4 * px + 2 * py + pc
            for a in range(n):
                cp = pltpu.make_async_remote_copy(
                    src_ref=src(a, peer), dst_ref=outs[a].at[me],
                    send_sem=send_sems.at[a * 7 + k - 1], recv_sem=recv_sems.at[a * 7 + k - 1],
                    device_id=(px, py, pc), device_id_type=pl.DeviceIdType.MESH)
                cp.start()
                remote.append(cp)
        for cp in remote:
            cp.wait()
        for cp in local:
            cp.wait()

    any_spec = pl.BlockSpec(memory_space=pl.ANY)
    return pl.pallas_call(
        body, name=name, out_shape=out_shape,
        in_specs=[any_spec] * n, out_specs=[any_spec] * n,
        scratch_shapes=[pltpu.SemaphoreType.DMA((7 * n,)), pltpu.SemaphoreType.DMA((7 * n,)),
                        pltpu.SemaphoreType.DMA((n,))],
        compiler_params=pltpu.CompilerParams(has_side_effects=True),
    )(*arrays)


def _in_pieces():
    runs = []
    for lo, hi, dst in sorted(IN_SEGMENTS, key=lambda s: s[2]):
        col = lo
        while col < hi:
            d = col // IN_SHARD
            end = min(hi, (d + 1) * IN_SHARD)
            runs.append((d, col - d * IN_SHARD, end - d * IN_SHARD))
            col = end
    return runs


def _cast_bf16(name, w):
    shape = w.shape
    w2 = w.reshape(-1, shape[-1])
    r, c = w2.shape
    tr = 512 if r % 512 == 0 else r

    def body(x_ref, o_ref):
        o_ref[...] = x_ref[...].astype(BF16)

    spec = pl.BlockSpec((tr, c), lambda i: (i, 0))
    out = pl.pallas_call(body, name=name, out_shape=jax.ShapeDtypeStruct((r, c), BF16), grid=(r // tr,),
                         in_specs=[spec], out_specs=spec, compiler_params=_cp("parallel"))(w2)
    return out.reshape(shape)


def _repack_w_in(gathered):
    tr = 256
    runs = _in_pieces()

    def body(x_ref, o_ref):
        parts = [x_ref[d, 0, :, a:b].astype(F32) for d, a, b in runs]
        parts.append(jnp.zeros((tr, P_IN - D_IN), F32))
        o_ref[0] = jnp.concatenate(parts, axis=1).astype(BF16)

    return pl.pallas_call(
        body, name="repack_w_in", out_shape=jax.ShapeDtypeStruct((DEPTH, D_MODEL, P_IN), BF16),
        grid=(DEPTH, D_MODEL // tr),
        in_specs=[pl.BlockSpec((N_DEV, 1, tr, IN_SHARD), lambda l, i: (0, l, i, 0))],
        out_specs=pl.BlockSpec((1, tr, P_IN), lambda l, i: (l, i, 0)),
        compiler_params=_cp("parallel", "parallel"),
    )(gathered)


def _unpack_w_in(parts):
    tr = 256
    runs = _in_pieces()
    widths = [p.shape[-1] for p in parts]

    def body(a_ref, b_ref, c_ref, d_ref, o_ref):
        full = jnp.concatenate([a_ref[0], b_ref[0], c_ref[0], d_ref[0]], axis=1)
        col = 0
        per_dev = [[] for _ in range(N_DEV)]
        for d, a, b in runs:
            per_dev[d].append((a, full[:, col:col + (b - a)]))
            col += b - a
        for d in range(N_DEV):
            pieces = [v for _, v in sorted(per_dev[d], key=lambda av: av[0])]
            o_ref[d, 0] = (jnp.concatenate(pieces, axis=1) if len(pieces) > 1 else pieces[0]).astype(BF16)

    return pl.pallas_call(
        body, name="unpack_w_in", out_shape=jax.ShapeDtypeStruct((N_DEV, DEPTH, D_MODEL, IN_SHARD), BF16),
        grid=(DEPTH, D_MODEL // tr),
        in_specs=[pl.BlockSpec((1, tr, w), lambda l, i: (l, i, 0)) for w in widths],
        out_specs=pl.BlockSpec((N_DEV, 1, tr, IN_SHARD), lambda l, i: (0, l, i, 0)),
        compiler_params=_cp("parallel", "parallel"),
    )(*parts)


def _repack_cols(name, gathered, dtype):
    _, depth, k, n = gathered.shape

    def body(x_ref, o_ref):
        o_ref[0] = jnp.concatenate([x_ref[d, 0] for d in range(N_DEV)], axis=1).astype(dtype)

    return pl.pallas_call(
        body, name=name, out_shape=jax.ShapeDtypeStruct((depth, k, N_DEV * n), dtype), grid=(depth,),
        in_specs=[pl.BlockSpec((N_DEV, 1, k, n), lambda l: (0, l, 0, 0))],
        out_specs=pl.BlockSpec((1, k, N_DEV * n), lambda l: (l, 0, 0)),
        compiler_params=_cp("parallel"),
    )(gathered)


def _unpack_cols(name, full, n, dtype):
    depth, k, _ = full.shape

    def body(x_ref, o_ref):
        for d in range(N_DEV):
            o_ref[d, 0] = x_ref[0, :, d * n:(d + 1) * n].astype(dtype)

    return pl.pallas_call(
        body, name=name, out_shape=jax.ShapeDtypeStruct((N_DEV, depth, k, n), dtype), grid=(depth,),
        in_specs=[pl.BlockSpec((1, k, N_DEV * n), lambda l: (l, 0, 0))],
        out_specs=pl.BlockSpec((N_DEV, 1, k, n), lambda l: (0, l, 0, 0)),
        compiler_params=_cp("parallel"),
    )(full)


def _repack_rows(name, gathered):
    _, depth, r, n = gathered.shape

    def body(x_ref, o_ref):
        for d in range(N_DEV):
            o_ref[0, d * r:(d + 1) * r, :] = x_ref[d, 0].astype(BF16)

    return pl.pallas_call(
        body, name=name, out_shape=jax.ShapeDtypeStruct((depth, N_DEV * r, n), BF16), grid=(depth,),
        in_specs=[pl.BlockSpec((N_DEV, 1, r, n), lambda l: (0, l, 0, 0))],
        out_specs=pl.BlockSpec((1, N_DEV * r, n), lambda l: (l, 0, 0)),
        compiler_params=_cp("parallel"),
    )(gathered)


def _unpack_rows(name, full, r):
    depth, _, n = full.shape

    def body(x_ref, o_ref):
        for d in range(N_DEV):
            o_ref[d, 0] = x_ref[0, d * r:(d + 1) * r, :].astype(BF16)

    return pl.pallas_call(
        body, name=name, out_shape=jax.ShapeDtypeStruct((N_DEV, depth, r, n), BF16), grid=(depth,),
        in_specs=[pl.BlockSpec((1, N_DEV * r, n), lambda l: (l, 0, 0))],
        out_specs=pl.BlockSpec((N_DEV, 1, r, n), lambda l: (0, l, 0, 0)),
        compiler_params=_cp("parallel"),
    )(full)


def _embed(x, meta_g):
    seq = x.shape[0]
    n_blk = seq // HEAD_ROWS + 1

    def body(x_ref, m_ref, o_ref):
        i = pl.program_id(0)

        @pl.when(i == 0)
        def _():
            meta = jnp.concatenate([m_ref[d] for d in range(N_DEV)], axis=1)
            o_ref[...] = jnp.concatenate([jnp.zeros((PAD_FRONT, D_MODEL), F32), meta], axis=0)

        @pl.when(i > 0)
        def _():
            o_ref[...] = x_ref[...]

    return pl.pallas_call(
        body, name="embed", out_shape=jax.ShapeDtypeStruct((seq + HEAD_ROWS, D_MODEL), F32), grid=(n_blk,),
        in_specs=[pl.BlockSpec((HEAD_ROWS, D_MODEL), lambda i: (jnp.maximum(i - 1, 0), 0)),
                  pl.BlockSpec((N_DEV, N_META, 128), lambda i: (0, 0, 0))],
        out_specs=pl.BlockSpec((HEAD_ROWS, D_MODEL), lambda i: (i, 0)),
        compiler_params=_cp("parallel"),
    )(x, meta_g)


def _rope_table(pos_col, freq_row):
    n_rows = pos_col.shape[0]
    tr = 128

    def body(p_ref, f_ref, o_ref):
        ang = p_ref[...] * f_ref[...]
        lane = lax.broadcasted_iota(jnp.int32, ang.shape, 1)
        o_ref[...] = jnp.where(lane < QK_ROPE, jnp.cos(ang), jnp.sin(ang))

    return pl.pallas_call(
        body, name="rope_table", out_shape=jax.ShapeDtypeStruct((n_rows, 128), F32), grid=(n_rows // tr,),
        in_specs=[pl.BlockSpec((tr, 1), lambda i: (i, 0)), pl.BlockSpec((1, 128), lambda i: (0, 0))],
        out_specs=pl.BlockSpec((tr, 128), lambda i: (i, 0)),
        compiler_params=_cp("parallel"),
    )(pos_col, freq_row)


def _loss_head(h, target):
    n_rows = h.shape[0]
    n_blk = n_rows // HEAD_ROWS

    def body(h_ref, t_ref, loss_ref, dh_ref):
        i = pl.program_id(0)

        @pl.when(i == 0)
        def _():
            loss_ref[...] = jnp.zeros(loss_ref.shape, F32)
            dh_ref[...] = jnp.zeros(dh_ref.shape, F32)

        @pl.when(i > 0)
        def _():
            err = h_ref[...] - t_ref[...]
            dh_ref[...] = err * (1.0 / D_MODEL)
            loss_ref[...] += 0.5 * jnp.sum(jnp.mean(err * err, axis=-1, keepdims=True))

    return pl.pallas_call(
        body, name="loss_head",
        out_shape=(jax.ShapeDtypeStruct((8, 128), F32), jax.ShapeDtypeStruct((n_rows, D_MODEL), F32)),
        grid=(n_blk,),
        in_specs=[pl.BlockSpec((HEAD_ROWS, D_MODEL), lambda i: (i, 0)),
                  pl.BlockSpec((HEAD_ROWS, D_MODEL), lambda i: (jnp.maximum(i - 1, 0), 0))],
        out_specs=(pl.BlockSpec((8, 128), lambda i: (0, 0)), pl.BlockSpec((HEAD_ROWS, D_MODEL), lambda i: (i, 0))),
        compiler_params=_cp("arbitrary"),
    )(h, target)


PROJ_CHUNKS = (0, 1024, 2048, 3072, 4096, P_IN)


def _norm_proj(h_res, gain, w_in):
    n_rows = h_res.shape[0]
    tm = _row_tile(n_rows)

    def body(x_ref, g_ref, w_ref, hb_ref, p_ref):
        h, _, _ = _rms(x_ref[...], g_ref[...])
        hb = h.astype(BF16)
        hb_ref[...] = hb
        for c0, c1 in zip(PROJ_CHUNKS[:-1], PROJ_CHUNKS[1:]):
            p_ref[:, c0:c1] = jnp.dot(hb, w_ref[:, c0:c1], preferred_element_type=F32)

    return pl.pallas_call(
        body, name="norm_proj",
        out_shape=(jax.ShapeDtypeStruct((n_rows, D_MODEL), BF16), jax.ShapeDtypeStruct((n_rows, P_IN), F32)),
        grid=(n_rows // tm,),
        in_specs=[pl.BlockSpec((tm, D_MODEL), lambda i: (i, 0)), pl.BlockSpec((1, D_MODEL), lambda i: (0, 0)),
                  pl.BlockSpec((D_MODEL, P_IN), lambda i: (0, 0))],
        out_specs=(pl.BlockSpec((tm, D_MODEL), lambda i: (i, 0)), pl.BlockSpec((tm, P_IN), lambda i: (i, 0))),
        compiler_params=_cp("parallel"),
    )(h_res, gain, w_in)


def _pool_math(u, halo, z, wg_ref, scale, row0):
    tm = u.shape[0]
    ext = jnp.concatenate([halo, u], axis=0)
    t1 = (row0 + lax.broadcasted_iota(jnp.int32, (tm, 1), 0) - (PAD_FRONT - 1)).astype(F32)
    mixed, yg = [], []
    for g, w in enumerate(POOL_WINDOWS):
        a = ext[:, g * 128:(g + 1) * 128]
        k = 1
        while k < w:
            a = a + pltpu.roll(a, k, 0)
            k *= 2
        cnt = jnp.clip(t1, 1.0, float(w))
        mg = a[HALO:, :] / cnt - u[:, g * 128:(g + 1) * 128]
        mixed.append(mg)
        yg.append(_mm(mg, wg_ref[g]))
    mixed = jnp.concatenate(mixed, axis=1)
    yg = jnp.concatenate(yg, axis=1)
    ys = yg * scale
    sig = _sigmoid(z)
    return mixed, yg, ys, sig


def _halo_above(tm):
    return lambda i: (jnp.maximum(i * (tm // HALO) - 1, 0), 0)


def _pool_fwd(proj, wg, scale, w_up):
    n_rows = proj.shape[0]
    tm = _row_tile(n_rows)

    def body(uz_ref, halo_ref, wg_ref, sc_ref, wup_ref, y_ref):
        uz = uz_ref[...]
        u, z = uz[:, :POOL_WIDTH], uz[:, POOL_WIDTH:]
        _, _, ys, sig = _pool_math(u, halo_ref[...], z, wg_ref, sc_ref[...], pl.program_id(0) * tm)
        y_ref[...] = _mm(ys * (z * sig), wup_ref[...])

    return pl.pallas_call(
        body, name="pool_fwd", out_shape=jax.ShapeDtypeStruct((n_rows, D_MODEL), F32), grid=(n_rows // tm,),
        in_specs=[pl.BlockSpec((tm, 1024), lambda i: (i, 0)), pl.BlockSpec((HALO, POOL_WIDTH), _halo_above(tm)),
                  pl.BlockSpec((4, 128, 128), lambda i: (0, 0, 0)), pl.BlockSpec((1, POOL_WIDTH), lambda i: (0, 0)),
                  pl.BlockSpec((POOL_WIDTH, D_MODEL), lambda i: (0, 0))],
        out_specs=pl.BlockSpec((tm, D_MODEL), lambda i: (i, 0)),
        compiler_params=_cp("parallel"),
    )(proj, proj, wg, scale, w_up)


def _mla_heads(cq, ckv, kr, gqa, gkva, wq_ref, wkv_ref):
    cqn, cq_h, cq_inv = _rms(cq, gqa)
    ckvn, ckv_h, ckv_inv = _rms(ckv, gkva)
    qraw = _mm(cqn, wq_ref[...])
    kvraw = _mm(ckvn, wkv_ref[...])
    return cqn, cq_h, cq_inv, ckvn, ckv_h, ckv_inv, qraw, kvraw


def _mla_pre_fwd(proj, tab, gqa, gkva, wq, wkv, gqn, gkn):
    n_rows = proj.shape[0]
    tm = _row_tile(n_rows)

    def body(c_ref, kr_ref, tab_ref, gqa_ref, gkva_ref, wq_ref, wkv_ref, gqn_ref, gkn_ref, q_ref, k_ref, v_ref):
        c = c_ref[...]
        kr = kr_ref[:, 0:QK_ROPE]
        cos, sin = tab_ref[:, 0:QK_ROPE], tab_ref[:, QK_ROPE:2 * QK_ROPE]
        _, _, _, _, _, _, qraw, kvraw = _mla_heads(c[:, :Q_RANK], c[:, Q_RANK:], kr, gqa_ref[...], gkva_ref[...],
                                                  wq_ref, wkv_ref)
        zpad = jnp.zeros((tm, 128 - QK_DIM), F32)
        for h in range(N_HEADS):
            qn, _, _ = _rms(qraw[:, QK_DIM * h:QK_DIM * (h + 1)], gqn_ref[...])
            q_ref[h] = jnp.concatenate([qn[:, :QK_NOPE], _rope(qn[:, QK_NOPE:], cos, sin), zpad], axis=1).astype(BF16)
            kin = jnp.concatenate([kvraw[:, 128 * h:128 * h + QK_NOPE], kr], axis=1)
            kn, _, _ = _rms(kin, gkn_ref[...])
            k_ref[h] = jnp.concatenate([kn[:, :QK_NOPE], _rope(kn[:, QK_NOPE:], cos, sin), zpad], axis=1).astype(BF16)
            v_ref[h] = kvraw[:, 128 * h + QK_NOPE:128 * (h + 1)].astype(BF16)

    full = lambda *s: pl.BlockSpec(s, lambda i: (0,) * len(s))
    return pl.pallas_call(
        body, name="mla_pre_fwd",
        out_shape=(jax.ShapeDtypeStruct((N_HEADS, n_rows, 128), BF16), jax.ShapeDtypeStruct((N_HEADS, n_rows, 128), BF16),
                   jax.ShapeDtypeStruct((N_HEADS, n_rows, V_DIM), BF16)),
        grid=(n_rows // tm,),
        in_specs=[pl.BlockSpec((tm, 1024), lambda i: (i, COL_CQKV // 1024)),
                  pl.BlockSpec((tm, 128), lambda i: (i, COL_KR // 128)),
                  pl.BlockSpec((tm, 128), lambda i: (i, 0)),
                  full(1, Q_RANK), full(1, KV_RANK), full(Q_RANK, Q_RANK), full(KV_RANK, 1024), full(1, QK_DIM), full(1, QK_DIM)],
        out_specs=(pl.BlockSpec((N_HEADS, tm, 128), lambda i: (0, i, 0)), pl.BlockSpec((N_HEADS, tm, 128), lambda i: (0, i, 0)),
                   pl.BlockSpec((N_HEADS, tm, V_DIM), lambda i: (0, i, 0))),
        compiler_params=_cp("parallel"),
    )(proj, proj, tab, gqa, gkva, wq, wkv, gqn, gkn)


def _attn_mask(row0, col0, n_r, n_c):
    row = row0 + lax.broadcasted_iota(jnp.int32, (n_r, n_c), 0)
    col = col0 + lax.broadcasted_iota(jnp.int32, (n_r, n_c), 1)
    return (col <= row) & (col >= PAD_FRONT)


ATTN_SCALE = 1.0 / math.sqrt(QK_DIM)
EXP2_SCALE = ATTN_SCALE * math.log2(math.e)
LOG2_E = math.log2(math.e)


def _row_chunks(t, size):
    return [(r, min(size, t - r)) for r in range(0, t, size)]


def _attn_fwd(q, k, v):
    n_rows = q.shape[1]
    t = _attn_tile(n_rows)
    nb = n_rows // t

    def body(q_ref, k_ref, v_ref, o_ref, lse_ref, m_sc, l_sc, acc_sc):
        qi, ki = pl.program_id(1), pl.program_id(2)

        @pl.when(ki == 0)
        def _():
            m_sc[...] = jnp.full(m_sc.shape, -jnp.inf, F32)
            l_sc[...] = jnp.zeros(l_sc.shape, F32)
            acc_sc[...] = jnp.zeros(acc_sc.shape, F32)

        def update(masked):
            heads = range(HEADS_PER_STEP)
            s = [_mm_nt(q_ref[h], k_ref[h]) for h in heads]
            if masked:
                mask = _attn_mask(qi * t, ki * t, t, t)
                s = [jnp.where(mask, sh, MASK_VALUE) for sh in s]
            p = []
            for h in heads:
                m_prev = m_sc[h]
                m_new = jnp.maximum(m_prev, jnp.max(s[h], axis=1, keepdims=True))
                alpha = jnp.exp2((m_prev - m_new) * EXP2_SCALE)
                ph = jnp.exp2((s[h] - m_new) * EXP2_SCALE)
                l_sc[h] = alpha * l_sc[h] + jnp.sum(ph, axis=1, keepdims=True)
                acc_sc[h] = alpha * acc_sc[h]
                m_sc[h] = m_new
                p.append(ph)
            for h in heads:
                acc_sc[h] += _mm(p[h], v_ref[h])

        edge = (ki == qi) | (ki == 0)

        @pl.when((ki <= qi) & edge)
        def _():
            update(True)

        @pl.when((ki < qi) & (ki > 0))
        def _():
            update(False)

        @pl.when(ki == nb - 1)
        def _():
            o_ref[...] = acc_sc[...] / l_sc[...]
            lse_ref[...] = m_sc[...] * ATTN_SCALE + jnp.log(l_sc[...])

    hp = HEADS_PER_STEP
    kv_idx = lambda h, qi, ki: (h, jnp.minimum(ki, qi), 0)
    return pl.pallas_call(
        body, name="attn_fwd",
        out_shape=(jax.ShapeDtypeStruct((N_HEADS, n_rows, V_DIM), F32), jax.ShapeDtypeStruct((N_HEADS, n_rows, 1), F32)),
        grid=(N_HEADS // hp, nb, nb),
        in_specs=[pl.BlockSpec((hp, t, 128), lambda h, qi, ki: (h, qi, 0)), pl.BlockSpec((hp, t, 128), kv_idx),
                  pl.BlockSpec((hp, t, V_DIM), kv_idx)],
        out_specs=(pl.BlockSpec((hp, t, V_DIM), lambda h, qi, ki: (h, qi, 0)), pl.BlockSpec((hp, t, 1), lambda h, qi, ki: (h, qi, 0))),
        scratch_shapes=[pltpu.VMEM((hp, t, 1), F32), pltpu.VMEM((hp, t, 1), F32), pltpu.VMEM((hp, t, V_DIM), F32)],
        compiler_params=_cp("parallel", "parallel", "arbitrary"),
    )(q, k, v)


def _merge_math(o_ref, z, gates, y_pool, y_mla_fn):
    o_cat = jnp.concatenate([o_ref[h] for h in range(N_HEADS)], axis=1)
    sig_z = _sigmoid(z)
    a_mla = o_cat * (z * sig_z)
    y_mla = y_mla_fn(a_mla)
    sgp, sgm = _sigmoid(gates[:, :D_MODEL]), _sigmoid(gates[:, D_MODEL:])
    merged = sgp * y_pool + sgm * y_mla
    return o_cat, sig_z, a_mla, y_mla, sgp, sgm, merged


def _mla_post_fwd(h_res, proj, o, y_pool, w_mla_up, w_out):
    n_rows = h_res.shape[0]
    tm = _row_tile(n_rows)

    def body(h_ref, g_ref, z_ref, o_ref, yp_ref, wup_ref, wout_ref, ymla_ref, hn_ref):
        _, _, _, y_mla, _, _, merged = _merge_math(o_ref, z_ref[...], g_ref[...], yp_ref[...],
                                                   lambda a: _mm(a, wup_ref[...]))
        ymla_ref[...] = y_mla
        hn_ref[...] = h_ref[...] + _mm(merged, wout_ref[...])

    row = lambda w, c: pl.BlockSpec((tm, w), lambda i: (i, c))
    return pl.pallas_call(
        body, name="mla_post_fwd",
        out_shape=(jax.ShapeDtypeStruct((n_rows, D_MODEL), F32), jax.ShapeDtypeStruct((n_rows, D_MODEL), F32)),
        grid=(n_rows // tm,),
        in_specs=[row(D_MODEL, 0), row(2048, COL_GATES // 2048), row(MLA_WIDTH, COL_ZMLA // MLA_WIDTH),
                  pl.BlockSpec((N_HEADS, tm, V_DIM), lambda i: (0, i, 0)), row(D_MODEL, 0),
                  pl.BlockSpec((MLA_WIDTH, D_MODEL), lambda i: (0, 0)), pl.BlockSpec((D_MODEL, D_MODEL), lambda i: (0, 0))],
        out_specs=(row(D_MODEL, 0), row(D_MODEL, 0)),
        compiler_params=_cp("parallel"),
    )(h_res, proj, proj, o, y_pool, w_mla_up, w_out)


def _dsilu(z, sig):
    return sig * (1.0 + z * (1.0 - sig))


def _acc(ref, val, first):
    @pl.when(first)
    def _():
        ref[...] = val

    @pl.when(jnp.logical_not(first))
    def _():
        ref[...] += val


def _mla_post_bwd(dh, proj, o, y_pool, y_mla, w_mla_up, w_out):
    n_rows = dh.shape[0]
    tm = _row_tile(n_rows)

    def body(dh_ref, g_ref, z_ref, o_ref, yp_ref, ym_ref, wup_ref, wout_ref,
             dgz_ref, dyp_ref, do_ref, delta_ref, dwout_ref, dwup_ref):
        first = pl.program_id(0) == 0
        z, y_pool, y_mla = z_ref[...], yp_ref[...], ym_ref[...]
        o_cat, sig_z, a_mla, _, sgp, sgm, merged = _merge_math(o_ref, z, g_ref[...], y_pool, lambda a: y_mla)
        dhv = dh_ref[...]
        dmerged = _mm_nt(dhv, wout_ref[...])
        _acc(dwout_ref, _mm_tn(merged, dhv), first)
        dyp_ref[...] = dmerged * sgp
        dy_mla = dmerged * sgm
        da = _mm_nt(dy_mla, wup_ref[...])
        _acc(dwup_ref, _mm_tn(a_mla, dy_mla), first)
        do_cat = da * (z * sig_z)
        dgz_ref[:, 0:D_MODEL] = dmerged * y_pool * (sgp * (1.0 - sgp))
        dgz_ref[:, D_MODEL:2 * D_MODEL] = dmerged * y_mla * (sgm * (1.0 - sgm))
        dgz_ref[:, 2 * D_MODEL:] = da * o_cat * _dsilu(z, sig_z)
        for h in range(N_HEADS):
            doh = do_cat[:, V_DIM * h:V_DIM * (h + 1)]
            do_ref[h] = doh
            delta_ref[h] = jnp.sum(doh * o_cat[:, V_DIM * h:V_DIM * (h + 1)], axis=1, keepdims=True)

    row = lambda w, c: pl.BlockSpec((tm, w), lambda i: (i, c))
    head = lambda w: pl.BlockSpec((N_HEADS, tm, w), lambda i: (0, i, 0))
    const = lambda r, c: pl.BlockSpec((r, c), lambda i: (0, 0))
    return pl.pallas_call(
        body, name="mla_post_bwd",
        out_shape=(jax.ShapeDtypeStruct((n_rows, 2560), F32), jax.ShapeDtypeStruct((n_rows, D_MODEL), F32),
                   jax.ShapeDtypeStruct((N_HEADS, n_rows, V_DIM), F32), jax.ShapeDtypeStruct((N_HEADS, n_rows, 1), F32),
                   jax.ShapeDtypeStruct((D_MODEL, D_MODEL), F32), jax.ShapeDtypeStruct((MLA_WIDTH, D_MODEL), F32)),
        grid=(n_rows // tm,),
        in_specs=[row(D_MODEL, 0), row(2048, COL_GATES // 2048), row(MLA_WIDTH, COL_ZMLA // MLA_WIDTH), head(V_DIM),
                  row(D_MODEL, 0), row(D_MODEL, 0), const(MLA_WIDTH, D_MODEL), const(D_MODEL, D_MODEL)],
        out_specs=(row(2560, 0), row(D_MODEL, 0), head(V_DIM), head(1), const(D_MODEL, D_MODEL), const(MLA_WIDTH, D_MODEL)),
        compiler_params=_cp("arbitrary"),
    )(dh, proj, proj, o, y_pool, y_mla, w_mla_up, w_out)


def _attn_bwd(q, k, v, do, lse, delta):
    n_rows = q.shape[1]
    t = _attn_tile(n_rows)
    nb = n_rows // t

    def body(q_ref, k_ref, v_ref, do_ref, lse_ref, delta_ref, dq_ref, dk_ref, dv_ref, dk_sc, dv_sc):
        ki, qi = pl.program_id(1), pl.program_id(2)

        @pl.when((ki == 0) & (qi == 0))
        def _():
            dq_ref[...] = jnp.zeros(dq_ref.shape, F32)

        @pl.when(qi == 0)
        def _():
            dk_sc[...] = jnp.zeros(dk_sc.shape, F32)
            dv_sc[...] = jnp.zeros(dv_sc.shape, F32)

        def step(masked):
            heads = range(HEADS_PER_STEP)
            s = [_mm_nt(q_ref[h], k_ref[h]) for h in heads]
            dp = [_mm_nt(do_ref[h], v_ref[h]) for h in heads]
            if masked:
                mask = _attn_mask(qi * t, ki * t, t, t)
                s = [jnp.where(mask, sh, MASK_VALUE) for sh in s]
            p = [jnp.exp2(s[h] * EXP2_SCALE - lse_ref[h] * LOG2_E) for h in heads]
            ds = [p[h] * (dp[h] - delta_ref[h]) for h in heads]
            rows = pl.ds(pl.multiple_of(qi * t, t), t)
            for h in heads:
                dv_sc[h] += _mm_tn(p[h], do_ref[h])
                dq_ref[h, rows, :] += _mm(ds[h], k_ref[h]) * ATTN_SCALE
                dk_sc[h] += _mm_tn(ds[h], q_ref[h])

        edge = (ki == qi) | (ki == 0)

        @pl.when((qi >= ki) & edge)
        def _():
            step(True)

        @pl.when((qi > ki) & (ki > 0))
        def _():
            step(False)

        @pl.when(qi == nb - 1)
        def _():
            dk_ref[...] = dk_sc[...] * ATTN_SCALE
            dv_ref[...] = dv_sc[...]

    hp = HEADS_PER_STEP
    q_idx = lambda h, ki, qi: (h, jnp.maximum(qi, ki), 0)
    k_idx = lambda h, ki, qi: (h, ki, 0)
    return pl.pallas_call(
        body, name="attn_bwd",
        out_shape=(jax.ShapeDtypeStruct((N_HEADS, n_rows, 128), F32), jax.ShapeDtypeStruct((N_HEADS, n_rows, 128), F32),
                   jax.ShapeDtypeStruct((N_HEADS, n_rows, V_DIM), F32)),
        grid=(N_HEADS // hp, nb, nb),
        in_specs=[pl.BlockSpec((hp, t, 128), q_idx), pl.BlockSpec((hp, t, 128), k_idx), pl.BlockSpec((hp, t, V_DIM), k_idx),
                  pl.BlockSpec((hp, t, V_DIM), q_idx), pl.BlockSpec((hp, t, 1), q_idx), pl.BlockSpec((hp, t, 1), q_idx)],
        out_specs=(pl.BlockSpec((hp, n_rows, 128), lambda h, ki, qi: (h, 0, 0)), pl.BlockSpec((hp, t, 128), k_idx),
                   pl.BlockSpec((hp, t, V_DIM), k_idx)),
        scratch_shapes=[pltpu.VMEM((hp, t, 128), F32), pltpu.VMEM((hp, t, V_DIM), F32)],
        compiler_params=_cp("parallel", "arbitrary", "arbitrary"),
    )(q, k, v, do, lse, delta)


def _mla_pre_bwd(proj, tab, gqa, gkva, wq, wkv, gqn, gkn, dq, dk, dv):
    n_rows = proj.shape[0]
    tm = _row_tile(n_rows)

    def body(c_ref, kr_ref, tab_ref, gqa_ref, gkva_ref, wq_ref, wkv_ref, gqn_ref, gkn_ref, dq_ref, dk_ref, dv_ref,
             dc_ref, dkr_ref, dwq_ref, dwkv_ref, dgqa_ref, dgkva_ref, dgqn_ref, dgkn_ref):
        first = pl.program_id(0) == 0
        c = c_ref[...]
        kr = kr_ref[:, 0:QK_ROPE]
        cos, sin = tab_ref[:, 0:QK_ROPE], tab_ref[:, QK_ROPE:2 * QK_ROPE]
        gqa, gkva, gqn, gkn = gqa_ref[...], gkva_ref[...], gqn_ref[...], gkn_ref[...]
        cqn, cq_h, cq_inv, ckvn, ckv_h, ckv_inv, qraw, kvraw = _mla_heads(c[:, :Q_RANK], c[:, Q_RANK:], kr, gqa, gkva,
                                                                          wq_ref, wkv_ref)
        dq_parts, dkv_parts = [], []
        dkr = jnp.zeros((tm, QK_ROPE), F32)
        dgqn = jnp.zeros((1, QK_DIM), F32)
        dgkn = jnp.zeros((1, QK_DIM), F32)
        for h in range(N_HEADS):
            _, xh, inv = _rms(qraw[:, QK_DIM * h:QK_DIM * (h + 1)], gqn)
            dqo = dq_ref[h]
            dqn = jnp.concatenate([dqo[:, :QK_NOPE], _rope_bwd(dqo[:, QK_NOPE:QK_DIM], cos, sin)], axis=1)
            dx, dg = _rms_bwd(dqn, xh, inv, gqn)
            dq_parts.append(dx)
            dgqn = dgqn + dg
            kin = jnp.concatenate([kvraw[:, 128 * h:128 * h + QK_NOPE], kr], axis=1)
            _, xh, inv = _rms(kin, gkn)
            dko = dk_ref[h]
            dkn = jnp.concatenate([dko[:, :QK_NOPE], _rope_bwd(dko[:, QK_NOPE:QK_DIM], cos, sin)], axis=1)
            dx, dg = _rms_bwd(dkn, xh, inv, gkn)
            dgkn = dgkn + dg
            dkv_parts += [dx[:, :QK_NOPE], dv_ref[h]]
            dkr = dkr + dx[:, QK_NOPE:]
        dqraw = jnp.concatenate(dq_parts, axis=1)
        dkvraw = jnp.concatenate(dkv_parts, axis=1)
        _acc(dwq_ref, _mm_tn(cqn, dqraw), first)
        _acc(dwkv_ref, _mm_tn(ckvn, dkvraw), first)
        dcq, dg1 = _rms_bwd(_mm_nt(dqraw, wq_ref[...]), cq_h, cq_inv, gqa)
        dckv, dg2 = _rms_bwd(_mm_nt(dkvraw, wkv_ref[...]), ckv_h, ckv_inv, gkva)
        _acc(dgqa_ref, dg1, first)
        _acc(dgkva_ref, dg2, first)
        _acc(dgqn_ref, dgqn, first)
        _acc(dgkn_ref, dgkn, first)
        dc_ref[...] = jnp.concatenate([dcq, dckv], axis=1)
        dkr_ref[...] = jnp.concatenate([dkr, jnp.zeros((tm, 128 - QK_ROPE), F32)], axis=1)

    full = lambda *s: pl.BlockSpec(s, lambda i: (0,) * len(s))
    head = lambda w: pl.BlockSpec((N_HEADS, tm, w), lambda i: (0, i, 0))
    return pl.pallas_call(
        body, name="mla_pre_bwd",
        out_shape=(jax.ShapeDtypeStruct((n_rows, 1024), F32), jax.ShapeDtypeStruct((n_rows, 128), F32),
                   jax.ShapeDtypeStruct((Q_RANK, Q_RANK), F32), jax.ShapeDtypeStruct((KV_RANK, 1024), F32),
                   jax.ShapeDtypeStruct((1, Q_RANK), F32), jax.ShapeDtypeStruct((1, KV_RANK), F32),
                   jax.ShapeDtypeStruct((1, QK_DIM), F32), jax.ShapeDtypeStruct((1, QK_DIM), F32)),
        grid=(n_rows // tm,),
        in_specs=[pl.BlockSpec((tm, 1024), lambda i: (i, COL_CQKV // 1024)),
                  pl.BlockSpec((tm, 128), lambda i: (i, COL_KR // 128)),
                  pl.BlockSpec((tm, 128), lambda i: (i, 0)),
                  full(1, Q_RANK), full(1, KV_RANK), full(Q_RANK, Q_RANK), full(KV_RANK, 1024), full(1, QK_DIM), full(1, QK_DIM),
                  head(128), head(128), head(V_DIM)],
        out_specs=(pl.BlockSpec((tm, 1024), lambda i: (i, 0)), pl.BlockSpec((tm, 128), lambda i: (i, 0)),
                   full(Q_RANK, Q_RANK), full(KV_RANK, 1024), full(1, Q_RANK), full(1, KV_RANK), full(1, QK_DIM), full(1, QK_DIM)),
        compiler_params=_cp("arbitrary"),
    )(proj, proj, tab, gqa, gkva, wq, wkv, gqn, gkn, dq, dk, dv)


def _pool_bwd_a(proj, dy_pool, wg, scale, w_up):
    n_rows = proj.shape[0]
    tm = _row_tile(n_rows)

    def body(uz_ref, halo_ref, dy_ref, wg_ref, sc_ref, wup_ref, dmz_ref, dwg_ref, dsc_ref, dwup_ref):
        first = pl.program_id(0) == 0
        uz = uz_ref[...]
        u, z = uz[:, :POOL_WIDTH], uz[:, POOL_WIDTH:]
        scale_v = sc_ref[...]
        mixed, yg, ys, sig = _pool_math(u, halo_ref[...], z, wg_ref, scale_v, pl.program_id(0) * tm)
        sp = z * sig
        dy = dy_ref[...]
        da = _mm_nt(dy, wup_ref[...])
        _acc(dwup_ref, _mm_tn(ys * sp, dy), first)
        dys = da * sp
        _acc(dsc_ref, jnp.sum(dys * yg, axis=0, keepdims=True), first)
        dyg = dys * scale_v
        for g in range(4):
            cols = slice(g * 128, (g + 1) * 128)
            dmz_ref[:, cols] = _mm_nt(dyg[:, cols], wg_ref[g])
            _acc(dwg_ref.at[g], _mm_tn(mixed[:, cols], dyg[:, cols]), first)
        dmz_ref[:, POOL_WIDTH:] = da * ys * _dsilu(z, sig)

    return pl.pallas_call(
        body, name="pool_bwd_a",
        out_shape=(jax.ShapeDtypeStruct((n_rows, 1024), F32), jax.ShapeDtypeStruct((4, 128, 128), F32),
                   jax.ShapeDtypeStruct((1, POOL_WIDTH), F32), jax.ShapeDtypeStruct((POOL_WIDTH, D_MODEL), F32)),
        grid=(n_rows // tm,),
        in_specs=[pl.BlockSpec((tm, 1024), lambda i: (i, 0)), pl.BlockSpec((HALO, POOL_WIDTH), _halo_above(tm)),
                  pl.BlockSpec((tm, D_MODEL), lambda i: (i, 0)),
                  pl.BlockSpec((4, 128, 128), lambda i: (0, 0, 0)), pl.BlockSpec((1, POOL_WIDTH), lambda i: (0, 0)),
                  pl.BlockSpec((POOL_WIDTH, D_MODEL), lambda i: (0, 0))],
        out_specs=(pl.BlockSpec((tm, 1024), lambda i: (i, 0)), pl.BlockSpec((4, 128, 128), lambda i: (0, 0, 0)),
                   pl.BlockSpec((1, POOL_WIDTH), lambda i: (0, 0)), pl.BlockSpec((POOL_WIDTH, D_MODEL), lambda i: (0, 0))),
        compiler_params=_cp("arbitrary"),
    )(proj, proj, dy_pool, wg, scale, w_up)


def _pool_bwd_b(dmz):
    n_rows = dmz.shape[0]
    tm = _row_tile(n_rows)
    n_tiles = n_rows // tm
    n_ext = tm + HALO

    def body(dmz_ref, halo_ref, o_ref):
        i = pl.program_id(0)
        v = dmz_ref[...]
        dm = v[:, :POOL_WIDTH]
        halo = jnp.where(i == n_tiles - 1, 0.0, halo_ref[...])
        ext = jnp.concatenate([dm, halo], axis=0)
        t1 = (i * tm + lax.broadcasted_iota(jnp.int32, (n_ext, 1), 0) - (PAD_FRONT - 1)).astype(F32)
        du = []
        for g, w in enumerate(POOL_WINDOWS):
            cols = slice(g * 128, (g + 1) * 128)
            a = ext[:, cols] / jnp.clip(t1, 1.0, float(w))
            k = 1
            while k < w:
                a = a + pltpu.roll(a, n_ext - k, 0)
                k *= 2
            du.append(a[:tm, :] - dm[:, cols])
        o_ref[...] = jnp.concatenate(du + [v[:, POOL_WIDTH:]], axis=1)

    last_halo = n_rows // HALO - 1
    return pl.pallas_call(
        body, name="pool_bwd_b", out_shape=jax.ShapeDtypeStruct((n_rows, 1024), F32), grid=(n_tiles,),
        in_specs=[pl.BlockSpec((tm, 1024), lambda i: (i, 0)),
                  pl.BlockSpec((HALO, POOL_WIDTH), lambda i: (jnp.minimum((i + 1) * (tm // HALO), last_halo), 0))],
        out_specs=pl.BlockSpec((tm, 1024), lambda i: (i, 0)),
        compiler_params=_cp("parallel"),
    )(dmz, dmz)


def _norm_proj_bwd(d_uz, d_cqkv, d_gz, d_kr, w_in, h_res, gain, dh_out):
    n_rows = h_res.shape[0]
    tm = _row_tile(n_rows)
    pieces = ((COL_UZ, 1024), (COL_CQKV, 1024), (COL_GATES, 2560), (COL_KR, 128))

    def body(a_ref, b_ref, c_ref, d_ref, w_ref, x_ref, g_ref, dho_ref, dhi_ref, dg_ref):
        i = pl.program_id(0)
        dh = None
        for ref, (c0, wd) in zip((a_ref, b_ref, c_ref, d_ref), pieces):
            part = _mm_nt(ref[...], w_ref[:, c0:c0 + wd])
            dh = part if dh is None else dh + part
        g = g_ref[...]
        _, xh, inv = _rms(x_ref[...], g)
        dx, dg = _rms_bwd(dh, xh, inv, g)
        _acc(dg_ref, dg, i == 0)
        row = i * tm + lax.broadcasted_iota(jnp.int32, (tm, 1), 0)
        dhi_ref[...] = jnp.where(row >= PAD_FRONT, dho_ref[...] + dx, 0.0)

    row = lambda w: pl.BlockSpec((tm, w), lambda i: (i, 0))
    return pl.pallas_call(
        body, name="norm_proj_bwd",
        out_shape=(jax.ShapeDtypeStruct((n_rows, D_MODEL), F32), jax.ShapeDtypeStruct((1, D_MODEL), F32)),
        grid=(n_rows // tm,),
        in_specs=[row(1024), row(1024), row(2560), row(128), pl.BlockSpec((D_MODEL, P_IN), lambda i: (0, 0)),
                  row(D_MODEL), pl.BlockSpec((1, D_MODEL), lambda i: (0, 0)), row(D_MODEL)],
        out_specs=(row(D_MODEL), pl.BlockSpec((1, D_MODEL), lambda i: (0, 0))),
        compiler_params=_cp("arbitrary"),
    )(d_uz, d_cqkv, d_gz, d_kr, w_in, h_res, gain, dh_out)


def _weight_grad(name, a, b):
    k_rows, m = a.shape
    n = b.shape[1]
    tk = 640 if k_rows % 640 == 0 else 128
    tn = 512 if n % 512 == 0 else 128

    def body(a_ref, b_ref, o_ref):
        _acc(o_ref, _mm_tn(a_ref[...], b_ref[...]), pl.program_id(1) == 0)

    return pl.pallas_call(
        body, name=name, out_shape=jax.ShapeDtypeStruct((m, n), F32), grid=(n // tn, k_rows // tk),
        in_specs=[pl.BlockSpec((tk, m), lambda j, k: (k, 0)), pl.BlockSpec((tk, tn), lambda j, k: (k, j))],
        out_specs=pl.BlockSpec((m, tn), lambda j, k: (0, j)),
        compiler_params=_cp("parallel", "arbitrary"),
    )(a, b)


def _adamw(name, w, m, v, gbuf):
    r, c = w.shape
    tr = r
    if r * c * 4 > (1 << 20):
        tr = 256 if r % 256 == 0 else 128
    assert r % tr == 0, (name, r, tr)

    def body(w_ref, m_ref, v_ref, g_ref, go_ref, d_ref, mo_ref, vo_ref):
        g = g_ref[0].astype(F32)
        for s in range(1, N_DEV):
            g = g + g_ref[s].astype(F32)
        go_ref[...] = g
        m_new = ADAM_B1 * m_ref[...] + (1.0 - ADAM_B1) * g
        v_new = ADAM_B2 * v_ref[...] + (1.0 - ADAM_B2) * (g * g)
        mo_ref[...] = m_new
        vo_ref[...] = v_new
        m_hat = m_new / (1.0 - ADAM_B1 ** ADAM_STEP)
        v_hat = v_new / (1.0 - ADAM_B2 ** ADAM_STEP)
        d_ref[...] = -ADAM_LR * (m_hat / (jnp.sqrt(v_hat) + ADAM_EPS) + ADAM_WD * w_ref[...])

    spec = pl.BlockSpec((tr, c), lambda i: (i, 0))
    sds = jax.ShapeDtypeStruct((r, c), F32)
    return pl.pallas_call(
        body, name=name, out_shape=(sds, sds, sds, sds), grid=(r // tr,),
        in_specs=[spec, spec, spec, pl.BlockSpec((N_DEV, tr, c), lambda i: (0, i, 0))],
        out_specs=(spec, spec, spec, spec),
        compiler_params=_cp("parallel"),
    )(w, m, v, gbuf)


SMALL = ("norm_gain", "pool_w_group", "pool_scale", "q_a_norm_gain", "kv_a_norm_gain", "q_norm_gain", "k_norm_gain")


def _pack_small(parts):
    rows = []
    for p in parts:
        flat = p.reshape(-1)
        pad = (-flat.shape[0]) % 1024
        if pad:
            flat = jnp.concatenate([flat, jnp.zeros((pad,), F32)])
        rows.append(flat.reshape(-1, 128))
    n_rows = sum(r.shape[0] for r in rows)
    tail = (-n_rows) % 256
    if tail:
        rows.append(jnp.zeros((tail, 128), F32))
    return jnp.concatenate(rows, axis=0)


def _unpack_small(packed, shapes):
    out, r0 = [], 0
    for shp in shapes:
        n = math.prod(shp)
        n_rows = (n + 1023) // 1024 * 8
        out.append(packed[r0:r0 + n_rows].reshape(-1)[:n].reshape(shp))
        r0 += n_rows
    return out


def kernel(x, positions, meta_tokens, norm_gain, w_in, pool_w_group, pool_scale, pool_w_up, q_a_norm_gain, kv_a_norm_gain, w_q_b, w_kv_b, q_norm_gain, k_norm_gain, mla_w_up, w_out, loss_target, m_meta_tokens, m_norm_gain, m_w_in, m_pool_w_group, m_pool_scale, m_pool_w_up, m_q_a_norm_gain, m_kv_a_norm_gain, m_w_q_b, m_w_kv_b, m_q_norm_gain, m_k_norm_gain, m_mla_w_up, m_w_out, v_meta_tokens, v_norm_gain, v_w_in, v_pool_w_group, v_pool_scale, v_pool_w_up, v_q_a_norm_gain, v_kv_a_norm_gain, v_w_q_b, v_w_kv_b, v_q_norm_gain, v_k_norm_gain, v_mla_w_up, v_w_out):
    seq = x.shape[1]
    x2, target = x[0], loss_target[0]

    big = dict(w_in=w_in, pool_w_up=pool_w_up, w_q_b=w_q_b, w_kv_b=w_kv_b, mla_w_up=mla_w_up, w_out=w_out)
    g_meta, g_in, g_pup, g_qb, g_kvb, g_mup, g_out = _exchange(
        "gather_weights", [meta_tokens] + [_cast_bf16("cast_" + n, w) for n, w in big.items()], [])
    w_in_p = _repack_w_in(g_in)
    w_pup = _repack_cols("repack_pool_up", g_pup, BF16)
    w_kvb = _repack_cols("repack_kv_b", g_kvb, BF16)
    w_mup = _repack_cols("repack_mla_up", g_mup, BF16)
    w_qb = _repack_rows("repack_q_b", g_qb)
    w_o = _repack_rows("repack_out", g_out)

    h = _embed(x2, g_meta)
    pos = jnp.concatenate([jnp.zeros((PAD_FRONT,), jnp.int32), jnp.arange(N_META, dtype=jnp.int32),
                           positions[0] + N_META]).astype(F32).reshape(-1, 1)
    half = QK_ROPE // 2
    inv_freq = ROPE_THETA ** (-jnp.arange(half, dtype=F32) / half)
    tab = _rope_table(pos, jnp.tile(inv_freq, 128 // half).reshape(1, 128))

    row = lambda a, l: a[l].reshape(1, -1)
    saved = []
    for l in range(DEPTH):
        hb, proj = _norm_proj(h, row(norm_gain, l), w_in_p[l])
        y_pool = _pool_fwd(proj, pool_w_group[l], row(pool_scale, l), w_pup[l])
        q, k, v = _mla_pre_fwd(proj, tab, row(q_a_norm_gain, l), row(kv_a_norm_gain, l), w_qb[l], w_kvb[l],
                               row(q_norm_gain, l), row(k_norm_gain, l))
        o, lse = _attn_fwd(q, k, v)
        y_mla, h_next = _mla_post_fwd(h, proj, o, y_pool, w_mup[l], w_o[l])
        saved.append((h, hb, proj, y_pool, q, k, v, o, lse, y_mla))
        h = h_next

    loss_part, dh = _loss_head(h, target)
    loss = lax.psum(loss_part[0, 0], MESH_AXES)

    grads = {n: [None] * DEPTH for n in ("norm_gain", "w_in", "pool_w_group", "pool_scale", "pool_w_up", "q_a_norm_gain",
                                         "kv_a_norm_gain", "w_q_b", "w_kv_b", "q_norm_gain", "k_norm_gain", "mla_w_up", "w_out")}
    for l in reversed(range(DEPTH)):
        h_l, hb, proj, y_pool, q, k, v, o, lse, y_mla = saved[l]
        d_gz, dy_pool, do, delta, grads["w_out"][l], grads["mla_w_up"][l] = _mla_post_bwd(
            dh, proj, o, y_pool, y_mla, w_mup[l], w_o[l])
        dq, dk, dv = _attn_bwd(q, k, v, do, lse, delta)
        (d_cqkv, d_kr, grads["w_q_b"][l], grads["w_kv_b"][l], grads["q_a_norm_gain"][l], grads["kv_a_norm_gain"][l],
         grads["q_norm_gain"][l], grads["k_norm_gain"][l]) = _mla_pre_bwd(
            proj, tab, row(q_a_norm_gain, l), row(kv_a_norm_gain, l), w_qb[l], w_kvb[l], row(q_norm_gain, l),
            row(k_norm_gain, l), dq, dk, dv)
        dmz, grads["pool_w_group"][l], grads["pool_scale"][l], grads["pool_w_up"][l] = _pool_bwd_a(
            proj, dy_pool, pool_w_group[l], row(pool_scale, l), w_pup[l])
        d_uz = _pool_bwd_b(dmz)
        grads["w_in"][l] = tuple(_weight_grad("dw_in", hb, d) for d in (d_uz, d_cqkv, d_gz, d_kr))
        dh, grads["norm_gain"][l] = _norm_proj_bwd(d_uz, d_cqkv, d_gz, d_kr, w_in_p[l], h_l, row(norm_gain, l), dh)

    grad_x = dh[HEAD_ROWS:][None]
    d_meta = dh[PAD_FRONT:HEAD_ROWS]

    stack = lambda n: jnp.stack(grads[n], axis=0)
    dw_in_parts = [jnp.stack([grads["w_in"][l][j] for l in range(DEPTH)], axis=0) for j in range(4)]
    scat = {
        "meta_tokens": _unpack_cols("unpack_meta", d_meta[None], 128, F32)[:, 0],
        "w_in": _unpack_w_in(dw_in_parts),
        "pool_w_up": _unpack_cols("unpack_pool_up", stack("pool_w_up"), 128, BF16),
        "w_q_b": _unpack_rows("unpack_q_b", stack("w_q_b"), Q_RANK // N_DEV),
        "w_kv_b": _unpack_cols("unpack_kv_b", stack("w_kv_b"), 128, BF16),
        "mla_w_up": _unpack_cols("unpack_mla_up", stack("mla_w_up"), 128, BF16),
        "w_out": _unpack_rows("unpack_out", stack("w_out"), D_MODEL // N_DEV),
    }
    small_shapes = [a.shape for a in (norm_gain, pool_w_group, pool_scale, q_a_norm_gain, kv_a_norm_gain, q_norm_gain, k_norm_gain)]
    small_grad = _pack_small([stack(n).reshape(s) for n, s in zip(SMALL, small_shapes)])
    sharded = list(scat)
    received = _exchange("exchange_grads", [small_grad], [scat[n] for n in sharded])
    small_buf, shard_bufs = received[0], dict(zip(sharded, received[1:]))

    given = dict(meta_tokens=(meta_tokens, m_meta_tokens, v_meta_tokens), w_in=(w_in, m_w_in, v_w_in),
                 pool_w_up=(pool_w_up, m_pool_w_up, v_pool_w_up), w_q_b=(w_q_b, m_w_q_b, v_w_q_b),
                 w_kv_b=(w_kv_b, m_w_kv_b, v_w_kv_b), mla_w_up=(mla_w_up, m_mla_w_up, v_mla_w_up),
                 w_out=(w_out, m_w_out, v_w_out))
    result = {}
    for n in sharded:
        w, m, v = given[n]
        cols = w.shape[-1]
        res = _adamw("adamw_" + n, w.reshape(-1, cols), m.reshape(-1, cols), v.reshape(-1, cols),
                     shard_bufs[n].reshape(N_DEV, -1, cols))
        result[n] = [r.reshape(w.shape) for r in res]
    small_w = (norm_gain, pool_w_group, pool_scale, q_a_norm_gain, kv_a_norm_gain, q_norm_gain, k_norm_gain)
    small_m = (m_norm_gain, m_pool_w_group, m_pool_scale, m_q_a_norm_gain, m_kv_a_norm_gain, m_q_norm_gain, m_k_norm_gain)
    small_v = (v_norm_gain, v_pool_w_group, v_pool_scale, v_q_a_norm_gain, v_kv_a_norm_gain, v_q_norm_gain, v_k_norm_gain)
    res = _adamw("adamw_small", _pack_small(small_w), _pack_small(small_m), _pack_small(small_v), small_buf)
    small_res = [_unpack_small(r, small_shapes) for r in res]
    for j, n in enumerate(SMALL):
        result[n] = [small_res[kind][j] for kind in range(4)]

    order = ("meta_tokens", "norm_gain", "w_in", "pool_w_group", "pool_scale", "pool_w_up", "q_a_norm_gain",
             "kv_a_norm_gain", "w_q_b", "w_kv_b", "q_norm_gain", "k_norm_gain", "mla_w_up", "w_out")
    outs = [loss, grad_x]
    for kind in range(4):
        outs += [result[n][kind] for n in order]
    return tuple(outs)
```

```python
import math

import jax
import jax.numpy as jnp
from jax import lax
from jax.experimental import pallas as pl
from jax.experimental.pallas import tpu as pltpu

F32 = jnp.float32
BF16 = jnp.bfloat16

D_MODEL = 1024
DEPTH = 4
N_META = 16
PAD_FRONT = 112
HEAD_ROWS = PAD_FRONT + N_META
POOL_WIDTH = 512
POOL_WINDOWS = (2, 4, 8, 16)
POOL_GROUP_DIM = 128
HALO = 16
N_HEADS = 8
HEADS_PER_STEP = 2
QK_NOPE = 64
QK_ROPE = 32
QK_DIM = 96
V_DIM = 64
MLA_WIDTH = 512
Q_RANK = 768
KV_RANK = 256
ROPE_THETA = 10000.0
NORM_EPS = 1e-6
MASK_VALUE = -1e30
D_IN = 4640
N_DEV = 8
IN_SHARD = D_IN // N_DEV

P_IN = 4736
IN_SEGMENTS = ((0, 2048, 0), (2048, 2080, 4608), (2080, 2592, 4096), (2592, 3616, 2048), (3616, 4640, 3072))
COL_UZ, COL_CQKV, COL_GATES, COL_ZMLA, COL_KR = 0, 1024, 2048, 4096, 4608

ADAM_LR = 0.001
ADAM_B1 = 0.9
ADAM_B2 = 0.999
ADAM_EPS = 1e-08
ADAM_WD = 0.01
ADAM_STEP = 10

VMEM_LIMIT = 56 * 1024 * 1024
MESH_AXES = ("x", "y", "c")


def _cp(*sem):
    return pltpu.CompilerParams(dimension_semantics=sem, vmem_limit_bytes=VMEM_LIMIT)


def _row_tile(n_rows):
    return 320 if n_rows % 320 == 0 else 128


def _attn_tile(n_rows):
    return 640 if (n_rows % 640 == 0 and n_rows > 640) else 128


def _mm(a, b):
    return jnp.dot(a.astype(BF16), b.astype(BF16), preferred_element_type=F32)


def _mm_nt(a, b):
    return lax.dot_general(a.astype(BF16), b.astype(BF16), (((1,), (1,)), ((), ())), preferred_element_type=F32)


def _mm_tn(a, b):
    return lax.dot_general(a.astype(BF16), b.astype(BF16), (((0,), (0,)), ((), ())), preferred_element_type=F32)


def _sigmoid(z):
    return 1.0 / (1.0 + jnp.exp(-z))


def _rms(x, g):
    inv = lax.rsqrt(jnp.mean(x * x, axis=-1, keepdims=True) + NORM_EPS)
    xh = x * inv
    return xh * g, xh, inv


def _rms_bwd(dy, xh, inv, g):
    dg = jnp.sum(dy * xh, axis=0, keepdims=True)
    dxh = dy * g
    dx = inv * (dxh - xh * jnp.mean(dxh * xh, axis=-1, keepdims=True))
    return dx, dg


def _rope(r, cos, sin):
    rot = jnp.concatenate([-r[:, 16:32], r[:, 0:16]], axis=1)
    return r * cos + rot * sin


def _rope_bwd(dy, cos, sin):
    ys = dy * sin
    return dy * cos + jnp.concatenate([ys[:, 16:32], -ys[:, 0:16]], axis=1)


class _Exchange:
    def __init__(self, gather_list, scatter_list):
        self.arrays = list(gather_list) + list(scatter_list)
        self.n_g, self.n = len(gather_list), len(self.arrays)
        self.out_shape = [jax.ShapeDtypeStruct((N_DEV,) + a.shape, a.dtype) for a in gather_list]
        self.out_shape += [jax.ShapeDtypeStruct(a.shape, a.dtype) for a in scatter_list]
        self.specs = [pl.BlockSpec(memory_space=pl.ANY)] * self.n
        self.sems = [pltpu.SemaphoreType.DMA((7 * self.n,)), pltpu.SemaphoreType.DMA((7 * self.n,)),
                     pltpu.SemaphoreType.DMA((self.n,))]

    def copies(self, ins, outs, sems):
        send_sems, recv_sems, local_sems = sems
        x, y, c = lax.axis_index("x"), lax.axis_index("y"), lax.axis_index("c")
        me = 4 * x + 2 * y + c

        def src(a, slot):
            return ins[a] if a < self.n_g else ins[a].at[slot]

        out = [pltpu.make_async_copy(src(a, me), outs[a].at[me], local_sems.at[a]) for a in range(self.n)]
        for k in range(1, N_DEV):
            px = 1 - x if k & 4 else x
            py = 1 - y if k & 2 else y
            pc = 1 - c if k & 1 else c
            peer = 4 * px + 2 * py + pc
            for a in range(self.n):
                out.append(pltpu.make_async_remote_copy(
                    src_ref=src(a, peer), dst_ref=outs[a].at[me],
                    send_sem=send_sems.at[a * 7 + k - 1], recv_sem=recv_sems.at[a * 7 + k - 1],
                    device_id=(px, py, pc), device_id_type=pl.DeviceIdType.MESH))
        return out


def _exchange(name, gather_list, scatter_list):
    ex = _Exchange(gather_list, scatter_list)
    n = ex.n

    def body(*refs):
        copies = ex.copies(refs[:n], refs[n:2 * n], refs[2 * n:])
        for cp in copies:
            cp.start()
        for cp in copies:
            cp.wait()

    return pl.pallas_call(
        body, name=name, out_shape=ex.out_shape, in_specs=ex.specs, out_specs=ex.specs, scratch_shapes=ex.sems,
        compiler_params=pltpu.CompilerParams(has_side_effects=True),
    )(*ex.arrays)


def _call_carrying(ex, body, name, out_shape, grid, in_specs, out_specs, scratch_shapes, semantics, args):
    if ex is None:
        return pl.pallas_call(body, name=name, out_shape=out_shape, grid=grid, in_specs=in_specs, out_specs=out_specs,
                              scratch_shapes=scratch_shapes, compiler_params=_cp(*semantics))(*args)
    n, n_in, n_out, n_scr = ex.n, len(in_specs), len(out_shape), len(scratch_shapes)

    def carrying(*refs):
        ins, refs = refs[:n_in], refs[n_in:]
        c_in, refs = refs[:n], refs[n:]
        outs, refs = refs[:n_out], refs[n_out:]
        c_out, refs = refs[:n], refs[n:]
        scratch, sems = refs[:n_scr], refs[n_scr:]
        ids = [pl.program_id(a) for a in range(len(grid))]
        first, last = ids[0] == 0, ids[0] == grid[0] - 1
        for a in range(1, len(grid)):
            first, last = first & (ids[a] == 0), last & (ids[a] == grid[a] - 1)
        copies = ex.copies(c_in, c_out, sems)

        @pl.when(first)
        def _():
            for cp in copies:
                cp.start()

        body(*ins, *outs, *scratch)

        @pl.when(last)
        def _():
            for cp in copies:
                cp.wait()

    return pl.pallas_call(
        carrying, name=name + "_x", out_shape=list(out_shape) + ex.out_shape, grid=grid,
        in_specs=list(in_specs) + ex.specs, out_specs=list(out_specs) + ex.specs,
        scratch_shapes=list(scratch_shapes) + ex.sems, compiler_params=_cp(*["arbitrary"] * len(grid)),
    )(*args, *ex.arrays)


def _in_pieces():
    runs = []
    for lo, hi, dst in sorted(IN_SEGMENTS, key=lambda s: s[2]):
        col = lo
        while col < hi:
            d = col // IN_SHARD
            end = min(hi, (d + 1) * IN_SHARD)
            runs.append((d, col - d * IN_SHARD, end - d * IN_SHARD))
            col = end
    return runs


def _cast_bf16(name, w):
    shape = w.shape
    w2 = w.reshape(-1, shape[-1])
    r, c = w2.shape
    tr = 512 if r % 512 == 0 else r

    def body(x_ref, o_ref):
        o_ref[...] = x_ref[...].astype(BF16)

    spec = pl.BlockSpec((tr, c), lambda i: (i, 0))
    out = pl.pallas_call(body, name=name, out_shape=jax.ShapeDtypeStruct((r, c), BF16), grid=(r // tr,),
                         in_specs=[spec], out_specs=spec, compiler_params=_cp("parallel"))(w2)
    return out.reshape(shape)


def _repack_w_in(gathered):
    tr = 256
    runs = _in_pieces()

    def body(x_ref, o_ref):
        parts = [x_ref[d, :, a:b].astype(F32) for d, a, b in runs]
        parts.append(jnp.zeros((tr, P_IN - D_IN), F32))
        o_ref[...] = jnp.concatenate(parts, axis=1).astype(BF16)

    return pl.pallas_call(
        body, name="repack_w_in", out_shape=jax.ShapeDtypeStruct((D_MODEL, P_IN), BF16), grid=(D_MODEL // tr,),
        in_specs=[pl.BlockSpec((N_DEV, tr, IN_SHARD), lambda i: (0, i, 0))],
        out_specs=pl.BlockSpec((tr, P_IN), lambda i: (i, 0)),
        compiler_params=_cp("parallel"),
    )(gathered)


def _unpack_w_in(parts):
    tr = 256
    runs = _in_pieces()
    widths = [p.shape[-1] for p in parts]

    def body(a_ref, b_ref, c_ref, d_ref, o_ref):
        full = jnp.concatenate([a_ref[...], b_ref[...], c_ref[...], d_ref[...]], axis=1)
        col = 0
        per_dev = [[] for _ in range(N_DEV)]
        for d, a, b in runs:
            per_dev[d].append((a, full[:, col:col + (b - a)]))
            col += b - a
        for d in range(N_DEV):
            pieces = [v for _, v in sorted(per_dev[d], key=lambda av: av[0])]
            o_ref[d] = (jnp.concatenate(pieces, axis=1) if len(pieces) > 1 else pieces[0]).astype(BF16)

    return pl.pallas_call(
        body, name="unpack_w_in", out_shape=jax.ShapeDtypeStruct((N_DEV, D_MODEL, IN_SHARD), BF16), grid=(D_MODEL // tr,),
        in_specs=[pl.BlockSpec((tr, w), lambda i: (i, 0)) for w in widths],
        out_specs=pl.BlockSpec((N_DEV, tr, IN_SHARD), lambda i: (0, i, 0)),
        compiler_params=_cp("parallel"),
    )(*parts)


def _repack_cols(name, gathered, dtype):
    _, depth, k, n = gathered.shape

    def body(x_ref, o_ref):
        o_ref[0] = jnp.concatenate([x_ref[d, 0] for d in range(N_DEV)], axis=1).astype(dtype)

    return pl.pallas_call(
        body, name=name, out_shape=jax.ShapeDtypeStruct((depth, k, N_DEV * n), dtype), grid=(depth,),
        in_specs=[pl.BlockSpec((N_DEV, 1, k, n), lambda l: (0, l, 0, 0))],
        out_specs=pl.BlockSpec((1, k, N_DEV * n), lambda l: (l, 0, 0)),
        compiler_params=_cp("parallel"),
    )(gathered)


def _unpack_cols(name, full, n, dtype):
    depth, k, _ = full.shape

    def body(x_ref, o_ref):
        for d in range(N_DEV):
            o_ref[d, 0] = x_ref[0, :, d * n:(d + 1) * n].astype(dtype)

    return pl.pallas_call(
        body, name=name, out_shape=jax.ShapeDtypeStruct((N_DEV, depth, k, n), dtype), grid=(depth,),
        in_specs=[pl.BlockSpec((1, k, N_DEV * n), lambda l: (l, 0, 0))],
        out_specs=pl.BlockSpec((N_DEV, 1, k, n), lambda l: (0, l, 0, 0)),
        compiler_params=_cp("parallel"),
    )(full)


def _repack_rows(name, gathered):
    _, depth, r, n = gathered.shape

    def body(x_ref, o_ref):
        for d in range(N_DEV):
            o_ref[0, d * r:(d + 1) * r, :] = x_ref[d, 0].astype(BF16)

    return pl.pallas_call(
        body, name=name, out_shape=jax.ShapeDtypeStruct((depth, N_DEV * r, n), BF16), grid=(depth,),
        in_specs=[pl.BlockSpec((N_DEV, 1, r, n), lambda l: (0, l, 0, 0))],
        out_specs=pl.BlockSpec((1, N_DEV * r, n), lambda l: (l, 0, 0)),
        compiler_params=_cp("parallel"),
    )(gathered)


def _unpack_rows(name, full, r):
    depth, _, n = full.shape

    def body(x_ref, o_ref):
        for d in range(N_DEV):
            o_ref[d, 0] = x_ref[0, d * r:(d + 1) * r, :].astype(BF16)

    return pl.pallas_call(
        body, name=name, out_shape=jax.ShapeDtypeStruct((N_DEV, depth, r, n), BF16), grid=(depth,),
        in_specs=[pl.BlockSpec((1, N_DEV * r, n), lambda l: (l, 0, 0))],
        out_specs=pl.BlockSpec((N_DEV, 1, r, n), lambda l: (0, l, 0, 0)),
        compiler_params=_cp("parallel"),
    )(full)


def _embed(x, meta_g):
    seq = x.shape[0]
    n_blk = seq // HEAD_ROWS + 1

    def body(x_ref, m_ref, o_ref):
        i = pl.program_id(0)

        @pl.when(i == 0)
        def _():
            meta = jnp.concatenate([m_ref[d] for d in range(N_DEV)], axis=1)
            o_ref[...] = jnp.concatenate([jnp.zeros((PAD_FRONT, D_MODEL), F32), meta], axis=0)

        @pl.when(i > 0)
        def _():
            o_ref[...] = x_ref[...]

    return pl.pallas_call(
        body, name="embed", out_shape=jax.ShapeDtypeStruct((seq + HEAD_ROWS, D_MODEL), F32), grid=(n_blk,),
        in_specs=[pl.BlockSpec((HEAD_ROWS, D_MODEL), lambda i: (jnp.maximum(i - 1, 0), 0)),
                  pl.BlockSpec((N_DEV, N_META, 128), lambda i: (0, 0, 0))],
        out_specs=pl.BlockSpec((HEAD_ROWS, D_MODEL), lambda i: (i, 0)),
        compiler_params=_cp("parallel"),
    )(x, meta_g)


def _rope_table(pos_col, freq_row):
    n_rows = pos_col.shape[0]
    tr = 128

    def body(p_ref, f_ref, o_ref):
        ang = p_ref[...] * f_ref[...]
        lane = lax.broadcasted_iota(jnp.int32, ang.shape, 1)
        o_ref[...] = jnp.where(lane < QK_ROPE, jnp.cos(ang), jnp.sin(ang))

    return pl.pallas_call(
        body, name="rope_table", out_shape=jax.ShapeDtypeStruct((n_rows, 128), F32), grid=(n_rows // tr,),
        in_specs=[pl.BlockSpec((tr, 1), lambda i: (i, 0)), pl.BlockSpec((1, 128), lambda i: (0, 0))],
        out_specs=pl.BlockSpec((tr, 128), lambda i: (i, 0)),
        compiler_params=_cp("parallel"),
    )(pos_col, freq_row)


def _loss_head(h, target):
    n_rows = h.shape[0]
    n_blk = n_rows // HEAD_ROWS

    def body(h_ref, t_ref, loss_ref, dh_ref):
        i = pl.program_id(0)

        @pl.when(i == 0)
        def _():
            loss_ref[...] = jnp.zeros(loss_ref.shape, F32)
            dh_ref[...] = jnp.zeros(dh_ref.shape, F32)

        @pl.when(i > 0)
        def _():
            err = h_ref[...] - t_ref[...]
            dh_ref[...] = err * (1.0 / D_MODEL)
            loss_ref[...] += 0.5 * jnp.sum(jnp.mean(err * err, axis=-1, keepdims=True))

    return pl.pallas_call(
        body, name="loss_head",
        out_shape=(jax.ShapeDtypeStruct((8, 128), F32), jax.ShapeDtypeStruct((n_rows, D_MODEL), F32)),
        grid=(n_blk,),
        in_specs=[pl.BlockSpec((HEAD_ROWS, D_MODEL), lambda i: (i, 0)),
                  pl.BlockSpec((HEAD_ROWS, D_MODEL), lambda i: (jnp.maximum(i - 1, 0), 0))],
        out_specs=(pl.BlockSpec((8, 128), lambda i: (0, 0)), pl.BlockSpec((HEAD_ROWS, D_MODEL), lambda i: (i, 0))),
        compiler_params=_cp("arbitrary"),
    )(h, target)


PROJ_CHUNKS = (0, 1024, 2048, 3072, 4096, P_IN)


def _norm_proj(h_res, gain, w_in):
    n_rows = h_res.shape[0]
    tm = _row_tile(n_rows)

    def body(x_ref, g_ref, w_ref, hb_ref, p_ref):
        h, _, _ = _rms(x_ref[...], g_ref[...])
        hb = h.astype(BF16)
        hb_ref[...] = hb
        for c0, c1 in zip(PROJ_CHUNKS[:-1], PROJ_CHUNKS[1:]):
            p_ref[:, c0:c1] = jnp.dot(hb, w_ref[:, c0:c1], preferred_element_type=F32)

    return pl.pallas_call(
        body, name="norm_proj",
        out_shape=(jax.ShapeDtypeStruct((n_rows, D_MODEL), BF16), jax.ShapeDtypeStruct((n_rows, P_IN), F32)),
        grid=(n_rows // tm,),
        in_specs=[pl.BlockSpec((tm, D_MODEL), lambda i: (i, 0)), pl.BlockSpec((1, D_MODEL), lambda i: (0, 0)),
                  pl.BlockSpec((D_MODEL, P_IN), lambda i: (0, 0))],
        out_specs=(pl.BlockSpec((tm, D_MODEL), lambda i: (i, 0)), pl.BlockSpec((tm, P_IN), lambda i: (i, 0))),
        compiler_params=_cp("parallel"),
    )(h_res, gain, w_in)


def _pool_math(u, halo, z, wg_ref, scale, row0):
    tm = u.shape[0]
    ext = jnp.concatenate([halo, u], axis=0)
    t1 = (row0 + lax.broadcasted_iota(jnp.int32, (tm, 1), 0) - (PAD_FRONT - 1)).astype(F32)
    mixed, yg = [], []
    for g, w in enumerate(POOL_WINDOWS):
        a = ext[:, g * 128:(g + 1) * 128]
        k = 1
        while k < w:
            a = a + pltpu.roll(a, k, 0)
            k *= 2
        cnt = jnp.clip(t1, 1.0, float(w))
        mg = a[HALO:, :] / cnt - u[:, g * 128:(g + 1) * 128]
        mixed.append(mg)
        yg.append(_mm(mg, wg_ref[g]))
    mixed = jnp.concatenate(mixed, axis=1)
    yg = jnp.concatenate(yg, axis=1)
    ys = yg * scale
    sig = _sigmoid(z)
    return mixed, yg, ys, sig


def _halo_above(tm):
    return lambda i: (jnp.maximum(i * (tm // HALO) - 1, 0), 0)


def _pool_fwd(proj, wg, scale, w_up):
    n_rows = proj.shape[0]
    tm = _row_tile(n_rows)

    def body(uz_ref, halo_ref, wg_ref, sc_ref, wup_ref, y_ref):
        uz = uz_ref[...]
        u, z = uz[:, :POOL_WIDTH], uz[:, POOL_WIDTH:]
        _, _, ys, sig = _pool_math(u, halo_ref[...], z, wg_ref, sc_ref[...], pl.program_id(0) * tm)
        y_ref[...] = _mm(ys * (z * sig), wup_ref[...])

    return pl.pallas_call(
        body, name="pool_fwd", out_shape=jax.ShapeDtypeStruct((n_rows, D_MODEL), F32), grid=(n_rows // tm,),
        in_specs=[pl.BlockSpec((tm, 1024), lambda i: (i, 0)), pl.BlockSpec((HALO, POOL_WIDTH), _halo_above(tm)),
                  pl.BlockSpec((4, 128, 128), lambda i: (0, 0, 0)), pl.BlockSpec((1, POOL_WIDTH), lambda i: (0, 0)),
                  pl.BlockSpec((POOL_WIDTH, D_MODEL), lambda i: (0, 0))],
        out_specs=pl.BlockSpec((tm, D_MODEL), lambda i: (i, 0)),
        compiler_params=_cp("parallel"),
    )(proj, proj, wg, scale, w_up)


def _mla_heads(cq, ckv, kr, gqa, gkva, wq_ref, wkv_ref):
    cqn, cq_h, cq_inv = _rms(cq, gqa)
    ckvn, ckv_h, ckv_inv = _rms(ckv, gkva)
    qraw = _mm(cqn, wq_ref[...])
    kvraw = _mm(ckvn, wkv_ref[...])
    return cqn, cq_h, cq_inv, ckvn, ckv_h, ckv_inv, qraw, kvraw


def _mla_pre_fwd(proj, tab, gqa, gkva, wq, wkv, gqn, gkn):
    n_rows = proj.shape[0]
    tm = _row_tile(n_rows)

    def body(c_ref, kr_ref, tab_ref, gqa_ref, gkva_ref, wq_ref, wkv_ref, gqn_ref, gkn_ref, q_ref, k_ref, v_ref):
        c = c_ref[...]
        kr = kr_ref[:, 0:QK_ROPE]
        cos, sin = tab_ref[:, 0:QK_ROPE], tab_ref[:, QK_ROPE:2 * QK_ROPE]
        _, _, _, _, _, _, qraw, kvraw = _mla_heads(c[:, :Q_RANK], c[:, Q_RANK:], kr, gqa_ref[...], gkva_ref[...],
                                                  wq_ref, wkv_ref)
        zpad = jnp.zeros((tm, 128 - QK_DIM), F32)
        for h in range(N_HEADS):
            qn, _, _ = _rms(qraw[:, QK_DIM * h:QK_DIM * (h + 1)], gqn_ref[...])
            q_ref[h] = jnp.concatenate([qn[:, :QK_NOPE], _rope(qn[:, QK_NOPE:], cos, sin), zpad], axis=1).astype(BF16)
            kin = jnp.concatenate([kvraw[:, 128 * h:128 * h + QK_NOPE], kr], axis=1)
            kn, _, _ = _rms(kin, gkn_ref[...])
            k_ref[h] = jnp.concatenate([kn[:, :QK_NOPE], _rope(kn[:, QK_NOPE:], cos, sin), zpad], axis=1).astype(BF16)
            v_ref[h] = kvraw[:, 128 * h + QK_NOPE:128 * (h + 1)].astype(BF16)

    full = lambda *s: pl.BlockSpec(s, lambda i: (0,) * len(s))
    return pl.pallas_call(
        body, name="mla_pre_fwd",
        out_shape=(jax.ShapeDtypeStruct((N_HEADS, n_rows, 128), BF16), jax.ShapeDtypeStruct((N_HEADS, n_rows, 128), BF16),
                   jax.ShapeDtypeStruct((N_HEADS, n_rows, V_DIM), BF16)),
        grid=(n_rows // tm,),
        in_specs=[pl.BlockSpec((tm, 1024), lambda i: (i, COL_CQKV // 1024)),
                  pl.BlockSpec((tm, 128), lambda i: (i, COL_KR // 128)),
                  pl.BlockSpec((tm, 128), lambda i: (i, 0)),
                  full(1, Q_RANK), full(1, KV_RANK), full(Q_RANK, Q_RANK), full(KV_RANK, 1024), full(1, QK_DIM), full(1, QK_DIM)],
        out_specs=(pl.BlockSpec((N_HEADS, tm, 128), lambda i: (0, i, 0)), pl.BlockSpec((N_HEADS, tm, 128), lambda i: (0, i, 0)),
                   pl.BlockSpec((N_HEADS, tm, V_DIM), lambda i: (0, i, 0))),
        compiler_params=_cp("parallel"),
    )(proj, proj, tab, gqa, gkva, wq, wkv, gqn, gkn)


def _attn_mask(row0, col0, n_r, n_c):
    row = row0 + lax.broadcasted_iota(jnp.int32, (n_r, n_c), 0)
    col = col0 + lax.broadcasted_iota(jnp.int32, (n_r, n_c), 1)
    return (col <= row) & (col >= PAD_FRONT)


ATTN_SCALE = 1.0 / math.sqrt(QK_DIM)
EXP2_SCALE = ATTN_SCALE * math.log2(math.e)
LOG2_E = math.log2(math.e)


def _row_chunks(t, size):
    return [(r, min(size, t - r)) for r in range(0, t, size)]


def _attn_fwd(q, k, v, ex=None):
    n_rows = q.shape[1]
    t = _attn_tile(n_rows)
    nb = n_rows // t

    def body(q_ref, k_ref, v_ref, o_ref, lse_ref, m_sc, l_sc, acc_sc):
        qi, ki = pl.program_id(1), pl.program_id(2)

        @pl.when(ki == 0)
        def _():
            m_sc[...] = jnp.full(m_sc.shape, -jnp.inf, F32)
            l_sc[...] = jnp.zeros(l_sc.shape, F32)
            acc_sc[...] = jnp.zeros(acc_sc.shape, F32)

        def update(masked):
            heads = range(HEADS_PER_STEP)
            s = [_mm_nt(q_ref[h], k_ref[h]) for h in heads]
            if masked:
                mask = _attn_mask(qi * t, ki * t, t, t)
                s = [jnp.where(mask, sh, MASK_VALUE) for sh in s]
            p = []
            for h in heads:
                m_prev = m_sc[h]
                m_new = jnp.maximum(m_prev, jnp.max(s[h], axis=1, keepdims=True))
                alpha = jnp.exp2((m_prev - m_new) * EXP2_SCALE)
                ph = jnp.exp2((s[h] - m_new) * EXP2_SCALE)
                l_sc[h] = alpha * l_sc[h] + jnp.sum(ph, axis=1, keepdims=True)
                acc_sc[h] = alpha * acc_sc[h]
                m_sc[h] = m_new
                p.append(ph)
            for h in heads:
                acc_sc[h] += _mm(p[h], v_ref[h])

        edge = (ki == qi) | (ki == 0)

        @pl.when((ki <= qi) & edge)
        def _():
            update(True)

        @pl.when((ki < qi) & (ki > 0))
        def _():
            update(False)

        @pl.when(ki == nb - 1)
        def _():
            o_ref[...] = acc_sc[...] / l_sc[...]
            lse_ref[...] = m_sc[...] * ATTN_SCALE + jnp.log(l_sc[...])

    hp = HEADS_PER_STEP
    kv_idx = lambda h, qi, ki: (h, jnp.minimum(ki, qi), 0)
    return _call_carrying(
        ex, body, "attn_fwd",
        out_shape=(jax.ShapeDtypeStruct((N_HEADS, n_rows, V_DIM), F32), jax.ShapeDtypeStruct((N_HEADS, n_rows, 1), F32)),
        grid=(N_HEADS // hp, nb, nb),
        in_specs=[pl.BlockSpec((hp, t, 128), lambda h, qi, ki: (h, qi, 0)), pl.BlockSpec((hp, t, 128), kv_idx),
                  pl.BlockSpec((hp, t, V_DIM), kv_idx)],
        out_specs=(pl.BlockSpec((hp, t, V_DIM), lambda h, qi, ki: (h, qi, 0)), pl.BlockSpec((hp, t, 1), lambda h, qi, ki: (h, qi, 0))),
        scratch_shapes=[pltpu.VMEM((hp, t, 1), F32), pltpu.VMEM((hp, t, 1), F32), pltpu.VMEM((hp, t, V_DIM), F32)],
        semantics=("parallel", "parallel", "arbitrary"), args=(q, k, v))


def _merge_math(o_ref, z, gates, y_pool, y_mla_fn):
    o_cat = jnp.concatenate([o_ref[h] for h in range(N_HEADS)], axis=1)
    sig_z = _sigmoid(z)
    a_mla = o_cat * (z * sig_z)
    y_mla = y_mla_fn(a_mla)
    sgp, sgm = _sigmoid(gates[:, :D_MODEL]), _sigmoid(gates[:, D_MODEL:])
    merged = sgp * y_pool + sgm * y_mla
    return o_cat, sig_z, a_mla, y_mla, sgp, sgm, merged


def _mla_post_fwd(h_res, proj, o, y_pool, w_mla_up, w_out):
    n_rows = h_res.shape[0]
    tm = _row_tile(n_rows)

    def body(h_ref, g_ref, z_ref, o_ref, yp_ref, wup_ref, wout_ref, ymla_ref, hn_ref):
        _, _, _, y_mla, _, _, merged = _merge_math(o_ref, z_ref[...], g_ref[...], yp_ref[...],
                                                   lambda a: _mm(a, wup_ref[...]))
        ymla_ref[...] = y_mla
        hn_ref[...] = h_ref[...] + _mm(merged, wout_ref[...])

    row = lambda w, c: pl.BlockSpec((tm, w), lambda i: (i, c))
    return pl.pallas_call(
        body, name="mla_post_fwd",
        out_shape=(jax.ShapeDtypeStruct((n_rows, D_MODEL), F32), jax.ShapeDtypeStruct((n_rows, D_MODEL), F32)),
        grid=(n_rows // tm,),
        in_specs=[row(D_MODEL, 0), row(2048, COL_GATES // 2048), row(MLA_WIDTH, COL_ZMLA // MLA_WIDTH),
                  pl.BlockSpec((N_HEADS, tm, V_DIM), lambda i: (0, i, 0)), row(D_MODEL, 0),
                  pl.BlockSpec((MLA_WIDTH, D_MODEL), lambda i: (0, 0)), pl.BlockSpec((D_MODEL, D_MODEL), lambda i: (0, 0))],
        out_specs=(row(D_MODEL, 0), row(D_MODEL, 0)),
        compiler_params=_cp("parallel"),
    )(h_res, proj, proj, o, y_pool, w_mla_up, w_out)


def _dsilu(z, sig):
    return sig * (1.0 + z * (1.0 - sig))


def _acc(ref, val, first):
    @pl.when(first)
    def _():
        ref[...] = val

    @pl.when(jnp.logical_not(first))
    def _():
        ref[...] += val


def _mla_post_bwd(dh, proj, o, y_pool, y_mla, w_mla_up, w_out):
    n_rows = dh.shape[0]
    tm = _row_tile(n_rows)

    def body(dh_ref, g_ref, z_ref, o_ref, yp_ref, ym_ref, wup_ref, wout_ref,
             dgz_ref, dyp_ref, do_ref, delta_ref, dwout_ref, dwup_ref):
        first = pl.program_id(0) == 0
        z, y_pool, y_mla = z_ref[...], yp_ref[...], ym_ref[...]
        o_cat, sig_z, a_mla, _, sgp, sgm, merged = _merge_math(o_ref, z, g_ref[...], y_pool, lambda a: y_mla)
        dhv = dh_ref[...]
        dmerged = _mm_nt(dhv, wout_ref[...])
        _acc(dwout_ref, _mm_tn(merged, dhv), first)
        dyp_ref[...] = dmerged * sgp
        dy_mla = dmerged * sgm
        da = _mm_nt(dy_mla, wup_ref[...])
        _acc(dwup_ref, _mm_tn(a_mla, dy_mla), first)
        do_cat = da * (z * sig_z)
        dgz_ref[:, 0:D_MODEL] = dmerged * y_pool * (sgp * (1.0 - sgp))
        dgz_ref[:, D_MODEL:2 * D_MODEL] = dmerged * y_mla * (sgm * (1.0 - sgm))
        dgz_ref[:, 2 * D_MODEL:] = da * o_cat * _dsilu(z, sig_z)
        for h in range(N_HEADS):
            doh = do_cat[:, V_DIM * h:V_DIM * (h + 1)]
            do_ref[h] = doh
            delta_ref[h] = jnp.sum(doh * o_cat[:, V_DIM * h:V_DIM * (h + 1)], axis=1, keepdims=True)

    row = lambda w, c: pl.BlockSpec((tm, w), lambda i: (i, c))
    head = lambda w: pl.BlockSpec((N_HEADS, tm, w), lambda i: (0, i, 0))
    const = lambda r, c: pl.BlockSpec((r, c), lambda i: (0, 0))
    return pl.pallas_call(
        body, name="mla_post_bwd",
        out_shape=(jax.ShapeDtypeStruct((n_rows, 2560), F32), jax.ShapeDtypeStruct((n_rows, D_MODEL), F32),
                   jax.ShapeDtypeStruct((N_HEADS, n_rows, V_DIM), F32), jax.ShapeDtypeStruct((N_HEADS, n_rows, 1), F32),
                   jax.ShapeDtypeStruct((D_MODEL, D_MODEL), F32), jax.ShapeDtypeStruct((MLA_WIDTH, D_MODEL), F32)),
        grid=(n_rows // tm,),
        in_specs=[row(D_MODEL, 0), row(2048, COL_GATES // 2048), row(MLA_WIDTH, COL_ZMLA // MLA_WIDTH), head(V_DIM),
                  row(D_MODEL, 0), row(D_MODEL, 0), const(MLA_WIDTH, D_MODEL), const(D_MODEL, D_MODEL)],
        out_specs=(row(2560, 0), row(D_MODEL, 0), head(V_DIM), head(1), const(D_MODEL, D_MODEL), const(MLA_WIDTH, D_MODEL)),
        compiler_params=_cp("arbitrary"),
    )(dh, proj, proj, o, y_pool, y_mla, w_mla_up, w_out)


def _attn_bwd(q, k, v, do, lse, delta, ex=None):
    n_rows = q.shape[1]
    t = _attn_tile(n_rows)
    nb = n_rows // t

    def body(q_ref, k_ref, v_ref, do_ref, lse_ref, delta_ref, dq_ref, dk_ref, dv_ref, dk_sc, dv_sc):
        ki, qi = pl.program_id(1), pl.program_id(2)

        @pl.when((ki == 0) & (qi == 0))
        def _():
            dq_ref[...] = jnp.zeros(dq_ref.shape, F32)

        @pl.when(qi == 0)
        def _():
            dk_sc[...] = jnp.zeros(dk_sc.shape, F32)
            dv_sc[...] = jnp.zeros(dv_sc.shape, F32)

        def step(masked):
            heads = range(HEADS_PER_STEP)
            s = [_mm_nt(q_ref[h], k_ref[h]) for h in heads]
            dp = [_mm_nt(do_ref[h], v_ref[h]) for h in heads]
            if masked:
                mask = _attn_mask(qi * t, ki * t, t, t)
                s = [jnp.where(mask, sh, MASK_VALUE) for sh in s]
            p = [jnp.exp2(s[h] * EXP2_SCALE - lse_ref[h] * LOG2_E) for h in heads]
            ds = [p[h] * (dp[h] - delta_ref[h]) for h in heads]
            rows = pl.ds(pl.multiple_of(qi * t, t), t)
            for h in heads:
                dv_sc[h] += _mm_tn(p[h], do_ref[h])
                dq_ref[h, rows, :] += _mm(ds[h], k_ref[h]) * ATTN_SCALE
                dk_sc[h] += _mm_tn(ds[h], q_ref[h])

        edge = (ki == qi) | (ki == 0)

        @pl.when((qi >= ki) & edge)
        def _():
            step(True)

        @pl.when((qi > ki) & (ki > 0))
        def _():
            step(False)

        @pl.when(qi == nb - 1)
        def _():
            dk_ref[...] = dk_sc[...] * ATTN_SCALE
            dv_ref[...] = dv_sc[...]

    hp = HEADS_PER_STEP
    q_idx = lambda h, ki, qi: (h, jnp.maximum(qi, ki), 0)
    k_idx = lambda h, ki, qi: (h, ki, 0)
    return _call_carrying(
        ex, body, "attn_bwd",
        out_shape=(jax.ShapeDtypeStruct((N_HEADS, n_rows, 128), F32), jax.ShapeDtypeStruct((N_HEADS, n_rows, 128), F32),
                   jax.ShapeDtypeStruct((N_HEADS, n_rows, V_DIM), F32)),
        grid=(N_HEADS // hp, nb, nb),
        in_specs=[pl.BlockSpec((hp, t, 128), q_idx), pl.BlockSpec((hp, t, 128), k_idx), pl.BlockSpec((hp, t, V_DIM), k_idx),
                  pl.BlockSpec((hp, t, V_DIM), q_idx), pl.BlockSpec((hp, t, 1), q_idx), pl.BlockSpec((hp, t, 1), q_idx)],
        out_specs=(pl.BlockSpec((hp, n_rows, 128), lambda h, ki, qi: (h, 0, 0)), pl.BlockSpec((hp, t, 128), k_idx),
                   pl.BlockSpec((hp, t, V_DIM), k_idx)),
        scratch_shapes=[pltpu.VMEM((hp, t, 128), F32), pltpu.VMEM((hp, t, V_DIM), F32)],
        semantics=("parallel", "arbitrary", "arbitrary"), args=(q, k, v, do, lse, delta))


def _mla_pre_bwd(proj, tab, gqa, gkva, wq, wkv, gqn, gkn, dq, dk, dv):
    n_rows = proj.shape[0]
    tm = _row_tile(n_rows)

    def body(c_ref, kr_ref, tab_ref, gqa_ref, gkva_ref, wq_ref, wkv_ref, gqn_ref, gkn_ref, dq_ref, dk_ref, dv_ref,
             dc_ref, dkr_ref, dwq_ref, dwkv_ref, dgqa_ref, dgkva_ref, dgqn_ref, dgkn_ref):
        first = pl.program_id(0) == 0
        c = c_ref[...]
        kr = kr_ref[:, 0:QK_ROPE]
        cos, sin = tab_ref[:, 0:QK_ROPE], tab_ref[:, QK_ROPE:2 * QK_ROPE]
        gqa, gkva, gqn, gkn = gqa_ref[...], gkva_ref[...], gqn_ref[...], gkn_ref[...]
        cqn, cq_h, cq_inv, ckvn, ckv_h, ckv_inv, qraw, kvraw = _mla_heads(c[:, :Q_RANK], c[:, Q_RANK:], kr, gqa, gkva,
                                                                          wq_ref, wkv_ref)
        dq_parts, dkv_parts = [], []
        dkr = jnp.zeros((tm, QK_ROPE), F32)
        dgqn = jnp.zeros((1, QK_DIM), F32)
        dgkn = jnp.zeros((1, QK_DIM), F32)
        for h in range(N_HEADS):
            _, xh, inv = _rms(qraw[:, QK_DIM * h:QK_DIM * (h + 1)], gqn)
            dqo = dq_ref[h]
            dqn = jnp.concatenate([dqo[:, :QK_NOPE], _rope_bwd(dqo[:, QK_NOPE:QK_DIM], cos, sin)], axis=1)
            dx, dg = _rms_bwd(dqn, xh, inv, gqn)
            dq_parts.append(dx)
            dgqn = dgqn + dg
            kin = jnp.concatenate([kvraw[:, 128 * h:128 * h + QK_NOPE], kr], axis=1)
            _, xh, inv = _rms(kin, gkn)
            dko = dk_ref[h]
            dkn = jnp.concatenate([dko[:, :QK_NOPE], _rope_bwd(dko[:, QK_NOPE:QK_DIM], cos, sin)], axis=1)
            dx, dg = _rms_bwd(dkn, xh, inv, gkn)
            dgkn = dgkn + dg
            dkv_parts += [dx[:, :QK_NOPE], dv_ref[h]]
            dkr = dkr + dx[:, QK_NOPE:]
        dqraw = jnp.concatenate(dq_parts, axis=1)
        dkvraw = jnp.concatenate(dkv_parts, axis=1)
        _acc(dwq_ref, _mm_tn(cqn, dqraw), first)
        _acc(dwkv_ref, _mm_tn(ckvn, dkvraw), first)
        dcq, dg1 = _rms_bwd(_mm_nt(dqraw, wq_ref[...]), cq_h, cq_inv, gqa)
        dckv, dg2 = _rms_bwd(_mm_nt(dkvraw, wkv_ref[...]), ckv_h, ckv_inv, gkva)
        _acc(dgqa_ref, dg1, first)
        _acc(dgkva_ref, dg2, first)
        _acc(dgqn_ref, dgqn, first)
        _acc(dgkn_ref, dgkn, first)
        dc_ref[...] = jnp.concatenate([dcq, dckv], axis=1)
        dkr_ref[...] = jnp.concatenate([dkr, jnp.zeros((tm, 128 - QK_ROPE), F32)], axis=1)

    full = lambda *s: pl.BlockSpec(s, lambda i: (0,) * len(s))
    head = lambda w: pl.BlockSpec((N_HEADS, tm, w), lambda i: (0, i, 0))
    return pl.pallas_call(
        body, name="mla_pre_bwd",
        out_shape=(jax.ShapeDtypeStruct((n_rows, 1024), F32), jax.ShapeDtypeStruct((n_rows, 128), F32),
                   jax.ShapeDtypeStruct((Q_RANK, Q_RANK), F32), jax.ShapeDtypeStruct((KV_RANK, 1024), F32),
                   jax.ShapeDtypeStruct((1, Q_RANK), F32), jax.ShapeDtypeStruct((1, KV_RANK), F32),
                   jax.ShapeDtypeStruct((1, QK_DIM), F32), jax.ShapeDtypeStruct((1, QK_DIM), F32)),
        grid=(n_rows // tm,),
        in_specs=[pl.BlockSpec((tm, 1024), lambda i: (i, COL_CQKV // 1024)),
                  pl.BlockSpec((tm, 128), lambda i: (i, COL_KR // 128)),
                  pl.BlockSpec((tm, 128), lambda i: (i, 0)),
                  full(1, Q_RANK), full(1, KV_RANK), full(Q_RANK, Q_RANK), full(KV_RANK, 1024), full(1, QK_DIM), full(1, QK_DIM),
                  head(128), head(128), head(V_DIM)],
        out_specs=(pl.BlockSpec((tm, 1024), lambda i: (i, 0)), pl.BlockSpec((tm, 128), lambda i: (i, 0)),
                   full(Q_RANK, Q_RANK), full(KV_RANK, 1024), full(1, Q_RANK), full(1, KV_RANK), full(1, QK_DIM), full(1, QK_DIM)),
        compiler_params=_cp("arbitrary"),
    )(proj, proj, tab, gqa, gkva, wq, wkv, gqn, gkn, dq, dk, dv)


def _pool_bwd_a(proj, dy_pool, wg, scale, w_up):
    n_rows = proj.shape[0]
    tm = _row_tile(n_rows)

    def body(uz_ref, halo_ref, dy_ref, wg_ref, sc_ref, wup_ref, dmz_ref, dwg_ref, dsc_ref, dwup_ref):
        first = pl.program_id(0) == 0
        uz = uz_ref[...]
        u, z = uz[:, :POOL_WIDTH], uz[:, POOL_WIDTH:]
        scale_v = sc_ref[...]
        mixed, yg, ys, sig = _pool_math(u, halo_ref[...], z, wg_ref, scale_v, pl.program_id(0) * tm)
        sp = z * sig
        dy = dy_ref[...]
        da = _mm_nt(dy, wup_ref[...])
        _acc(dwup_ref, _mm_tn(ys * sp, dy), first)
        dys = da * sp
        _acc(dsc_ref, jnp.sum(dys * yg, axis=0, keepdims=True), first)
        dyg = dys * scale_v
        for g in range(4):
            cols = slice(g * 128, (g + 1) * 128)
            dmz_ref[:, cols] = _mm_nt(dyg[:, cols], wg_ref[g])
            _acc(dwg_ref.at[g], _mm_tn(mixed[:, cols], dyg[:, cols]), first)
        dmz_ref[:, POOL_WIDTH:] = da * ys * _dsilu(z, sig)

    return pl.pallas_call(
        body, name="pool_bwd_a",
        out_shape=(jax.ShapeDtypeStruct((n_rows, 1024), F32), jax.ShapeDtypeStruct((4, 128, 128), F32),
                   jax.ShapeDtypeStruct((1, POOL_WIDTH), F32), jax.ShapeDtypeStruct((POOL_WIDTH, D_MODEL), F32)),
        grid=(n_rows // tm,),
        in_specs=[pl.BlockSpec((tm, 1024), lambda i: (i, 0)), pl.BlockSpec((HALO, POOL_WIDTH), _halo_above(tm)),
                  pl.BlockSpec((tm, D_MODEL), lambda i: (i, 0)),
                  pl.BlockSpec((4, 128, 128), lambda i: (0, 0, 0)), pl.BlockSpec((1, POOL_WIDTH), lambda i: (0, 0)),
                  pl.BlockSpec((POOL_WIDTH, D_MODEL), lambda i: (0, 0))],
        out_specs=(pl.BlockSpec((tm, 1024), lambda i: (i, 0)), pl.BlockSpec((4, 128, 128), lambda i: (0, 0, 0)),
                   pl.BlockSpec((1, POOL_WIDTH), lambda i: (0, 0)), pl.BlockSpec((POOL_WIDTH, D_MODEL), lambda i: (0, 0))),
        compiler_params=_cp("arbitrary"),
    )(proj, proj, dy_pool, wg, scale, w_up)


def _pool_bwd_b(dmz):
    n_rows = dmz.shape[0]
    tm = _row_tile(n_rows)
    n_tiles = n_rows // tm
    n_ext = tm + HALO

    def body(dmz_ref, halo_ref, o_ref):
        i = pl.program_id(0)
        v = dmz_ref[...]
        dm = v[:, :POOL_WIDTH]
        halo = jnp.where(i == n_tiles - 1, 0.0, halo_ref[...])
        ext = jnp.concatenate([dm, halo], axis=0)
        t1 = (i * tm + lax.broadcasted_iota(jnp.int32, (n_ext, 1), 0) - (PAD_FRONT - 1)).astype(F32)
        du = []
        for g, w in enumerate(POOL_WINDOWS):
            cols = slice(g * 128, (g + 1) * 128)
            a = ext[:, cols] / jnp.clip(t1, 1.0, float(w))
            k = 1
            while k < w:
                a = a + pltpu.roll(a, n_ext - k, 0)
                k *= 2
            du.append(a[:tm, :] - dm[:, cols])
        o_ref[...] = jnp.concatenate(du + [v[:, POOL_WIDTH:]], axis=1)

    last_halo = n_rows // HALO - 1
    return pl.pallas_call(
        body, name="pool_bwd_b", out_shape=jax.ShapeDtypeStruct((n_rows, 1024), F32), grid=(n_tiles,),
        in_specs=[pl.BlockSpec((tm, 1024), lambda i: (i, 0)),
                  pl.BlockSpec((HALO, POOL_WIDTH), lambda i: (jnp.minimum((i + 1) * (tm // HALO), last_halo), 0))],
        out_specs=pl.BlockSpec((tm, 1024), lambda i: (i, 0)),
        compiler_params=_cp("parallel"),
    )(dmz, dmz)


def _norm_proj_bwd(d_uz, d_cqkv, d_gz, d_kr, w_in, h_res, gain, dh_out):
    n_rows = h_res.shape[0]
    tm = _row_tile(n_rows)
    pieces = ((COL_UZ, 1024), (COL_CQKV, 1024), (COL_GATES, 2560), (COL_KR, 128))

    def body(a_ref, b_ref, c_ref, d_ref, w_ref, x_ref, g_ref, dho_ref, dhi_ref, dg_ref):
        i = pl.program_id(0)
        dh = None
        for ref, (c0, wd) in zip((a_ref, b_ref, c_ref, d_ref), pieces):
            part = _mm_nt(ref[...], w_ref[:, c0:c0 + wd])
            dh = part if dh is None else dh + part
        g = g_ref[...]
        _, xh, inv = _rms(x_ref[...], g)
        dx, dg = _rms_bwd(dh, xh, inv, g)
        _acc(dg_ref, dg, i == 0)
        row = i * tm + lax.broadcasted_iota(jnp.int32, (tm, 1), 0)
        dhi_ref[...] = jnp.where(row >= PAD_FRONT, dho_ref[...] + dx, 0.0)

    row = lambda w: pl.BlockSpec((tm, w), lambda i: (i, 0))
    return pl.pallas_call(
        body, name="norm_proj_bwd",
        out_shape=(jax.ShapeDtypeStruct((n_rows, D_MODEL), F32), jax.ShapeDtypeStruct((1, D_MODEL), F32)),
        grid=(n_rows // tm,),
        in_specs=[row(1024), row(1024), row(2560), row(128), pl.BlockSpec((D_MODEL, P_IN), lambda i: (0, 0)),
                  row(D_MODEL), pl.BlockSpec((1, D_MODEL), lambda i: (0, 0)), row(D_MODEL)],
        out_specs=(row(D_MODEL), pl.BlockSpec((1, D_MODEL), lambda i: (0, 0))),
        compiler_params=_cp("arbitrary"),
    )(d_uz, d_cqkv, d_gz, d_kr, w_in, h_res, gain, dh_out)


def _weight_grad(name, a, b):
    k_rows, m = a.shape
    n = b.shape[1]
    tk = 640 if k_rows % 640 == 0 else 128
    tn = 512 if n % 512 == 0 else 128

    def body(a_ref, b_ref, o_ref):
        _acc(o_ref, _mm_tn(a_ref[...], b_ref[...]), pl.program_id(1) == 0)

    return pl.pallas_call(
        body, name=name, out_shape=jax.ShapeDtypeStruct((m, n), F32), grid=(n // tn, k_rows // tk),
        in_specs=[pl.BlockSpec((tk, m), lambda j, k: (k, 0)), pl.BlockSpec((tk, tn), lambda j, k: (k, j))],
        out_specs=pl.BlockSpec((m, tn), lambda j, k: (0, j)),
        compiler_params=_cp("parallel", "arbitrary"),
    )(a, b)


def _adamw(name, w, m, v, gbufs):
    r, c = w.shape
    depth = len(gbufs)
    r_l = r // depth
    tr = r_l
    if r_l * c * 4 > (1 << 20):
        tr = 256 if r_l % 256 == 0 else 128
    assert r_l * depth == r and r_l % tr == 0, (name, r, depth, tr)
    n_t = r_l // tr

    def body(w_ref, m_ref, v_ref, *refs):
        g_refs, (go_ref, d_ref, mo_ref, vo_ref) = refs[:depth], refs[depth:]

        def update(g_ref):
            g = g_ref[0].astype(F32)
            for s in range(1, N_DEV):
                g = g + g_ref[s].astype(F32)
            go_ref[...] = g
            m_new = ADAM_B1 * m_ref[...] + (1.0 - ADAM_B1) * g
            v_new = ADAM_B2 * v_ref[...] + (1.0 - ADAM_B2) * (g * g)
            mo_ref[...] = m_new
            vo_ref[...] = v_new
            m_hat = m_new / (1.0 - ADAM_B1 ** ADAM_STEP)
            v_hat = v_new / (1.0 - ADAM_B2 ** ADAM_STEP)
            d_ref[...] = -ADAM_LR * (m_hat / (jnp.sqrt(v_hat) + ADAM_EPS) + ADAM_WD * w_ref[...])

        for j in range(depth):
            pl.when(pl.program_id(0) == j)(lambda j=j: update(g_refs[j]))

    def g_spec(j):
        return pl.BlockSpec((N_DEV, tr, c), lambda l, i: (0, jnp.where(l == j, i, jnp.where(l < j, 0, n_t - 1)), 0))

    spec = pl.BlockSpec((tr, c), lambda l, i: (l * n_t + i, 0))
    sds = jax.ShapeDtypeStruct((r, c), F32)
    return pl.pallas_call(
        body, name=name, out_shape=(sds, sds, sds, sds), grid=(depth, n_t),
        in_specs=[spec, spec, spec] + [g_spec(j) for j in range(depth)],
        out_specs=(spec, spec, spec, spec),
        compiler_params=_cp("arbitrary", "arbitrary"),
    )(w, m, v, *gbufs)


SMALL = ("norm_gain", "pool_w_group", "pool_scale", "q_a_norm_gain", "kv_a_norm_gain", "q_norm_gain", "k_norm_gain")


def _pack_small(parts):
    rows = []
    for p in parts:
        flat = p.reshape(-1)
        pad = (-flat.shape[0]) % 1024
        if pad:
            flat = jnp.concatenate([flat, jnp.zeros((pad,), F32)])
        rows.append(flat.reshape(-1, 128))
    n_rows = sum(r.shape[0] for r in rows)
    tail = (-n_rows) % 256
    if tail:
        rows.append(jnp.zeros((tail, 128), F32))
    return jnp.concatenate(rows, axis=0)


def _unpack_small(packed, shapes):
    out, r0 = [], 0
    for shp in shapes:
        n = math.prod(shp)
        n_rows = (n + 1023) // 1024 * 8
        out.append(packed[r0:r0 + n_rows].reshape(-1)[:n].reshape(shp))
        r0 += n_rows
    return out


def kernel(x, positions, meta_tokens, norm_gain, w_in, pool_w_group, pool_scale, pool_w_up, q_a_norm_gain, kv_a_norm_gain, w_q_b, w_kv_b, q_norm_gain, k_norm_gain, mla_w_up, w_out, loss_target, m_meta_tokens, m_norm_gain, m_w_in, m_pool_w_group, m_pool_scale, m_pool_w_up, m_q_a_norm_gain, m_kv_a_norm_gain, m_w_q_b, m_w_kv_b, m_q_norm_gain, m_k_norm_gain, m_mla_w_up, m_w_out, v_meta_tokens, v_norm_gain, v_w_in, v_pool_w_group, v_pool_scale, v_pool_w_up, v_q_a_norm_gain, v_kv_a_norm_gain, v_w_q_b, v_w_kv_b, v_q_norm_gain, v_k_norm_gain, v_mla_w_up, v_w_out):
    x2, target = x[0], loss_target[0]

    big = dict(w_in=w_in, pool_w_up=pool_w_up, w_q_b=w_q_b, w_kv_b=w_kv_b, mla_w_up=mla_w_up, w_out=w_out)
    names_big = list(big)
    shards = {n: _cast_bf16("cast_" + n, w) for n, w in big.items()}

    def layer_shards(l):
        return [shards[n][l] for n in names_big]

    def repack(gathered):
        g_in, g_pup, g_qb, g_kvb, g_mup, g_out = gathered
        return dict(w_in=_repack_w_in(g_in),
                    pool_up=_repack_cols("repack_pool_up", g_pup[:, None], BF16)[0],
                    q_b=_repack_rows("repack_q_b", g_qb[:, None])[0],
                    kv_b=_repack_cols("repack_kv_b", g_kvb[:, None], BF16)[0],
                    mla_up=_repack_cols("repack_mla_up", g_mup[:, None], BF16)[0],
                    out=_repack_rows("repack_out", g_out[:, None])[0])

    first = _exchange("gather_first", [meta_tokens] + layer_shards(0), [])
    weights = [repack(first[1:])]

    h = _embed(x2, first[0])
    pos = jnp.concatenate([jnp.zeros((PAD_FRONT,), jnp.int32), jnp.arange(N_META, dtype=jnp.int32),
                           positions[0] + N_META]).astype(F32).reshape(-1, 1)
    half = QK_ROPE // 2
    inv_freq = ROPE_THETA ** (-jnp.arange(half, dtype=F32) / half)
    tab = _rope_table(pos, jnp.tile(inv_freq, 128 // half).reshape(1, 128))

    row = lambda a, l: a[l].reshape(1, -1)
    saved = []
    for l in range(DEPTH):
        w = weights[l]
        hb, proj = _norm_proj(h, row(norm_gain, l), w["w_in"])
        y_pool = _pool_fwd(proj, pool_w_group[l], row(pool_scale, l), w["pool_up"])
        q, k, v = _mla_pre_fwd(proj, tab, row(q_a_norm_gain, l), row(kv_a_norm_gain, l), w["q_b"], w["kv_b"],
                               row(q_norm_gain, l), row(k_norm_gain, l))
        ex = _Exchange(layer_shards(l + 1), []) if l + 1 < DEPTH else None
        res = _attn_fwd(q, k, v, ex)
        o, lse = res[0], res[1]
        if ex is not None:
            weights.append(repack(res[2:]))
        y_mla, h_next = _mla_post_fwd(h, proj, o, y_pool, w["mla_up"], w["out"])
        saved.append((h, hb, proj, y_pool, q, k, v, o, lse, y_mla))
        h = h_next

    loss_part, dh = _loss_head(h, target)
    loss = lax.psum(loss_part[0, 0], MESH_AXES)

    grads = {n: [None] * DEPTH for n in SMALL}
    pending = None
    received = [None] * DEPTH
    for l in reversed(range(DEPTH)):
        w = weights[l]
        h_l, hb, proj, y_pool, q, k, v, o, lse, y_mla = saved[l]
        d_gz, dy_pool, do, delta, g_out, g_mla_up = _mla_post_bwd(dh, proj, o, y_pool, y_mla, w["mla_up"], w["out"])
        ex = _Exchange([], pending) if pending is not None else None
        res = _attn_bwd(q, k, v, do, lse, delta, ex)
        dq, dk, dv = res[0], res[1], res[2]
        if ex is not None:
            received[l + 1] = res[3:]
        (d_cqkv, d_kr, g_q_b, g_kv_b, grads["q_a_norm_gain"][l], grads["kv_a_norm_gain"][l],
         grads["q_norm_gain"][l], grads["k_norm_gain"][l]) = _mla_pre_bwd(
            proj, tab, row(q_a_norm_gain, l), row(kv_a_norm_gain, l), w["q_b"], w["kv_b"], row(q_norm_gain, l),
            row(k_norm_gain, l), dq, dk, dv)
        dmz, grads["pool_w_group"][l], grads["pool_scale"][l], g_pool_up = _pool_bwd_a(
            proj, dy_pool, pool_w_group[l], row(pool_scale, l), w["pool_up"])
        d_uz = _pool_bwd_b(dmz)
        g_in = [_weight_grad("dw_in", hb, d) for d in (d_uz, d_cqkv, d_gz, d_kr)]
        dh, grads["norm_gain"][l] = _norm_proj_bwd(d_uz, d_cqkv, d_gz, d_kr, w["w_in"], h_l, row(norm_gain, l), dh)
        pending = [_unpack_w_in(g_in),
                   _unpack_cols("unpack_pool_up", g_pool_up[None], 128, BF16)[:, 0],
                   _unpack_rows("unpack_q_b", g_q_b[None], Q_RANK // N_DEV)[:, 0],
                   _unpack_cols("unpack_kv_b", g_kv_b[None], 128, BF16)[:, 0],
                   _unpack_cols("unpack_mla_up", g_mla_up[None], 128, BF16)[:, 0],
                   _unpack_rows("unpack_out", g_out[None], D_MODEL // N_DEV)[:, 0]]

    grad_x = dh[HEAD_ROWS:][None]
    d_meta = dh[PAD_FRONT:HEAD_ROWS]

    small_shapes = [a.shape for a in (norm_gain, pool_w_group, pool_scale, q_a_norm_gain, kv_a_norm_gain, q_norm_gain, k_norm_gain)]
    small_grad = _pack_small([jnp.stack(grads[n], axis=0).reshape(s) for n, s in zip(SMALL, small_shapes)])
    meta_parts = _unpack_cols("unpack_meta", d_meta[None], 128, F32)[:, 0]
    last = _exchange("exchange_last", [small_grad], [meta_parts] + pending)
    small_buf = last[0]
    received[0] = last[2:]
    sharded = ["meta_tokens"] + names_big
    shard_bufs = {"meta_tokens": [last[1]]}
    for j, n in enumerate(names_big):
        shard_bufs[n] = [received[l][j] for l in range(DEPTH)]

    given = dict(meta_tokens=(meta_tokens, m_meta_tokens, v_meta_tokens), w_in=(w_in, m_w_in, v_w_in),
                 pool_w_up=(pool_w_up, m_pool_w_up, v_pool_w_up), w_q_b=(w_q_b, m_w_q_b, v_w_q_b),
                 w_kv_b=(w_kv_b, m_w_kv_b, v_w_kv_b), mla_w_up=(mla_w_up, m_mla_w_up, v_mla_w_up),
                 w_out=(w_out, m_w_out, v_w_out))
    result = {}
    for n in sharded:
        w, m, v = given[n]
        cols = w.shape[-1]
        res = _adamw("adamw_" + n, w.reshape(-1, cols), m.reshape(-1, cols), v.reshape(-1, cols), shard_bufs[n])
        result[n] = [r.reshape(w.shape) for r in res]
    small_w = (norm_gain, pool_w_group, pool_scale, q_a_norm_gain, kv_a_norm_gain, q_norm_gain, k_norm_gain)
    small_m = (m_norm_gain, m_pool_w_group, m_pool_scale, m_q_a_norm_gain, m_kv_a_norm_gain, m_q_norm_gain, m_k_norm_gain)
    small_v = (v_norm_gain, v_pool_w_group, v_pool_scale, v_q_a_norm_gain, v_kv_a_norm_gain, v_q_norm_gain, v_k_norm_gain)
    res = _adamw("adamw_small", _pack_small(small_w), _pack_small(small_m), _pack_small(small_v), [small_buf])
    small_res = [_unpack_small(r, small_shapes) for r in res]
    for j, n in enumerate(SMALL):
        result[n] = [small_res[kind][j] for kind in range(4)]

    order = ("meta_tokens", "norm_gain", "w_in", "pool_w_group", "pool_scale", "pool_w_up", "q_a_norm_gain",
             "kv_a_norm_gain", "w_q_b", "w_kv_b", "q_norm_gain", "k_norm_gain", "mla_w_up", "w_out")
    outs = [loss, grad_x]
    for kind in range(4):
        outs += [result[n][kind] for n in order]
    return tuple(outs)
```

```python
import math

import jax
import jax.numpy as jnp
from jax import lax
from jax.experimental import pallas as pl
from jax.experimental.pallas import tpu as pltpu

F32 = jnp.float32
BF16 = jnp.bfloat16

D_MODEL = 1024
DEPTH = 4
N_META = 16
PAD_FRONT = 112
HEAD_ROWS = PAD_FRONT + N_META
POOL_WIDTH = 512
POOL_WINDOWS = (2, 4, 8, 16)
POOL_GROUP_DIM = 128
HALO = 16
N_HEADS = 8
HEADS_PER_STEP = 2
HEADS_PER_STEP_FWD = 4
QK_NOPE = 64
QK_ROPE = 32
QK_DIM = 96
V_DIM = 64
MLA_WIDTH = 512
Q_RANK = 768
KV_RANK = 256
ROPE_THETA = 10000.0
NORM_EPS = 1e-6
MASK_VALUE = -1e30
D_IN = 4640
N_DEV = 8
IN_SHARD = D_IN // N_DEV

P_IN = 4736
IN_SEGMENTS = ((0, 2048, 0), (2048, 2080, 4608), (2080, 2592, 4096), (2592, 3616, 2048), (3616, 4640, 3072))
COL_UZ, COL_CQKV, COL_GATES, COL_ZMLA, COL_KR = 0, 1024, 2048, 4096, 4608

ADAM_LR = 0.001
ADAM_B1 = 0.9
ADAM_B2 = 0.999
ADAM_EPS = 1e-08
ADAM_WD = 0.01
ADAM_STEP = 10

VMEM_LIMIT = 56 * 1024 * 1024
MESH_AXES = ("x", "y", "c")


def _cp(*sem):
    return pltpu.CompilerParams(dimension_semantics=sem, vmem_limit_bytes=VMEM_LIMIT)


def _row_tile(n_rows):
    return 320 if n_rows % 320 == 0 else 128


def _attn_tile(n_rows):
    return 640 if (n_rows % 640 == 0 and n_rows > 640) else 128


def _mm(a, b):
    return jnp.dot(a.astype(BF16), b.astype(BF16), preferred_element_type=F32)


def _mm_nt(a, b):
    return lax.dot_general(a.astype(BF16), b.astype(BF16), (((1,), (1,)), ((), ())), preferred_element_type=F32)


def _mm_tn(a, b):
    return lax.dot_general(a.astype(BF16), b.astype(BF16), (((0,), (0,)), ((), ())), preferred_element_type=F32)


def _sigmoid(z):
    return 1.0 / (1.0 + jnp.exp(-z))


def _rms(x, g):
    inv = lax.rsqrt(jnp.mean(x * x, axis=-1, keepdims=True) + NORM_EPS)
    xh = x * inv
    return xh * g, xh, inv


def _rms_bwd(dy, xh, inv, g):
    dg = jnp.sum(dy * xh, axis=0, keepdims=True)
    dxh = dy * g
    dx = inv * (dxh - xh * jnp.mean(dxh * xh, axis=-1, keepdims=True))
    return dx, dg


def _rope(r, cos, sin):
    rot = jnp.concatenate([-r[:, 16:32], r[:, 0:16]], axis=1)
    return r * cos + rot * sin


def _rope_bwd(dy, cos, sin):
    ys = dy * sin
    return dy * cos + jnp.concatenate([ys[:, 16:32], -ys[:, 0:16]], axis=1)


class _Exchange:
    def __init__(self, gather_list, scatter_list):
        self.arrays = list(gather_list) + list(scatter_list)
        self.n_g, self.n = len(gather_list), len(self.arrays)
        self.out_shape = [jax.ShapeDtypeStruct((N_DEV,) + a.shape, a.dtype) for a in gather_list]
        self.out_shape += [jax.ShapeDtypeStruct(a.shape, a.dtype) for a in scatter_list]
        self.specs = [pl.BlockSpec(memory_space=pl.ANY)] * self.n
        self.sems = [pltpu.SemaphoreType.DMA((7 * self.n,)), pltpu.SemaphoreType.DMA((7 * self.n,)),
                     pltpu.SemaphoreType.DMA((self.n,))]

    def copies(self, ins, outs, sems):
        send_sems, recv_sems, local_sems = sems
        x, y, c = lax.axis_index("x"), lax.axis_index("y"), lax.axis_index("c")
        me = 4 * x + 2 * y + c

        def src(a, slot):
            return ins[a] if a < self.n_g else ins[a].at[slot]

        out = [pltpu.make_async_copy(src(a, me), outs[a].at[me], local_sems.at[a]) for a in range(self.n)]
        for k in range(1, N_DEV):
            px = 1 - x if k & 4 else x
            py = 1 - y if k & 2 else y
            pc = 1 - c if k & 1 else c
            peer = 4 * px + 2 * py + pc
            for a in range(self.n):
                out.append(pltpu.make_async_remote_copy(
                    src_ref=src(a, peer), dst_ref=outs[a].at[me],
                    send_sem=send_sems.at[a * 7 + k - 1], recv_sem=recv_sems.at[a * 7 + k - 1],
                    device_id=(px, py, pc), device_id_type=pl.DeviceIdType.MESH))
        return out


def _exchange(name, gather_list, scatter_list):
    ex = _Exchange(gather_list, scatter_list)
    n = ex.n

    def body(*refs):
        copies = ex.copies(refs[:n], refs[n:2 * n], refs[2 * n:])
        for cp in copies:
            cp.start()
        for cp in copies:
            cp.wait()

    return pl.pallas_call(
        body, name=name, out_shape=ex.out_shape, in_specs=ex.specs, out_specs=ex.specs, scratch_shapes=ex.sems,
        compiler_params=pltpu.CompilerParams(has_side_effects=True),
    )(*ex.arrays)


def _call_carrying(ex, body, name, out_shape, grid, in_specs, out_specs, scratch_shapes, semantics, args):
    if ex is None:
        return pl.pallas_call(body, name=name, out_shape=out_shape, grid=grid, in_specs=in_specs, out_specs=out_specs,
                              scratch_shapes=scratch_shapes, compiler_params=_cp(*semantics))(*args)
    n, n_in, n_out, n_scr = ex.n, len(in_specs), len(out_shape), len(scratch_shapes)

    def carrying(*refs):
        ins, refs = refs[:n_in], refs[n_in:]
        c_in, refs = refs[:n], refs[n:]
        outs, refs = refs[:n_out], refs[n_out:]
        c_out, refs = refs[:n], refs[n:]
        scratch, sems = refs[:n_scr], refs[n_scr:]
        ids = [pl.program_id(a) for a in range(len(grid))]
        first, last = ids[0] == 0, ids[0] == grid[0] - 1
        for a in range(1, len(grid)):
            first, last = first & (ids[a] == 0), last & (ids[a] == grid[a] - 1)
        copies = ex.copies(c_in, c_out, sems)

        @pl.when(first)
        def _():
            for cp in copies:
                cp.start()

        body(*ins, *outs, *scratch)

        @pl.when(last)
        def _():
            for cp in copies:
                cp.wait()

    return pl.pallas_call(
        carrying, name=name + "_x", out_shape=list(out_shape) + ex.out_shape, grid=grid,
        in_specs=list(in_specs) + ex.specs, out_specs=list(out_specs) + ex.specs,
        scratch_shapes=list(scratch_shapes) + ex.sems, compiler_params=_cp(*["arbitrary"] * len(grid)),
    )(*args, *ex.arrays)


def _in_pieces():
    runs = []
    for lo, hi, dst in sorted(IN_SEGMENTS, key=lambda s: s[2]):
        col = lo
        while col < hi:
            d = col // IN_SHARD
            end = min(hi, (d + 1) * IN_SHARD)
            runs.append((d, col - d * IN_SHARD, end - d * IN_SHARD))
            col = end
    return runs


def _cast_bf16(name, w):
    shape = w.shape
    w2 = w.reshape(-1, shape[-1])
    r, c = w2.shape
    tr = 512 if r % 512 == 0 else r

    def body(x_ref, o_ref):
        o_ref[...] = x_ref[...].astype(BF16)

    spec = pl.BlockSpec((tr, c), lambda i: (i, 0))
    out = pl.pallas_call(body, name=name, out_shape=jax.ShapeDtypeStruct((r, c), BF16), grid=(r // tr,),
                         in_specs=[spec], out_specs=spec, compiler_params=_cp("parallel"))(w2)
    return out.reshape(shape)


def _repack_w_in(gathered):
    tr = 256
    runs = _in_pieces()

    def body(x_ref, o_ref):
        parts = [x_ref[d, :, a:b].astype(F32) for d, a, b in runs]
        parts.append(jnp.zeros((tr, P_IN - D_IN), F32))
        o_ref[...] = jnp.concatenate(parts, axis=1).astype(BF16)

    return pl.pallas_call(
        body, name="repack_w_in", out_shape=jax.ShapeDtypeStruct((D_MODEL, P_IN), BF16), grid=(D_MODEL // tr,),
        in_specs=[pl.BlockSpec((N_DEV, tr, IN_SHARD), lambda i: (0, i, 0))],
        out_specs=pl.BlockSpec((tr, P_IN), lambda i: (i, 0)),
        compiler_params=_cp("parallel"),
    )(gathered)


def _unpack_w_in(parts):
    tr = 256
    runs = _in_pieces()
    widths = [p.shape[-1] for p in parts]

    def body(a_ref, b_ref, c_ref, d_ref, o_ref):
        full = jnp.concatenate([a_ref[...], b_ref[...], c_ref[...], d_ref[...]], axis=1)
        col = 0
        per_dev = [[] for _ in range(N_DEV)]
        for d, a, b in runs:
            per_dev[d].append((a, full[:, col:col + (b - a)]))
            col += b - a
        for d in range(N_DEV):
            pieces = [v for _, v in sorted(per_dev[d], key=lambda av: av[0])]
            o_ref[d] = (jnp.concatenate(pieces, axis=1) if len(pieces) > 1 else pieces[0]).astype(BF16)

    return pl.pallas_call(
        body, name="unpack_w_in", out_shape=jax.ShapeDtypeStruct((N_DEV, D_MODEL, IN_SHARD), BF16), grid=(D_MODEL // tr,),
        in_specs=[pl.BlockSpec((tr, w), lambda i: (i, 0)) for w in widths],
        out_specs=pl.BlockSpec((N_DEV, tr, IN_SHARD), lambda i: (0, i, 0)),
        compiler_params=_cp("parallel"),
    )(*parts)


def _repack_cols(name, gathered, dtype):
    _, depth, k, n = gathered.shape

    def body(x_ref, o_ref):
        o_ref[0] = jnp.concatenate([x_ref[d, 0] for d in range(N_DEV)], axis=1).astype(dtype)

    return pl.pallas_call(
        body, name=name, out_shape=jax.ShapeDtypeStruct((depth, k, N_DEV * n), dtype), grid=(depth,),
        in_specs=[pl.BlockSpec((N_DEV, 1, k, n), lambda l: (0, l, 0, 0))],
        out_specs=pl.BlockSpec((1, k, N_DEV * n), lambda l: (l, 0, 0)),
        compiler_params=_cp("parallel"),
    )(gathered)


def _unpack_cols(name, full, n, dtype):
    depth, k, _ = full.shape

    def body(x_ref, o_ref):
        for d in range(N_DEV):
            o_ref[d, 0] = x_ref[0, :, d * n:(d + 1) * n].astype(dtype)

    return pl.pallas_call(
        body, name=name, out_shape=jax.ShapeDtypeStruct((N_DEV, depth, k, n), dtype), grid=(depth,),
        in_specs=[pl.BlockSpec((1, k, N_DEV * n), lambda l: (l, 0, 0))],
        out_specs=pl.BlockSpec((N_DEV, 1, k, n), lambda l: (0, l, 0, 0)),
        compiler_params=_cp("parallel"),
    )(full)


def _repack_rows(name, gathered):
    _, depth, r, n = gathered.shape

    def body(x_ref, o_ref):
        for d in range(N_DEV):
            o_ref[0, d * r:(d + 1) * r, :] = x_ref[d, 0].astype(BF16)

    return pl.pallas_call(
        body, name=name, out_shape=jax.ShapeDtypeStruct((depth, N_DEV * r, n), BF16), grid=(depth,),
        in_specs=[pl.BlockSpec((N_DEV, 1, r, n), lambda l: (0, l, 0, 0))],
        out_specs=pl.BlockSpec((1, N_DEV * r, n), lambda l: (l, 0, 0)),
        compiler_params=_cp("parallel"),
    )(gathered)


def _unpack_rows(name, full, r):
    depth, _, n = full.shape

    def body(x_ref, o_ref):
        for d in range(N_DEV):
            o_ref[d, 0] = x_ref[0, d * r:(d + 1) * r, :].astype(BF16)

    return pl.pallas_call(
        body, name=name, out_shape=jax.ShapeDtypeStruct((N_DEV, depth, r, n), BF16), grid=(depth,),
        in_specs=[pl.BlockSpec((1, N_DEV * r, n), lambda l: (l, 0, 0))],
        out_specs=pl.BlockSpec((N_DEV, 1, r, n), lambda l: (0, l, 0, 0)),
        compiler_params=_cp("parallel"),
    )(full)


def _embed(x, meta_g):
    seq = x.shape[0]
    n_blk = seq // HEAD_ROWS + 1

    def body(x_ref, m_ref, o_ref):
        i = pl.program_id(0)

        @pl.when(i == 0)
        def _():
            meta = jnp.concatenate([m_ref[d] for d in range(N_DEV)], axis=1)
            o_ref[...] = jnp.concatenate([jnp.zeros((PAD_FRONT, D_MODEL), F32), meta], axis=0)

        @pl.when(i > 0)
        def _():
            o_ref[...] = x_ref[...]

    return pl.pallas_call(
        body, name="embed", out_shape=jax.ShapeDtypeStruct((seq + HEAD_ROWS, D_MODEL), F32), grid=(n_blk,),
        in_specs=[pl.BlockSpec((HEAD_ROWS, D_MODEL), lambda i: (jnp.maximum(i - 1, 0), 0)),
                  pl.BlockSpec((N_DEV, N_META, 128), lambda i: (0, 0, 0))],
        out_specs=pl.BlockSpec((HEAD_ROWS, D_MODEL), lambda i: (i, 0)),
        compiler_params=_cp("parallel"),
    )(x, meta_g)


def _rope_table(pos_col, freq_row):
    n_rows = pos_col.shape[0]
    tr = 128

    def body(p_ref, f_ref, o_ref):
        ang = p_ref[...] * f_ref[...]
        lane = lax.broadcasted_iota(jnp.int32, ang.shape, 1)
        o_ref[...] = jnp.where(lane < QK_ROPE, jnp.cos(ang), jnp.sin(ang))

    return pl.pallas_call(
        body, name="rope_table", out_shape=jax.ShapeDtypeStruct((n_rows, 128), F32), grid=(n_rows // tr,),
        in_specs=[pl.BlockSpec((tr, 1), lambda i: (i, 0)), pl.BlockSpec((1, 128), lambda i: (0, 0))],
        out_specs=pl.BlockSpec((tr, 128), lambda i: (i, 0)),
        compiler_params=_cp("parallel"),
    )(pos_col, freq_row)


def _loss_head(h, target):
    n_rows = h.shape[0]
    n_blk = n_rows // HEAD_ROWS

    def body(h_ref, t_ref, loss_ref, dh_ref):
        i = pl.program_id(0)

        @pl.when(i == 0)
        def _():
            loss_ref[...] = jnp.zeros(loss_ref.shape, F32)
            dh_ref[...] = jnp.zeros(dh_ref.shape, F32)

        @pl.when(i > 0)
        def _():
            err = h_ref[...] - t_ref[...]
            dh_ref[...] = err * (1.0 / D_MODEL)
            loss_ref[...] += 0.5 * jnp.sum(jnp.mean(err * err, axis=-1, keepdims=True))

    return pl.pallas_call(
        body, name="loss_head",
        out_shape=(jax.ShapeDtypeStruct((8, 128), F32), jax.ShapeDtypeStruct((n_rows, D_MODEL), F32)),
        grid=(n_blk,),
        in_specs=[pl.BlockSpec((HEAD_ROWS, D_MODEL), lambda i: (i, 0)),
                  pl.BlockSpec((HEAD_ROWS, D_MODEL), lambda i: (jnp.maximum(i - 1, 0), 0))],
        out_specs=(pl.BlockSpec((8, 128), lambda i: (0, 0)), pl.BlockSpec((HEAD_ROWS, D_MODEL), lambda i: (i, 0))),
        compiler_params=_cp("arbitrary"),
    )(h, target)


PROJ_CHUNKS = (0, 1024, 2048, 3072, 4096, P_IN)


def _norm_proj(h_res, gain, w_in):
    n_rows = h_res.shape[0]
    tm = _row_tile(n_rows)

    def body(x_ref, g_ref, w_ref, hb_ref, p_ref):
        h, _, _ = _rms(x_ref[...], g_ref[...])
        hb = h.astype(BF16)
        hb_ref[...] = hb
        for c0, c1 in zip(PROJ_CHUNKS[:-1], PROJ_CHUNKS[1:]):
            p_ref[:, c0:c1] = jnp.dot(hb, w_ref[:, c0:c1], preferred_element_type=F32)

    return pl.pallas_call(
        body, name="norm_proj",
        out_shape=(jax.ShapeDtypeStruct((n_rows, D_MODEL), BF16), jax.ShapeDtypeStruct((n_rows, P_IN), F32)),
        grid=(n_rows // tm,),
        in_specs=[pl.BlockSpec((tm, D_MODEL), lambda i: (i, 0)), pl.BlockSpec((1, D_MODEL), lambda i: (0, 0)),
                  pl.BlockSpec((D_MODEL, P_IN), lambda i: (0, 0))],
        out_specs=(pl.BlockSpec((tm, D_MODEL), lambda i: (i, 0)), pl.BlockSpec((tm, P_IN), lambda i: (i, 0))),
        compiler_params=_cp("parallel"),
    )(h_res, gain, w_in)


def _pool_math(u, halo, z, wg_ref, scale, row0):
    tm = u.shape[0]
    ext = jnp.concatenate([halo, u], axis=0)
    t1 = (row0 + lax.broadcasted_iota(jnp.int32, (tm, 1), 0) - (PAD_FRONT - 1)).astype(F32)
    mixed, yg = [], []
    for g, w in enumerate(POOL_WINDOWS):
        a = ext[:, g * 128:(g + 1) * 128]
        k = 1
        while k < w:
            a = a + pltpu.roll(a, k, 0)
            k *= 2
        cnt = jnp.clip(t1, 1.0, float(w))
        mg = a[HALO:, :] / cnt - u[:, g * 128:(g + 1) * 128]
        mixed.append(mg)
        yg.append(_mm(mg, wg_ref[g]))
    mixed = jnp.concatenate(mixed, axis=1)
    yg = jnp.concatenate(yg, axis=1)
    ys = yg * scale
    sig = _sigmoid(z)
    return mixed, yg, ys, sig


def _halo_above(tm):
    return lambda i: (jnp.maximum(i * (tm // HALO) - 1, 0), 0)


def _pool_fwd(proj, wg, scale, w_up):
    n_rows = proj.shape[0]
    tm = _row_tile(n_rows)

    def body(uz_ref, halo_ref, wg_ref, sc_ref, wup_ref, y_ref):
        uz = uz_ref[...]
        u, z = uz[:, :POOL_WIDTH], uz[:, POOL_WIDTH:]
        _, _, ys, sig = _pool_math(u, halo_ref[...], z, wg_ref, sc_ref[...], pl.program_id(0) * tm)
        y_ref[...] = _mm(ys * (z * sig), wup_ref[...])

    return pl.pallas_call(
        body, name="pool_fwd", out_shape=jax.ShapeDtypeStruct((n_rows, D_MODEL), F32), grid=(n_rows // tm,),
        in_specs=[pl.BlockSpec((tm, 1024), lambda i: (i, 0)), pl.BlockSpec((HALO, POOL_WIDTH), _halo_above(tm)),
                  pl.BlockSpec((4, 128, 128), lambda i: (0, 0, 0)), pl.BlockSpec((1, POOL_WIDTH), lambda i: (0, 0)),
                  pl.BlockSpec((POOL_WIDTH, D_MODEL), lambda i: (0, 0))],
        out_specs=pl.BlockSpec((tm, D_MODEL), lambda i: (i, 0)),
        compiler_params=_cp("parallel"),
    )(proj, proj, wg, scale, w_up)


def _mla_heads(cq, ckv, kr, gqa, gkva, wq_ref, wkv_ref):
    cqn, cq_h, cq_inv = _rms(cq, gqa)
    ckvn, ckv_h, ckv_inv = _rms(ckv, gkva)
    qraw = _mm(cqn, wq_ref[...])
    kvraw = _mm(ckvn, wkv_ref[...])
    return cqn, cq_h, cq_inv, ckvn, ckv_h, ckv_inv, qraw, kvraw


def _mla_pre_fwd(proj, tab, gqa, gkva, wq, wkv, gqn, gkn):
    n_rows = proj.shape[0]
    tm = _row_tile(n_rows)

    def body(c_ref, kr_ref, tab_ref, gqa_ref, gkva_ref, wq_ref, wkv_ref, gqn_ref, gkn_ref, q_ref, k_ref, v_ref):
        c = c_ref[...]
        kr = kr_ref[:, 0:QK_ROPE]
        cos, sin = tab_ref[:, 0:QK_ROPE], tab_ref[:, QK_ROPE:2 * QK_ROPE]
        _, _, _, _, _, _, qraw, kvraw = _mla_heads(c[:, :Q_RANK], c[:, Q_RANK:], kr, gqa_ref[...], gkva_ref[...],
                                                  wq_ref, wkv_ref)
        zpad = jnp.zeros((tm, 128 - QK_DIM), F32)
        lane = lax.broadcasted_iota(jnp.int32, (tm, 128 - V_DIM), 1)
        vpad = jnp.where(lane == 0, 1.0, 0.0)
        for h in range(N_HEADS):
            qn, _, _ = _rms(qraw[:, QK_DIM * h:QK_DIM * (h + 1)], gqn_ref[...])
            q_ref[h] = jnp.concatenate([qn[:, :QK_NOPE], _rope(qn[:, QK_NOPE:], cos, sin), zpad], axis=1).astype(BF16)
            kin = jnp.concatenate([kvraw[:, 128 * h:128 * h + QK_NOPE], kr], axis=1)
            kn, _, _ = _rms(kin, gkn_ref[...])
            k_ref[h] = jnp.concatenate([kn[:, :QK_NOPE], _rope(kn[:, QK_NOPE:], cos, sin), zpad], axis=1).astype(BF16)
            v_ref[h] = jnp.concatenate([kvraw[:, 128 * h + QK_NOPE:128 * (h + 1)], vpad], axis=1).astype(BF16)

    full = lambda *s: pl.BlockSpec(s, lambda i: (0,) * len(s))
    return pl.pallas_call(
        body, name="mla_pre_fwd",
        out_shape=(jax.ShapeDtypeStruct((N_HEADS, n_rows, 128), BF16), jax.ShapeDtypeStruct((N_HEADS, n_rows, 128), BF16),
                   jax.ShapeDtypeStruct((N_HEADS, n_rows, 128), BF16)),
        grid=(n_rows // tm,),
        in_specs=[pl.BlockSpec((tm, 1024), lambda i: (i, COL_CQKV // 1024)),
                  pl.BlockSpec((tm, 128), lambda i: (i, COL_KR // 128)),
                  pl.BlockSpec((tm, 128), lambda i: (i, 0)),
                  full(1, Q_RANK), full(1, KV_RANK), full(Q_RANK, Q_RANK), full(KV_RANK, 1024), full(1, QK_DIM), full(1, QK_DIM)],
        out_specs=(pl.BlockSpec((N_HEADS, tm, 128), lambda i: (0, i, 0)), pl.BlockSpec((N_HEADS, tm, 128), lambda i: (0, i, 0)),
                   pl.BlockSpec((N_HEADS, tm, 128), lambda i: (0, i, 0))),
        compiler_params=_cp("parallel"),
    )(proj, proj, tab, gqa, gkva, wq, wkv, gqn, gkn)


def _attn_mask(row0, col0, n_r, n_c):
    row = row0 + lax.broadcasted_iota(jnp.int32, (n_r, n_c), 0)
    col = col0 + lax.broadcasted_iota(jnp.int32, (n_r, n_c), 1)
    return (col <= row) & (col >= PAD_FRONT)


ATTN_SCALE = 1.0 / math.sqrt(QK_DIM)
EXP2_SCALE = ATTN_SCALE * math.log2(math.e)
LOG2_E = math.log2(math.e)


def _row_chunks(t, size):
    return [(r, min(size, t - r)) for r in range(0, t, size)]


def _attn_fwd(q, k, v, ex=None):
    n_rows = q.shape[1]
    t = _attn_tile(n_rows)
    nb = n_rows // t

    def body(q_ref, k_ref, v_ref, o_ref, lse_ref, m_sc, acc_sc):
        qi, ki = pl.program_id(1), pl.program_id(2)

        @pl.when(ki == 0)
        def _():
            m_sc[...] = jnp.full(m_sc.shape, -jnp.inf, F32)
            acc_sc[...] = jnp.zeros(acc_sc.shape, F32)

        def update(masked):
            heads = range(hp)
            s = [_mm_nt(q_ref[h], k_ref[h]) for h in heads]
            if masked:
                mask = _attn_mask(qi * t, ki * t, t, t)
                s = [jnp.where(mask, sh, MASK_VALUE) for sh in s]
            p = []
            for h in heads:
                m_prev = m_sc[h]
                m_new = jnp.maximum(m_prev, jnp.max(s[h], axis=1, keepdims=True))
                alpha = jnp.exp2((m_prev - m_new) * EXP2_SCALE)
                p.append(jnp.exp2((s[h] - m_new) * EXP2_SCALE).astype(BF16))
                acc_sc[h] = alpha * acc_sc[h]
                m_sc[h] = m_new
            for h in heads:
                acc_sc[h] += _mm(p[h], v_ref[h])

        edge = (ki == qi) | (ki == 0)

        @pl.when((ki <= qi) & edge)
        def _():
            update(True)

        @pl.when((ki < qi) & (ki > 0))
        def _():
            update(False)

        @pl.when(ki == nb - 1)
        def _():
            acc = acc_sc[...]
            denom = acc[:, :, V_DIM:V_DIM + 1]
            o_ref[...] = acc[:, :, :V_DIM] / denom
            lse_ref[...] = m_sc[...] * ATTN_SCALE + jnp.log(denom)

    hp = HEADS_PER_STEP_FWD
    kv_idx = lambda h, qi, ki: (h, jnp.minimum(ki, qi), 0)
    return _call_carrying(
        ex, body, "attn_fwd",
        out_shape=(jax.ShapeDtypeStruct((N_HEADS, n_rows, V_DIM), F32), jax.ShapeDtypeStruct((N_HEADS, n_rows, 1), F32)),
        grid=(N_HEADS // hp, nb, nb),
        in_specs=[pl.BlockSpec((hp, t, 128), lambda h, qi, ki: (h, qi, 0)), pl.BlockSpec((hp, t, 128), kv_idx),
                  pl.BlockSpec((hp, t, 128), kv_idx)],
        out_specs=(pl.BlockSpec((hp, t, V_DIM), lambda h, qi, ki: (h, qi, 0)), pl.BlockSpec((hp, t, 1), lambda h, qi, ki: (h, qi, 0))),
        scratch_shapes=[pltpu.VMEM((hp, t, 1), F32), pltpu.VMEM((hp, t, 128), F32)],
        semantics=("parallel", "parallel", "arbitrary"), args=(q, k, v))


def _merge_math(o_ref, z, gates, y_pool, y_mla_fn):
    o_cat = jnp.concatenate([o_ref[h] for h in range(N_HEADS)], axis=1)
    sig_z = _sigmoid(z)
    a_mla = o_cat * (z * sig_z)
    y_mla = y_mla_fn(a_mla)
    sgp, sgm = _sigmoid(gates[:, :D_MODEL]), _sigmoid(gates[:, D_MODEL:])
    merged = sgp * y_pool + sgm * y_mla
    return o_cat, sig_z, a_mla, y_mla, sgp, sgm, merged


def _mla_post_fwd(h_res, proj, o, y_pool, w_mla_up, w_out):
    n_rows = h_res.shape[0]
    tm = _row_tile(n_rows)

    def body(h_ref, g_ref, z_ref, o_ref, yp_ref, wup_ref, wout_ref, ymla_ref, hn_ref):
        _, _, _, y_mla, _, _, merged = _merge_math(o_ref, z_ref[...], g_ref[...], yp_ref[...],
                                                   lambda a: _mm(a, wup_ref[...]))
        ymla_ref[...] = y_mla
        hn_ref[...] = h_ref[...] + _mm(merged, wout_ref[...])

    row = lambda w, c: pl.BlockSpec((tm, w), lambda i: (i, c))
    return pl.pallas_call(
        body, name="mla_post_fwd",
        out_shape=(jax.ShapeDtypeStruct((n_rows, D_MODEL), F32), jax.ShapeDtypeStruct((n_rows, D_MODEL), F32)),
        grid=(n_rows // tm,),
        in_specs=[row(D_MODEL, 0), row(2048, COL_GATES // 2048), row(MLA_WIDTH, COL_ZMLA // MLA_WIDTH),
                  pl.BlockSpec((N_HEADS, tm, V_DIM), lambda i: (0, i, 0)), row(D_MODEL, 0),
                  pl.BlockSpec((MLA_WIDTH, D_MODEL), lambda i: (0, 0)), pl.BlockSpec((D_MODEL, D_MODEL), lambda i: (0, 0))],
        out_specs=(row(D_MODEL, 0), row(D_MODEL, 0)),
        compiler_params=_cp("parallel"),
    )(h_res, proj, proj, o, y_pool, w_mla_up, w_out)


def _dsilu(z, sig):
    return sig * (1.0 + z * (1.0 - sig))


def _acc(ref, val, first):
    @pl.when(first)
    def _():
        ref[...] = val

    @pl.when(jnp.logical_not(first))
    def _():
        ref[...] += val


def _mla_post_bwd(dh, proj, o, y_pool, y_mla, w_mla_up, w_out):
    n_rows = dh.shape[0]
    tm = _row_tile(n_rows)

    def body(dh_ref, g_ref, z_ref, o_ref, yp_ref, ym_ref, wup_ref, wout_ref,
             dgz_ref, dyp_ref, do_ref, delta_ref, dwout_ref, dwup_ref):
        first = pl.program_id(0) == 0
        z, y_pool, y_mla = z_ref[...], yp_ref[...], ym_ref[...]
        o_cat, sig_z, a_mla, _, sgp, sgm, merged = _merge_math(o_ref, z, g_ref[...], y_pool, lambda a: y_mla)
        dhv = dh_ref[...]
        dmerged = _mm_nt(dhv, wout_ref[...])
        _acc(dwout_ref, _mm_tn(merged, dhv), first)
        dyp_ref[...] = dmerged * sgp
        dy_mla = dmerged * sgm
        da = _mm_nt(dy_mla, wup_ref[...])
        _acc(dwup_ref, _mm_tn(a_mla, dy_mla), first)
        do_cat = da * (z * sig_z)
        dgz_ref[:, 0:D_MODEL] = dmerged * y_pool * (sgp * (1.0 - sgp))
        dgz_ref[:, D_MODEL:2 * D_MODEL] = dmerged * y_mla * (sgm * (1.0 - sgm))
        dgz_ref[:, 2 * D_MODEL:] = da * o_cat * _dsilu(z, sig_z)
        for h in range(N_HEADS):
            doh = do_cat[:, V_DIM * h:V_DIM * (h + 1)]
            do_ref[h] = doh
            delta_ref[h] = jnp.sum(doh * o_cat[:, V_DIM * h:V_DIM * (h + 1)], axis=1, keepdims=True)

    row = lambda w, c: pl.BlockSpec((tm, w), lambda i: (i, c))
    head = lambda w: pl.BlockSpec((N_HEADS, tm, w), lambda i: (0, i, 0))
    const = lambda r, c: pl.BlockSpec((r, c), lambda i: (0, 0))
    return pl.pallas_call(
        body, name="mla_post_bwd",
        out_shape=(jax.ShapeDtypeStruct((n_rows, 2560), F32), jax.ShapeDtypeStruct((n_rows, D_MODEL), F32),
                   jax.ShapeDtypeStruct((N_HEADS, n_rows, V_DIM), F32), jax.ShapeDtypeStruct((N_HEADS, n_rows, 1), F32),
                   jax.ShapeDtypeStruct((D_MODEL, D_MODEL), F32), jax.ShapeDtypeStruct((MLA_WIDTH, D_MODEL), F32)),
        grid=(n_rows // tm,),
        in_specs=[row(D_MODEL, 0), row(2048, COL_GATES // 2048), row(MLA_WIDTH, COL_ZMLA // MLA_WIDTH), head(V_DIM),
                  row(D_MODEL, 0), row(D_MODEL, 0), const(MLA_WIDTH, D_MODEL), const(D_MODEL, D_MODEL)],
        out_specs=(row(2560, 0), row(D_MODEL, 0), head(V_DIM), head(1), const(D_MODEL, D_MODEL), const(MLA_WIDTH, D_MODEL)),
        compiler_params=_cp("arbitrary"),
    )(dh, proj, proj, o, y_pool, y_mla, w_mla_up, w_out)


def _attn_bwd(q, k, v, do, lse, delta, ex=None):
    n_rows = q.shape[1]
    t = _attn_tile(n_rows)
    nb = n_rows // t

    def body(q_ref, k_ref, v_ref, do_ref, lse_ref, delta_ref, dq_ref, dk_ref, dv_ref, dk_sc, dv_sc):
        ki, qi = pl.program_id(1), pl.program_id(2)

        @pl.when((ki == 0) & (qi == 0))
        def _():
            dq_ref[...] = jnp.zeros(dq_ref.shape, F32)

        @pl.when(qi == 0)
        def _():
            dk_sc[...] = jnp.zeros(dk_sc.shape, F32)
            dv_sc[...] = jnp.zeros(dv_sc.shape, F32)

        def step(masked):
            heads = range(HEADS_PER_STEP)
            s = [_mm_nt(q_ref[h], k_ref[h]) for h in heads]
            dp = [_mm_nt(do_ref[h], v_ref[h, :, 0:V_DIM]) for h in heads]
            if masked:
                mask = _attn_mask(qi * t, ki * t, t, t)
                s = [jnp.where(mask, sh, MASK_VALUE) for sh in s]
            p = [jnp.exp2(s[h] * EXP2_SCALE - lse_ref[h] * LOG2_E) for h in heads]
            ds = [p[h] * (dp[h] - delta_ref[h]) for h in heads]
            rows = pl.ds(pl.multiple_of(qi * t, t), t)
            for h in heads:
                dv_sc[h] += _mm_tn(p[h], do_ref[h])
                dq_ref[h, rows, :] += _mm(ds[h], k_ref[h]) * ATTN_SCALE
                dk_sc[h] += _mm_tn(ds[h], q_ref[h])

        edge = (ki == qi) | (ki == 0)

        @pl.when((qi >= ki) & edge)
        def _():
            step(True)

        @pl.when((qi > ki) & (ki > 0))
        def _():
            step(False)

        @pl.when(qi == nb - 1)
        def _():
            dk_ref[...] = dk_sc[...] * ATTN_SCALE
            dv_ref[...] = dv_sc[...]

    hp = HEADS_PER_STEP
    q_idx = lambda h, ki, qi: (h, jnp.maximum(qi, ki), 0)
    k_idx = lambda h, ki, qi: (h, ki, 0)
    return _call_carrying(
        ex, body, "attn_bwd",
        out_shape=(jax.ShapeDtypeStruct((N_HEADS, n_rows, 128), F32), jax.ShapeDtypeStruct((N_HEADS, n_rows, 128), F32),
                   jax.ShapeDtypeStruct((N_HEADS, n_rows, V_DIM), F32)),
        grid=(N_HEADS // hp, nb, nb),
        in_specs=[pl.BlockSpec((hp, t, 128), q_idx), pl.BlockSpec((hp, t, 128), k_idx), pl.BlockSpec((hp, t, 128), k_idx),
                  pl.BlockSpec((hp, t, V_DIM), q_idx), pl.BlockSpec((hp, t, 1), q_idx), pl.BlockSpec((hp, t, 1), q_idx)],
        out_specs=(pl.BlockSpec((hp, n_rows, 128), lambda h, ki, qi: (h, 0, 0)), pl.BlockSpec((hp, t, 128), k_idx),
                   pl.BlockSpec((hp, t, V_DIM), k_idx)),
        scratch_shapes=[pltpu.VMEM((hp, t, 128), F32), pltpu.VMEM((hp, t, V_DIM), F32)],
        semantics=("parallel", "arbitrary", "arbitrary"), args=(q, k, v, do, lse, delta))


def _mla_pre_bwd(proj, tab, gqa, gkva, wq, wkv, gqn, gkn, dq, dk, dv):
    n_rows = proj.shape[0]
    tm = _row_tile(n_rows)

    def body(c_ref, kr_ref, tab_ref, gqa_ref, gkva_ref, wq_ref, wkv_ref, gqn_ref, gkn_ref, dq_ref, dk_ref, dv_ref,
             dc_ref, dkr_ref, dwq_ref, dwkv_ref, dgqa_ref, dgkva_ref, dgqn_ref, dgkn_ref):
        first = pl.program_id(0) == 0
        c = c_ref[...]
        kr = kr_ref[:, 0:QK_ROPE]
        cos, sin = tab_ref[:, 0:QK_ROPE], tab_ref[:, QK_ROPE:2 * QK_ROPE]
        gqa, gkva, gqn, gkn = gqa_ref[...], gkva_ref[...], gqn_ref[...], gkn_ref[...]
        cqn, cq_h, cq_inv, ckvn, ckv_h, ckv_inv, qraw, kvraw = _mla_heads(c[:, :Q_RANK], c[:, Q_RANK:], kr, gqa, gkva,
                                                                          wq_ref, wkv_ref)
        dq_parts, dkv_parts = [], []
        dkr = jnp.zeros((tm, QK_ROPE), F32)
        dgqn = jnp.zeros((1, QK_DIM), F32)
        dgkn = jnp.zeros((1, QK_DIM), F32)
        for h in range(N_HEADS):
            _, xh, inv = _rms(qraw[:, QK_DIM * h:QK_DIM * (h + 1)], gqn)
            dqo = dq_ref[h]
            dqn = jnp.concatenate([dqo[:, :QK_NOPE], _rope_bwd(dqo[:, QK_NOPE:QK_DIM], cos, sin)], axis=1)
            dx, dg = _rms_bwd(dqn, xh, inv, gqn)
            dq_parts.append(dx)
            dgqn = dgqn + dg
            kin = jnp.concatenate([kvraw[:, 128 * h:128 * h + QK_NOPE], kr], axis=1)
            _, xh, inv = _rms(kin, gkn)
            dko = dk_ref[h]
            dkn = jnp.concatenate([dko[:, :QK_NOPE], _rope_bwd(dko[:, QK_NOPE:QK_DIM], cos, sin)], axis=1)
            dx, dg = _rms_bwd(dkn, xh, inv, gkn)
            dgkn = dgkn + dg
            dkv_parts += [dx[:, :QK_NOPE], dv_ref[h]]
            dkr = dkr + dx[:, QK_NOPE:]
        dqraw = jnp.concatenate(dq_parts, axis=1)
        dkvraw = jnp.concatenate(dkv_parts, axis=1)
        _acc(dwq_ref, _mm_tn(cqn, dqraw), first)
        _acc(dwkv_ref, _mm_tn(ckvn, dkvraw), first)
        dcq, dg1 = _rms_bwd(_mm_nt(dqraw, wq_ref[...]), cq_h, cq_inv, gqa)
        dckv, dg2 = _rms_bwd(_mm_nt(dkvraw, wkv_ref[...]), ckv_h, ckv_inv, gkva)
        _acc(dgqa_ref, dg1, first)
        _acc(dgkva_ref, dg2, first)
        _acc(dgqn_ref, dgqn, first)
        _acc(dgkn_ref, dgkn, first)
        dc_ref[...] = jnp.concatenate([dcq, dckv], axis=1)
        dkr_ref[...] = jnp.concatenate([dkr, jnp.zeros((tm, 128 - QK_ROPE), F32)], axis=1)

    full = lambda *s: pl.BlockSpec(s, lambda i: (0,) * len(s))
    head = lambda w: pl.BlockSpec((N_HEADS, tm, w), lambda i: (0, i, 0))
    return pl.pallas_call(
        body, name="mla_pre_bwd",
        out_shape=(jax.ShapeDtypeStruct((n_rows, 1024), F32), jax.ShapeDtypeStruct((n_rows, 128), F32),
                   jax.ShapeDtypeStruct((Q_RANK, Q_RANK), F32), jax.ShapeDtypeStruct((KV_RANK, 1024), F32),
                   jax.ShapeDtypeStruct((1, Q_RANK), F32), jax.ShapeDtypeStruct((1, KV_RANK), F32),
                   jax.ShapeDtypeStruct((1, QK_DIM), F32), jax.ShapeDtypeStruct((1, QK_DIM), F32)),
        grid=(n_rows // tm,),
        in_specs=[pl.BlockSpec((tm, 1024), lambda i: (i, COL_CQKV // 1024)),
                  pl.BlockSpec((tm, 128), lambda i: (i, COL_KR // 128)),
                  pl.BlockSpec((tm, 128), lambda i: (i, 0)),
                  full(1, Q_RANK), full(1, KV_RANK), full(Q_RANK, Q_RANK), full(KV_RANK, 1024), full(1, QK_DIM), full(1, QK_DIM),
                  head(128), head(128), head(V_DIM)],
        out_specs=(pl.BlockSpec((tm, 1024), lambda i: (i, 0)), pl.BlockSpec((tm, 128), lambda i: (i, 0)),
                   full(Q_RANK, Q_RANK), full(KV_RANK, 1024), full(1, Q_RANK), full(1, KV_RANK), full(1, QK_DIM), full(1, QK_DIM)),
        compiler_params=_cp("arbitrary"),
    )(proj, proj, tab, gqa, gkva, wq, wkv, gqn, gkn, dq, dk, dv)


def _pool_bwd_a(proj, dy_pool, wg, scale, w_up):
    n_rows = proj.shape[0]
    tm = _row_tile(n_rows)

    def body(uz_ref, halo_ref, dy_ref, wg_ref, sc_ref, wup_ref, dmz_ref, dwg_ref, dsc_ref, dwup_ref):
        first = pl.program_id(0) == 0
        uz = uz_ref[...]
        u, z = uz[:, :POOL_WIDTH], uz[:, POOL_WIDTH:]
        scale_v = sc_ref[...]
        mixed, yg, ys, sig = _pool_math(u, halo_ref[...], z, wg_ref, scale_v, pl.program_id(0) * tm)
        sp = z * sig
        dy = dy_ref[...]
        da = _mm_nt(dy, wup_ref[...])
        _acc(dwup_ref, _mm_tn(ys * sp, dy), first)
        dys = da * sp
        _acc(dsc_ref, jnp.sum(dys * yg, axis=0, keepdims=True), first)
        dyg = dys * scale_v
        for g in range(4):
            cols = slice(g * 128, (g + 1) * 128)
            dmz_ref[:, cols] = _mm_nt(dyg[:, cols], wg_ref[g])
            _acc(dwg_ref.at[g], _mm_tn(mixed[:, cols], dyg[:, cols]), first)
        dmz_ref[:, POOL_WIDTH:] = da * ys * _dsilu(z, sig)

    return pl.pallas_call(
        body, name="pool_bwd_a",
        out_shape=(jax.ShapeDtypeStruct((n_rows, 1024), F32), jax.ShapeDtypeStruct((4, 128, 128), F32),
                   jax.ShapeDtypeStruct((1, POOL_WIDTH), F32), jax.ShapeDtypeStruct((POOL_WIDTH, D_MODEL), F32)),
        grid=(n_rows // tm,),
        in_specs=[pl.BlockSpec((tm, 1024), lambda i: (i, 0)), pl.BlockSpec((HALO, POOL_WIDTH), _halo_above(tm)),
                  pl.BlockSpec((tm, D_MODEL), lambda i: (i, 0)),
                  pl.BlockSpec((4, 128, 128), lambda i: (0, 0, 0)), pl.BlockSpec((1, POOL_WIDTH), lambda i: (0, 0)),
                  pl.BlockSpec((POOL_WIDTH, D_MODEL), lambda i: (0, 0))],
        out_specs=(pl.BlockSpec((tm, 1024), lambda i: (i, 0)), pl.BlockSpec((4, 128, 128), lambda i: (0, 0, 0)),
                   pl.BlockSpec((1, POOL_WIDTH), lambda i: (0, 0)), pl.BlockSpec((POOL_WIDTH, D_MODEL), lambda i: (0, 0))),
        compiler_params=_cp("arbitrary"),
    )(proj, proj, dy_pool, wg, scale, w_up)


def _pool_bwd_b(dmz):
    n_rows = dmz.shape[0]
    tm = _row_tile(n_rows)
    n_tiles = n_rows // tm
    n_ext = tm + HALO

    def body(dmz_ref, halo_ref, o_ref):
        i = pl.program_id(0)
        v = dmz_ref[...]
        dm = v[:, :POOL_WIDTH]
        halo = jnp.where(i == n_tiles - 1, 0.0, halo_ref[...])
        ext = jnp.concatenate([dm, halo], axis=0)
        t1 = (i * tm + lax.broadcasted_iota(jnp.int32, (n_ext, 1), 0) - (PAD_FRONT - 1)).astype(F32)
        du = []
        for g, w in enumerate(POOL_WINDOWS):
            cols = slice(g * 128, (g + 1) * 128)
            a = ext[:, cols] / jnp.clip(t1, 1.0, float(w))
            k = 1
            while k < w:
                a = a + pltpu.roll(a, n_ext - k, 0)
                k *= 2
            du.append(a[:tm, :] - dm[:, cols])
        o_ref[...] = jnp.concatenate(du + [v[:, POOL_WIDTH:]], axis=1)

    last_halo = n_rows // HALO - 1
    return pl.pallas_call(
        body, name="pool_bwd_b", out_shape=jax.ShapeDtypeStruct((n_rows, 1024), F32), grid=(n_tiles,),
        in_specs=[pl.BlockSpec((tm, 1024), lambda i: (i, 0)),
                  pl.BlockSpec((HALO, POOL_WIDTH), lambda i: (jnp.minimum((i + 1) * (tm // HALO), last_halo), 0))],
        out_specs=pl.BlockSpec((tm, 1024), lambda i: (i, 0)),
        compiler_params=_cp("parallel"),
    )(dmz, dmz)


def _norm_proj_bwd(d_uz, d_cqkv, d_gz, d_kr, w_in, h_res, gain, dh_out):
    n_rows = h_res.shape[0]
    tm = _row_tile(n_rows)
    pieces = ((COL_UZ, 1024), (COL_CQKV, 1024), (COL_GATES, 2560), (COL_KR, 128))

    def body(a_ref, b_ref, c_ref, d_ref, w_ref, x_ref, g_ref, dho_ref, dhi_ref, dg_ref):
        i = pl.program_id(0)
        dh = None
        for ref, (c0, wd) in zip((a_ref, b_ref, c_ref, d_ref), pieces):
            part = _mm_nt(ref[...], w_ref[:, c0:c0 + wd])
            dh = part if dh is None else dh + part
        g = g_ref[...]
        _, xh, inv = _rms(x_ref[...], g)
        dx, dg = _rms_bwd(dh, xh, inv, g)
        _acc(dg_ref, dg, i == 0)
        row = i * tm + lax.broadcasted_iota(jnp.int32, (tm, 1), 0)
        dhi_ref[...] = jnp.where(row >= PAD_FRONT, dho_ref[...] + dx, 0.0)

    row = lambda w: pl.BlockSpec((tm, w), lambda i: (i, 0))
    return pl.pallas_call(
        body, name="norm_proj_bwd",
        out_shape=(jax.ShapeDtypeStruct((n_rows, D_MODEL), F32), jax.ShapeDtypeStruct((1, D_MODEL), F32)),
        grid=(n_rows // tm,),
        in_specs=[row(1024), row(1024), row(2560), row(128), pl.BlockSpec((D_MODEL, P_IN), lambda i: (0, 0)),
                  row(D_MODEL), pl.BlockSpec((1, D_MODEL), lambda i: (0, 0)), row(D_MODEL)],
        out_specs=(row(D_MODEL), pl.BlockSpec((1, D_MODEL), lambda i: (0, 0))),
        compiler_params=_cp("arbitrary"),
    )(d_uz, d_cqkv, d_gz, d_kr, w_in, h_res, gain, dh_out)


def _weight_grad(name, a, b):
    k_rows, m = a.shape
    n = b.shape[1]
    tk = 640 if k_rows % 640 == 0 else 128
    tn = 512 if n % 512 == 0 else 128

    def body(a_ref, b_ref, o_ref):
        _acc(o_ref, _mm_tn(a_ref[...], b_ref[...]), pl.program_id(1) == 0)

    return pl.pallas_call(
        body, name=name, out_shape=jax.ShapeDtypeStruct((m, n), F32), grid=(n // tn, k_rows // tk),
        in_specs=[pl.BlockSpec((tk, m), lambda j, k: (k, 0)), pl.BlockSpec((tk, tn), lambda j, k: (k, j))],
        out_specs=pl.BlockSpec((m, tn), lambda j, k: (0, j)),
        compiler_params=_cp("parallel", "arbitrary"),
    )(a, b)


def _adamw(name, w, m, v, gbufs):
    r, c = w.shape
    depth = len(gbufs)
    r_l = r // depth
    tr = r_l
    if r_l * c * 4 > (1 << 20):
        tr = 256 if r_l % 256 == 0 else 128
    assert r_l * depth == r and r_l % tr == 0, (name, r, depth, tr)
    n_t = r_l // tr

    def body(w_ref, m_ref, v_ref, *refs):
        g_refs, (go_ref, d_ref, mo_ref, vo_ref) = refs[:depth], refs[depth:]

        def update(g_ref):
            g = g_ref[0].astype(F32)
            for s in range(1, N_DEV):
                g = g + g_ref[s].astype(F32)
            go_ref[...] = g
            m_new = ADAM_B1 * m_ref[...] + (1.0 - ADAM_B1) * g
            v_new = ADAM_B2 * v_ref[...] + (1.0 - ADAM_B2) * (g * g)
            mo_ref[...] = m_new
            vo_ref[...] = v_new
            m_hat = m_new / (1.0 - ADAM_B1 ** ADAM_STEP)
            v_hat = v_new / (1.0 - ADAM_B2 ** ADAM_STEP)
            d_ref[...] = -ADAM_LR * (m_hat / (jnp.sqrt(v_hat) + ADAM_EPS) + ADAM_WD * w_ref[...])

        for j in range(depth):
            pl.when(pl.program_id(0) == j)(lambda j=j: update(g_refs[j]))

    def g_spec(j):
        return pl.BlockSpec((N_DEV, tr, c), lambda l, i: (0, jnp.where(l == j, i, jnp.where(l < j, 0, n_t - 1)), 0))

    spec = pl.BlockSpec((tr, c), lambda l, i: (l * n_t + i, 0))
    sds = jax.ShapeDtypeStruct((r, c), F32)
    return pl.pallas_call(
        body, name=name, out_shape=(sds, sds, sds, sds), grid=(depth, n_t),
        in_specs=[spec, spec, spec] + [g_spec(j) for j in range(depth)],
        out_specs=(spec, spec, spec, spec),
        compiler_params=_cp("arbitrary", "arbitrary"),
    )(w, m, v, *gbufs)


SMALL = ("norm_gain", "pool_w_group", "pool_scale", "q_a_norm_gain", "kv_a_norm_gain", "q_norm_gain", "k_norm_gain")


def _pack_small(parts):
    rows = []
    for p in parts:
        flat = p.reshape(-1)
        pad = (-flat.shape[0]) % 1024
        if pad:
            flat = jnp.concatenate([flat, jnp.zeros((pad,), F32)])
        rows.append(flat.reshape(-1, 128))
    n_rows = sum(r.shape[0] for r in rows)
    tail = (-n_rows) % 256
    if tail:
        rows.append(jnp.zeros((tail, 128), F32))
    return jnp.concatenate(rows, axis=0)


def _unpack_small(packed, shapes):
    out, r0 = [], 0
    for shp in shapes:
        n = math.prod(shp)
        n_rows = (n + 1023) // 1024 * 8
        out.append(packed[r0:r0 + n_rows].reshape(-1)[:n].reshape(shp))
        r0 += n_rows
    return out


def kernel(x, positions, meta_tokens, norm_gain, w_in, pool_w_group, pool_scale, pool_w_up, q_a_norm_gain, kv_a_norm_gain, w_q_b, w_kv_b, q_norm_gain, k_norm_gain, mla_w_up, w_out, loss_target, m_meta_tokens, m_norm_gain, m_w_in, m_pool_w_group, m_pool_scale, m_pool_w_up, m_q_a_norm_gain, m_kv_a_norm_gain, m_w_q_b, m_w_kv_b, m_q_norm_gain, m_k_norm_gain, m_mla_w_up, m_w_out, v_meta_tokens, v_norm_gain, v_w_in, v_pool_w_group, v_pool_scale, v_pool_w_up, v_q_a_norm_gain, v_kv_a_norm_gain, v_w_q_b, v_w_kv_b, v_q_norm_gain, v_k_norm_gain, v_mla_w_up, v_w_out):
    x2, target = x[0], loss_target[0]

    big = dict(w_in=w_in, pool_w_up=pool_w_up, w_q_b=w_q_b, w_kv_b=w_kv_b, mla_w_up=mla_w_up, w_out=w_out)
    names_big = list(big)
    shards = {n: _cast_bf16("cast_" + n, w) for n, w in big.items()}

    def layer_shards(l):
        return [shards[n][l] for n in names_big]

    def repack(gathered):
        g_in, g_pup, g_qb, g_kvb, g_mup, g_out = gathered
        return dict(w_in=_repack_w_in(g_in),
                    pool_up=_repack_cols("repack_pool_up", g_pup[:, None], BF16)[0],
                    q_b=_repack_rows("repack_q_b", g_qb[:, None])[0],
                    kv_b=_repack_cols("repack_kv_b", g_kvb[:, None], BF16)[0],
                    mla_up=_repack_cols("repack_mla_up", g_mup[:, None], BF16)[0],
                    out=_repack_rows("repack_out", g_out[:, None])[0])

    first = _exchange("gather_first", [meta_tokens] + layer_shards(0), [])
    weights = [repack(first[1:])]

    h = _embed(x2, first[0])
    pos = jnp.concatenate([jnp.zeros((PAD_FRONT,), jnp.int32), jnp.arange(N_META, dtype=jnp.int32),
                           positions[0] + N_META]).astype(F32).reshape(-1, 1)
    half = QK_ROPE // 2
    inv_freq = ROPE_THETA ** (-jnp.arange(half, dtype=F32) / half)
    tab = _rope_table(pos, jnp.tile(inv_freq, 128 // half).reshape(1, 128))

    row = lambda a, l: a[l].reshape(1, -1)
    saved = []
    for l in range(DEPTH):
        w = weights[l]
        hb, proj = _norm_proj(h, row(norm_gain, l), w["w_in"])
        y_pool = _pool_fwd(proj, pool_w_group[l], row(pool_scale, l), w["pool_up"])
        q, k, v = _mla_pre_fwd(proj, tab, row(q_a_norm_gain, l), row(kv_a_norm_gain, l), w["q_b"], w["kv_b"],
                               row(q_norm_gain, l), row(k_norm_gain, l))
        ex = _Exchange(layer_shards(l + 1), []) if l + 1 < DEPTH else None
        res = _attn_fwd(q, k, v, ex)
        o, lse = res[0], res[1]
        if ex is not None:
            weights.append(repack(res[2:]))
        y_mla, h_next = _mla_post_fwd(h, proj, o, y_pool, w["mla_up"], w["out"])
        saved.append((h, hb, proj, y_pool, q, k, v, o, lse, y_mla))
        h = h_next

    loss_part, dh = _loss_head(h, target)
    loss = lax.psum(loss_part[0, 0], MESH_AXES)

    grads = {n: [None] * DEPTH for n in SMALL}
    pending = None
    received = [None] * DEPTH
    for l in reversed(range(DEPTH)):
        w = weights[l]
        h_l, hb, proj, y_pool, q, k, v, o, lse, y_mla = saved[l]
        d_gz, dy_pool, do, delta, g_out, g_mla_up = _mla_post_bwd(dh, proj, o, y_pool, y_mla, w["mla_up"], w["out"])
        ex = _Exchange([], pending) if pending is not None else None
        res = _attn_bwd(q, k, v, do, lse, delta, ex)
        dq, dk, dv = res[0], res[1], res[2]
        if ex is not None:
            received[l + 1] = res[3:]
        (d_cqkv, d_kr, g_q_b, g_kv_b, grads["q_a_norm_gain"][l], grads["kv_a_norm_gain"][l],
         grads["q_norm_gain"][l], grads["k_norm_gain"][l]) = _mla_pre_bwd(
            proj, tab, row(q_a_norm_gain, l), row(kv_a_norm_gain, l), w["q_b"], w["kv_b"], row(q_norm_gain, l),
            row(k_norm_gain, l), dq, dk, dv)
        dmz, grads["pool_w_group"][l], grads["pool_scale"][l], g_pool_up = _pool_bwd_a(
            proj, dy_pool, pool_w_group[l], row(pool_scale, l), w["pool_up"])
        d_uz = _pool_bwd_b(dmz)
        g_in = [_weight_grad("dw_in", hb, d) for d in (d_uz, d_cqkv, d_gz, d_kr)]
        dh, grads["norm_gain"][l] = _norm_proj_bwd(d_uz, d_cqkv, d_gz, d_kr, w["w_in"], h_l, row(norm_gain, l), dh)
        pending = [_unpack_w_in(g_in),
                   _unpack_cols("unpack_pool_up", g_pool_up[None], 128, BF16)[:, 0],
                   _unpack_rows("unpack_q_b", g_q_b[None], Q_RANK // N_DEV)[:, 0],
                   _unpack_cols("unpack_kv_b", g_kv_b[None], 128, BF16)[:, 0],
                   _unpack_cols("unpack_mla_up", g_mla_up[None], 128, BF16)[:, 0],
                   _unpack_rows("unpack_out", g_out[None], D_MODEL // N_DEV)[:, 0]]

    grad_x = dh[HEAD_ROWS:][None]
    d_meta = dh[PAD_FRONT:HEAD_ROWS]

    small_shapes = [a.shape for a in (norm_gain, pool_w_group, pool_scale, q_a_norm_gain, kv_a_norm_gain, q_norm_gain, k_norm_gain)]
    small_grad = _pack_small([jnp.stack(grads[n], axis=0).reshape(s) for n, s in zip(SMALL, small_shapes)])
    meta_parts = _unpack_cols("unpack_meta", d_meta[None], 128, F32)[:, 0]
    last = _exchange("exchange_last", [small_grad], [meta_parts] + pending)
    small_buf = last[0]
    received[0] = last[2:]
    sharded = ["meta_tokens"] + names_big
    shard_bufs = {"meta_tokens": [last[1]]}
    for j, n in enumerate(names_big):
        shard_bufs[n] = [received[l][j] for l in range(DEPTH)]

    given = dict(meta_tokens=(meta_tokens, m_meta_tokens, v_meta_tokens), w_in=(w_in, m_w_in, v_w_in),
                 pool_w_up=(pool_w_up, m_pool_w_up, v_pool_w_up), w_q_b=(w_q_b, m_w_q_b, v_w_q_b),
                 w_kv_b=(w_kv_b, m_w_kv_b, v_w_kv_b), mla_w_up=(mla_w_up, m_mla_w_up, v_mla_w_up),
                 w_out=(w_out, m_w_out, v_w_out))
    result = {}
    for n in sharded:
        w, m, v = given[n]
        cols = w.shape[-1]
        res = _adamw("adamw_" + n, w.reshape(-1, cols), m.reshape(-1, cols), v.reshape(-1, cols), shard_bufs[n])
        result[n] = [r.reshape(w.shape) for r in res]
    small_w = (norm_gain, pool_w_group, pool_scale, q_a_norm_gain, kv_a_norm_gain, q_norm_gain, k_norm_gain)
    small_m = (m_norm_gain, m_pool_w_group, m_pool_scale, m_q_a_norm_gain, m_kv_a_norm_gain, m_q_norm_gain, m_k_norm_gain)
    small_v = (v_norm_gain, v_pool_w_group, v_pool_scale, v_q_a_norm_gain, v_kv_a_norm_gain, v_q_norm_gain, v_k_norm_gain)
    res = _adamw("adamw_small", _pack_small(small_w), _pack_small(small_m), _pack_small(small_v), [small_buf])
    small_res = [_unpack_small(r, small_shapes) for r in res]
    for j, n in enumerate(SMALL):
        result[n] = [small_res[kind][j] for kind in range(4)]

    order = ("meta_tokens", "norm_gain", "w_in", "pool_w_group", "pool_scale", "pool_w_up", "q_a_norm_gain",
             "kv_a_norm_gain", "w_q_b", "w_kv_b", "q_norm_gain", "k_norm_gain", "mla_w_up", "w_out")
    outs = [loss, grad_x]
    for kind in range(4):
        outs += [result[n][kind] for n in order]
    return tuple(outs)
```

```python
import math

import jax
import jax.numpy as jnp
from jax import lax
from jax.experimental import pallas as pl
from jax.experimental.pallas import tpu as pltpu

F32 = jnp.float32
BF16 = jnp.bfloat16

D_MODEL = 1024
DEPTH = 4
N_META = 16
PAD_FRONT = 112
HEAD_ROWS = PAD_FRONT + N_META
POOL_WIDTH = 512
POOL_WINDOWS = (2, 4, 8, 16)
POOL_GROUP_DIM = 128
HALO = 16
N_HEADS = 8
HEADS_PER_STEP = 2
HEADS_PER_STEP_FWD = 4
QK_NOPE = 64
QK_ROPE = 32
QK_DIM = 96
V_DIM = 64
MLA_WIDTH = 512
Q_RANK = 768
KV_RANK = 256
ROPE_THETA = 10000.0
NORM_EPS = 1e-6
MASK_VALUE = -1e30
D_IN = 4640
N_DEV = 8
IN_SHARD = D_IN // N_DEV

P_IN = 4736
IN_SEGMENTS = ((0, 2048, 0), (2048, 2080, 4608), (2080, 2592, 4096), (2592, 3616, 2048), (3616, 4640, 3072))
COL_UZ, COL_CQKV, COL_GATES, COL_ZMLA, COL_KR = 0, 1024, 2048, 4096, 4608

ADAM_LR = 0.001
ADAM_B1 = 0.9
ADAM_B2 = 0.999
ADAM_EPS = 1e-08
ADAM_WD = 0.01
ADAM_STEP = 10

VMEM_LIMIT = 56 * 1024 * 1024
MESH_AXES = ("x", "y", "c")


def _cp(*sem):
    return pltpu.CompilerParams(dimension_semantics=sem, vmem_limit_bytes=VMEM_LIMIT)


def _row_tile(n_rows):
    return 320 if n_rows % 320 == 0 else 128


def _attn_tile(n_rows):
    return 640 if (n_rows % 640 == 0 and n_rows > 640) else 128


def _mm(a, b):
    return jnp.dot(a.astype(BF16), b.astype(BF16), preferred_element_type=F32)


def _mm_nt(a, b):
    return lax.dot_general(a.astype(BF16), b.astype(BF16), (((1,), (1,)), ((), ())), preferred_element_type=F32)


def _mm_tn(a, b):
    return lax.dot_general(a.astype(BF16), b.astype(BF16), (((0,), (0,)), ((), ())), preferred_element_type=F32)


def _sigmoid(z):
    return 1.0 / (1.0 + jnp.exp(-z))


def _rms(x, g):
    inv = lax.rsqrt(jnp.mean(x * x, axis=-1, keepdims=True) + NORM_EPS)
    xh = x * inv
    return xh * g, xh, inv


def _rms_bwd(dy, xh, inv, g):
    dg = jnp.sum(dy * xh, axis=0, keepdims=True)
    dxh = dy * g
    dx = inv * (dxh - xh * jnp.mean(dxh * xh, axis=-1, keepdims=True))
    return dx, dg


def _rope(x, tab):
    return x * tab[:, 0:128] + pltpu.roll(x, 112, 1) * tab[:, 128:256] + pltpu.roll(x, 16, 1) * tab[:, 256:384]


def _rope_bwd(dy, tab):
    return dy * tab[:, 0:128] + pltpu.roll(dy * tab[:, 128:256], 16, 1) + pltpu.roll(dy * tab[:, 256:384], 112, 1)


def _head_rms(x, g):
    inv = lax.rsqrt(jnp.sum(x * x, axis=-1, keepdims=True) * (1.0 / QK_DIM) + NORM_EPS)
    xh = x * inv
    return xh * g, xh, inv


def _head_rms_bwd(dy, xh, inv, g):
    dg = jnp.sum(dy * xh, axis=0, keepdims=True)
    dxh = dy * g
    dx = inv * (dxh - xh * (jnp.sum(dxh * xh, axis=-1, keepdims=True) * (1.0 / QK_DIM)))
    return dx, dg


def _pad_gain(g_ref):
    return jnp.concatenate([g_ref[...], jnp.zeros((1, 128 - QK_DIM), F32)], axis=1)


class _Exchange:
    def __init__(self, gather_list, scatter_list):
        self.arrays = list(gather_list) + list(scatter_list)
        self.n_g, self.n = len(gather_list), len(self.arrays)
        self.out_shape = [jax.ShapeDtypeStruct((N_DEV,) + a.shape, a.dtype) for a in gather_list]
        self.out_shape += [jax.ShapeDtypeStruct(a.shape, a.dtype) for a in scatter_list]
        self.specs = [pl.BlockSpec(memory_space=pl.ANY)] * self.n
        self.sems = [pltpu.SemaphoreType.DMA((7 * self.n,)), pltpu.SemaphoreType.DMA((7 * self.n,)),
                     pltpu.SemaphoreType.DMA((self.n,))]

    def copies(self, ins, outs, sems):
        send_sems, recv_sems, local_sems = sems
        x, y, c = lax.axis_index("x"), lax.axis_index("y"), lax.axis_index("c")
        me = 4 * x + 2 * y + c

        def src(a, slot):
            return ins[a] if a < self.n_g else ins[a].at[slot]

        out = [pltpu.make_async_copy(src(a, me), outs[a].at[me], local_sems.at[a]) for a in range(self.n)]
        for k in range(1, N_DEV):
            px = 1 - x if k & 4 else x
            py = 1 - y if k & 2 else y
            pc = 1 - c if k & 1 else c
            peer = 4 * px + 2 * py + pc
            for a in range(self.n):
                out.append(pltpu.make_async_remote_copy(
                    src_ref=src(a, peer), dst_ref=outs[a].at[me],
                    send_sem=send_sems.at[a * 7 + k - 1], recv_sem=recv_sems.at[a * 7 + k - 1],
                    device_id=(px, py, pc), device_id_type=pl.DeviceIdType.MESH))
        return out


def _exchange(name, gather_list, scatter_list):
    ex = _Exchange(gather_list, scatter_list)
    n = ex.n

    def body(*refs):
        copies = ex.copies(refs[:n], refs[n:2 * n], refs[2 * n:])
        for cp in copies:
            cp.start()
        for cp in copies:
            cp.wait()

    return pl.pallas_call(
        body, name=name, out_shape=ex.out_shape, in_specs=ex.specs, out_specs=ex.specs, scratch_shapes=ex.sems,
        compiler_params=pltpu.CompilerParams(has_side_effects=True),
    )(*ex.arrays)


def _call_carrying(ex, body, name, out_shape, grid, in_specs, out_specs, scratch_shapes, semantics, args):
    if ex is None:
        return pl.pallas_call(body, name=name, out_shape=out_shape, grid=grid, in_specs=in_specs, out_specs=out_specs,
                              scratch_shapes=scratch_shapes, compiler_params=_cp(*semantics))(*args)
    n, n_in, n_out, n_scr = ex.n, len(in_specs), len(out_shape), len(scratch_shapes)

    def carrying(*refs):
        ins, refs = refs[:n_in], refs[n_in:]
        c_in, refs = refs[:n], refs[n:]
        outs, refs = refs[:n_out], refs[n_out:]
        c_out, refs = refs[:n], refs[n:]
        scratch, sems = refs[:n_scr], refs[n_scr:]
        ids = [pl.program_id(a) for a in range(len(grid))]
        first, last = ids[0] == 0, ids[0] == grid[0] - 1
        for a in range(1, len(grid)):
            first, last = first & (ids[a] == 0), last & (ids[a] == grid[a] - 1)
        copies = ex.copies(c_in, c_out, sems)

        @pl.when(first)
        def _():
            for cp in copies:
                cp.start()

        body(*ins, *outs, *scratch)

        @pl.when(last)
        def _():
            for cp in copies:
                cp.wait()

    return pl.pallas_call(
        carrying, name=name + "_x", out_shape=list(out_shape) + ex.out_shape, grid=grid,
        in_specs=list(in_specs) + ex.specs, out_specs=list(out_specs) + ex.specs,
        scratch_shapes=list(scratch_shapes) + ex.sems, compiler_params=_cp(*["arbitrary"] * len(grid)),
    )(*args, *ex.arrays)


def _in_pieces():
    runs = []
    for lo, hi, dst in sorted(IN_SEGMENTS, key=lambda s: s[2]):
        col = lo
        while col < hi:
            d = col // IN_SHARD
            end = min(hi, (d + 1) * IN_SHARD)
            runs.append((d, col - d * IN_SHARD, end - d * IN_SHARD))
            col = end
    return runs


def _cast_bf16(name, w):
    shape = w.shape
    w2 = w.reshape(-1, shape[-1])
    r, c = w2.shape
    tr = 512 if r % 512 == 0 else r

    def body(x_ref, o_ref):
        o_ref[...] = x_ref[...].astype(BF16)

    spec = pl.BlockSpec((tr, c), lambda i: (i, 0))
    out = pl.pallas_call(body, name=name, out_shape=jax.ShapeDtypeStruct((r, c), BF16), grid=(r // tr,),
                         in_specs=[spec], out_specs=spec, compiler_params=_cp("parallel"))(w2)
    return out.reshape(shape)


def _repack_w_in(gathered):
    tr = 256
    runs = _in_pieces()

    def body(x_ref, o_ref):
        parts = [x_ref[d, :, a:b].astype(F32) for d, a, b in runs]
        parts.append(jnp.zeros((tr, P_IN - D_IN), F32))
        o_ref[...] = jnp.concatenate(parts, axis=1).astype(BF16)

    return pl.pallas_call(
        body, name="repack_w_in", out_shape=jax.ShapeDtypeStruct((D_MODEL, P_IN), BF16), grid=(D_MODEL // tr,),
        in_specs=[pl.BlockSpec((N_DEV, tr, IN_SHARD), lambda i: (0, i, 0))],
        out_specs=pl.BlockSpec((tr, P_IN), lambda i: (i, 0)),
        compiler_params=_cp("parallel"),
    )(gathered)


def _unpack_w_in(parts):
    tr = 256
    runs = _in_pieces()
    widths = [p.shape[-1] for p in parts]

    def body(a_ref, b_ref, c_ref, d_ref, o_ref):
        full = jnp.concatenate([a_ref[...], b_ref[...], c_ref[...], d_ref[...]], axis=1)
        col = 0
        per_dev = [[] for _ in range(N_DEV)]
        for d, a, b in runs:
            per_dev[d].append((a, full[:, col:col + (b - a)]))
            col += b - a
        for d in range(N_DEV):
            pieces = [v for _, v in sorted(per_dev[d], key=lambda av: av[0])]
            o_ref[d] = (jnp.concatenate(pieces, axis=1) if len(pieces) > 1 else pieces[0]).astype(BF16)

    return pl.pallas_call(
        body, name="unpack_w_in", out_shape=jax.ShapeDtypeStruct((N_DEV, D_MODEL, IN_SHARD), BF16), grid=(D_MODEL // tr,),
        in_specs=[pl.BlockSpec((tr, w), lambda i: (i, 0)) for w in widths],
        out_specs=pl.BlockSpec((N_DEV, tr, IN_SHARD), lambda i: (0, i, 0)),
        compiler_params=_cp("parallel"),
    )(*parts)


def _repack_cols(name, gathered, dtype):
    _, depth, k, n = gathered.shape

    def body(x_ref, o_ref):
        o_ref[0] = jnp.concatenate([x_ref[d, 0] for d in range(N_DEV)], axis=1).astype(dtype)

    return pl.pallas_call(
        body, name=name, out_shape=jax.ShapeDtypeStruct((depth, k, N_DEV * n), dtype), grid=(depth,),
        in_specs=[pl.BlockSpec((N_DEV, 1, k, n), lambda l: (0, l, 0, 0))],
        out_specs=pl.BlockSpec((1, k, N_DEV * n), lambda l: (l, 0, 0)),
        compiler_params=_cp("parallel"),
    )(gathered)


def _unpack_cols(name, full, n, dtype):
    depth, k, _ = full.shape

    def body(x_ref, o_ref):
        for d in range(N_DEV):
            o_ref[d, 0] = x_ref[0, :, d * n:(d + 1) * n].astype(dtype)

    return pl.pallas_call(
        body, name=name, out_shape=jax.ShapeDtypeStruct((N_DEV, depth, k, n), dtype), grid=(depth,),
        in_specs=[pl.BlockSpec((1, k, N_DEV * n), lambda l: (l, 0, 0))],
        out_specs=pl.BlockSpec((N_DEV, 1, k, n), lambda l: (0, l, 0, 0)),
        compiler_params=_cp("parallel"),
    )(full)


def _repack_rows(name, gathered):
    _, depth, r, n = gathered.shape

    def body(x_ref, o_ref):
        for d in range(N_DEV):
            o_ref[0, d * r:(d + 1) * r, :] = x_ref[d, 0].astype(BF16)

    return pl.pallas_call(
        body, name=name, out_shape=jax.ShapeDtypeStruct((depth, N_DEV * r, n), BF16), grid=(depth,),
        in_specs=[pl.BlockSpec((N_DEV, 1, r, n), lambda l: (0, l, 0, 0))],
        out_specs=pl.BlockSpec((1, N_DEV * r, n), lambda l: (l, 0, 0)),
        compiler_params=_cp("parallel"),
    )(gathered)


def _unpack_rows(name, full, r):
    depth, _, n = full.shape

    def body(x_ref, o_ref):
        for d in range(N_DEV):
            o_ref[d, 0] = x_ref[0, d * r:(d + 1) * r, :].astype(BF16)

    return pl.pallas_call(
        body, name=name, out_shape=jax.ShapeDtypeStruct((N_DEV, depth, r, n), BF16), grid=(depth,),
        in_specs=[pl.BlockSpec((1, N_DEV * r, n), lambda l: (l, 0, 0))],
        out_specs=pl.BlockSpec((N_DEV, 1, r, n), lambda l: (0, l, 0, 0)),
        compiler_params=_cp("parallel"),
    )(full)


def _repack_q_heads(gathered):
    r = Q_RANK // N_DEV

    def body(x_ref, o_ref):
        z = jnp.zeros((r, 128 - QK_DIM), F32)
        for d in range(N_DEV):
            x = x_ref[d].astype(F32)
            parts = []
            for h in range(N_HEADS):
                parts += [x[:, QK_DIM * h:QK_DIM * (h + 1)], z]
            o_ref[d * r:(d + 1) * r, :] = jnp.concatenate(parts, axis=1).astype(BF16)

    return pl.pallas_call(body, name="repack_q_heads", out_shape=jax.ShapeDtypeStruct((Q_RANK, 1024), BF16),
                          compiler_params=_cp())(gathered)


def _unpack_q_heads(dwq):
    r = Q_RANK // N_DEV

    def body(x_ref, o_ref):
        for d in range(N_DEV):
            x = x_ref[d * r:(d + 1) * r, :]
            o_ref[d] = jnp.concatenate([x[:, 128 * h:128 * h + QK_DIM] for h in range(N_HEADS)], axis=1).astype(BF16)

    return pl.pallas_call(body, name="unpack_q_heads", out_shape=jax.ShapeDtypeStruct((N_DEV, r, Q_RANK), BF16),
                          compiler_params=_cp())(dwq)


def _repack_kv_heads(gathered):
    def body(x_ref, wk_ref, wv_ref):
        z = jnp.zeros((KV_RANK, 128 - QK_NOPE), F32)
        for h in range(N_HEADS):
            x = x_ref[h].astype(F32)
            wk_ref[:, 128 * h:128 * (h + 1)] = jnp.concatenate([x[:, :QK_NOPE], z], axis=1).astype(BF16)
            wv_ref[:, 128 * h:128 * (h + 1)] = jnp.concatenate([x[:, QK_NOPE:], z], axis=1).astype(BF16)

    sds = jax.ShapeDtypeStruct((KV_RANK, 1024), BF16)
    return pl.pallas_call(body, name="repack_kv_heads", out_shape=(sds, sds), compiler_params=_cp())(gathered)


def _unpack_kv_heads(dwk, dwv):
    def body(k_ref, v_ref, o_ref):
        for h in range(N_HEADS):
            o_ref[h] = jnp.concatenate([k_ref[:, 128 * h:128 * h + QK_NOPE], v_ref[:, 128 * h:128 * h + V_DIM]],
                                       axis=1).astype(BF16)

    return pl.pallas_call(body, name="unpack_kv_heads", out_shape=jax.ShapeDtypeStruct((N_DEV, KV_RANK, 128), BF16),
                          compiler_params=_cp())(dwk, dwv)


def _embed(x, meta_g):
    seq = x.shape[0]
    n_blk = seq // HEAD_ROWS + 1

    def body(x_ref, m_ref, o_ref):
        i = pl.program_id(0)

        @pl.when(i == 0)
        def _():
            meta = jnp.concatenate([m_ref[d] for d in range(N_DEV)], axis=1)
            o_ref[...] = jnp.concatenate([jnp.zeros((PAD_FRONT, D_MODEL), F32), meta], axis=0)

        @pl.when(i > 0)
        def _():
            o_ref[...] = x_ref[...]

    return pl.pallas_call(
        body, name="embed", out_shape=jax.ShapeDtypeStruct((seq + HEAD_ROWS, D_MODEL), F32), grid=(n_blk,),
        in_specs=[pl.BlockSpec((HEAD_ROWS, D_MODEL), lambda i: (jnp.maximum(i - 1, 0), 0)),
                  pl.BlockSpec((N_DEV, N_META, 128), lambda i: (0, 0, 0))],
        out_specs=pl.BlockSpec((HEAD_ROWS, D_MODEL), lambda i: (i, 0)),
        compiler_params=_cp("parallel"),
    )(x, meta_g)


def _rope_table(pos_col, freq_row):
    n_rows = pos_col.shape[0]
    tr = 128

    def body(p_ref, f_ref, o_ref):
        ang = p_ref[...] * f_ref[...]
        lane = lax.broadcasted_iota(jnp.int32, ang.shape, 1)
        cosv, sinv = jnp.cos(ang), jnp.sin(ang)
        half = QK_ROPE // 2
        o_ref[:, 0:128] = jnp.where(lane < QK_NOPE, 1.0, jnp.where(lane < QK_DIM, cosv, 0.0))
        o_ref[:, 128:256] = jnp.where((lane >= QK_NOPE) & (lane < QK_NOPE + half), -sinv, 0.0)
        o_ref[:, 256:384] = jnp.where((lane >= QK_NOPE + half) & (lane < QK_DIM), sinv, 0.0)

    return pl.pallas_call(
        body, name="rope_table", out_shape=jax.ShapeDtypeStruct((n_rows, 384), F32), grid=(n_rows // tr,),
        in_specs=[pl.BlockSpec((tr, 1), lambda i: (i, 0)), pl.BlockSpec((1, 128), lambda i: (0, 0))],
        out_specs=pl.BlockSpec((tr, 384), lambda i: (i, 0)),
        compiler_params=_cp("parallel"),
    )(pos_col, freq_row)


def _loss_head(h, target):
    n_rows = h.shape[0]
    n_blk = n_rows // HEAD_ROWS

    def body(h_ref, t_ref, loss_ref, dh_ref):
        i = pl.program_id(0)

        @pl.when(i == 0)
        def _():
            loss_ref[...] = jnp.zeros(loss_ref.shape, F32)
            dh_ref[...] = jnp.zeros(dh_ref.shape, F32)

        @pl.when(i > 0)
        def _():
            err = h_ref[...] - t_ref[...]
            dh_ref[...] = err * (1.0 / D_MODEL)
            loss_ref[...] += 0.5 * jnp.sum(jnp.mean(err * err, axis=-1, keepdims=True))

    return pl.pallas_call(
        body, name="loss_head",
        out_shape=(jax.ShapeDtypeStruct((8, 128), F32), jax.ShapeDtypeStruct((n_rows, D_MODEL), F32)),
        grid=(n_blk,),
        in_specs=[pl.BlockSpec((HEAD_ROWS, D_MODEL), lambda i: (i, 0)),
                  pl.BlockSpec((HEAD_ROWS, D_MODEL), lambda i: (jnp.maximum(i - 1, 0), 0))],
        out_specs=(pl.BlockSpec((8, 128), lambda i: (0, 0)), pl.BlockSpec((HEAD_ROWS, D_MODEL), lambda i: (i, 0))),
        compiler_params=_cp("arbitrary"),
    )(h, target)


PROJ_CHUNKS = (0, 1024, 2048, 3072, 4096, P_IN)


def _norm_proj(h_res, gain, w_in):
    n_rows = h_res.shape[0]
    tm = _row_tile(n_rows)

    def body(x_ref, g_ref, w_ref, hb_ref, p_ref):
        h, _, _ = _rms(x_ref[...], g_ref[...])
        hb = h.astype(BF16)
        hb_ref[...] = hb
        for c0, c1 in zip(PROJ_CHUNKS[:-1], PROJ_CHUNKS[1:]):
            p_ref[:, c0:c1] = jnp.dot(hb, w_ref[:, c0:c1], preferred_element_type=F32)

    return pl.pallas_call(
        body, name="norm_proj",
        out_shape=(jax.ShapeDtypeStruct((n_rows, D_MODEL), BF16), jax.ShapeDtypeStruct((n_rows, P_IN), F32)),
        grid=(n_rows // tm,),
        in_specs=[pl.BlockSpec((tm, D_MODEL), lambda i: (i, 0)), pl.BlockSpec((1, D_MODEL), lambda i: (0, 0)),
                  pl.BlockSpec((D_MODEL, P_IN), lambda i: (0, 0))],
        out_specs=(pl.BlockSpec((tm, D_MODEL), lambda i: (i, 0)), pl.BlockSpec((tm, P_IN), lambda i: (i, 0))),
        compiler_params=_cp("parallel"),
    )(h_res, gain, w_in)


def _pool_math(u, halo, z, wg_ref, scale, row0):
    tm = u.shape[0]
    ext = jnp.concatenate([halo, u], axis=0)
    t1 = (row0 + lax.broadcasted_iota(jnp.int32, (tm, 1), 0) - (PAD_FRONT - 1)).astype(F32)
    mixed, yg = [], []
    for g, w in enumerate(POOL_WINDOWS):
        a = ext[:, g * 128:(g + 1) * 128]
        k = 1
        while k < w:
            a = a + pltpu.roll(a, k, 0)
            k *= 2
        cnt = jnp.clip(t1, 1.0, float(w))
        mg = a[HALO:, :] / cnt - u[:, g * 128:(g + 1) * 128]
        mixed.append(mg)
        yg.append(_mm(mg, wg_ref[g]))
    mixed = jnp.concatenate(mixed, axis=1)
    yg = jnp.concatenate(yg, axis=1)
    ys = yg * scale
    sig = _sigmoid(z)
    return mixed, yg, ys, sig


def _halo_above(tm):
    return lambda i: (jnp.maximum(i * (tm // HALO) - 1, 0), 0)


def _pool_fwd(proj, wg, scale, w_up):
    n_rows = proj.shape[0]
    tm = _row_tile(n_rows)

    def body(uz_ref, halo_ref, wg_ref, sc_ref, wup_ref, y_ref):
        uz = uz_ref[...]
        u, z = uz[:, :POOL_WIDTH], uz[:, POOL_WIDTH:]
        _, _, ys, sig = _pool_math(u, halo_ref[...], z, wg_ref, sc_ref[...], pl.program_id(0) * tm)
        y_ref[...] = _mm(ys * (z * sig), wup_ref[...])

    return pl.pallas_call(
        body, name="pool_fwd", out_shape=jax.ShapeDtypeStruct((n_rows, D_MODEL), F32), grid=(n_rows // tm,),
        in_specs=[pl.BlockSpec((tm, 1024), lambda i: (i, 0)), pl.BlockSpec((HALO, POOL_WIDTH), _halo_above(tm)),
                  pl.BlockSpec((4, 128, 128), lambda i: (0, 0, 0)), pl.BlockSpec((1, POOL_WIDTH), lambda i: (0, 0)),
                  pl.BlockSpec((POOL_WIDTH, D_MODEL), lambda i: (0, 0))],
        out_specs=pl.BlockSpec((tm, D_MODEL), lambda i: (i, 0)),
        compiler_params=_cp("parallel"),
    )(proj, proj, wg, scale, w_up)


def _mla_pre_fwd(proj, tab, gqa, gkva, wq, wk, wv, gqn, gkn):
    n_rows = proj.shape[0]
    tm = _row_tile(n_rows)

    def body(c_ref, kr_ref, tab_ref, gqa_ref, gkva_ref, wq_ref, wk_ref, wv_ref, gqn_ref, gkn_ref, q_ref, k_ref, v_ref):
        c, tab = c_ref[...], tab_ref[...]
        cqn, _, _ = _rms(c[:, :Q_RANK], gqa_ref[...])
        ckvn, _, _ = _rms(c[:, Q_RANK:], gkva_ref[...])
        qp, kp, vp = _mm(cqn, wq_ref[...]), _mm(ckvn, wk_ref[...]), _mm(ckvn, wv_ref[...])
        kr = pltpu.roll(kr_ref[...], QK_NOPE, 1)
        gqn, gkn = _pad_gain(gqn_ref), _pad_gain(gkn_ref)
        lane = lax.broadcasted_iota(jnp.int32, (tm, 128), 1)
        one = jnp.where(lane == V_DIM, 1.0, 0.0)
        for h in range(N_HEADS):
            blk = slice(128 * h, 128 * (h + 1))
            qn, _, _ = _head_rms(qp[:, blk], gqn)
            q_ref[h] = _rope(qn, tab).astype(BF16)
            kn, _, _ = _head_rms(kp[:, blk] + kr, gkn)
            k_ref[h] = _rope(kn, tab).astype(BF16)
            v_ref[h] = (vp[:, blk] + one).astype(BF16)

    full = lambda *s: pl.BlockSpec(s, lambda i: (0,) * len(s))
    head = pl.BlockSpec((N_HEADS, tm, 128), lambda i: (0, i, 0))
    sds = jax.ShapeDtypeStruct((N_HEADS, n_rows, 128), BF16)
    return pl.pallas_call(
        body, name="mla_pre_fwd", out_shape=(sds, sds, sds), grid=(n_rows // tm,),
        in_specs=[pl.BlockSpec((tm, 1024), lambda i: (i, COL_CQKV // 1024)),
                  pl.BlockSpec((tm, 128), lambda i: (i, COL_KR // 128)),
                  pl.BlockSpec((tm, 384), lambda i: (i, 0)),
                  full(1, Q_RANK), full(1, KV_RANK), full(Q_RANK, 1024), full(KV_RANK, 1024), full(KV_RANK, 1024),
                  full(1, QK_DIM), full(1, QK_DIM)],
        out_specs=(head, head, head),
        compiler_params=_cp("parallel"),
    )(proj, proj, tab, gqa, gkva, wq, wk, wv, gqn, gkn)


def _attn_mask(row0, col0, n_r, n_c):
    row = row0 + lax.broadcasted_iota(jnp.int32, (n_r, n_c), 0)
    col = col0 + lax.broadcasted_iota(jnp.int32, (n_r, n_c), 1)
    return (col <= row) & (col >= PAD_FRONT)


ATTN_SCALE = 1.0 / math.sqrt(QK_DIM)
EXP2_SCALE = ATTN_SCALE * math.log2(math.e)
LOG2_E = math.log2(math.e)


def _row_chunks(t, size):
    return [(r, min(size, t - r)) for r in range(0, t, size)]


def _attn_fwd(q, k, v, ex=None):
    n_rows = q.shape[1]
    t = _attn_tile(n_rows)
    nb = n_rows // t

    def body(q_ref, k_ref, v_ref, o_ref, lse_ref, m_sc, acc_sc):
        qi, ki = pl.program_id(1), pl.program_id(2)

        @pl.when(ki == 0)
        def _():
            m_sc[...] = jnp.full(m_sc.shape, -jnp.inf, F32)
            acc_sc[...] = jnp.zeros(acc_sc.shape, F32)

        def update(masked):
            heads = range(hp)
            s = [_mm_nt(q_ref[h], k_ref[h]) for h in heads]
            if masked:
                mask = _attn_mask(qi * t, ki * t, t, t)
                s = [jnp.where(mask, sh, MASK_VALUE) for sh in s]
            p = []
            for h in heads:
                m_prev = m_sc[h]
                m_new = jnp.maximum(m_prev, jnp.max(s[h], axis=1, keepdims=True))
                alpha = jnp.exp2((m_prev - m_new) * EXP2_SCALE)
                p.append(jnp.exp2((s[h] - m_new) * EXP2_SCALE).astype(BF16))
                acc_sc[h] = alpha * acc_sc[h]
                m_sc[h] = m_new
            for h in heads:
                acc_sc[h] += _mm(p[h], v_ref[h])

        edge = (ki == qi) | (ki == 0)

        @pl.when((ki <= qi) & edge)
        def _():
            update(True)

        @pl.when((ki < qi) & (ki > 0))
        def _():
            update(False)

        @pl.when(ki == nb - 1)
        def _():
            acc = acc_sc[...]
            denom = acc[:, :, V_DIM:V_DIM + 1]
            o_ref[...] = acc[:, :, :V_DIM] / denom
            lse_ref[...] = m_sc[...] * ATTN_SCALE + jnp.log(denom)

    hp = HEADS_PER_STEP_FWD
    kv_idx = lambda h, qi, ki: (h, jnp.minimum(ki, qi), 0)
    return _call_carrying(
        ex, body, "attn_fwd",
        out_shape=(jax.ShapeDtypeStruct((N_HEADS, n_rows, V_DIM), F32), jax.ShapeDtypeStruct((N_HEADS, n_rows, 1), F32)),
        grid=(N_HEADS // hp, nb, nb),
        in_specs=[pl.BlockSpec((hp, t, 128), lambda h, qi, ki: (h, qi, 0)), pl.BlockSpec((hp, t, 128), kv_idx),
                  pl.BlockSpec((hp, t, 128), kv_idx)],
        out_specs=(pl.BlockSpec((hp, t, V_DIM), lambda h, qi, ki: (h, qi, 0)), pl.BlockSpec((hp, t, 1), lambda h, qi, ki: (h, qi, 0))),
        scratch_shapes=[pltpu.VMEM((hp, t, 1), F32), pltpu.VMEM((hp, t, 128), F32)],
        semantics=("parallel", "parallel", "arbitrary"), args=(q, k, v))


def _merge_math(o_ref, z, gates, y_pool, y_mla_fn):
    o_cat = jnp.concatenate([o_ref[h] for h in range(N_HEADS)], axis=1)
    sig_z = _sigmoid(z)
    a_mla = o_cat * (z * sig_z)
    y_mla = y_mla_fn(a_mla)
    sgp, sgm = _sigmoid(gates[:, :D_MODEL]), _sigmoid(gates[:, D_MODEL:])
    merged = sgp * y_pool + sgm * y_mla
    return o_cat, sig_z, a_mla, y_mla, sgp, sgm, merged


def _mla_post_fwd(h_res, proj, o, y_pool, w_mla_up, w_out):
    n_rows = h_res.shape[0]
    tm = _row_tile(n_rows)

    def body(h_ref, g_ref, z_ref, o_ref, yp_ref, wup_ref, wout_ref, ymla_ref, hn_ref):
        _, _, _, y_mla, _, _, merged = _merge_math(o_ref, z_ref[...], g_ref[...], yp_ref[...],
                                                   lambda a: _mm(a, wup_ref[...]))
        ymla_ref[...] = y_mla
        hn_ref[...] = h_ref[...] + _mm(merged, wout_ref[...])

    row = lambda w, c: pl.BlockSpec((tm, w), lambda i: (i, c))
    return pl.pallas_call(
        body, name="mla_post_fwd",
        out_shape=(jax.ShapeDtypeStruct((n_rows, D_MODEL), F32), jax.ShapeDtypeStruct((n_rows, D_MODEL), F32)),
        grid=(n_rows // tm,),
        in_specs=[row(D_MODEL, 0), row(2048, COL_GATES // 2048), row(MLA_WIDTH, COL_ZMLA // MLA_WIDTH),
                  pl.BlockSpec((N_HEADS, tm, V_DIM), lambda i: (0, i, 0)), row(D_MODEL, 0),
                  pl.BlockSpec((MLA_WIDTH, D_MODEL), lambda i: (0, 0)), pl.BlockSpec((D_MODEL, D_MODEL), lambda i: (0, 0))],
        out_specs=(row(D_MODEL, 0), row(D_MODEL, 0)),
        compiler_params=_cp("parallel"),
    )(h_res, proj, proj, o, y_pool, w_mla_up, w_out)


def _dsilu(z, sig):
    return sig * (1.0 + z * (1.0 - sig))


def _acc(ref, val, first):
    @pl.when(first)
    def _():
        ref[...] = val

    @pl.when(jnp.logical_not(first))
    def _():
        ref[...] += val


def _mla_post_bwd(dh, proj, o, y_pool, y_mla, w_mla_up, w_out):
    n_rows = dh.shape[0]
    tm = _row_tile(n_rows)

    def body(dh_ref, g_ref, z_ref, o_ref, yp_ref, ym_ref, wup_ref, wout_ref,
             dgz_ref, dyp_ref, do_ref, delta_ref, dwout_ref, dwup_ref):
        first = pl.program_id(0) == 0
        z, y_pool, y_mla = z_ref[...], yp_ref[...], ym_ref[...]
        o_cat, sig_z, a_mla, _, sgp, sgm, merged = _merge_math(o_ref, z, g_ref[...], y_pool, lambda a: y_mla)
        dhv = dh_ref[...]
        dmerged = _mm_nt(dhv, wout_ref[...])
        _acc(dwout_ref, _mm_tn(merged, dhv), first)
        dyp_ref[...] = dmerged * sgp
        dy_mla = dmerged * sgm
        da = _mm_nt(dy_mla, wup_ref[...])
        _acc(dwup_ref, _mm_tn(a_mla, dy_mla), first)
        do_cat = da * (z * sig_z)
        dgz_ref[:, 0:D_MODEL] = dmerged * y_pool * (sgp * (1.0 - sgp))
        dgz_ref[:, D_MODEL:2 * D_MODEL] = dmerged * y_mla * (sgm * (1.0 - sgm))
        dgz_ref[:, 2 * D_MODEL:] = da * o_cat * _dsilu(z, sig_z)
        for h in range(N_HEADS):
            doh = do_cat[:, V_DIM * h:V_DIM * (h + 1)]
            do_ref[h] = doh
            delta_ref[h] = jnp.sum(doh * o_cat[:, V_DIM * h:V_DIM * (h + 1)], axis=1, keepdims=True)

    row = lambda w, c: pl.BlockSpec((tm, w), lambda i: (i, c))
    head = lambda w: pl.BlockSpec((N_HEADS, tm, w), lambda i: (0, i, 0))
    const = lambda r, c: pl.BlockSpec((r, c), lambda i: (0, 0))
    return pl.pallas_call(
        body, name="mla_post_bwd",
        out_shape=(jax.ShapeDtypeStruct((n_rows, 2560), F32), jax.ShapeDtypeStruct((n_rows, D_MODEL), F32),
                   jax.ShapeDtypeStruct((N_HEADS, n_rows, V_DIM), F32), jax.ShapeDtypeStruct((N_HEADS, n_rows, 1), F32),
                   jax.ShapeDtypeStruct((D_MODEL, D_MODEL), F32), jax.ShapeDtypeStruct((MLA_WIDTH, D_MODEL), F32)),
        grid=(n_rows // tm,),
        in_specs=[row(D_MODEL, 0), row(2048, COL_GATES // 2048), row(MLA_WIDTH, COL_ZMLA // MLA_WIDTH), head(V_DIM),
                  row(D_MODEL, 0), row(D_MODEL, 0), const(MLA_WIDTH, D_MODEL), const(D_MODEL, D_MODEL)],
        out_specs=(row(2560, 0), row(D_MODEL, 0), head(V_DIM), head(1), const(D_MODEL, D_MODEL), const(MLA_WIDTH, D_MODEL)),
        compiler_params=_cp("arbitrary"),
    )(dh, proj, proj, o, y_pool, y_mla, w_mla_up, w_out)


def _attn_bwd(q, k, v, do, lse, delta, ex=None):
    n_rows = q.shape[1]
    t = _attn_tile(n_rows)
    nb = n_rows // t

    def body(q_ref, k_ref, v_ref, do_ref, lse_ref, delta_ref, dq_ref, dk_ref, dv_ref, dk_sc, dv_sc):
        ki, qi = pl.program_id(1), pl.program_id(2)

        @pl.when((ki == 0) & (qi == 0))
        def _():
            dq_ref[...] = jnp.zeros(dq_ref.shape, F32)

        @pl.when(qi == 0)
        def _():
            dk_sc[...] = jnp.zeros(dk_sc.shape, F32)
            dv_sc[...] = jnp.zeros(dv_sc.shape, F32)

        def step(masked):
            heads = range(HEADS_PER_STEP)
            s = [_mm_nt(q_ref[h], k_ref[h]) for h in heads]
            dp = [_mm_nt(do_ref[h], v_ref[h, :, 0:V_DIM]) for h in heads]
            if masked:
                mask = _attn_mask(qi * t, ki * t, t, t)
                s = [jnp.where(mask, sh, MASK_VALUE) for sh in s]
            p = [jnp.exp2(s[h] * EXP2_SCALE - lse_ref[h] * LOG2_E) for h in heads]
            ds = [p[h] * (dp[h] - delta_ref[h]) for h in heads]
            rows = pl.ds(pl.multiple_of(qi * t, t), t)
            for h in heads:
                dv_sc[h] += _mm_tn(p[h], do_ref[h])
                dq_ref[h, rows, :] += _mm(ds[h], k_ref[h]) * ATTN_SCALE
                dk_sc[h] += _mm_tn(ds[h], q_ref[h])

        edge = (ki == qi) | (ki == 0)

        @pl.when((qi >= ki) & edge)
        def _():
            step(True)

        @pl.when((qi > ki) & (ki > 0))
        def _():
            step(False)

        @pl.when(qi == nb - 1)
        def _():
            dk_ref[...] = dk_sc[...] * ATTN_SCALE
            dv_ref[...] = dv_sc[...]

    hp = HEADS_PER_STEP
    q_idx = lambda h, ki, qi: (h, jnp.maximum(qi, ki), 0)
    k_idx = lambda h, ki, qi: (h, ki, 0)
    return _call_carrying(
        ex, body, "attn_bwd",
        out_shape=(jax.ShapeDtypeStruct((N_HEADS, n_rows, 128), F32), jax.ShapeDtypeStruct((N_HEADS, n_rows, 128), F32),
                   jax.ShapeDtypeStruct((N_HEADS, n_rows, V_DIM), F32)),
        grid=(N_HEADS // hp, nb, nb),
        in_specs=[pl.BlockSpec((hp, t, 128), q_idx), pl.BlockSpec((hp, t, 128), k_idx), pl.BlockSpec((hp, t, 128), k_idx),
                  pl.BlockSpec((hp, t, V_DIM), q_idx), pl.BlockSpec((hp, t, 1), q_idx), pl.BlockSpec((hp, t, 1), q_idx)],
        out_specs=(pl.BlockSpec((hp, n_rows, 128), lambda h, ki, qi: (h, 0, 0)), pl.BlockSpec((hp, t, 128), k_idx),
                   pl.BlockSpec((hp, t, V_DIM), k_idx)),
        scratch_shapes=[pltpu.VMEM((hp, t, 128), F32), pltpu.VMEM((hp, t, V_DIM), F32)],
        semantics=("parallel", "arbitrary", "arbitrary"), args=(q, k, v, do, lse, delta))


def _mla_pre_bwd(proj, tab, gqa, gkva, wq, wk, wv, gqn, gkn, dq, dk, dv):
    n_rows = proj.shape[0]
    tm = _row_tile(n_rows)

    def body(c_ref, kr_ref, tab_ref, gqa_ref, gkva_ref, wq_ref, wk_ref, wv_ref, gqn_ref, gkn_ref, dq_ref, dk_ref, dv_ref,
             dc_ref, dkr_ref, dwq_ref, dwk_ref, dwv_ref, dgqa_ref, dgkva_ref, dgqn_ref, dgkn_ref):
        first = pl.program_id(0) == 0
        c, tab = c_ref[...], tab_ref[...]
        gqa, gkva = gqa_ref[...], gkva_ref[...]
        cqn, cq_h, cq_inv = _rms(c[:, :Q_RANK], gqa)
        ckvn, ckv_h, ckv_inv = _rms(c[:, Q_RANK:], gkva)
        qp, kp = _mm(cqn, wq_ref[...]), _mm(ckvn, wk_ref[...])
        kr = pltpu.roll(kr_ref[...], QK_NOPE, 1)
        gqn, gkn = _pad_gain(gqn_ref), _pad_gain(gkn_ref)
        dq_parts, dk_parts, dv_parts = [], [], []
        dkr = jnp.zeros((tm, 128), F32)
        dgqn = jnp.zeros((1, 128), F32)
        dgkn = jnp.zeros((1, 128), F32)
        zv = jnp.zeros((tm, 128 - V_DIM), F32)
        for h in range(N_HEADS):
            blk = slice(128 * h, 128 * (h + 1))
            _, xh, inv = _head_rms(qp[:, blk], gqn)
            dx, dg = _head_rms_bwd(_rope_bwd(dq_ref[h], tab), xh, inv, gqn)
            dq_parts.append(dx)
            dgqn = dgqn + dg
            _, xh, inv = _head_rms(kp[:, blk] + kr, gkn)
            dx, dg = _head_rms_bwd(_rope_bwd(dk_ref[h], tab), xh, inv, gkn)
            dk_parts.append(dx)
            dgkn = dgkn + dg
            dkr = dkr + dx
            dv_parts.append(jnp.concatenate([dv_ref[h], zv], axis=1))
        dqp = jnp.concatenate(dq_parts, axis=1)
        dkp = jnp.concatenate(dk_parts, axis=1)
        dvp = jnp.concatenate(dv_parts, axis=1)
        _acc(dwq_ref, _mm_tn(cqn, dqp), first)
        _acc(dwk_ref, _mm_tn(ckvn, dkp), first)
        _acc(dwv_ref, _mm_tn(ckvn, dvp), first)
        dcq, dg1 = _rms_bwd(_mm_nt(dqp, wq_ref[...]), cq_h, cq_inv, gqa)
        dckv, dg2 = _rms_bwd(_mm_nt(dkp, wk_ref[...]) + _mm_nt(dvp, wv_ref[...]), ckv_h, ckv_inv, gkva)
        _acc(dgqa_ref, dg1, first)
        _acc(dgkva_ref, dg2, first)
        _acc(dgqn_ref, dgqn, first)
        _acc(dgkn_ref, dgkn, first)
        dc_ref[...] = jnp.concatenate([dcq, dckv], axis=1)
        lane = lax.broadcasted_iota(jnp.int32, (tm, 128), 1)
        dkr_ref[...] = jnp.where(lane < QK_ROPE, pltpu.roll(dkr, 128 - QK_NOPE, 1), 0.0)

    full = lambda *s: pl.BlockSpec(s, lambda i: (0,) * len(s))
    head = lambda w: pl.BlockSpec((N_HEADS, tm, w), lambda i: (0, i, 0))
    return pl.pallas_call(
        body, name="mla_pre_bwd",
        out_shape=(jax.ShapeDtypeStruct((n_rows, 1024), F32), jax.ShapeDtypeStruct((n_rows, 128), F32),
                   jax.ShapeDtypeStruct((Q_RANK, 1024), F32), jax.ShapeDtypeStruct((KV_RANK, 1024), F32),
                   jax.ShapeDtypeStruct((KV_RANK, 1024), F32),
                   jax.ShapeDtypeStruct((1, Q_RANK), F32), jax.ShapeDtypeStruct((1, KV_RANK), F32),
                   jax.ShapeDtypeStruct((1, 128), F32), jax.ShapeDtypeStruct((1, 128), F32)),
        grid=(n_rows // tm,),
        in_specs=[pl.BlockSpec((tm, 1024), lambda i: (i, COL_CQKV // 1024)),
                  pl.BlockSpec((tm, 128), lambda i: (i, COL_KR // 128)),
                  pl.BlockSpec((tm, 384), lambda i: (i, 0)),
                  full(1, Q_RANK), full(1, KV_RANK), full(Q_RANK, 1024), full(KV_RANK, 1024), full(KV_RANK, 1024),
                  full(1, QK_DIM), full(1, QK_DIM), head(128), head(128), head(V_DIM)],
        out_specs=(pl.BlockSpec((tm, 1024), lambda i: (i, 0)), pl.BlockSpec((tm, 128), lambda i: (i, 0)),
                   full(Q_RANK, 1024), full(KV_RANK, 1024), full(KV_RANK, 1024), full(1, Q_RANK), full(1, KV_RANK),
                   full(1, 128), full(1, 128)),
        compiler_params=_cp("arbitrary"),
    )(proj, proj, tab, gqa, gkva, wq, wk, wv, gqn, gkn, dq, dk, dv)


def _pool_bwd_a(proj, dy_pool, wg, scale, w_up):
    n_rows = proj.shape[0]
    tm = _row_tile(n_rows)

    def body(uz_ref, halo_ref, dy_ref, wg_ref, sc_ref, wup_ref, dmz_ref, dwg_ref, dsc_ref, dwup_ref):
        first = pl.program_id(0) == 0
        uz = uz_ref[...]
        u, z = uz[:, :POOL_WIDTH], uz[:, POOL_WIDTH:]
        scale_v = sc_ref[...]
        mixed, yg, ys, sig = _pool_math(u, halo_ref[...], z, wg_ref, scale_v, pl.program_id(0) * tm)
        sp = z * sig
        dy = dy_ref[...]
        da = _mm_nt(dy, wup_ref[...])
        _acc(dwup_ref, _mm_tn(ys * sp, dy), first)
        dys = da * sp
        _acc(dsc_ref, jnp.sum(dys * yg, axis=0, keepdims=True), first)
        dyg = dys * scale_v
        for g in range(4):
            cols = slice(g * 128, (g + 1) * 128)
            dmz_ref[:, cols] = _mm_nt(dyg[:, cols], wg_ref[g])
            _acc(dwg_ref.at[g], _mm_tn(mixed[:, cols], dyg[:, cols]), first)
        dmz_ref[:, POOL_WIDTH:] = da * ys * _dsilu(z, sig)

    return pl.pallas_call(
        body, name="pool_bwd_a",
        out_shape=(jax.ShapeDtypeStruct((n_rows, 1024), F32), jax.ShapeDtypeStruct((4, 128, 128), F32),
                   jax.ShapeDtypeStruct((1, POOL_WIDTH), F32), jax.ShapeDtypeStruct((POOL_WIDTH, D_MODEL), F32)),
        grid=(n_rows // tm,),
        in_specs=[pl.BlockSpec((tm, 1024), lambda i: (i, 0)), pl.BlockSpec((HALO, POOL_WIDTH), _halo_above(tm)),
                  pl.BlockSpec((tm, D_MODEL), lambda i: (i, 0)),
                  pl.BlockSpec((4, 128, 128), lambda i: (0, 0, 0)), pl.BlockSpec((1, POOL_WIDTH), lambda i: (0, 0)),
                  pl.BlockSpec((POOL_WIDTH, D_MODEL), lambda i: (0, 0))],
        out_specs=(pl.BlockSpec((tm, 1024), lambda i: (i, 0)), pl.BlockSpec((4, 128, 128), lambda i: (0, 0, 0)),
                   pl.BlockSpec((1, POOL_WIDTH), lambda i: (0, 0)), pl.BlockSpec((POOL_WIDTH, D_MODEL), lambda i: (0, 0))),
        compiler_params=_cp("arbitrary"),
    )(proj, proj, dy_pool, wg, scale, w_up)


def _pool_bwd_b(dmz):
    n_rows = dmz.shape[0]
    tm = _row_tile(n_rows)
    n_tiles = n_rows // tm
    n_ext = tm + HALO

    def body(dmz_ref, halo_ref, o_ref):
        i = pl.program_id(0)
        v = dmz_ref[...]
        dm = v[:, :POOL_WIDTH]
        halo = jnp.where(i == n_tiles - 1, 0.0, halo_ref[...])
        ext = jnp.concatenate([dm, halo], axis=0)
        t1 = (i * tm + lax.broadcasted_iota(jnp.int32, (n_ext, 1), 0) - (PAD_FRONT - 1)).astype(F32)
        du = []
        for g, w in enumerate(POOL_WINDOWS):
            cols = slice(g * 128, (g + 1) * 128)
            a = ext[:, cols] / jnp.clip(t1, 1.0, float(w))
            k = 1
            while k < w:
                a = a + pltpu.roll(a, n_ext - k, 0)
                k *= 2
            du.append(a[:tm, :] - dm[:, cols])
        o_ref[...] = jnp.concatenate(du + [v[:, POOL_WIDTH:]], axis=1)

    last_halo = n_rows // HALO - 1
    return pl.pallas_call(
        body, name="pool_bwd_b", out_shape=jax.ShapeDtypeStruct((n_rows, 1024), F32), grid=(n_tiles,),
        in_specs=[pl.BlockSpec((tm, 1024), lambda i: (i, 0)),
                  pl.BlockSpec((HALO, POOL_WIDTH), lambda i: (jnp.minimum((i + 1) * (tm // HALO), last_halo), 0))],
        out_specs=pl.BlockSpec((tm, 1024), lambda i: (i, 0)),
        compiler_params=_cp("parallel"),
    )(dmz, dmz)


def _norm_proj_bwd(d_uz, d_cqkv, d_gz, d_kr, w_in, h_res, gain, dh_out):
    n_rows = h_res.shape[0]
    tm = _row_tile(n_rows)
    pieces = ((COL_UZ, 1024), (COL_CQKV, 1024), (COL_GATES, 2560), (COL_KR, 128))

    def body(a_ref, b_ref, c_ref, d_ref, w_ref, x_ref, g_ref, dho_ref, dhi_ref, dg_ref):
        i = pl.program_id(0)
        dh = None
        for ref, (c0, wd) in zip((a_ref, b_ref, c_ref, d_ref), pieces):
            part = _mm_nt(ref[...], w_ref[:, c0:c0 + wd])
            dh = part if dh is None else dh + part
        g = g_ref[...]
        _, xh, inv = _rms(x_ref[...], g)
        dx, dg = _rms_bwd(dh, xh, inv, g)
        _acc(dg_ref, dg, i == 0)
        row = i * tm + lax.broadcasted_iota(jnp.int32, (tm, 1), 0)
        dhi_ref[...] = jnp.where(row >= PAD_FRONT, dho_ref[...] + dx, 0.0)

    row = lambda w: pl.BlockSpec((tm, w), lambda i: (i, 0))
    return pl.pallas_call(
        body, name="norm_proj_bwd",
        out_shape=(jax.ShapeDtypeStruct((n_rows, D_MODEL), F32), jax.ShapeDtypeStruct((1, D_MODEL), F32)),
        grid=(n_rows // tm,),
        in_specs=[row(1024), row(1024), row(2560), row(128), pl.BlockSpec((D_MODEL, P_IN), lambda i: (0, 0)),
                  row(D_MODEL), pl.BlockSpec((1, D_MODEL), lambda i: (0, 0)), row(D_MODEL)],
        out_specs=(row(D_MODEL), pl.BlockSpec((1, D_MODEL), lambda i: (0, 0))),
        compiler_params=_cp("arbitrary"),
    )(d_uz, d_cqkv, d_gz, d_kr, w_in, h_res, gain, dh_out)


def _weight_grad(name, a, b):
    k_rows, m = a.shape
    n = b.shape[1]
    tk = 640 if k_rows % 640 == 0 else 128
    tn = 512 if n % 512 == 0 else 128

    def body(a_ref, b_ref, o_ref):
        _acc(o_ref, _mm_tn(a_ref[...], b_ref[...]), pl.program_id(1) == 0)

    return pl.pallas_call(
        body, name=name, out_shape=jax.ShapeDtypeStruct((m, n), F32), grid=(n // tn, k_rows // tk),
        in_specs=[pl.BlockSpec((tk, m), lambda j, k: (k, 0)), pl.BlockSpec((tk, tn), lambda j, k: (k, j))],
        out_specs=pl.BlockSpec((m, tn), lambda j, k: (0, j)),
        compiler_params=_cp("parallel", "arbitrary"),
    )(a, b)


def _adamw(name, w, m, v, gbufs):
    r, c = w.shape
    depth = len(gbufs)
    r_l = r // depth
    tr = r_l
    if r_l * c * 4 > (1 << 20):
        tr = 256 if r_l % 256 == 0 else 128
    assert r_l * depth == r and r_l % tr == 0, (name, r, depth, tr)
    n_t = r_l // tr

    def body(w_ref, m_ref, v_ref, *refs):
        g_refs, (go_ref, d_ref, mo_ref, vo_ref) = refs[:depth], refs[depth:]

        def update(g_ref):
            g = g_ref[0].astype(F32)
            for s in range(1, N_DEV):
                g = g + g_ref[s].astype(F32)
            go_ref[...] = g
            m_new = ADAM_B1 * m_ref[...] + (1.0 - ADAM_B1) * g
            v_new = ADAM_B2 * v_ref[...] + (1.0 - ADAM_B2) * (g * g)
            mo_ref[...] = m_new
            vo_ref[...] = v_new
            m_hat = m_new / (1.0 - ADAM_B1 ** ADAM_STEP)
            v_hat = v_new / (1.0 - ADAM_B2 ** ADAM_STEP)
            d_ref[...] = -ADAM_LR * (m_hat / (jnp.sqrt(v_hat) + ADAM_EPS) + ADAM_WD * w_ref[...])

        for j in range(depth):
            pl.when(pl.program_id(0) == j)(lambda j=j: update(g_refs[j]))

    def g_spec(j):
        return pl.BlockSpec((N_DEV, tr, c), lambda l, i: (0, jnp.where(l == j, i, jnp.where(l < j, 0, n_t - 1)), 0))

    spec = pl.BlockSpec((tr, c), lambda l, i: (l * n_t + i, 0))
    sds = jax.ShapeDtypeStruct((r, c), F32)
    return pl.pallas_call(
        body, name=name, out_shape=(sds, sds, sds, sds), grid=(depth, n_t),
        in_specs=[spec, spec, spec] + [g_spec(j) for j in range(depth)],
        out_specs=(spec, spec, spec, spec),
        compiler_params=_cp("arbitrary", "arbitrary"),
    )(w, m, v, *gbufs)


SMALL = ("norm_gain", "pool_w_group", "pool_scale", "q_a_norm_gain", "kv_a_norm_gain", "q_norm_gain", "k_norm_gain")


def _pack_small(parts):
    rows = []
    for p in parts:
        flat = p.reshape(-1)
        pad = (-flat.shape[0]) % 1024
        if pad:
            flat = jnp.concatenate([flat, jnp.zeros((pad,), F32)])
        rows.append(flat.reshape(-1, 128))
    n_rows = sum(r.shape[0] for r in rows)
    tail = (-n_rows) % 256
    if tail:
        rows.append(jnp.zeros((tail, 128), F32))
    return jnp.concatenate(rows, axis=0)


def _unpack_small(packed, shapes):
    out, r0 = [], 0
    for shp in shapes:
        n = math.prod(shp)
        n_rows = (n + 1023) // 1024 * 8
        out.append(packed[r0:r0 + n_rows].reshape(-1)[:n].reshape(shp))
        r0 += n_rows
    return out


def kernel(x, positions, meta_tokens, norm_gain, w_in, pool_w_group, pool_scale, pool_w_up, q_a_norm_gain, kv_a_norm_gain, w_q_b, w_kv_b, q_norm_gain, k_norm_gain, mla_w_up, w_out, loss_target, m_meta_tokens, m_norm_gain, m_w_in, m_pool_w_group, m_pool_scale, m_pool_w_up, m_q_a_norm_gain, m_kv_a_norm_gain, m_w_q_b, m_w_kv_b, m_q_norm_gain, m_k_norm_gain, m_mla_w_up, m_w_out, v_meta_tokens, v_norm_gain, v_w_in, v_pool_w_group, v_pool_scale, v_pool_w_up, v_q_a_norm_gain, v_kv_a_norm_gain, v_w_q_b, v_w_kv_b, v_q_norm_gain, v_k_norm_gain, v_mla_w_up, v_w_out):
    x2, target = x[0], loss_target[0]

    big = dict(w_in=w_in, pool_w_up=pool_w_up, w_q_b=w_q_b, w_kv_b=w_kv_b, mla_w_up=mla_w_up, w_out=w_out)
    names_big = list(big)
    shards = {n: _cast_bf16("cast_" + n, w) for n, w in big.items()}

    def layer_shards(l):
        return [shards[n][l] for n in names_big]

    def repack(gathered):
        g_in, g_pup, g_qb, g_kvb, g_mup, g_out = gathered
        return dict(w_in=_repack_w_in(g_in),
                    pool_up=_repack_cols("repack_pool_up", g_pup[:, None], BF16)[0],
                    q_heads=_repack_q_heads(g_qb),
                    kv_heads=_repack_kv_heads(g_kvb),
                    mla_up=_repack_cols("repack_mla_up", g_mup[:, None], BF16)[0],
                    out=_repack_rows("repack_out", g_out[:, None])[0])

    first = _exchange("gather_first", [meta_tokens] + layer_shards(0), [])
    weights = [repack(first[1:])]

    h = _embed(x2, first[0])
    pos = jnp.concatenate([jnp.zeros((PAD_FRONT,), jnp.int32), jnp.arange(N_META, dtype=jnp.int32),
                           positions[0] + N_META]).astype(F32).reshape(-1, 1)
    half = QK_ROPE // 2
    inv_freq = ROPE_THETA ** (-jnp.arange(half, dtype=F32) / half)
    tab = _rope_table(pos, jnp.tile(inv_freq, 128 // half).reshape(1, 128))

    row = lambda a, l: a[l].reshape(1, -1)
    saved = []
    for l in range(DEPTH):
        w = weights[l]
        hb, proj = _norm_proj(h, row(norm_gain, l), w["w_in"])
        y_pool = _pool_fwd(proj, pool_w_group[l], row(pool_scale, l), w["pool_up"])
        q, k, v = _mla_pre_fwd(proj, tab, row(q_a_norm_gain, l), row(kv_a_norm_gain, l), w["q_heads"], *w["kv_heads"],
                               row(q_norm_gain, l), row(k_norm_gain, l))
        ex = _Exchange(layer_shards(l + 1), []) if l + 1 < DEPTH else None
        res = _attn_fwd(q, k, v, ex)
        o, lse = res[0], res[1]
        if ex is not None:
            weights.append(repack(res[2:]))
        y_mla, h_next = _mla_post_fwd(h, proj, o, y_pool, w["mla_up"], w["out"])
        saved.append((h, hb, proj, y_pool, q, k, v, o, lse, y_mla))
        h = h_next

    loss_part, dh = _loss_head(h, target)
    loss = lax.psum(loss_part[0, 0], MESH_AXES)

    grads = {n: [None] * DEPTH for n in SMALL}
    pending = None
    received = [None] * DEPTH
    for l in reversed(range(DEPTH)):
        w = weights[l]
        h_l, hb, proj, y_pool, q, k, v, o, lse, y_mla = saved[l]
        d_gz, dy_pool, do, delta, g_out, g_mla_up = _mla_post_bwd(dh, proj, o, y_pool, y_mla, w["mla_up"], w["out"])
        ex = _Exchange([], pending) if pending is not None else None
        res = _attn_bwd(q, k, v, do, lse, delta, ex)
        dq, dk, dv = res[0], res[1], res[2]
        if ex is not None:
            received[l + 1] = res[3:]
        (d_cqkv, d_kr, g_q, g_k, g_v, grads["q_a_norm_gain"][l], grads["kv_a_norm_gain"][l], g_qn, g_kn) = _mla_pre_bwd(
            proj, tab, row(q_a_norm_gain, l), row(kv_a_norm_gain, l), w["q_heads"], *w["kv_heads"], row(q_norm_gain, l),
            row(k_norm_gain, l), dq, dk, dv)
        grads["q_norm_gain"][l], grads["k_norm_gain"][l] = g_qn[:, :QK_DIM], g_kn[:, :QK_DIM]
        dmz, grads["pool_w_group"][l], grads["pool_scale"][l], g_pool_up = _pool_bwd_a(
            proj, dy_pool, pool_w_group[l], row(pool_scale, l), w["pool_up"])
        d_uz = _pool_bwd_b(dmz)
        g_in = [_weight_grad("dw_in", hb, d) for d in (d_uz, d_cqkv, d_gz, d_kr)]
        dh, grads["norm_gain"][l] = _norm_proj_bwd(d_uz, d_cqkv, d_gz, d_kr, w["w_in"], h_l, row(norm_gain, l), dh)
        pending = [_unpack_w_in(g_in),
                   _unpack_cols("unpack_pool_up", g_pool_up[None], 128, BF16)[:, 0],
                   _unpack_q_heads(g_q),
                   _unpack_kv_heads(g_k, g_v),
                   _unpack_cols("unpack_mla_up", g_mla_up[None], 128, BF16)[:, 0],
                   _unpack_rows("unpack_out", g_out[None], D_MODEL // N_DEV)[:, 0]]

    grad_x = dh[HEAD_ROWS:][None]
    d_meta = dh[PAD_FRONT:HEAD_ROWS]

    small_shapes = [a.shape for a in (norm_gain, pool_w_group, pool_scale, q_a_norm_gain, kv_a_norm_gain, q_norm_gain, k_norm_gain)]
    small_grad = _pack_small([jnp.stack(grads[n], axis=0).reshape(s) for n, s in zip(SMALL, small_shapes)])
    meta_parts = _unpack_cols("unpack_meta", d_meta[None], 128, F32)[:, 0]
    last = _exchange("exchange_last", [small_grad], [meta_parts] + pending)
    small_buf = last[0]
    received[0] = last[2:]
    sharded = ["meta_tokens"] + names_big
    shard_bufs = {"meta_tokens": [last[1]]}
    for j, n in enumerate(names_big):
        shard_bufs[n] = [received[l][j] for l in range(DEPTH)]

    given = dict(meta_tokens=(meta_tokens, m_meta_tokens, v_meta_tokens), w_in=(w_in, m_w_in, v_w_in),
                 pool_w_up=(pool_w_up, m_pool_w_up, v_pool_w_up), w_q_b=(w_q_b, m_w_q_b, v_w_q_b),
                 w_kv_b=(w_kv_b, m_w_kv_b, v_w_kv_b), mla_w_up=(mla_w_up, m_mla_w_up, v_mla_w_up),
                 w_out=(w_out, m_w_out, v_w_out))
    result = {}
    for n in sharded:
        w, m, v = given[n]
        cols = w.shape[-1]
        res = _adamw("adamw_" + n, w.reshape(-1, cols), m.reshape(-1, cols), v.reshape(-1, cols), shard_bufs[n])
        result[n] = [r.reshape(w.shape) for r in res]
    small_w = (norm_gain, pool_w_group, pool_scale, q_a_norm_gain, kv_a_norm_gain, q_norm_gain, k_norm_gain)
    small_m = (m_norm_gain, m_pool_w_group, m_pool_scale, m_q_a_norm_gain, m_kv_a_norm_gain, m_q_norm_gain, m_k_norm_gain)
    small_v = (v_norm_gain, v_pool_w_group, v_pool_scale, v_q_a_norm_gain, v_kv_a_norm_gain, v_q_norm_gain, v_k_norm_gain)
    res = _adamw("adamw_small", _pack_small(small_w), _pack_small(small_m), _pack_small(small_v), [small_buf])
    small_res = [_unpack_small(r, small_shapes) for r in res]
    for j, n in enumerate(SMALL):
        result[n] = [small_res[kind][j] for kind in range(4)]

    order = ("meta_tokens", "norm_gain", "w_in", "pool_w_group", "pool_scale", "pool_w_up", "q_a_norm_gain",
             "kv_a_norm_gain", "w_q_b", "w_kv_b", "q_norm_gain", "k_norm_gain", "mla_w_up", "w_out")
    outs = [loss, grad_x]
    for kind in range(4):
        outs += [result[n][kind] for n in order]
    return tuple(outs)
```

```python
import math

import jax
import jax.numpy as jnp
from jax import lax
from jax.experimental import pallas as pl
from jax.experimental.pallas import tpu as pltpu

F32 = jnp.float32
BF16 = jnp.bfloat16

D_MODEL = 1024
DEPTH = 4
N_META = 16
PAD_FRONT = 112
HEAD_ROWS = PAD_FRONT + N_META
POOL_WIDTH = 512
POOL_WINDOWS = (2, 4, 8, 16)
POOL_GROUP_DIM = 128
HALO = 16
N_HEADS = 8
HEADS_PER_STEP = 4
HEADS_PER_STEP_FWD = 4
QK_NOPE = 64
QK_ROPE = 32
QK_DIM = 96
V_DIM = 64
MLA_WIDTH = 512
Q_RANK = 768
KV_RANK = 256
ROPE_THETA = 10000.0
NORM_EPS = 1e-6
MASK_VALUE = -1e30
D_IN = 4640
N_DEV = 8
IN_SHARD = D_IN // N_DEV

P_IN = 4736
IN_SEGMENTS = ((0, 2048, 0), (2048, 2080, 4608), (2080, 2592, 4096), (2592, 3616, 2048), (3616, 4640, 3072))
COL_UZ, COL_CQKV, COL_GATES, COL_ZMLA, COL_KR = 0, 1024, 2048, 4096, 4608

ADAM_LR = 0.001
ADAM_B1 = 0.9
ADAM_B2 = 0.999
ADAM_EPS = 1e-08
ADAM_WD = 0.01
ADAM_STEP = 10

VMEM_LIMIT = 56 * 1024 * 1024
MESH_AXES = ("x", "y", "c")


def _cp(*sem):
    return pltpu.CompilerParams(dimension_semantics=sem, vmem_limit_bytes=VMEM_LIMIT)


def _row_tile(n_rows):
    return 320 if n_rows % 320 == 0 else 128


def _attn_tile(n_rows):
    return 640 if (n_rows % 640 == 0 and n_rows > 640) else 128


def _mm(a, b):
    return jnp.dot(a.astype(BF16), b.astype(BF16), preferred_element_type=F32)


def _mm_nt(a, b):
    return lax.dot_general(a.astype(BF16), b.astype(BF16), (((1,), (1,)), ((), ())), preferred_element_type=F32)


def _mm_tn(a, b):
    return lax.dot_general(a.astype(BF16), b.astype(BF16), (((0,), (0,)), ((), ())), preferred_element_type=F32)


def _sigmoid(z):
    return 1.0 / (1.0 + jnp.exp(-z))


def _rms(x, g):
    inv = lax.rsqrt(jnp.mean(x * x, axis=-1, keepdims=True) + NORM_EPS)
    xh = x * inv
    return xh * g, xh, inv


def _rms_bwd(dy, xh, inv, g):
    dg = jnp.sum(dy * xh, axis=0, keepdims=True)
    dxh = dy * g
    dx = inv * (dxh - xh * jnp.mean(dxh * xh, axis=-1, keepdims=True))
    return dx, dg


def _rope(x, tab):
    return x * tab[:, 0:128] + pltpu.roll(x, 112, 1) * tab[:, 128:256] + pltpu.roll(x, 16, 1) * tab[:, 256:384]


def _rope_bwd(dy, tab):
    return dy * tab[:, 0:128] + pltpu.roll(dy * tab[:, 128:256], 16, 1) + pltpu.roll(dy * tab[:, 256:384], 112, 1)


def _head_rms(x, g):
    inv = lax.rsqrt(jnp.sum(x * x, axis=-1, keepdims=True) * (1.0 / QK_DIM) + NORM_EPS)
    xh = x * inv
    return xh * g, xh, inv


def _head_rms_bwd(dy, xh, inv, g):
    dg = jnp.sum(dy * xh, axis=0, keepdims=True)
    dxh = dy * g
    dx = inv * (dxh - xh * (jnp.sum(dxh * xh, axis=-1, keepdims=True) * (1.0 / QK_DIM)))
    return dx, dg


def _pad_gain(g_ref):
    return jnp.concatenate([g_ref[...], jnp.zeros((1, 128 - QK_DIM), F32)], axis=1)


class _Exchange:
    def __init__(self, gather_list, scatter_list):
        self.arrays = list(gather_list) + list(scatter_list)
        self.n_g, self.n = len(gather_list), len(self.arrays)
        self.out_shape = [jax.ShapeDtypeStruct((N_DEV,) + a.shape, a.dtype) for a in gather_list]
        self.out_shape += [jax.ShapeDtypeStruct(a.shape, a.dtype) for a in scatter_list]
        self.specs = [pl.BlockSpec(memory_space=pl.ANY)] * self.n
        self.sems = [pltpu.SemaphoreType.DMA((7 * self.n,)), pltpu.SemaphoreType.DMA((7 * self.n,)),
                     pltpu.SemaphoreType.DMA((self.n,))]

    def copies(self, ins, outs, sems):
        send_sems, recv_sems, local_sems = sems
        x, y, c = lax.axis_index("x"), lax.axis_index("y"), lax.axis_index("c")
        me = 4 * x + 2 * y + c

        def src(a, slot):
            return ins[a] if a < self.n_g else ins[a].at[slot]

        out = [pltpu.make_async_copy(src(a, me), outs[a].at[me], local_sems.at[a]) for a in range(self.n)]
        for k in range(1, N_DEV):
            px = 1 - x if k & 4 else x
            py = 1 - y if k & 2 else y
            pc = 1 - c if k & 1 else c
            peer = 4 * px + 2 * py + pc
            for a in range(self.n):
                out.append(pltpu.make_async_remote_copy(
                    src_ref=src(a, peer), dst_ref=outs[a].at[me],
                    send_sem=send_sems.at[a * 7 + k - 1], recv_sem=recv_sems.at[a * 7 + k - 1],
                    device_id=(px, py, pc), device_id_type=pl.DeviceIdType.MESH))
        return out


def _exchange(name, gather_list, scatter_list):
    ex = _Exchange(gather_list, scatter_list)
    n = ex.n

    def body(*refs):
        copies = ex.copies(refs[:n], refs[n:2 * n], refs[2 * n:])
        for cp in copies:
            cp.start()
        for cp in copies:
            cp.wait()

    return pl.pallas_call(
        body, name=name, out_shape=ex.out_shape, in_specs=ex.specs, out_specs=ex.specs, scratch_shapes=ex.sems,
        compiler_params=pltpu.CompilerParams(has_side_effects=True),
    )(*ex.arrays)


def _call_carrying(ex, body, name, out_shape, grid, in_specs, out_specs, scratch_shapes, semantics, args):
    if ex is None:
        return pl.pallas_call(body, name=name, out_shape=out_shape, grid=grid, in_specs=in_specs, out_specs=out_specs,
                              scratch_shapes=scratch_shapes, compiler_params=_cp(*semantics))(*args)
    n, n_in, n_out, n_scr = ex.n, len(in_specs), len(out_shape), len(scratch_shapes)

    def carrying(*refs):
        ins, refs = refs[:n_in], refs[n_in:]
        c_in, refs = refs[:n], refs[n:]
        outs, refs = refs[:n_out], refs[n_out:]
        c_out, refs = refs[:n], refs[n:]
        scratch, sems = refs[:n_scr], refs[n_scr:]
        ids = [pl.program_id(a) for a in range(len(grid))]
        first, last = ids[0] == 0, ids[0] == grid[0] - 1
        for a in range(1, len(grid)):
            first, last = first & (ids[a] == 0), last & (ids[a] == grid[a] - 1)
        copies = ex.copies(c_in, c_out, sems)

        @pl.when(first)
        def _():
            for cp in copies:
                cp.start()

        body(*ins, *outs, *scratch)

        @pl.when(last)
        def _():
            for cp in copies:
                cp.wait()

    return pl.pallas_call(
        carrying, name=name + "_x", out_shape=list(out_shape) + ex.out_shape, grid=grid,
        in_specs=list(in_specs) + ex.specs, out_specs=list(out_specs) + ex.specs,
        scratch_shapes=list(scratch_shapes) + ex.sems, compiler_params=_cp(*["arbitrary"] * len(grid)),
    )(*args, *ex.arrays)


def _in_pieces():
    runs = []
    for lo, hi, dst in sorted(IN_SEGMENTS, key=lambda s: s[2]):
        col = lo
        while col < hi:
            d = col // IN_SHARD
            end = min(hi, (d + 1) * IN_SHARD)
            runs.append((d, col - d * IN_SHARD, end - d * IN_SHARD))
            col = end
    return runs


def _cast_bf16(name, w):
    shape = w.shape
    w2 = w.reshape(-1, shape[-1])
    r, c = w2.shape
    tr = 512 if r % 512 == 0 else r

    def body(x_ref, o_ref):
        o_ref[...] = x_ref[...].astype(BF16)

    spec = pl.BlockSpec((tr, c), lambda i: (i, 0))
    out = pl.pallas_call(body, name=name, out_shape=jax.ShapeDtypeStruct((r, c), BF16), grid=(r // tr,),
                         in_specs=[spec], out_specs=spec, compiler_params=_cp("parallel"))(w2)
    return out.reshape(shape)


def _repack_w_in(gathered):
    tr = 256
    runs = _in_pieces()

    def body(x_ref, o_ref):
        parts = [x_ref[d, :, a:b].astype(F32) for d, a, b in runs]
        parts.append(jnp.zeros((tr, P_IN - D_IN), F32))
        o_ref[...] = jnp.concatenate(parts, axis=1).astype(BF16)

    return pl.pallas_call(
        body, name="repack_w_in", out_shape=jax.ShapeDtypeStruct((D_MODEL, P_IN), BF16), grid=(D_MODEL // tr,),
        in_specs=[pl.BlockSpec((N_DEV, tr, IN_SHARD), lambda i: (0, i, 0))],
        out_specs=pl.BlockSpec((tr, P_IN), lambda i: (i, 0)),
        compiler_params=_cp("parallel"),
    )(gathered)


def _unpack_w_in(parts):
    tr = 256
    runs = _in_pieces()
    widths = [p.shape[-1] for p in parts]

    def body(a_ref, b_ref, c_ref, d_ref, o_ref):
        full = jnp.concatenate([a_ref[...], b_ref[...], c_ref[...], d_ref[...]], axis=1)
        col = 0
        per_dev = [[] for _ in range(N_DEV)]
        for d, a, b in runs:
            per_dev[d].append((a, full[:, col:col + (b - a)]))
            col += b - a
        for d in range(N_DEV):
            pieces = [v for _, v in sorted(per_dev[d], key=lambda av: av[0])]
            o_ref[d] = (jnp.concatenate(pieces, axis=1) if len(pieces) > 1 else pieces[0]).astype(BF16)

    return pl.pallas_call(
        body, name="unpack_w_in", out_shape=jax.ShapeDtypeStruct((N_DEV, D_MODEL, IN_SHARD), BF16), grid=(D_MODEL // tr,),
        in_specs=[pl.BlockSpec((tr, w), lambda i: (i, 0)) for w in widths],
        out_specs=pl.BlockSpec((N_DEV, tr, IN_SHARD), lambda i: (0, i, 0)),
        compiler_params=_cp("parallel"),
    )(*parts)


def _repack_cols(name, gathered, dtype):
    _, depth, k, n = gathered.shape

    def body(x_ref, o_ref):
        o_ref[0] = jnp.concatenate([x_ref[d, 0] for d in range(N_DEV)], axis=1).astype(dtype)

    return pl.pallas_call(
        body, name=name, out_shape=jax.ShapeDtypeStruct((depth, k, N_DEV * n), dtype), grid=(depth,),
        in_specs=[pl.BlockSpec((N_DEV, 1, k, n), lambda l: (0, l, 0, 0))],
        out_specs=pl.BlockSpec((1, k, N_DEV * n), lambda l: (l, 0, 0)),
        compiler_params=_cp("parallel"),
    )(gathered)


def _unpack_cols(name, full, n, dtype):
    depth, k, _ = full.shape

    def body(x_ref, o_ref):
        for d in range(N_DEV):
            o_ref[d, 0] = x_ref[0, :, d * n:(d + 1) * n].astype(dtype)

    return pl.pallas_call(
        body, name=name, out_shape=jax.ShapeDtypeStruct((N_DEV, depth, k, n), dtype), grid=(depth,),
        in_specs=[pl.BlockSpec((1, k, N_DEV * n), lambda l: (l, 0, 0))],
        out_specs=pl.BlockSpec((N_DEV, 1, k, n), lambda l: (0, l, 0, 0)),
        compiler_params=_cp("parallel"),
    )(full)


def _repack_rows(name, gathered):
    _, depth, r, n = gathered.shape

    def body(x_ref, o_ref):
        for d in range(N_DEV):
            o_ref[0, d * r:(d + 1) * r, :] = x_ref[d, 0].astype(BF16)

    return pl.pallas_call(
        body, name=name, out_shape=jax.ShapeDtypeStruct((depth, N_DEV * r, n), BF16), grid=(depth,),
        in_specs=[pl.BlockSpec((N_DEV, 1, r, n), lambda l: (0, l, 0, 0))],
        out_specs=pl.BlockSpec((1, N_DEV * r, n), lambda l: (l, 0, 0)),
        compiler_params=_cp("parallel"),
    )(gathered)


def _unpack_rows(name, full, r):
    depth, _, n = full.shape

    def body(x_ref, o_ref):
        for d in range(N_DEV):
            o_ref[d, 0] = x_ref[0, d * r:(d + 1) * r, :].astype(BF16)

    return pl.pallas_call(
        body, name=name, out_shape=jax.ShapeDtypeStruct((N_DEV, depth, r, n), BF16), grid=(depth,),
        in_specs=[pl.BlockSpec((1, N_DEV * r, n), lambda l: (l, 0, 0))],
        out_specs=pl.BlockSpec((N_DEV, 1, r, n), lambda l: (0, l, 0, 0)),
        compiler_params=_cp("parallel"),
    )(full)


def _repack_q_heads(gathered):
    r = Q_RANK // N_DEV

    def body(x_ref, o_ref):
        z = jnp.zeros((r, 128 - QK_DIM), F32)
        for d in range(N_DEV):
            x = x_ref[d].astype(F32)
            parts = []
            for h in range(N_HEADS):
                parts += [x[:, QK_DIM * h:QK_DIM * (h + 1)], z]
            o_ref[d * r:(d + 1) * r, :] = jnp.concatenate(parts, axis=1).astype(BF16)

    return pl.pallas_call(body, name="repack_q_heads", out_shape=jax.ShapeDtypeStruct((Q_RANK, 1024), BF16),
                          compiler_params=_cp())(gathered)


def _unpack_q_heads(dwq):
    r = Q_RANK // N_DEV

    def body(x_ref, o_ref):
        for d in range(N_DEV):
            x = x_ref[d * r:(d + 1) * r, :]
            o_ref[d] = jnp.concatenate([x[:, 128 * h:128 * h + QK_DIM] for h in range(N_HEADS)], axis=1).astype(BF16)

    return pl.pallas_call(body, name="unpack_q_heads", out_shape=jax.ShapeDtypeStruct((N_DEV, r, Q_RANK), BF16),
                          compiler_params=_cp())(dwq)


def _repack_kv_heads(gathered):
    def body(x_ref, wk_ref, wv_ref):
        z = jnp.zeros((KV_RANK, 128 - QK_NOPE), F32)
        for h in range(N_HEADS):
            x = x_ref[h].astype(F32)
            wk_ref[:, 128 * h:128 * (h + 1)] = jnp.concatenate([x[:, :QK_NOPE], z], axis=1).astype(BF16)
            wv_ref[:, 128 * h:128 * (h + 1)] = jnp.concatenate([x[:, QK_NOPE:], z], axis=1).astype(BF16)

    sds = jax.ShapeDtypeStruct((KV_RANK, 1024), BF16)
    return pl.pallas_call(body, name="repack_kv_heads", out_shape=(sds, sds), compiler_params=_cp())(gathered)


def _unpack_kv_heads(dwk, dwv):
    def body(k_ref, v_ref, o_ref):
        for h in range(N_HEADS):
            o_ref[h] = jnp.concatenate([k_ref[:, 128 * h:128 * h + QK_NOPE], v_ref[:, 128 * h:128 * h + V_DIM]],
                                       axis=1).astype(BF16)

    return pl.pallas_call(body, name="unpack_kv_heads", out_shape=jax.ShapeDtypeStruct((N_DEV, KV_RANK, 128), BF16),
                          compiler_params=_cp())(dwk, dwv)


def _embed(x, meta_g):
    seq = x.shape[0]
    n_blk = seq // HEAD_ROWS + 1

    def body(x_ref, m_ref, o_ref):
        i = pl.program_id(0)

        @pl.when(i == 0)
        def _():
            meta = jnp.concatenate([m_ref[d] for d in range(N_DEV)], axis=1)
            o_ref[...] = jnp.concatenate([jnp.zeros((PAD_FRONT, D_MODEL), F32), meta], axis=0)

        @pl.when(i > 0)
        def _():
            o_ref[...] = x_ref[...]

    return pl.pallas_call(
        body, name="embed", out_shape=jax.ShapeDtypeStruct((seq + HEAD_ROWS, D_MODEL), F32), grid=(n_blk,),
        in_specs=[pl.BlockSpec((HEAD_ROWS, D_MODEL), lambda i: (jnp.maximum(i - 1, 0), 0)),
                  pl.BlockSpec((N_DEV, N_META, 128), lambda i: (0, 0, 0))],
        out_specs=pl.BlockSpec((HEAD_ROWS, D_MODEL), lambda i: (i, 0)),
        compiler_params=_cp("parallel"),
    )(x, meta_g)


def _rope_table(pos_col, freq_row):
    n_rows = pos_col.shape[0]
    tr = 128

    def body(p_ref, f_ref, o_ref):
        ang = p_ref[...] * f_ref[...]
        lane = lax.broadcasted_iota(jnp.int32, ang.shape, 1)
        cosv, sinv = jnp.cos(ang), jnp.sin(ang)
        half = QK_ROPE // 2
        o_ref[:, 0:128] = jnp.where(lane < QK_NOPE, 1.0, jnp.where(lane < QK_DIM, cosv, 0.0))
        o_ref[:, 128:256] = jnp.where((lane >= QK_NOPE) & (lane < QK_NOPE + half), -sinv, 0.0)
        o_ref[:, 256:384] = jnp.where((lane >= QK_NOPE + half) & (lane < QK_DIM), sinv, 0.0)

    return pl.pallas_call(
        body, name="rope_table", out_shape=jax.ShapeDtypeStruct((n_rows, 384), F32), grid=(n_rows // tr,),
        in_specs=[pl.BlockSpec((tr, 1), lambda i: (i, 0)), pl.BlockSpec((1, 128), lambda i: (0, 0))],
        out_specs=pl.BlockSpec((tr, 384), lambda i: (i, 0)),
        compiler_params=_cp("parallel"),
    )(pos_col, freq_row)


def _loss_head(h, target):
    n_rows = h.shape[0]
    n_blk = n_rows // HEAD_ROWS

    def body(h_ref, t_ref, loss_ref, dh_ref):
        i = pl.program_id(0)

        @pl.when(i == 0)
        def _():
            loss_ref[...] = jnp.zeros(loss_ref.shape, F32)
            dh_ref[...] = jnp.zeros(dh_ref.shape, F32)

        @pl.when(i > 0)
        def _():
            err = h_ref[...] - t_ref[...]
            dh_ref[...] = err * (1.0 / D_MODEL)
            loss_ref[...] += 0.5 * jnp.sum(jnp.mean(err * err, axis=-1, keepdims=True))

    return pl.pallas_call(
        body, name="loss_head",
        out_shape=(jax.ShapeDtypeStruct((8, 128), F32), jax.ShapeDtypeStruct((n_rows, D_MODEL), F32)),
        grid=(n_blk,),
        in_specs=[pl.BlockSpec((HEAD_ROWS, D_MODEL), lambda i: (i, 0)),
                  pl.BlockSpec((HEAD_ROWS, D_MODEL), lambda i: (jnp.maximum(i - 1, 0), 0))],
        out_specs=(pl.BlockSpec((8, 128), lambda i: (0, 0)), pl.BlockSpec((HEAD_ROWS, D_MODEL), lambda i: (i, 0))),
        compiler_params=_cp("arbitrary"),
    )(h, target)


PROJ_CHUNKS = (0, 1024, 2048, 3072, 4096, P_IN)


def _norm_proj(h_res, gain, w_in):
    n_rows = h_res.shape[0]
    tm = _row_tile(n_rows)

    def body(x_ref, g_ref, w_ref, hb_ref, p_ref):
        h, _, _ = _rms(x_ref[...], g_ref[...])
        hb = h.astype(BF16)
        hb_ref[...] = hb
        for c0, c1 in zip(PROJ_CHUNKS[:-1], PROJ_CHUNKS[1:]):
            p_ref[:, c0:c1] = jnp.dot(hb, w_ref[:, c0:c1], preferred_element_type=F32)

    return pl.pallas_call(
        body, name="norm_proj",
        out_shape=(jax.ShapeDtypeStruct((n_rows, D_MODEL), BF16), jax.ShapeDtypeStruct((n_rows, P_IN), F32)),
        grid=(n_rows // tm,),
        in_specs=[pl.BlockSpec((tm, D_MODEL), lambda i: (i, 0)), pl.BlockSpec((1, D_MODEL), lambda i: (0, 0)),
                  pl.BlockSpec((D_MODEL, P_IN), lambda i: (0, 0))],
        out_specs=(pl.BlockSpec((tm, D_MODEL), lambda i: (i, 0)), pl.BlockSpec((tm, P_IN), lambda i: (i, 0))),
        compiler_params=_cp("parallel"),
    )(h_res, gain, w_in)


def _pool_math(u, halo, z, wg_ref, scale, row0):
    tm = u.shape[0]
    ext = jnp.concatenate([halo, u], axis=0)
    t1 = (row0 + lax.broadcasted_iota(jnp.int32, (tm, 1), 0) - (PAD_FRONT - 1)).astype(F32)
    mixed, yg = [], []
    for g, w in enumerate(POOL_WINDOWS):
        a = ext[:, g * 128:(g + 1) * 128]
        k = 1
        while k < w:
            a = a + pltpu.roll(a, k, 0)
            k *= 2
        cnt = jnp.clip(t1, 1.0, float(w))
        mg = a[HALO:, :] / cnt - u[:, g * 128:(g + 1) * 128]
        mixed.append(mg)
        yg.append(_mm(mg, wg_ref[g]))
    mixed = jnp.concatenate(mixed, axis=1)
    yg = jnp.concatenate(yg, axis=1)
    ys = yg * scale
    sig = _sigmoid(z)
    return mixed, yg, ys, sig


def _halo_above(tm):
    return lambda i: (jnp.maximum(i * (tm // HALO) - 1, 0), 0)


def _pool_fwd(proj, wg, scale, w_up):
    n_rows = proj.shape[0]
    tm = _row_tile(n_rows)

    def body(uz_ref, halo_ref, wg_ref, sc_ref, wup_ref, y_ref):
        uz = uz_ref[...]
        u, z = uz[:, :POOL_WIDTH], uz[:, POOL_WIDTH:]
        _, _, ys, sig = _pool_math(u, halo_ref[...], z, wg_ref, sc_ref[...], pl.program_id(0) * tm)
        y_ref[...] = _mm(ys * (z * sig), wup_ref[...])

    return pl.pallas_call(
        body, name="pool_fwd", out_shape=jax.ShapeDtypeStruct((n_rows, D_MODEL), F32), grid=(n_rows // tm,),
        in_specs=[pl.BlockSpec((tm, 1024), lambda i: (i, 0)), pl.BlockSpec((HALO, POOL_WIDTH), _halo_above(tm)),
                  pl.BlockSpec((4, 128, 128), lambda i: (0, 0, 0)), pl.BlockSpec((1, POOL_WIDTH), lambda i: (0, 0)),
                  pl.BlockSpec((POOL_WIDTH, D_MODEL), lambda i: (0, 0))],
        out_specs=pl.BlockSpec((tm, D_MODEL), lambda i: (i, 0)),
        compiler_params=_cp("parallel"),
    )(proj, proj, wg, scale, w_up)


def _mla_pre_fwd(proj, tab, gqa, gkva, wq, wk, wv, gqn, gkn):
    n_rows = proj.shape[0]
    tm = _row_tile(n_rows)

    def body(c_ref, kr_ref, tab_ref, gqa_ref, gkva_ref, wq_ref, wk_ref, wv_ref, gqn_ref, gkn_ref, q_ref, k_ref, v_ref):
        c, tab = c_ref[...], tab_ref[...]
        cqn, _, _ = _rms(c[:, :Q_RANK], gqa_ref[...])
        ckvn, _, _ = _rms(c[:, Q_RANK:], gkva_ref[...])
        qp, kp, vp = _mm(cqn, wq_ref[...]), _mm(ckvn, wk_ref[...]), _mm(ckvn, wv_ref[...])
        kr = pltpu.roll(kr_ref[...], QK_NOPE, 1)
        gqn, gkn = _pad_gain(gqn_ref), _pad_gain(gkn_ref)
        lane = lax.broadcasted_iota(jnp.int32, (tm, 128), 1)
        one = jnp.where(lane == V_DIM, 1.0, 0.0)
        for h in range(N_HEADS):
            blk = slice(128 * h, 128 * (h + 1))
            qn, _, _ = _head_rms(qp[:, blk], gqn)
            q_ref[h] = _rope(qn, tab).astype(BF16)
            kn, _, _ = _head_rms(kp[:, blk] + kr, gkn)
            k_ref[h] = _rope(kn, tab).astype(BF16)
            v_ref[h] = (vp[:, blk] + one).astype(BF16)

    full = lambda *s: pl.BlockSpec(s, lambda i: (0,) * len(s))
    head = pl.BlockSpec((N_HEADS, tm, 128), lambda i: (0, i, 0))
    sds = jax.ShapeDtypeStruct((N_HEADS, n_rows, 128), BF16)
    return pl.pallas_call(
        body, name="mla_pre_fwd", out_shape=(sds, sds, sds), grid=(n_rows // tm,),
        in_specs=[pl.BlockSpec((tm, 1024), lambda i: (i, COL_CQKV // 1024)),
                  pl.BlockSpec((tm, 128), lambda i: (i, COL_KR // 128)),
                  pl.BlockSpec((tm, 384), lambda i: (i, 0)),
                  full(1, Q_RANK), full(1, KV_RANK), full(Q_RANK, 1024), full(KV_RANK, 1024), full(KV_RANK, 1024),
                  full(1, QK_DIM), full(1, QK_DIM)],
        out_specs=(head, head, head),
        compiler_params=_cp("parallel"),
    )(proj, proj, tab, gqa, gkva, wq, wk, wv, gqn, gkn)


def _attn_mask(row0, col0, n_r, n_c):
    row = row0 + lax.broadcasted_iota(jnp.int32, (n_r, n_c), 0)
    col = col0 + lax.broadcasted_iota(jnp.int32, (n_r, n_c), 1)
    return (col <= row) & (col >= PAD_FRONT)


ATTN_SCALE = 1.0 / math.sqrt(QK_DIM)
EXP2_SCALE = ATTN_SCALE * math.log2(math.e)
LOG2_E = math.log2(math.e)


def _row_chunks(t, size):
    return [(r, min(size, t - r)) for r in range(0, t, size)]


def _attn_fwd(q, k, v, ex=None):
    n_rows = q.shape[1]
    t = _attn_tile(n_rows)
    nb = n_rows // t

    def body(q_ref, k_ref, v_ref, o_ref, lse_ref, m_sc, acc_sc):
        qi, ki = pl.program_id(1), pl.program_id(2)

        @pl.when(ki == 0)
        def _():
            m_sc[...] = jnp.full(m_sc.shape, -jnp.inf, F32)
            acc_sc[...] = jnp.zeros(acc_sc.shape, F32)

        def update(masked):
            heads = range(hp)
            s = [_mm_nt(q_ref[h], k_ref[h]) for h in heads]
            if masked:
                mask = _attn_mask(qi * t, ki * t, t, t)
                s = [jnp.where(mask, sh, MASK_VALUE) for sh in s]
            p = []
            for h in heads:
                m_prev = m_sc[h]
                m_new = jnp.maximum(m_prev, jnp.max(s[h], axis=1, keepdims=True))
                alpha = jnp.exp2((m_prev - m_new) * EXP2_SCALE)
                p.append(jnp.exp2((s[h] - m_new) * EXP2_SCALE).astype(BF16))
                acc_sc[h] = alpha * acc_sc[h]
                m_sc[h] = m_new
            for h in heads:
                acc_sc[h] += _mm(p[h], v_ref[h])

        edge = (ki == qi) | (ki == 0)

        @pl.when((ki <= qi) & edge)
        def _():
            update(True)

        @pl.when((ki < qi) & (ki > 0))
        def _():
            update(False)

        @pl.when(ki == nb - 1)
        def _():
            acc = acc_sc[...]
            denom = acc[:, :, V_DIM:V_DIM + 1]
            o_ref[...] = acc[:, :, :V_DIM] / denom
            lse_ref[...] = m_sc[...] * ATTN_SCALE + jnp.log(denom)

    hp = HEADS_PER_STEP_FWD
    kv_idx = lambda h, qi, ki: (h, jnp.minimum(ki, qi), 0)
    return _call_carrying(
        ex, body, "attn_fwd",
        out_shape=(jax.ShapeDtypeStruct((N_HEADS, n_rows, V_DIM), F32), jax.ShapeDtypeStruct((N_HEADS, n_rows, 1), F32)),
        grid=(N_HEADS // hp, nb, nb),
        in_specs=[pl.BlockSpec((hp, t, 128), lambda h, qi, ki: (h, qi, 0)), pl.BlockSpec((hp, t, 128), kv_idx),
                  pl.BlockSpec((hp, t, 128), kv_idx)],
        out_specs=(pl.BlockSpec((hp, t, V_DIM), lambda h, qi, ki: (h, qi, 0)), pl.BlockSpec((hp, t, 1), lambda h, qi, ki: (h, qi, 0))),
        scratch_shapes=[pltpu.VMEM((hp, t, 1), F32), pltpu.VMEM((hp, t, 128), F32)],
        semantics=("parallel", "parallel", "arbitrary"), args=(q, k, v))


def _merge_math(o_ref, z, gates, y_pool, y_mla_fn):
    o_cat = jnp.concatenate([o_ref[h] for h in range(N_HEADS)], axis=1)
    sig_z = _sigmoid(z)
    a_mla = o_cat * (z * sig_z)
    y_mla = y_mla_fn(a_mla)
    sgp, sgm = _sigmoid(gates[:, :D_MODEL]), _sigmoid(gates[:, D_MODEL:])
    merged = sgp * y_pool + sgm * y_mla
    return o_cat, sig_z, a_mla, y_mla, sgp, sgm, merged


def _mla_post_fwd(h_res, proj, o, y_pool, w_mla_up, w_out):
    n_rows = h_res.shape[0]
    tm = _row_tile(n_rows)

    def body(h_ref, g_ref, z_ref, o_ref, yp_ref, wup_ref, wout_ref, ymla_ref, hn_ref):
        _, _, _, y_mla, _, _, merged = _merge_math(o_ref, z_ref[...], g_ref[...], yp_ref[...],
                                                   lambda a: _mm(a, wup_ref[...]))
        ymla_ref[...] = y_mla
        hn_ref[...] = h_ref[...] + _mm(merged, wout_ref[...])

    row = lambda w, c: pl.BlockSpec((tm, w), lambda i: (i, c))
    return pl.pallas_call(
        body, name="mla_post_fwd",
        out_shape=(jax.ShapeDtypeStruct((n_rows, D_MODEL), F32), jax.ShapeDtypeStruct((n_rows, D_MODEL), F32)),
        grid=(n_rows // tm,),
        in_specs=[row(D_MODEL, 0), row(2048, COL_GATES // 2048), row(MLA_WIDTH, COL_ZMLA // MLA_WIDTH),
                  pl.BlockSpec((N_HEADS, tm, V_DIM), lambda i: (0, i, 0)), row(D_MODEL, 0),
                  pl.BlockSpec((MLA_WIDTH, D_MODEL), lambda i: (0, 0)), pl.BlockSpec((D_MODEL, D_MODEL), lambda i: (0, 0))],
        out_specs=(row(D_MODEL, 0), row(D_MODEL, 0)),
        compiler_params=_cp("parallel"),
    )(h_res, proj, proj, o, y_pool, w_mla_up, w_out)


def _dsilu(z, sig):
    return sig * (1.0 + z * (1.0 - sig))


def _acc(ref, val, first):
    @pl.when(first)
    def _():
        ref[...] = val

    @pl.when(jnp.logical_not(first))
    def _():
        ref[...] += val


def _mla_post_bwd(dh, proj, o, y_pool, y_mla, w_mla_up, w_out):
    n_rows = dh.shape[0]
    tm = _row_tile(n_rows)

    def body(dh_ref, g_ref, z_ref, o_ref, yp_ref, ym_ref, wup_ref, wout_ref,
             dgz_ref, dyp_ref, do_ref, delta_ref, dwout_ref, dwup_ref):
        first = pl.program_id(0) == 0
        z, y_pool, y_mla = z_ref[...], yp_ref[...], ym_ref[...]
        o_cat, sig_z, a_mla, _, sgp, sgm, merged = _merge_math(o_ref, z, g_ref[...], y_pool, lambda a: y_mla)
        dhv = dh_ref[...]
        dmerged = _mm_nt(dhv, wout_ref[...])
        _acc(dwout_ref, _mm_tn(merged, dhv), first)
        dyp_ref[...] = dmerged * sgp
        dy_mla = dmerged * sgm
        da = _mm_nt(dy_mla, wup_ref[...])
        _acc(dwup_ref, _mm_tn(a_mla, dy_mla), first)
        do_cat = da * (z * sig_z)
        dgz_ref[:, 0:D_MODEL] = dmerged * y_pool * (sgp * (1.0 - sgp))
        dgz_ref[:, D_MODEL:2 * D_MODEL] = dmerged * y_mla * (sgm * (1.0 - sgm))
        dgz_ref[:, 2 * D_MODEL:] = da * o_cat * _dsilu(z, sig_z)
        for h in range(N_HEADS):
            doh = do_cat[:, V_DIM * h:V_DIM * (h + 1)]
            do_ref[h] = doh
            delta_ref[h] = jnp.sum(doh * o_cat[:, V_DIM * h:V_DIM * (h + 1)], axis=1, keepdims=True)

    row = lambda w, c: pl.BlockSpec((tm, w), lambda i: (i, c))
    head = lambda w: pl.BlockSpec((N_HEADS, tm, w), lambda i: (0, i, 0))
    const = lambda r, c: pl.BlockSpec((r, c), lambda i: (0, 0))
    return pl.pallas_call(
        body, name="mla_post_bwd",
        out_shape=(jax.ShapeDtypeStruct((n_rows, 2560), F32), jax.ShapeDtypeStruct((n_rows, D_MODEL), F32),
                   jax.ShapeDtypeStruct((N_HEADS, n_rows, V_DIM), F32), jax.ShapeDtypeStruct((N_HEADS, n_rows, 1), F32),
                   jax.ShapeDtypeStruct((D_MODEL, D_MODEL), F32), jax.ShapeDtypeStruct((MLA_WIDTH, D_MODEL), F32)),
        grid=(n_rows // tm,),
        in_specs=[row(D_MODEL, 0), row(2048, COL_GATES // 2048), row(MLA_WIDTH, COL_ZMLA // MLA_WIDTH), head(V_DIM),
                  row(D_MODEL, 0), row(D_MODEL, 0), const(MLA_WIDTH, D_MODEL), const(D_MODEL, D_MODEL)],
        out_specs=(row(2560, 0), row(D_MODEL, 0), head(V_DIM), head(1), const(D_MODEL, D_MODEL), const(MLA_WIDTH, D_MODEL)),
        compiler_params=_cp("arbitrary"),
    )(dh, proj, proj, o, y_pool, y_mla, w_mla_up, w_out)


def _attn_bwd(q, k, v, do, lse, delta, ex=None):
    n_rows = q.shape[1]
    t = _attn_tile(n_rows)
    nb = n_rows // t

    def body(q_ref, k_ref, v_ref, do_ref, lse_ref, delta_ref, dq_ref, dk_ref, dv_ref, dk_sc, dv_sc):
        ki, qi = pl.program_id(1), pl.program_id(2)

        @pl.when((ki == 0) & (qi == 0))
        def _():
            dq_ref[...] = jnp.zeros(dq_ref.shape, F32)

        @pl.when(qi == 0)
        def _():
            dk_sc[...] = jnp.zeros(dk_sc.shape, F32)
            dv_sc[...] = jnp.zeros(dv_sc.shape, F32)

        def step(masked):
            heads = range(HEADS_PER_STEP)
            s = [_mm_nt(q_ref[h], k_ref[h]) for h in heads]
            dp = [_mm_nt(do_ref[h], v_ref[h, :, 0:V_DIM]) for h in heads]
            if masked:
                mask = _attn_mask(qi * t, ki * t, t, t)
                s = [jnp.where(mask, sh, MASK_VALUE) for sh in s]
            p = [jnp.exp2(s[h] * EXP2_SCALE - lse_ref[h] * LOG2_E) for h in heads]
            ds = [p[h] * (dp[h] - delta_ref[h]) for h in heads]
            rows = pl.ds(pl.multiple_of(qi * t, t), t)
            for h in heads:
                dv_sc[h] += _mm_tn(p[h], do_ref[h])
                dq_ref[h, rows, :] += _mm(ds[h], k_ref[h]) * ATTN_SCALE
                dk_sc[h] += _mm_tn(ds[h], q_ref[h])

        edge = (ki == qi) | (ki == 0)

        @pl.when((qi >= ki) & edge)
        def _():
            step(True)

        @pl.when((qi > ki) & (ki > 0))
        def _():
            step(False)

        @pl.when(qi == nb - 1)
        def _():
            dk_ref[...] = dk_sc[...] * ATTN_SCALE
            dv_ref[...] = dv_sc[...]

    hp = HEADS_PER_STEP
    q_idx = lambda h, ki, qi: (h, jnp.maximum(qi, ki), 0)
    k_idx = lambda h, ki, qi: (h, ki, 0)
    return _call_carrying(
        ex, body, "attn_bwd",
        out_shape=(jax.ShapeDtypeStruct((N_HEADS, n_rows, 128), F32), jax.ShapeDtypeStruct((N_HEADS, n_rows, 128), F32),
                   jax.ShapeDtypeStruct((N_HEADS, n_rows, V_DIM), F32)),
        grid=(N_HEADS // hp, nb, nb),
        in_specs=[pl.BlockSpec((hp, t, 128), q_idx), pl.BlockSpec((hp, t, 128), k_idx), pl.BlockSpec((hp, t, 128), k_idx),
                  pl.BlockSpec((hp, t, V_DIM), q_idx), pl.BlockSpec((hp, t, 1), q_idx), pl.BlockSpec((hp, t, 1), q_idx)],
        out_specs=(pl.BlockSpec((hp, n_rows, 128), lambda h, ki, qi: (h, 0, 0), pipeline_mode=pl.Buffered(1)),
                   pl.BlockSpec((hp, t, 128), k_idx),
                   pl.BlockSpec((hp, t, V_DIM), k_idx)),
        scratch_shapes=[pltpu.VMEM((hp, t, 128), F32), pltpu.VMEM((hp, t, V_DIM), F32)],
        semantics=("parallel", "arbitrary", "arbitrary"), args=(q, k, v, do, lse, delta))


def _mla_pre_bwd(proj, tab, gqa, gkva, wq, wk, wv, gqn, gkn, dq, dk, dv):
    n_rows = proj.shape[0]
    tm = _row_tile(n_rows)

    def body(c_ref, kr_ref, tab_ref, gqa_ref, gkva_ref, wq_ref, wk_ref, wv_ref, gqn_ref, gkn_ref, dq_ref, dk_ref, dv_ref,
             dc_ref, dkr_ref, dwq_ref, dwk_ref, dwv_ref, dgqa_ref, dgkva_ref, dgqn_ref, dgkn_ref):
        first = pl.program_id(0) == 0
        c, tab = c_ref[...], tab_ref[...]
        gqa, gkva = gqa_ref[...], gkva_ref[...]
        cqn, cq_h, cq_inv = _rms(c[:, :Q_RANK], gqa)
        ckvn, ckv_h, ckv_inv = _rms(c[:, Q_RANK:], gkva)
        qp, kp = _mm(cqn, wq_ref[...]), _mm(ckvn, wk_ref[...])
        kr = pltpu.roll(kr_ref[...], QK_NOPE, 1)
        gqn, gkn = _pad_gain(gqn_ref), _pad_gain(gkn_ref)
        dq_parts, dk_parts, dv_parts = [], [], []
        dkr = jnp.zeros((tm, 128), F32)
        dgqn = jnp.zeros((1, 128), F32)
        dgkn = jnp.zeros((1, 128), F32)
        zv = jnp.zeros((tm, 128 - V_DIM), F32)
        for h in range(N_HEADS):
            blk = slice(128 * h, 128 * (h + 1))
            _, xh, inv = _head_rms(qp[:, blk], gqn)
            dx, dg = _head_rms_bwd(_rope_bwd(dq_ref[h], tab), xh, inv, gqn)
            dq_parts.append(dx)
            dgqn = dgqn + dg
            _, xh, inv = _head_rms(kp[:, blk] + kr, gkn)
            dx, dg = _head_rms_bwd(_rope_bwd(dk_ref[h], tab), xh, inv, gkn)
            dk_parts.append(dx)
            dgkn = dgkn + dg
            dkr = dkr + dx
            dv_parts.append(jnp.concatenate([dv_ref[h], zv], axis=1))
        dqp = jnp.concatenate(dq_parts, axis=1)
        dkp = jnp.concatenate(dk_parts, axis=1)
        dvp = jnp.concatenate(dv_parts, axis=1)
        _acc(dwq_ref, _mm_tn(cqn, dqp), first)
        _acc(dwk_ref, _mm_tn(ckvn, dkp), first)
        _acc(dwv_ref, _mm_tn(ckvn, dvp), first)
        dcq, dg1 = _rms_bwd(_mm_nt(dqp, wq_ref[...]), cq_h, cq_inv, gqa)
        dckv, dg2 = _rms_bwd(_mm_nt(dkp, wk_ref[...]) + _mm_nt(dvp, wv_ref[...]), ckv_h, ckv_inv, gkva)
        _acc(dgqa_ref, dg1, first)
        _acc(dgkva_ref, dg2, first)
        _acc(dgqn_ref, dgqn, first)
        _acc(dgkn_ref, dgkn, first)
        dc_ref[...] = jnp.concatenate([dcq, dckv], axis=1)
        lane = lax.broadcasted_iota(jnp.int32, (tm, 128), 1)
        dkr_ref[...] = jnp.where(lane < QK_ROPE, pltpu.roll(dkr, 128 - QK_NOPE, 1), 0.0)

    full = lambda *s: pl.BlockSpec(s, lambda i: (0,) * len(s))
    head = lambda w: pl.BlockSpec((N_HEADS, tm, w), lambda i: (0, i, 0))
    return pl.pallas_call(
        body, name="mla_pre_bwd",
        out_shape=(jax.ShapeDtypeStruct((n_rows, 1024), F32), jax.ShapeDtypeStruct((n_rows, 128), F32),
                   jax.ShapeDtypeStruct((Q_RANK, 1024), F32), jax.ShapeDtypeStruct((KV_RANK, 1024), F32),
                   jax.ShapeDtypeStruct((KV_RANK, 1024), F32),
                   jax.ShapeDtypeStruct((1, Q_RANK), F32), jax.ShapeDtypeStruct((1, KV_RANK), F32),
                   jax.ShapeDtypeStruct((1, 128), F32), jax.ShapeDtypeStruct((1, 128), F32)),
        grid=(n_rows // tm,),
        in_specs=[pl.BlockSpec((tm, 1024), lambda i: (i, COL_CQKV // 1024)),
                  pl.BlockSpec((tm, 128), lambda i: (i, COL_KR // 128)),
                  pl.BlockSpec((tm, 384), lambda i: (i, 0)),
                  full(1, Q_RANK), full(1, KV_RANK), full(Q_RANK, 1024), full(KV_RANK, 1024), full(KV_RANK, 1024),
                  full(1, QK_DIM), full(1, QK_DIM), head(128), head(128), head(V_DIM)],
        out_specs=(pl.BlockSpec((tm, 1024), lambda i: (i, 0)), pl.BlockSpec((tm, 128), lambda i: (i, 0)),
                   full(Q_RANK, 1024), full(KV_RANK, 1024), full(KV_RANK, 1024), full(1, Q_RANK), full(1, KV_RANK),
                   full(1, 128), full(1, 128)),
        compiler_params=_cp("arbitrary"),
    )(proj, proj, tab, gqa, gkva, wq, wk, wv, gqn, gkn, dq, dk, dv)


def _pool_bwd_a(proj, dy_pool, wg, scale, w_up):
    n_rows = proj.shape[0]
    tm = _row_tile(n_rows)

    def body(uz_ref, halo_ref, dy_ref, wg_ref, sc_ref, wup_ref, dmz_ref, dwg_ref, dsc_ref, dwup_ref):
        first = pl.program_id(0) == 0
        uz = uz_ref[...]
        u, z = uz[:, :POOL_WIDTH], uz[:, POOL_WIDTH:]
        scale_v = sc_ref[...]
        mixed, yg, ys, sig = _pool_math(u, halo_ref[...], z, wg_ref, scale_v, pl.program_id(0) * tm)
        sp = z * sig
        dy = dy_ref[...]
        da = _mm_nt(dy, wup_ref[...])
        _acc(dwup_ref, _mm_tn(ys * sp, dy), first)
        dys = da * sp
        _acc(dsc_ref, jnp.sum(dys * yg, axis=0, keepdims=True), first)
        dyg = dys * scale_v
        for g in range(4):
            cols = slice(g * 128, (g + 1) * 128)
            dmz_ref[:, cols] = _mm_nt(dyg[:, cols], wg_ref[g])
            _acc(dwg_ref.at[g], _mm_tn(mixed[:, cols], dyg[:, cols]), first)
        dmz_ref[:, POOL_WIDTH:] = da * ys * _dsilu(z, sig)

    return pl.pallas_call(
        body, name="pool_bwd_a",
        out_shape=(jax.ShapeDtypeStruct((n_rows, 1024), F32), jax.ShapeDtypeStruct((4, 128, 128), F32),
                   jax.ShapeDtypeStruct((1, POOL_WIDTH), F32), jax.ShapeDtypeStruct((POOL_WIDTH, D_MODEL), F32)),
        grid=(n_rows // tm,),
        in_specs=[pl.BlockSpec((tm, 1024), lambda i: (i, 0)), pl.BlockSpec((HALO, POOL_WIDTH), _halo_above(tm)),
                  pl.BlockSpec((tm, D_MODEL), lambda i: (i, 0)),
                  pl.BlockSpec((4, 128, 128), lambda i: (0, 0, 0)), pl.BlockSpec((1, POOL_WIDTH), lambda i: (0, 0)),
                  pl.BlockSpec((POOL_WIDTH, D_MODEL), lambda i: (0, 0))],
        out_specs=(pl.BlockSpec((tm, 1024), lambda i: (i, 0)), pl.BlockSpec((4, 128, 128), lambda i: (0, 0, 0)),
                   pl.BlockSpec((1, POOL_WIDTH), lambda i: (0, 0)), pl.BlockSpec((POOL_WIDTH, D_MODEL), lambda i: (0, 0))),
        compiler_params=_cp("arbitrary"),
    )(proj, proj, dy_pool, wg, scale, w_up)


def _pool_bwd_b(dmz):
    n_rows = dmz.shape[0]
    tm = _row_tile(n_rows)
    n_tiles = n_rows // tm
    n_ext = tm + HALO

    def body(dmz_ref, halo_ref, o_ref):
        i = pl.program_id(0)
        v = dmz_ref[...]
        dm = v[:, :POOL_WIDTH]
        halo = jnp.where(i == n_tiles - 1, 0.0, halo_ref[...])
        ext = jnp.concatenate([dm, halo], axis=0)
        t1 = (i * tm + lax.broadcasted_iota(jnp.int32, (n_ext, 1), 0) - (PAD_FRONT - 1)).astype(F32)
        du = []
        for g, w in enumerate(POOL_WINDOWS):
            cols = slice(g * 128, (g + 1) * 128)
            a = ext[:, cols] / jnp.clip(t1, 1.0, float(w))
            k = 1
            while k < w:
                a = a + pltpu.roll(a, n_ext - k, 0)
                k *= 2
            du.append(a[:tm, :] - dm[:, cols])
        o_ref[...] = jnp.concatenate(du + [v[:, POOL_WIDTH:]], axis=1)

    last_halo = n_rows // HALO - 1
    return pl.pallas_call(
        body, name="pool_bwd_b", out_shape=jax.ShapeDtypeStruct((n_rows, 1024), F32), grid=(n_tiles,),
        in_specs=[pl.BlockSpec((tm, 1024), lambda i: (i, 0)),
                  pl.BlockSpec((HALO, POOL_WIDTH), lambda i: (jnp.minimum((i + 1) * (tm // HALO), last_halo), 0))],
        out_specs=pl.BlockSpec((tm, 1024), lambda i: (i, 0)),
        compiler_params=_cp("parallel"),
    )(dmz, dmz)


def _norm_proj_bwd(d_uz, d_cqkv, d_gz, d_kr, w_in, h_res, gain, dh_out):
    n_rows = h_res.shape[0]
    tm = _row_tile(n_rows)
    pieces = ((COL_UZ, 1024), (COL_CQKV, 1024), (COL_GATES, 2560), (COL_KR, 128))

    def body(a_ref, b_ref, c_ref, d_ref, w_ref, x_ref, g_ref, dho_ref, dhi_ref, dg_ref):
        i = pl.program_id(0)
        dh = None
        for ref, (c0, wd) in zip((a_ref, b_ref, c_ref, d_ref), pieces):
            part = _mm_nt(ref[...], w_ref[:, c0:c0 + wd])
            dh = part if dh is None else dh + part
        g = g_ref[...]
        _, xh, inv = _rms(x_ref[...], g)
        dx, dg = _rms_bwd(dh, xh, inv, g)
        _acc(dg_ref, dg, i == 0)
        row = i * tm + lax.broadcasted_iota(jnp.int32, (tm, 1), 0)
        dhi_ref[...] = jnp.where(row >= PAD_FRONT, dho_ref[...] + dx, 0.0)

    row = lambda w: pl.BlockSpec((tm, w), lambda i: (i, 0))
    return pl.pallas_call(
        body, name="norm_proj_bwd",
        out_shape=(jax.ShapeDtypeStruct((n_rows, D_MODEL), F32), jax.ShapeDtypeStruct((1, D_MODEL), F32)),
        grid=(n_rows // tm,),
        in_specs=[row(1024), row(1024), row(2560), row(128), pl.BlockSpec((D_MODEL, P_IN), lambda i: (0, 0)),
                  row(D_MODEL), pl.BlockSpec((1, D_MODEL), lambda i: (0, 0)), row(D_MODEL)],
        out_specs=(row(D_MODEL), pl.BlockSpec((1, D_MODEL), lambda i: (0, 0))),
        compiler_params=_cp("arbitrary"),
    )(d_uz, d_cqkv, d_gz, d_kr, w_in, h_res, gain, dh_out)


def _weight_grad(name, a, b):
    k_rows, m = a.shape
    n = b.shape[1]
    tk = next(t for t in (1664, 640, 128) if k_rows % t == 0)
    tn = next(t for t in (1280, 1024, 512, 128) if n % t == 0)

    def body(a_ref, b_ref, o_ref):
        _acc(o_ref, _mm_tn(a_ref[...], b_ref[...]), pl.program_id(1) == 0)

    return pl.pallas_call(
        body, name=name, out_shape=jax.ShapeDtypeStruct((m, n), F32), grid=(n // tn, k_rows // tk),
        in_specs=[pl.BlockSpec((tk, m), lambda j, k: (k, 0)), pl.BlockSpec((tk, tn), lambda j, k: (k, j))],
        out_specs=pl.BlockSpec((m, tn), lambda j, k: (0, j)),
        compiler_params=_cp("parallel", "arbitrary"),
    )(a, b)


def _adamw(name, w, m, v, gbufs):
    r, c = w.shape
    depth = len(gbufs)
    r_l = r // depth
    tr = r_l
    if r_l * c * 4 > (1 << 20):
        tr = 256 if r_l % 256 == 0 else 128
    assert r_l * depth == r and r_l % tr == 0, (name, r, depth, tr)
    n_t = r_l // tr

    def body(w_ref, m_ref, v_ref, *refs):
        g_refs, (go_ref, d_ref, mo_ref, vo_ref) = refs[:depth], refs[depth:]

        def update(g_ref):
            g = g_ref[0].astype(F32)
            for s in range(1, N_DEV):
                g = g + g_ref[s].astype(F32)
            go_ref[...] = g
            m_new = ADAM_B1 * m_ref[...] + (1.0 - ADAM_B1) * g
            v_new = ADAM_B2 * v_ref[...] + (1.0 - ADAM_B2) * (g * g)
            mo_ref[...] = m_new
            vo_ref[...] = v_new
            m_hat = m_new / (1.0 - ADAM_B1 ** ADAM_STEP)
            v_hat = v_new / (1.0 - ADAM_B2 ** ADAM_STEP)
            d_ref[...] = -ADAM_LR * (m_hat / (jnp.sqrt(v_hat) + ADAM_EPS) + ADAM_WD * w_ref[...])

        for j in range(depth):
            pl.when(pl.program_id(0) == j)(lambda j=j: update(g_refs[j]))

    def g_spec(j):
        return pl.BlockSpec((N_DEV, tr, c), lambda l, i: (0, jnp.where(l == j, i, jnp.where(l < j, 0, n_t - 1)), 0))

    spec = pl.BlockSpec((tr, c), lambda l, i: (l * n_t + i, 0))
    sds = jax.ShapeDtypeStruct((r, c), F32)
    return pl.pallas_call(
        body, name=name, out_shape=(sds, sds, sds, sds), grid=(depth, n_t),
        in_specs=[spec, spec, spec] + [g_spec(j) for j in range(depth)],
        out_specs=(spec, spec, spec, spec),
        compiler_params=_cp("arbitrary", "arbitrary"),
    )(w, m, v, *gbufs)


SMALL = ("norm_gain", "pool_w_group", "pool_scale", "q_a_norm_gain", "kv_a_norm_gain", "q_norm_gain", "k_norm_gain")


def _pack_small(parts):
    rows = []
    for p in parts:
        flat = p.reshape(-1)
        pad = (-flat.shape[0]) % 1024
        if pad:
            flat = jnp.concatenate([flat, jnp.zeros((pad,), F32)])
        rows.append(flat.reshape(-1, 128))
    n_rows = sum(r.shape[0] for r in rows)
    tail = (-n_rows) % 256
    if tail:
        rows.append(jnp.zeros((tail, 128), F32))
    return jnp.concatenate(rows, axis=0)


def _unpack_small(packed, shapes):
    out, r0 = [], 0
    for shp in shapes:
        n = math.prod(shp)
        n_rows = (n + 1023) // 1024 * 8
        out.append(packed[r0:r0 + n_rows].reshape(-1)[:n].reshape(shp))
        r0 += n_rows
    return out


def kernel(x, positions, meta_tokens, norm_gain, w_in, pool_w_group, pool_scale, pool_w_up, q_a_norm_gain, kv_a_norm_gain, w_q_b, w_kv_b, q_norm_gain, k_norm_gain, mla_w_up, w_out, loss_target, m_meta_tokens, m_norm_gain, m_w_in, m_pool_w_group, m_pool_scale, m_pool_w_up, m_q_a_norm_gain, m_kv_a_norm_gain, m_w_q_b, m_w_kv_b, m_q_norm_gain, m_k_norm_gain, m_mla_w_up, m_w_out, v_meta_tokens, v_norm_gain, v_w_in, v_pool_w_group, v_pool_scale, v_pool_w_up, v_q_a_norm_gain, v_kv_a_norm_gain, v_w_q_b, v_w_kv_b, v_q_norm_gain, v_k_norm_gain, v_mla_w_up, v_w_out):
    x2, target = x[0], loss_target[0]

    big = dict(w_in=w_in, pool_w_up=pool_w_up, w_q_b=w_q_b, w_kv_b=w_kv_b, mla_w_up=mla_w_up, w_out=w_out)
    names_big = list(big)
    shards = {n: _cast_bf16("cast_" + n, w) for n, w in big.items()}

    def layer_shards(l):
        return [shards[n][l] for n in names_big]

    def repack(gathered):
        g_in, g_pup, g_qb, g_kvb, g_mup, g_out = gathered
        return dict(w_in=_repack_w_in(g_in),
                    pool_up=_repack_cols("repack_pool_up", g_pup[:, None], BF16)[0],
                    q_heads=_repack_q_heads(g_qb),
                    kv_heads=_repack_kv_heads(g_kvb),
                    mla_up=_repack_cols("repack_mla_up", g_mup[:, None], BF16)[0],
                    out=_repack_rows("repack_out", g_out[:, None])[0])

    first = _exchange("gather_first", [meta_tokens] + layer_shards(0), [])
    weights = [repack(first[1:])]

    h = _embed(x2, first[0])
    pos = jnp.concatenate([jnp.zeros((PAD_FRONT,), jnp.int32), jnp.arange(N_META, dtype=jnp.int32),
                           positions[0] + N_META]).astype(F32).reshape(-1, 1)
    half = QK_ROPE // 2
    inv_freq = ROPE_THETA ** (-jnp.arange(half, dtype=F32) / half)
    tab = _rope_table(pos, jnp.tile(inv_freq, 128 // half).reshape(1, 128))

    row = lambda a, l: a[l].reshape(1, -1)
    saved = []
    for l in range(DEPTH):
        w = weights[l]
        hb, proj = _norm_proj(h, row(norm_gain, l), w["w_in"])
        y_pool = _pool_fwd(proj, pool_w_group[l], row(pool_scale, l), w["pool_up"])
        q, k, v = _mla_pre_fwd(proj, tab, row(q_a_norm_gain, l), row(kv_a_norm_gain, l), w["q_heads"], *w["kv_heads"],
                               row(q_norm_gain, l), row(k_norm_gain, l))
        ex = _Exchange(layer_shards(l + 1), []) if l + 1 < DEPTH else None
        res = _attn_fwd(q, k, v, ex)
        o, lse = res[0], res[1]
        if ex is not None:
            weights.append(repack(res[2:]))
        y_mla, h_next = _mla_post_fwd(h, proj, o, y_pool, w["mla_up"], w["out"])
        saved.append((h, hb, proj, y_pool, q, k, v, o, lse, y_mla))
        h = h_next

    loss_part, dh = _loss_head(h, target)
    loss = lax.psum(loss_part[0, 0], MESH_AXES)

    grads = {n: [None] * DEPTH for n in SMALL}
    pending = None
    received = [None] * DEPTH
    for l in reversed(range(DEPTH)):
        w = weights[l]
        h_l, hb, proj, y_pool, q, k, v, o, lse, y_mla = saved[l]
        d_gz, dy_pool, do, delta, g_out, g_mla_up = _mla_post_bwd(dh, proj, o, y_pool, y_mla, w["mla_up"], w["out"])
        ex = _Exchange([], pending) if pending is not None else None
        res = _attn_bwd(q, k, v, do, lse, delta, ex)
        dq, dk, dv = res[0], res[1], res[2]
        if ex is not None:
            received[l + 1] = res[3:]
        (d_cqkv, d_kr, g_q, g_k, g_v, grads["q_a_norm_gain"][l], grads["kv_a_norm_gain"][l], g_qn, g_kn) = _mla_pre_bwd(
            proj, tab, row(q_a_norm_gain, l), row(kv_a_norm_gain, l), w["q_heads"], *w["kv_heads"], row(q_norm_gain, l),
            row(k_norm_gain, l), dq, dk, dv)
        grads["q_norm_gain"][l], grads["k_norm_gain"][l] = g_qn[:, :QK_DIM], g_kn[:, :QK_DIM]
        dmz, grads["pool_w_group"][l], grads["pool_scale"][l], g_pool_up = _pool_bwd_a(
            proj, dy_pool, pool_w_group[l], row(pool_scale, l), w["pool_up"])
        d_uz = _pool_bwd_b(dmz)
        g_in = [_weight_grad("dw_in", hb, d) for d in (d_uz, d_cqkv, d_gz, d_kr)]
        dh, grads["norm_gain"][l] = _norm_proj_bwd(d_uz, d_cqkv, d_gz, d_kr, w["w_in"], h_l, row(norm_gain, l), dh)
        pending = [_unpack_w_in(g_in),
                   _unpack_cols("unpack_pool_up", g_pool_up[None], 128, BF16)[:, 0],
                   _unpack_q_heads(g_q),
                   _unpack_kv_heads(g_k, g_v),
                   _unpack_cols("unpack_mla_up", g_mla_up[None], 128, BF16)[:, 0],
                   _unpack_rows("unpack_out", g_out[None], D_MODEL // N_DEV)[:, 0]]

    grad_x = dh[HEAD_ROWS:][None]
    d_meta = dh[PAD_FRONT:HEAD_ROWS]

    small_shapes = [a.shape for a in (norm_gain, pool_w_group, pool_scale, q_a_norm_gain, kv_a_norm_gain, q_norm_gain, k_norm_gain)]
    small_grad = _pack_small([jnp.stack(grads[n], axis=0).reshape(s) for n, s in zip(SMALL, small_shapes)])
    meta_parts = _unpack_cols("unpack_meta", d_meta[None], 128, F32)[:, 0]
    last = _exchange("exchange_last", [small_grad], [meta_parts] + pending)
    small_buf = last[0]
    received[0] = last[2:]
    sharded = ["meta_tokens"] + names_big
    shard_bufs = {"meta_tokens": [last[1]]}
    for j, n in enumerate(names_big):
        shard_bufs[n] = [received[l][j] for l in range(DEPTH)]

    given = dict(meta_tokens=(meta_tokens, m_meta_tokens, v_meta_tokens), w_in=(w_in, m_w_in, v_w_in),
                 pool_w_up=(pool_w_up, m_pool_w_up, v_pool_w_up), w_q_b=(w_q_b, m_w_q_b, v_w_q_b),
                 w_kv_b=(w_kv_b, m_w_kv_b, v_w_kv_b), mla_w_up=(mla_w_up, m_mla_w_up, v_mla_w_up),
                 w_out=(w_out, m_w_out, v_w_out))
    result = {}
    for n in sharded:
        w, m, v = given[n]
        cols = w.shape[-1]
        res = _adamw("adamw_" + n, w.reshape(-1, cols), m.reshape(-1, cols), v.reshape(-1, cols), shard_bufs[n])
        result[n] = [r.reshape(w.shape) for r in res]
    small_w = (norm_gain, pool_w_group, pool_scale, q_a_norm_gain, kv_a_norm_gain, q_norm_gain, k_norm_gain)
    small_m = (m_norm_gain, m_pool_w_group, m_pool_scale, m_q_a_norm_gain, m_kv_a_norm_gain, m_q_norm_gain, m_k_norm_gain)
    small_v = (v_norm_gain, v_pool_w_group, v_pool_scale, v_q_a_norm_gain, v_kv_a_norm_gain, v_q_norm_gain, v_k_norm_gain)
    res = _adamw("adamw_small", _pack_small(small_w), _pack_small(small_m), _pack_small(small_v), [small_buf])
    small_res = [_unpack_small(r, small_shapes) for r in res]
    for j, n in enumerate(SMALL):
        result[n] = [small_res[kind][j] for kind in range(4)]

    order = ("meta_tokens", "norm_gain", "w_in", "pool_w_group", "pool_scale", "pool_w_up", "q_a_norm_gain",
             "kv_a_norm_gain", "w_q_b", "w_kv_b", "q_norm_gain", "k_norm_gain", "mla_w_up", "w_out")
    outs = [loss, grad_x]
    for kind in range(4):
        outs += [result[n][kind] for n in order]
    return tuple(outs)
```

```python
import math

import jax
import jax.numpy as jnp
from jax import lax
from jax.experimental import pallas as pl
from jax.experimental.pallas import tpu as pltpu

F32 = jnp.float32
BF16 = jnp.bfloat16

D_MODEL = 1024
DEPTH = 4
N_META = 16
PAD_FRONT = 112
HEAD_ROWS = PAD_FRONT + N_META
POOL_WIDTH = 512
POOL_WINDOWS = (2, 4, 8, 16)
POOL_GROUP_DIM = 128
HALO = 16
N_HEADS = 8
HEADS_PER_STEP = 4
HEADS_PER_STEP_FWD = 8
QK_NOPE = 64
QK_ROPE = 32
QK_DIM = 96
V_DIM = 64
MLA_WIDTH = 512
Q_RANK = 768
KV_RANK = 256
ROPE_THETA = 10000.0
NORM_EPS = 1e-6
MASK_VALUE = -1e30
D_IN = 4640
N_DEV = 8
IN_SHARD = D_IN // N_DEV

P_IN = 4736
IN_SEGMENTS = ((0, 2048, 0), (2048, 2080, 4608), (2080, 2592, 4096), (2592, 3616, 2048), (3616, 4640, 3072))
COL_UZ, COL_CQKV, COL_GATES, COL_ZMLA, COL_KR = 0, 1024, 2048, 4096, 4608

ADAM_LR = 0.001
ADAM_B1 = 0.9
ADAM_B2 = 0.999
ADAM_EPS = 1e-08
ADAM_WD = 0.01
ADAM_STEP = 10

VMEM_LIMIT = 56 * 1024 * 1024
MESH_AXES = ("x", "y", "c")


def _cp(*sem):
    return pltpu.CompilerParams(dimension_semantics=sem, vmem_limit_bytes=VMEM_LIMIT)


def _row_tile(n_rows):
    return 320 if n_rows % 320 == 0 else 128


def _attn_tile(n_rows):
    return 640 if (n_rows % 640 == 0 and n_rows > 640) else 128


def _mm(a, b):
    return jnp.dot(a.astype(BF16), b.astype(BF16), preferred_element_type=F32)


def _mm_nt(a, b):
    return lax.dot_general(a.astype(BF16), b.astype(BF16), (((1,), (1,)), ((), ())), preferred_element_type=F32)


def _mm_tn(a, b):
    return lax.dot_general(a.astype(BF16), b.astype(BF16), (((0,), (0,)), ((), ())), preferred_element_type=F32)


def _sigmoid(z):
    return 1.0 / (1.0 + jnp.exp(-z))


def _rms(x, g):
    inv = lax.rsqrt(jnp.mean(x * x, axis=-1, keepdims=True) + NORM_EPS)
    xh = x * inv
    return xh * g, xh, inv


def _rms_bwd(dy, xh, inv, g):
    dg = jnp.sum(dy * xh, axis=0, keepdims=True)
    dxh = dy * g
    dx = inv * (dxh - xh * jnp.mean(dxh * xh, axis=-1, keepdims=True))
    return dx, dg


def _rope(x, tab):
    return x * tab[:, 0:128] + pltpu.roll(x, 112, 1) * tab[:, 128:256] + pltpu.roll(x, 16, 1) * tab[:, 256:384]


def _rope_bwd(dy, tab):
    return dy * tab[:, 0:128] + pltpu.roll(dy * tab[:, 128:256], 16, 1) + pltpu.roll(dy * tab[:, 256:384], 112, 1)


def _head_rms(x, g):
    inv = lax.rsqrt(jnp.sum(x * x, axis=-1, keepdims=True) * (1.0 / QK_DIM) + NORM_EPS)
    xh = x * inv
    return xh * g, xh, inv


def _head_rms_bwd(dy, xh, inv, g):
    dg = jnp.sum(dy * xh, axis=0, keepdims=True)
    dxh = dy * g
    dx = inv * (dxh - xh * (jnp.sum(dxh * xh, axis=-1, keepdims=True) * (1.0 / QK_DIM)))
    return dx, dg


def _pad_gain(g_ref):
    return jnp.concatenate([g_ref[...], jnp.zeros((1, 128 - QK_DIM), F32)], axis=1)


class _Exchange:
    def __init__(self, gather_list, scatter_list):
        self.arrays = list(gather_list) + list(scatter_list)
        self.n_g, self.n = len(gather_list), len(self.arrays)
        self.out_shape = [jax.ShapeDtypeStruct((N_DEV,) + a.shape, a.dtype) for a in gather_list]
        self.out_shape += [jax.ShapeDtypeStruct(a.shape, a.dtype) for a in scatter_list]
        self.specs = [pl.BlockSpec(memory_space=pl.ANY)] * self.n
        self.sems = [pltpu.SemaphoreType.DMA((7 * self.n,)), pltpu.SemaphoreType.DMA((7 * self.n,)),
                     pltpu.SemaphoreType.DMA((self.n,))]

    def copies(self, ins, outs, sems):
        send_sems, recv_sems, local_sems = sems
        x, y, c = lax.axis_index("x"), lax.axis_index("y"), lax.axis_index("c")
        me = 4 * x + 2 * y + c

        def src(a, slot):
            return ins[a] if a < self.n_g else ins[a].at[slot]

        out = [pltpu.make_async_copy(src(a, me), outs[a].at[me], local_sems.at[a]) for a in range(self.n)]
        for k in range(1, N_DEV):
            px = 1 - x if k & 4 else x
            py = 1 - y if k & 2 else y
            pc = 1 - c if k & 1 else c
            peer = 4 * px + 2 * py + pc
            for a in range(self.n):
                out.append(pltpu.make_async_remote_copy(
                    src_ref=src(a, peer), dst_ref=outs[a].at[me],
                    send_sem=send_sems.at[a * 7 + k - 1], recv_sem=recv_sems.at[a * 7 + k - 1],
                    device_id=(px, py, pc), device_id_type=pl.DeviceIdType.MESH))
        return out


def _exchange(name, gather_list, scatter_list):
    ex = _Exchange(gather_list, scatter_list)
    n = ex.n

    def body(*refs):
        copies = ex.copies(refs[:n], refs[n:2 * n], refs[2 * n:])
        for cp in copies:
            cp.start()
        for cp in copies:
            cp.wait()

    return pl.pallas_call(
        body, name=name, out_shape=ex.out_shape, in_specs=ex.specs, out_specs=ex.specs, scratch_shapes=ex.sems,
        compiler_params=pltpu.CompilerParams(has_side_effects=True),
    )(*ex.arrays)


def _call_carrying(ex, body, name, out_shape, grid, in_specs, out_specs, scratch_shapes, semantics, args):
    if ex is None:
        return pl.pallas_call(body, name=name, out_shape=out_shape, grid=grid, in_specs=in_specs, out_specs=out_specs,
                              scratch_shapes=scratch_shapes, compiler_params=_cp(*semantics))(*args)
    n, n_in, n_out, n_scr = ex.n, len(in_specs), len(out_shape), len(scratch_shapes)

    def carrying(*refs):
        ins, refs = refs[:n_in], refs[n_in:]
        c_in, refs = refs[:n], refs[n:]
        outs, refs = refs[:n_out], refs[n_out:]
        c_out, refs = refs[:n], refs[n:]
        scratch, sems = refs[:n_scr], refs[n_scr:]
        ids = [pl.program_id(a) for a in range(len(grid))]
        first, last = ids[0] == 0, ids[0] == grid[0] - 1
        for a in range(1, len(grid)):
            first, last = first & (ids[a] == 0), last & (ids[a] == grid[a] - 1)
        copies = ex.copies(c_in, c_out, sems)

        @pl.when(first)
        def _():
            for cp in copies:
                cp.start()

        body(*ins, *outs, *scratch)

        @pl.when(last)
        def _():
            for cp in copies:
                cp.wait()

    return pl.pallas_call(
        carrying, name=name + "_x", out_shape=list(out_shape) + ex.out_shape, grid=grid,
        in_specs=list(in_specs) + ex.specs, out_specs=list(out_specs) + ex.specs,
        scratch_shapes=list(scratch_shapes) + ex.sems, compiler_params=_cp(*["arbitrary"] * len(grid)),
    )(*args, *ex.arrays)


def _in_pieces():
    runs = []
    for lo, hi, dst in sorted(IN_SEGMENTS, key=lambda s: s[2]):
        col = lo
        while col < hi:
            d = col // IN_SHARD
            end = min(hi, (d + 1) * IN_SHARD)
            runs.append((d, col - d * IN_SHARD, end - d * IN_SHARD))
            col = end
    return runs


def _cast_bf16(name, w):
    shape = w.shape
    w2 = w.reshape(-1, shape[-1])
    r, c = w2.shape
    tr = 512 if r % 512 == 0 else r

    def body(x_ref, o_ref):
        o_ref[...] = x_ref[...].astype(BF16)

    spec = pl.BlockSpec((tr, c), lambda i: (i, 0))
    out = pl.pallas_call(body, name=name, out_shape=jax.ShapeDtypeStruct((r, c), BF16), grid=(r // tr,),
                         in_specs=[spec], out_specs=spec, compiler_params=_cp("parallel"))(w2)
    return out.reshape(shape)


def _repack_w_in(gathered):
    tr = 256
    runs = _in_pieces()

    def body(x_ref, o_ref):
        parts = [x_ref[d, :, a:b].astype(F32) for d, a, b in runs]
        parts.append(jnp.zeros((tr, P_IN - D_IN), F32))
        o_ref[...] = jnp.concatenate(parts, axis=1).astype(BF16)

    return pl.pallas_call(
        body, name="repack_w_in", out_shape=jax.ShapeDtypeStruct((D_MODEL, P_IN), BF16), grid=(D_MODEL // tr,),
        in_specs=[pl.BlockSpec((N_DEV, tr, IN_SHARD), lambda i: (0, i, 0))],
        out_specs=pl.BlockSpec((tr, P_IN), lambda i: (i, 0)),
        compiler_params=_cp("parallel"),
    )(gathered)


def _unpack_w_in(parts):
    tr = 256
    runs = _in_pieces()
    widths = [p.shape[-1] for p in parts]

    def body(a_ref, b_ref, c_ref, d_ref, o_ref):
        full = jnp.concatenate([a_ref[...], b_ref[...], c_ref[...], d_ref[...]], axis=1)
        col = 0
        per_dev = [[] for _ in range(N_DEV)]
        for d, a, b in runs:
            per_dev[d].append((a, full[:, col:col + (b - a)]))
            col += b - a
        for d in range(N_DEV):
            pieces = [v for _, v in sorted(per_dev[d], key=lambda av: av[0])]
            o_ref[d] = (jnp.concatenate(pieces, axis=1) if len(pieces) > 1 else pieces[0]).astype(BF16)

    return pl.pallas_call(
        body, name="unpack_w_in", out_shape=jax.ShapeDtypeStruct((N_DEV, D_MODEL, IN_SHARD), BF16), grid=(D_MODEL // tr,),
        in_specs=[pl.BlockSpec((tr, w), lambda i: (i, 0)) for w in widths],
        out_specs=pl.BlockSpec((N_DEV, tr, IN_SHARD), lambda i: (0, i, 0)),
        compiler_params=_cp("parallel"),
    )(*parts)


def _repack_cols(name, gathered, dtype):
    _, depth, k, n = gathered.shape

    def body(x_ref, o_ref):
        o_ref[0] = jnp.concatenate([x_ref[d, 0] for d in range(N_DEV)], axis=1).astype(dtype)

    return pl.pallas_call(
        body, name=name, out_shape=jax.ShapeDtypeStruct((depth, k, N_DEV * n), dtype), grid=(depth,),
        in_specs=[pl.BlockSpec((N_DEV, 1, k, n), lambda l: (0, l, 0, 0))],
        out_specs=pl.BlockSpec((1, k, N_DEV * n), lambda l: (l, 0, 0)),
        compiler_params=_cp("parallel"),
    )(gathered)


def _unpack_cols(name, full, n, dtype):
    depth, k, _ = full.shape

    def body(x_ref, o_ref):
        for d in range(N_DEV):
            o_ref[d, 0] = x_ref[0, :, d * n:(d + 1) * n].astype(dtype)

    return pl.pallas_call(
        body, name=name, out_shape=jax.ShapeDtypeStruct((N_DEV, depth, k, n), dtype), grid=(depth,),
        in_specs=[pl.BlockSpec((1, k, N_DEV * n), lambda l: (l, 0, 0))],
        out_specs=pl.BlockSpec((N_DEV, 1, k, n), lambda l: (0, l, 0, 0)),
        compiler_params=_cp("parallel"),
    )(full)


def _repack_rows(name, gathered):
    _, depth, r, n = gathered.shape

    def body(x_ref, o_ref):
        for d in range(N_DEV):
            o_ref[0, d * r:(d + 1) * r, :] = x_ref[d, 0].astype(BF16)

    return pl.pallas_call(
        body, name=name, out_shape=jax.ShapeDtypeStruct((depth, N_DEV * r, n), BF16), grid=(depth,),
        in_specs=[pl.BlockSpec((N_DEV, 1, r, n), lambda l: (0, l, 0, 0))],
        out_specs=pl.BlockSpec((1, N_DEV * r, n), lambda l: (l, 0, 0)),
        compiler_params=_cp("parallel"),
    )(gathered)


def _unpack_rows(name, full, r):
    depth, _, n = full.shape

    def body(x_ref, o_ref):
        for d in range(N_DEV):
            o_ref[d, 0] = x_ref[0, d * r:(d + 1) * r, :].astype(BF16)

    return pl.pallas_call(
        body, name=name, out_shape=jax.ShapeDtypeStruct((N_DEV, depth, r, n), BF16), grid=(depth,),
        in_specs=[pl.BlockSpec((1, N_DEV * r, n), lambda l: (l, 0, 0))],
        out_specs=pl.BlockSpec((N_DEV, 1, r, n), lambda l: (0, l, 0, 0)),
        compiler_params=_cp("parallel"),
    )(full)


def _repack_q_heads(gathered):
    r = Q_RANK // N_DEV

    def body(x_ref, o_ref):
        z = jnp.zeros((r, 128 - QK_DIM), F32)
        for d in range(N_DEV):
            x = x_ref[d].astype(F32)
            parts = []
            for h in range(N_HEADS):
                parts += [x[:, QK_DIM * h:QK_DIM * (h + 1)], z]
            o_ref[d * r:(d + 1) * r, :] = jnp.concatenate(parts, axis=1).astype(BF16)

    return pl.pallas_call(body, name="repack_q_heads", out_shape=jax.ShapeDtypeStruct((Q_RANK, 1024), BF16),
                          compiler_params=_cp())(gathered)


def _unpack_q_heads(dwq):
    r = Q_RANK // N_DEV

    def body(x_ref, o_ref):
        for d in range(N_DEV):
            x = x_ref[d * r:(d + 1) * r, :]
            o_ref[d] = jnp.concatenate([x[:, 128 * h:128 * h + QK_DIM] for h in range(N_HEADS)], axis=1).astype(BF16)

    return pl.pallas_call(body, name="unpack_q_heads", out_shape=jax.ShapeDtypeStruct((N_DEV, r, Q_RANK), BF16),
                          compiler_params=_cp())(dwq)


def _repack_kv_heads(gathered):
    def body(x_ref, wk_ref, wv_ref):
        z = jnp.zeros((KV_RANK, 128 - QK_NOPE), F32)
        for h in range(N_HEADS):
            x = x_ref[h].astype(F32)
            wk_ref[:, 128 * h:128 * (h + 1)] = jnp.concatenate([x[:, :QK_NOPE], z], axis=1).astype(BF16)
            wv_ref[:, 128 * h:128 * (h + 1)] = jnp.concatenate([x[:, QK_NOPE:], z], axis=1).astype(BF16)

    sds = jax.ShapeDtypeStruct((KV_RANK, 1024), BF16)
    return pl.pallas_call(body, name="repack_kv_heads", out_shape=(sds, sds), compiler_params=_cp())(gathered)


def _unpack_kv_heads(dwk, dwv):
    def body(k_ref, v_ref, o_ref):
        for h in range(N_HEADS):
            o_ref[h] = jnp.concatenate([k_ref[:, 128 * h:128 * h + QK_NOPE], v_ref[:, 128 * h:128 * h + V_DIM]],
                                       axis=1).astype(BF16)

    return pl.pallas_call(body, name="unpack_kv_heads", out_shape=jax.ShapeDtypeStruct((N_DEV, KV_RANK, 128), BF16),
                          compiler_params=_cp())(dwk, dwv)


def _embed(x, meta_g):
    seq = x.shape[0]
    n_blk = seq // HEAD_ROWS + 1

    def body(x_ref, m_ref, o_ref):
        i = pl.program_id(0)

        @pl.when(i == 0)
        def _():
            meta = jnp.concatenate([m_ref[d] for d in range(N_DEV)], axis=1)
            o_ref[...] = jnp.concatenate([jnp.zeros((PAD_FRONT, D_MODEL), F32), meta], axis=0)

        @pl.when(i > 0)
        def _():
            o_ref[...] = x_ref[...]

    return pl.pallas_call(
        body, name="embed", out_shape=jax.ShapeDtypeStruct((seq + HEAD_ROWS, D_MODEL), F32), grid=(n_blk,),
        in_specs=[pl.BlockSpec((HEAD_ROWS, D_MODEL), lambda i: (jnp.maximum(i - 1, 0), 0)),
                  pl.BlockSpec((N_DEV, N_META, 128), lambda i: (0, 0, 0))],
        out_specs=pl.BlockSpec((HEAD_ROWS, D_MODEL), lambda i: (i, 0)),
        compiler_params=_cp("parallel"),
    )(x, meta_g)


def _rope_table(pos_col, freq_row):
    n_rows = pos_col.shape[0]
    tr = 128

    def body(p_ref, f_ref, o_ref):
        ang = p_ref[...] * f_ref[...]
        lane = lax.broadcasted_iota(jnp.int32, ang.shape, 1)
        cosv, sinv = jnp.cos(ang), jnp.sin(ang)
        half = QK_ROPE // 2
        o_ref[:, 0:128] = jnp.where(lane < QK_NOPE, 1.0, jnp.where(lane < QK_DIM, cosv, 0.0))
        o_ref[:, 128:256] = jnp.where((lane >= QK_NOPE) & (lane < QK_NOPE + half), -sinv, 0.0)
        o_ref[:, 256:384] = jnp.where((lane >= QK_NOPE + half) & (lane < QK_DIM), sinv, 0.0)

    return pl.pallas_call(
        body, name="rope_table", out_shape=jax.ShapeDtypeStruct((n_rows, 384), F32), grid=(n_rows // tr,),
        in_specs=[pl.BlockSpec((tr, 1), lambda i: (i, 0)), pl.BlockSpec((1, 128), lambda i: (0, 0))],
        out_specs=pl.BlockSpec((tr, 384), lambda i: (i, 0)),
        compiler_params=_cp("parallel"),
    )(pos_col, freq_row)


def _loss_head(h, target):
    n_rows = h.shape[0]
    n_blk = n_rows // HEAD_ROWS

    def body(h_ref, t_ref, loss_ref, dh_ref):
        i = pl.program_id(0)

        @pl.when(i == 0)
        def _():
            loss_ref[...] = jnp.zeros(loss_ref.shape, F32)
            dh_ref[...] = jnp.zeros(dh_ref.shape, F32)

        @pl.when(i > 0)
        def _():
            err = h_ref[...] - t_ref[...]
            dh_ref[...] = err * (1.0 / D_MODEL)
            loss_ref[...] += 0.5 * jnp.sum(jnp.mean(err * err, axis=-1, keepdims=True))

    return pl.pallas_call(
        body, name="loss_head",
        out_shape=(jax.ShapeDtypeStruct((8, 128), F32), jax.ShapeDtypeStruct((n_rows, D_MODEL), F32)),
        grid=(n_blk,),
        in_specs=[pl.BlockSpec((HEAD_ROWS, D_MODEL), lambda i: (i, 0)),
                  pl.BlockSpec((HEAD_ROWS, D_MODEL), lambda i: (jnp.maximum(i - 1, 0), 0))],
        out_specs=(pl.BlockSpec((8, 128), lambda i: (0, 0)), pl.BlockSpec((HEAD_ROWS, D_MODEL), lambda i: (i, 0))),
        compiler_params=_cp("arbitrary"),
    )(h, target)


PROJ_CHUNKS = (0, 1024, 2048, 3072, 4096, P_IN)


def _norm_proj(h_res, gain, w_in):
    n_rows = h_res.shape[0]
    tm = _row_tile(n_rows)

    def body(x_ref, g_ref, w_ref, hb_ref, p_ref):
        h, _, _ = _rms(x_ref[...], g_ref[...])
        hb = h.astype(BF16)
        hb_ref[...] = hb
        for c0, c1 in zip(PROJ_CHUNKS[:-1], PROJ_CHUNKS[1:]):
            p_ref[:, c0:c1] = jnp.dot(hb, w_ref[:, c0:c1], preferred_element_type=F32)

    return pl.pallas_call(
        body, name="norm_proj",
        out_shape=(jax.ShapeDtypeStruct((n_rows, D_MODEL), BF16), jax.ShapeDtypeStruct((n_rows, P_IN), F32)),
        grid=(n_rows // tm,),
        in_specs=[pl.BlockSpec((tm, D_MODEL), lambda i: (i, 0)), pl.BlockSpec((1, D_MODEL), lambda i: (0, 0)),
                  pl.BlockSpec((D_MODEL, P_IN), lambda i: (0, 0))],
        out_specs=(pl.BlockSpec((tm, D_MODEL), lambda i: (i, 0)), pl.BlockSpec((tm, P_IN), lambda i: (i, 0))),
        compiler_params=_cp("parallel"),
    )(h_res, gain, w_in)


def _pool_math(u, halo, z, wg_ref, scale, row0):
    tm = u.shape[0]
    ext = jnp.concatenate([halo, u], axis=0)
    t1 = (row0 + lax.broadcasted_iota(jnp.int32, (tm, 1), 0) - (PAD_FRONT - 1)).astype(F32)
    mixed, yg = [], []
    for g, w in enumerate(POOL_WINDOWS):
        a = ext[:, g * 128:(g + 1) * 128]
        k = 1
        while k < w:
            a = a + pltpu.roll(a, k, 0)
            k *= 2
        cnt = jnp.clip(t1, 1.0, float(w))
        mg = a[HALO:, :] / cnt - u[:, g * 128:(g + 1) * 128]
        mixed.append(mg)
        yg.append(_mm(mg, wg_ref[g]))
    mixed = jnp.concatenate(mixed, axis=1)
    yg = jnp.concatenate(yg, axis=1)
    ys = yg * scale
    sig = _sigmoid(z)
    return mixed, yg, ys, sig


def _halo_above(tm):
    return lambda i: (jnp.maximum(i * (tm // HALO) - 1, 0), 0)


def _pool_fwd(proj, wg, scale, w_up):
    n_rows = proj.shape[0]
    tm = _row_tile(n_rows)

    def body(uz_ref, halo_ref, wg_ref, sc_ref, wup_ref, y_ref):
        uz = uz_ref[...]
        u, z = uz[:, :POOL_WIDTH], uz[:, POOL_WIDTH:]
        _, _, ys, sig = _pool_math(u, halo_ref[...], z, wg_ref, sc_ref[...], pl.program_id(0) * tm)
        y_ref[...] = _mm(ys * (z * sig), wup_ref[...])

    return pl.pallas_call(
        body, name="pool_fwd", out_shape=jax.ShapeDtypeStruct((n_rows, D_MODEL), F32), grid=(n_rows // tm,),
        in_specs=[pl.BlockSpec((tm, 1024), lambda i: (i, 0)), pl.BlockSpec((HALO, POOL_WIDTH), _halo_above(tm)),
                  pl.BlockSpec((4, 128, 128), lambda i: (0, 0, 0)), pl.BlockSpec((1, POOL_WIDTH), lambda i: (0, 0)),
                  pl.BlockSpec((POOL_WIDTH, D_MODEL), lambda i: (0, 0))],
        out_specs=pl.BlockSpec((tm, D_MODEL), lambda i: (i, 0)),
        compiler_params=_cp("parallel"),
    )(proj, proj, wg, scale, w_up)


def _mla_pre_fwd(proj, tab, gqa, gkva, wq, wk, wv, gqn, gkn):
    n_rows = proj.shape[0]
    tm = _row_tile(n_rows)

    def body(c_ref, kr_ref, tab_ref, gqa_ref, gkva_ref, wq_ref, wk_ref, wv_ref, gqn_ref, gkn_ref, q_ref, k_ref, v_ref):
        c, tab = c_ref[...], tab_ref[...]
        cqn, _, _ = _rms(c[:, :Q_RANK], gqa_ref[...])
        ckvn, _, _ = _rms(c[:, Q_RANK:], gkva_ref[...])
        qp, kp, vp = _mm(cqn, wq_ref[...]), _mm(ckvn, wk_ref[...]), _mm(ckvn, wv_ref[...])
        kr = pltpu.roll(kr_ref[...], QK_NOPE, 1)
        gqn, gkn = _pad_gain(gqn_ref), _pad_gain(gkn_ref)
        lane = lax.broadcasted_iota(jnp.int32, (tm, 128), 1)
        one = jnp.where(lane == V_DIM, 1.0, 0.0)
        for h in range(N_HEADS):
            blk = slice(128 * h, 128 * (h + 1))
            qn, _, _ = _head_rms(qp[:, blk], gqn)
            q_ref[h] = _rope(qn, tab).astype(BF16)
            kn, _, _ = _head_rms(kp[:, blk] + kr, gkn)
            k_ref[h] = _rope(kn, tab).astype(BF16)
            v_ref[h] = (vp[:, blk] + one).astype(BF16)

    full = lambda *s: pl.BlockSpec(s, lambda i: (0,) * len(s))
    head = pl.BlockSpec((N_HEADS, tm, 128), lambda i: (0, i, 0))
    sds = jax.ShapeDtypeStruct((N_HEADS, n_rows, 128), BF16)
    return pl.pallas_call(
        body, name="mla_pre_fwd", out_shape=(sds, sds, sds), grid=(n_rows // tm,),
        in_specs=[pl.BlockSpec((tm, 1024), lambda i: (i, COL_CQKV // 1024)),
                  pl.BlockSpec((tm, 128), lambda i: (i, COL_KR // 128)),
                  pl.BlockSpec((tm, 384), lambda i: (i, 0)),
                  full(1, Q_RANK), full(1, KV_RANK), full(Q_RANK, 1024), full(KV_RANK, 1024), full(KV_RANK, 1024),
                  full(1, QK_DIM), full(1, QK_DIM)],
        out_specs=(head, head, head),
        compiler_params=_cp("parallel"),
    )(proj, proj, tab, gqa, gkva, wq, wk, wv, gqn, gkn)


def _attn_mask(row0, col0, n_r, n_c):
    row = row0 + lax.broadcasted_iota(jnp.int32, (n_r, n_c), 0)
    col = col0 + lax.broadcasted_iota(jnp.int32, (n_r, n_c), 1)
    return (col <= row) & (col >= PAD_FRONT)


ATTN_SCALE = 1.0 / math.sqrt(QK_DIM)
EXP2_SCALE = ATTN_SCALE * math.log2(math.e)
LOG2_E = math.log2(math.e)


def _row_chunks(t, size):
    return [(r, min(size, t - r)) for r in range(0, t, size)]


def _attn_fwd(q, k, v, ex=None):
    n_rows = q.shape[1]
    t = _attn_tile(n_rows)
    nb = n_rows // t

    def body(q_ref, k_ref, v_ref, o_ref, lse_ref, m_sc, acc_sc):
        qi, ki = pl.program_id(1), pl.program_id(2)

        @pl.when(ki == 0)
        def _():
            m_sc[...] = jnp.full(m_sc.shape, -jnp.inf, F32)
            acc_sc[...] = jnp.zeros(acc_sc.shape, F32)

        def update(masked):
            heads = range(hp)
            s = [_mm_nt(q_ref[h], k_ref[h]) for h in heads]
            if masked:
                mask = _attn_mask(qi * t, ki * t, t, t)
                s = [jnp.where(mask, sh, MASK_VALUE) for sh in s]
            p = []
            for h in heads:
                m_prev = m_sc[h]
                m_new = jnp.maximum(m_prev, jnp.max(s[h], axis=1, keepdims=True))
                alpha = jnp.exp2((m_prev - m_new) * EXP2_SCALE)
                p.append(jnp.exp2((s[h] - m_new) * EXP2_SCALE).astype(BF16))
                acc_sc[h] = alpha * acc_sc[h]
                m_sc[h] = m_new
            for h in heads:
                acc_sc[h] += _mm(p[h], v_ref[h])

        edge = (ki == qi) | (ki == 0)

        @pl.when((ki <= qi) & edge)
        def _():
            update(True)

        @pl.when((ki < qi) & (ki > 0))
        def _():
            update(False)

        @pl.when(ki == nb - 1)
        def _():
            acc = acc_sc[...]
            denom = acc[:, :, V_DIM:V_DIM + 1]
            o_ref[...] = acc[:, :, :V_DIM] / denom
            lse_ref[...] = m_sc[...] * ATTN_SCALE + jnp.log(denom)

    hp = HEADS_PER_STEP_FWD
    kv_idx = lambda h, qi, ki: (h, jnp.minimum(ki, qi), 0)
    return _call_carrying(
        ex, body, "attn_fwd",
        out_shape=(jax.ShapeDtypeStruct((N_HEADS, n_rows, V_DIM), F32), jax.ShapeDtypeStruct((N_HEADS, n_rows, 1), F32)),
        grid=(N_HEADS // hp, nb, nb),
        in_specs=[pl.BlockSpec((hp, t, 128), lambda h, qi, ki: (h, qi, 0)), pl.BlockSpec((hp, t, 128), kv_idx),
                  pl.BlockSpec((hp, t, 128), kv_idx)],
        out_specs=(pl.BlockSpec((hp, t, V_DIM), lambda h, qi, ki: (h, qi, 0)), pl.BlockSpec((hp, t, 1), lambda h, qi, ki: (h, qi, 0))),
        scratch_shapes=[pltpu.VMEM((hp, t, 1), F32), pltpu.VMEM((hp, t, 128), F32)],
        semantics=("parallel", "parallel", "arbitrary"), args=(q, k, v))


def _merge_math(o_ref, z, gates, y_pool, y_mla_fn):
    o_cat = jnp.concatenate([o_ref[h] for h in range(N_HEADS)], axis=1)
    sig_z = _sigmoid(z)
    a_mla = o_cat * (z * sig_z)
    y_mla = y_mla_fn(a_mla)
    sgp, sgm = _sigmoid(gates[:, :D_MODEL]), _sigmoid(gates[:, D_MODEL:])
    merged = sgp * y_pool + sgm * y_mla
    return o_cat, sig_z, a_mla, y_mla, sgp, sgm, merged


def _mla_post_fwd(h_res, proj, o, y_pool, w_mla_up, w_out):
    n_rows = h_res.shape[0]
    tm = _row_tile(n_rows)

    def body(h_ref, g_ref, z_ref, o_ref, yp_ref, wup_ref, wout_ref, ymla_ref, hn_ref):
        _, _, _, y_mla, _, _, merged = _merge_math(o_ref, z_ref[...], g_ref[...], yp_ref[...],
                                                   lambda a: _mm(a, wup_ref[...]))
        ymla_ref[...] = y_mla
        hn_ref[...] = h_ref[...] + _mm(merged, wout_ref[...])

    row = lambda w, c: pl.BlockSpec((tm, w), lambda i: (i, c))
    return pl.pallas_call(
        body, name="mla_post_fwd",
        out_shape=(jax.ShapeDtypeStruct((n_rows, D_MODEL), F32), jax.ShapeDtypeStruct((n_rows, D_MODEL), F32)),
        grid=(n_rows // tm,),
        in_specs=[row(D_MODEL, 0), row(2048, COL_GATES // 2048), row(MLA_WIDTH, COL_ZMLA // MLA_WIDTH),
                  pl.BlockSpec((N_HEADS, tm, V_DIM), lambda i: (0, i, 0)), row(D_MODEL, 0),
                  pl.BlockSpec((MLA_WIDTH, D_MODEL), lambda i: (0, 0)), pl.BlockSpec((D_MODEL, D_MODEL), lambda i: (0, 0))],
        out_specs=(row(D_MODEL, 0), row(D_MODEL, 0)),
        compiler_params=_cp("parallel"),
    )(h_res, proj, proj, o, y_pool, w_mla_up, w_out)


def _dsilu(z, sig):
    return sig * (1.0 + z * (1.0 - sig))


def _acc(ref, val, first):
    @pl.when(first)
    def _():
        ref[...] = val

    @pl.when(jnp.logical_not(first))
    def _():
        ref[...] += val


def _mla_post_bwd(dh, proj, o, y_pool, y_mla, w_mla_up, w_out):
    n_rows = dh.shape[0]
    tm = _row_tile(n_rows)

    def body(dh_ref, g_ref, z_ref, o_ref, yp_ref, ym_ref, wup_ref, wout_ref,
             dgz_ref, dyp_ref, do_ref, delta_ref, dwout_ref, dwup_ref):
        first = pl.program_id(0) == 0
        z, y_pool, y_mla = z_ref[...], yp_ref[...], ym_ref[...]
        dhv = dh_ref[...]
        dmerged = _mm_nt(dhv, wout_ref[...])
        o_cat, sig_z, a_mla, _, sgp, sgm, merged = _merge_math(o_ref, z, g_ref[...], y_pool, lambda a: y_mla)
        dy_mla = dmerged * sgm
        da = _mm_nt(dy_mla, wup_ref[...])
        dyp_ref[...] = dmerged * sgp
        dgz_ref[:, 0:D_MODEL] = dmerged * y_pool * (sgp * (1.0 - sgp))
        dgz_ref[:, D_MODEL:2 * D_MODEL] = dmerged * y_mla * (sgm * (1.0 - sgm))
        _acc(dwout_ref, _mm_tn(merged, dhv), first)
        do_cat = da * (z * sig_z)
        dgz_ref[:, 2 * D_MODEL:] = da * o_cat * _dsilu(z, sig_z)
        for h in range(N_HEADS):
            doh = do_cat[:, V_DIM * h:V_DIM * (h + 1)]
            do_ref[h] = doh
            delta_ref[h] = jnp.sum(doh * o_cat[:, V_DIM * h:V_DIM * (h + 1)], axis=1, keepdims=True)
        _acc(dwup_ref, _mm_tn(a_mla, dy_mla), first)

    row = lambda w, c: pl.BlockSpec((tm, w), lambda i: (i, c))
    head = lambda w: pl.BlockSpec((N_HEADS, tm, w), lambda i: (0, i, 0))
    const = lambda r, c: pl.BlockSpec((r, c), lambda i: (0, 0))
    return pl.pallas_call(
        body, name="mla_post_bwd",
        out_shape=(jax.ShapeDtypeStruct((n_rows, 2560), F32), jax.ShapeDtypeStruct((n_rows, D_MODEL), F32),
                   jax.ShapeDtypeStruct((N_HEADS, n_rows, V_DIM), F32), jax.ShapeDtypeStruct((N_HEADS, n_rows, 1), F32),
                   jax.ShapeDtypeStruct((D_MODEL, D_MODEL), F32), jax.ShapeDtypeStruct((MLA_WIDTH, D_MODEL), F32)),
        grid=(n_rows // tm,),
        in_specs=[row(D_MODEL, 0), row(2048, COL_GATES // 2048), row(MLA_WIDTH, COL_ZMLA // MLA_WIDTH), head(V_DIM),
                  row(D_MODEL, 0), row(D_MODEL, 0), const(MLA_WIDTH, D_MODEL), const(D_MODEL, D_MODEL)],
        out_specs=(row(2560, 0), row(D_MODEL, 0), head(V_DIM), head(1), const(D_MODEL, D_MODEL), const(MLA_WIDTH, D_MODEL)),
        compiler_params=_cp("arbitrary"),
    )(dh, proj, proj, o, y_pool, y_mla, w_mla_up, w_out)


def _attn_bwd(q, k, v, do, lse, delta, ex=None):
    n_rows = q.shape[1]
    t = _attn_tile(n_rows)
    nb = n_rows // t

    def body(q_ref, k_ref, v_ref, do_ref, lse_ref, delta_ref, dq_ref, dk_ref, dv_ref, dk_sc, dv_sc):
        ki, qi = pl.program_id(1), pl.program_id(2)

        @pl.when((ki == 0) & (qi == 0))
        def _():
            dq_ref[...] = jnp.zeros(dq_ref.shape, F32)

        @pl.when(qi == 0)
        def _():
            dk_sc[...] = jnp.zeros(dk_sc.shape, F32)
            dv_sc[...] = jnp.zeros(dv_sc.shape, F32)

        def step(masked):
            heads = range(HEADS_PER_STEP)
            s = [_mm_nt(q_ref[h], k_ref[h]) for h in heads]
            dp = [_mm_nt(do_ref[h], v_ref[h, :, 0:V_DIM]) for h in heads]
            if masked:
                mask = _attn_mask(qi * t, ki * t, t, t)
                s = [jnp.where(mask, sh, MASK_VALUE) for sh in s]
            p = [jnp.exp2(s[h] * EXP2_SCALE - lse_ref[h] * LOG2_E) for h in heads]
            ds = [p[h] * (dp[h] - delta_ref[h]) for h in heads]
            rows = pl.ds(pl.multiple_of(qi * t, t), t)
            for h in heads:
                dv_sc[h] += _mm_tn(p[h], do_ref[h])
                dq_ref[h, rows, :] += _mm(ds[h], k_ref[h]) * ATTN_SCALE
                dk_sc[h] += _mm_tn(ds[h], q_ref[h])

        edge = (ki == qi) | (ki == 0)

        @pl.when((qi >= ki) & edge)
        def _():
            step(True)

        @pl.when((qi > ki) & (ki > 0))
        def _():
            step(False)

        @pl.when(qi == nb - 1)
        def _():
            dk_ref[...] = dk_sc[...] * ATTN_SCALE
            dv_ref[...] = dv_sc[...]

    hp = HEADS_PER_STEP
    q_idx = lambda h, ki, qi: (h, jnp.maximum(qi, ki), 0)
    k_idx = lambda h, ki, qi: (h, ki, 0)
    return _call_carrying(
        ex, body, "attn_bwd",
        out_shape=(jax.ShapeDtypeStruct((N_HEADS, n_rows, 128), F32), jax.ShapeDtypeStruct((N_HEADS, n_rows, 128), F32),
                   jax.ShapeDtypeStruct((N_HEADS, n_rows, V_DIM), F32)),
        grid=(N_HEADS // hp, nb, nb),
        in_specs=[pl.BlockSpec((hp, t, 128), q_idx), pl.BlockSpec((hp, t, 128), k_idx), pl.BlockSpec((hp, t, 128), k_idx),
                  pl.BlockSpec((hp, t, V_DIM), q_idx), pl.BlockSpec((hp, t, 1), q_idx), pl.BlockSpec((hp, t, 1), q_idx)],
        out_specs=(pl.BlockSpec((hp, n_rows, 128), lambda h, ki, qi: (h, 0, 0), pipeline_mode=pl.Buffered(1)),
                   pl.BlockSpec((hp, t, 128), k_idx),
                   pl.BlockSpec((hp, t, V_DIM), k_idx)),
        scratch_shapes=[pltpu.VMEM((hp, t, 128), F32), pltpu.VMEM((hp, t, V_DIM), F32)],
        semantics=("parallel", "arbitrary", "arbitrary"), args=(q, k, v, do, lse, delta))


def _mla_pre_bwd(proj, tab, gqa, gkva, wq, wk, wv, gqn, gkn, dq, dk, dv):
    n_rows = proj.shape[0]
    tm = _row_tile(n_rows)

    def body(c_ref, kr_ref, tab_ref, gqa_ref, gkva_ref, wq_ref, wk_ref, wv_ref, gqn_ref, gkn_ref, dq_ref, dk_ref, dv_ref,
             dc_ref, dkr_ref, dwq_ref, dwk_ref, dwv_ref, dgqa_ref, dgkva_ref, dgqn_ref, dgkn_ref):
        first = pl.program_id(0) == 0
        c, tab = c_ref[...], tab_ref[...]
        gqa, gkva = gqa_ref[...], gkva_ref[...]
        cqn, cq_h, cq_inv = _rms(c[:, :Q_RANK], gqa)
        ckvn, ckv_h, ckv_inv = _rms(c[:, Q_RANK:], gkva)
        qp, kp = _mm(cqn, wq_ref[...]), _mm(ckvn, wk_ref[...])
        kr = pltpu.roll(kr_ref[...], QK_NOPE, 1)
        gqn, gkn = _pad_gain(gqn_ref), _pad_gain(gkn_ref)
        dq_parts, dk_parts, dv_parts = [], [], []
        dkr = jnp.zeros((tm, 128), F32)
        dgqn = jnp.zeros((1, 128), F32)
        dgkn = jnp.zeros((1, 128), F32)
        zv = jnp.zeros((tm, 128 - V_DIM), F32)
        for h in range(N_HEADS):
            blk = slice(128 * h, 128 * (h + 1))
            _, xh, inv = _head_rms(qp[:, blk], gqn)
            dx, dg = _head_rms_bwd(_rope_bwd(dq_ref[h], tab), xh, inv, gqn)
            dq_parts.append(dx)
            dgqn = dgqn + dg
            _, xh, inv = _head_rms(kp[:, blk] + kr, gkn)
            dx, dg = _head_rms_bwd(_rope_bwd(dk_ref[h], tab), xh, inv, gkn)
            dk_parts.append(dx)
            dgkn = dgkn + dg
            dkr = dkr + dx
            dv_parts.append(jnp.concatenate([dv_ref[h], zv], axis=1))
        dqp = jnp.concatenate(dq_parts, axis=1)
        dkp = jnp.concatenate(dk_parts, axis=1)
        dvp = jnp.concatenate(dv_parts, axis=1)
        _acc(dwq_ref, _mm_tn(cqn, dqp), first)
        _acc(dwk_ref, _mm_tn(ckvn, dkp), first)
        _acc(dwv_ref, _mm_tn(ckvn, dvp), first)
        dcq, dg1 = _rms_bwd(_mm_nt(dqp, wq_ref[...]), cq_h, cq_inv, gqa)
        dckv, dg2 = _rms_bwd(_mm_nt(dkp, wk_ref[...]) + _mm_nt(dvp, wv_ref[...]), ckv_h, ckv_inv, gkva)
        _acc(dgqa_ref, dg1, first)
        _acc(dgkva_ref, dg2, first)
        _acc(dgqn_ref, dgqn, first)
        _acc(dgkn_ref, dgkn, first)
        dc_ref[...] = jnp.concatenate([dcq, dckv], axis=1)
        lane = lax.broadcasted_iota(jnp.int32, (tm, 128), 1)
        dkr_ref[...] = jnp.where(lane < QK_ROPE, pltpu.roll(dkr, 128 - QK_NOPE, 1), 0.0)

    full = lambda *s: pl.BlockSpec(s, lambda i: (0,) * len(s))
    head = lambda w: pl.BlockSpec((N_HEADS, tm, w), lambda i: (0, i, 0))
    return pl.pallas_call(
        body, name="mla_pre_bwd",
        out_shape=(jax.ShapeDtypeStruct((n_rows, 1024), F32), jax.ShapeDtypeStruct((n_rows, 128), F32),
                   jax.ShapeDtypeStruct((Q_RANK, 1024), F32), jax.ShapeDtypeStruct((KV_RANK, 1024), F32),
                   jax.ShapeDtypeStruct((KV_RANK, 1024), F32),
                   jax.ShapeDtypeStruct((1, Q_RANK), F32), jax.ShapeDtypeStruct((1, KV_RANK), F32),
                   jax.ShapeDtypeStruct((1, 128), F32), jax.ShapeDtypeStruct((1, 128), F32)),
        grid=(n_rows // tm,),
        in_specs=[pl.BlockSpec((tm, 1024), lambda i: (i, COL_CQKV // 1024)),
                  pl.BlockSpec((tm, 128), lambda i: (i, COL_KR // 128)),
                  pl.BlockSpec((tm, 384), lambda i: (i, 0)),
                  full(1, Q_RANK), full(1, KV_RANK), full(Q_RANK, 1024), full(KV_RANK, 1024), full(KV_RANK, 1024),
                  full(1, QK_DIM), full(1, QK_DIM), head(128), head(128), head(V_DIM)],
        out_specs=(pl.BlockSpec((tm, 1024), lambda i: (i, 0)), pl.BlockSpec((tm, 128), lambda i: (i, 0)),
                   full(Q_RANK, 1024), full(KV_RANK, 1024), full(KV_RANK, 1024), full(1, Q_RANK), full(1, KV_RANK),
                   full(1, 128), full(1, 128)),
        compiler_params=_cp("arbitrary"),
    )(proj, proj, tab, gqa, gkva, wq, wk, wv, gqn, gkn, dq, dk, dv)


def _pool_bwd_a(proj, dy_pool, wg, scale, w_up):
    n_rows = proj.shape[0]
    tm = _row_tile(n_rows)

    def body(uz_ref, halo_ref, dy_ref, wg_ref, sc_ref, wup_ref, dmz_ref, dwg_ref, dsc_ref, dwup_ref):
        first = pl.program_id(0) == 0
        uz = uz_ref[...]
        u, z = uz[:, :POOL_WIDTH], uz[:, POOL_WIDTH:]
        scale_v = sc_ref[...]
        mixed, yg, ys, sig = _pool_math(u, halo_ref[...], z, wg_ref, scale_v, pl.program_id(0) * tm)
        sp = z * sig
        dy = dy_ref[...]
        da = _mm_nt(dy, wup_ref[...])
        _acc(dwup_ref, _mm_tn(ys * sp, dy), first)
        dys = da * sp
        _acc(dsc_ref, jnp.sum(dys * yg, axis=0, keepdims=True), first)
        dyg = dys * scale_v
        for g in range(4):
            cols = slice(g * 128, (g + 1) * 128)
            dmz_ref[:, cols] = _mm_nt(dyg[:, cols], wg_ref[g])
            _acc(dwg_ref.at[g], _mm_tn(mixed[:, cols], dyg[:, cols]), first)
        dmz_ref[:, POOL_WIDTH:] = da * ys * _dsilu(z, sig)

    return pl.pallas_call(
        body, name="pool_bwd_a",
        out_shape=(jax.ShapeDtypeStruct((n_rows, 1024), F32), jax.ShapeDtypeStruct((4, 128, 128), F32),
                   jax.ShapeDtypeStruct((1, POOL_WIDTH), F32), jax.ShapeDtypeStruct((POOL_WIDTH, D_MODEL), F32)),
        grid=(n_rows // tm,),
        in_specs=[pl.BlockSpec((tm, 1024), lambda i: (i, 0)), pl.BlockSpec((HALO, POOL_WIDTH), _halo_above(tm)),
                  pl.BlockSpec((tm, D_MODEL), lambda i: (i, 0)),
                  pl.BlockSpec((4, 128, 128), lambda i: (0, 0, 0)), pl.BlockSpec((1, POOL_WIDTH), lambda i: (0, 0)),
                  pl.BlockSpec((POOL_WIDTH, D_MODEL), lambda i: (0, 0))],
        out_specs=(pl.BlockSpec((tm, 1024), lambda i: (i, 0)), pl.BlockSpec((4, 128, 128), lambda i: (0, 0, 0)),
                   pl.BlockSpec((1, POOL_WIDTH), lambda i: (0, 0)), pl.BlockSpec((POOL_WIDTH, D_MODEL), lambda i: (0, 0))),
        compiler_params=_cp("arbitrary"),
    )(proj, proj, dy_pool, wg, scale, w_up)


def _pool_bwd_b(dmz):
    n_rows = dmz.shape[0]
    tm = _row_tile(n_rows)
    n_tiles = n_rows // tm
    n_ext = tm + HALO

    def body(dmz_ref, halo_ref, o_ref):
        i = pl.program_id(0)
        v = dmz_ref[...]
        dm = v[:, :POOL_WIDTH]
        halo = jnp.where(i == n_tiles - 1, 0.0, halo_ref[...])
        ext = jnp.concatenate([dm, halo], axis=0)
        t1 = (i * tm + lax.broadcasted_iota(jnp.int32, (n_ext, 1), 0) - (PAD_FRONT - 1)).astype(F32)
        du = []
        for g, w in enumerate(POOL_WINDOWS):
            cols = slice(g * 128, (g + 1) * 128)
            a = ext[:, cols] / jnp.clip(t1, 1.0, float(w))
            k = 1
            while k < w:
                a = a + pltpu.roll(a, n_ext - k, 0)
                k *= 2
            du.append(a[:tm, :] - dm[:, cols])
        o_ref[...] = jnp.concatenate(du + [v[:, POOL_WIDTH:]], axis=1)

    last_halo = n_rows // HALO - 1
    return pl.pallas_call(
        body, name="pool_bwd_b", out_shape=jax.ShapeDtypeStruct((n_rows, 1024), F32), grid=(n_tiles,),
        in_specs=[pl.BlockSpec((tm, 1024), lambda i: (i, 0)),
                  pl.BlockSpec((HALO, POOL_WIDTH), lambda i: (jnp.minimum((i + 1) * (tm // HALO), last_halo), 0))],
        out_specs=pl.BlockSpec((tm, 1024), lambda i: (i, 0)),
        compiler_params=_cp("parallel"),
    )(dmz, dmz)


def _norm_proj_bwd(d_uz, d_cqkv, d_gz, d_kr, w_in, h_res, gain, dh_out):
    n_rows = h_res.shape[0]
    tm = _row_tile(n_rows)
    pieces = ((COL_UZ, 1024), (COL_CQKV, 1024), (COL_GATES, 2560), (COL_KR, 128))

    def body(a_ref, b_ref, c_ref, d_ref, w_ref, x_ref, g_ref, dho_ref, dhi_ref, dg_ref):
        i = pl.program_id(0)
        dh = None
        for ref, (c0, wd) in zip((a_ref, b_ref, c_ref, d_ref), pieces):
            part = _mm_nt(ref[...], w_ref[:, c0:c0 + wd])
            dh = part if dh is None else dh + part
        g = g_ref[...]
        _, xh, inv = _rms(x_ref[...], g)
        dx, dg = _rms_bwd(dh, xh, inv, g)
        _acc(dg_ref, dg, i == 0)
        row = i * tm + lax.broadcasted_iota(jnp.int32, (tm, 1), 0)
        dhi_ref[...] = jnp.where(row >= PAD_FRONT, dho_ref[...] + dx, 0.0)

    row = lambda w: pl.BlockSpec((tm, w), lambda i: (i, 0))
    return pl.pallas_call(
        body, name="norm_proj_bwd",
        out_shape=(jax.ShapeDtypeStruct((n_rows, D_MODEL), F32), jax.ShapeDtypeStruct((1, D_MODEL), F32)),
        grid=(n_rows // tm,),
        in_specs=[row(1024), row(1024), row(2560), row(128), pl.BlockSpec((D_MODEL, P_IN), lambda i: (0, 0)),
                  row(D_MODEL), pl.BlockSpec((1, D_MODEL), lambda i: (0, 0)), row(D_MODEL)],
        out_specs=(row(D_MODEL), pl.BlockSpec((1, D_MODEL), lambda i: (0, 0))),
        compiler_params=_cp("arbitrary"),
    )(d_uz, d_cqkv, d_gz, d_kr, w_in, h_res, gain, dh_out)


def _weight_grad(name, a, b):
    k_rows, m = a.shape
    n = b.shape[1]
    tk = next(t for t in (1664, 640, 128) if k_rows % t == 0)
    tn = next(t for t in (1280, 1024, 512, 128) if n % t == 0)

    def body(a_ref, b_ref, o_ref):
        _acc(o_ref, _mm_tn(a_ref[...], b_ref[...]), pl.program_id(1) == 0)

    return pl.pallas_call(
        body, name=name, out_shape=jax.ShapeDtypeStruct((m, n), F32), grid=(n // tn, k_rows // tk),
        in_specs=[pl.BlockSpec((tk, m), lambda j, k: (k, 0)), pl.BlockSpec((tk, tn), lambda j, k: (k, j))],
        out_specs=pl.BlockSpec((m, tn), lambda j, k: (0, j)),
        compiler_params=_cp("parallel", "arbitrary"),
    )(a, b)


def _adamw(name, w, m, v, gbufs):
    r, c = w.shape
    depth = len(gbufs)
    r_l = r // depth
    tr = r_l
    if r_l * c * 4 > (1 << 20):
        tr = 256 if r_l % 256 == 0 else 128
    assert r_l * depth == r and r_l % tr == 0, (name, r, depth, tr)
    n_t = r_l // tr

    def body(w_ref, m_ref, v_ref, *refs):
        g_refs, (go_ref, d_ref, mo_ref, vo_ref) = refs[:depth], refs[depth:]

        def update(g_ref):
            g = g_ref[0].astype(F32)
            for s in range(1, N_DEV):
                g = g + g_ref[s].astype(F32)
            go_ref[...] = g
            m_new = ADAM_B1 * m_ref[...] + (1.0 - ADAM_B1) * g
            v_new = ADAM_B2 * v_ref[...] + (1.0 - ADAM_B2) * (g * g)
            mo_ref[...] = m_new
            vo_ref[...] = v_new
            m_hat = m_new / (1.0 - ADAM_B1 ** ADAM_STEP)
            v_hat = v_new / (1.0 - ADAM_B2 ** ADAM_STEP)
            d_ref[...] = -ADAM_LR * (m_hat / (jnp.sqrt(v_hat) + ADAM_EPS) + ADAM_WD * w_ref[...])

        for j in range(depth):
            pl.when(pl.program_id(0) == j)(lambda j=j: update(g_refs[j]))

    def g_spec(j):
        return pl.BlockSpec((N_DEV, tr, c), lambda l, i: (0, jnp.where(l == j, i, jnp.where(l < j, 0, n_t - 1)), 0))

    spec = pl.BlockSpec((tr, c), lambda l, i: (l * n_t + i, 0))
    sds = jax.ShapeDtypeStruct((r, c), F32)
    return pl.pallas_call(
        body, name=name, out_shape=(sds, sds, sds, sds), grid=(depth, n_t),
        in_specs=[spec, spec, spec] + [g_spec(j) for j in range(depth)],
        out_specs=(spec, spec, spec, spec),
        compiler_params=_cp("arbitrary", "arbitrary"),
    )(w, m, v, *gbufs)


SMALL = ("norm_gain", "pool_w_group", "pool_scale", "q_a_norm_gain", "kv_a_norm_gain", "q_norm_gain", "k_norm_gain")


def _pack_small(parts):
    rows = []
    for p in parts:
        flat = p.reshape(-1)
        pad = (-flat.shape[0]) % 1024
        if pad:
            flat = jnp.concatenate([flat, jnp.zeros((pad,), F32)])
        rows.append(flat.reshape(-1, 128))
    n_rows = sum(r.shape[0] for r in rows)
    tail = (-n_rows) % 256
    if tail:
        rows.append(jnp.zeros((tail, 128), F32))
    return jnp.concatenate(rows, axis=0)


def _unpack_small(packed, shapes):
    out, r0 = [], 0
    for shp in shapes:
        n = math.prod(shp)
        n_rows = (n + 1023) // 1024 * 8
        out.append(packed[r0:r0 + n_rows].reshape(-1)[:n].reshape(shp))
        r0 += n_rows
    return out


def kernel(x, positions, meta_tokens, norm_gain, w_in, pool_w_group, pool_scale, pool_w_up, q_a_norm_gain, kv_a_norm_gain, w_q_b, w_kv_b, q_norm_gain, k_norm_gain, mla_w_up, w_out, loss_target, m_meta_tokens, m_norm_gain, m_w_in, m_pool_w_group, m_pool_scale, m_pool_w_up, m_q_a_norm_gain, m_kv_a_norm_gain, m_w_q_b, m_w_kv_b, m_q_norm_gain, m_k_norm_gain, m_mla_w_up, m_w_out, v_meta_tokens, v_norm_gain, v_w_in, v_pool_w_group, v_pool_scale, v_pool_w_up, v_q_a_norm_gain, v_kv_a_norm_gain, v_w_q_b, v_w_kv_b, v_q_norm_gain, v_k_norm_gain, v_mla_w_up, v_w_out):
    x2, target = x[0], loss_target[0]

    big = dict(w_in=w_in, pool_w_up=pool_w_up, w_q_b=w_q_b, w_kv_b=w_kv_b, mla_w_up=mla_w_up, w_out=w_out)
    names_big = list(big)
    shards = {n: _cast_bf16("cast_" + n, w) for n, w in big.items()}

    def layer_shards(l):
        return [shards[n][l] for n in names_big]

    def repack(gathered):
        g_in, g_pup, g_qb, g_kvb, g_mup, g_out = gathered
        return dict(w_in=_repack_w_in(g_in),
                    pool_up=_repack_cols("repack_pool_up", g_pup[:, None], BF16)[0],
                    q_heads=_repack_q_heads(g_qb),
                    kv_heads=_repack_kv_heads(g_kvb),
                    mla_up=_repack_cols("repack_mla_up", g_mup[:, None], BF16)[0],
                    out=_repack_rows("repack_out", g_out[:, None])[0])

    first = _exchange("gather_first", [meta_tokens] + layer_shards(0), [])
    weights = [repack(first[1:])]

    h = _embed(x2, first[0])
    pos = jnp.concatenate([jnp.zeros((PAD_FRONT,), jnp.int32), jnp.arange(N_META, dtype=jnp.int32),
                           positions[0] + N_META]).astype(F32).reshape(-1, 1)
    half = QK_ROPE // 2
    inv_freq = ROPE_THETA ** (-jnp.arange(half, dtype=F32) / half)
    tab = _rope_table(pos, jnp.tile(inv_freq, 128 // half).reshape(1, 128))

    row = lambda a, l: a[l].reshape(1, -1)
    saved = []
    for l in range(DEPTH):
        w = weights[l]
        hb, proj = _norm_proj(h, row(norm_gain, l), w["w_in"])
        y_pool = _pool_fwd(proj, pool_w_group[l], row(pool_scale, l), w["pool_up"])
        q, k, v = _mla_pre_fwd(proj, tab, row(q_a_norm_gain, l), row(kv_a_norm_gain, l), w["q_heads"], *w["kv_heads"],
                               row(q_norm_gain, l), row(k_norm_gain, l))
        ex = _Exchange(layer_shards(l + 1), []) if l + 1 < DEPTH else None
        res = _attn_fwd(q, k, v, ex)
        o, lse = res[0], res[1]
        if ex is not None:
            weights.append(repack(res[2:]))
        y_mla, h_next = _mla_post_fwd(h, proj, o, y_pool, w["mla_up"], w["out"])
        saved.append((h, hb, proj, y_pool, q, k, v, o, lse, y_mla))
        h = h_next

    loss_part, dh = _loss_head(h, target)
    loss = lax.psum(loss_part[0, 0], MESH_AXES)

    grads = {n: [None] * DEPTH for n in SMALL}
    pending = None
    received = [None] * DEPTH
    for l in reversed(range(DEPTH)):
        w = weights[l]
        h_l, hb, proj, y_pool, q, k, v, o, lse, y_mla = saved[l]
        d_gz, dy_pool, do, delta, g_out, g_mla_up = _mla_post_bwd(dh, proj, o, y_pool, y_mla, w["mla_up"], w["out"])
        ex = _Exchange([], pending) if pending is not None else None
        res = _attn_bwd(q, k, v, do, lse, delta, ex)
        dq, dk, dv = res[0], res[1], res[2]
        if ex is not None:
            received[l + 1] = res[3:]
        (d_cqkv, d_kr, g_q, g_k, g_v, grads["q_a_norm_gain"][l], grads["kv_a_norm_gain"][l], g_qn, g_kn) = _mla_pre_bwd(
            proj, tab, row(q_a_norm_gain, l), row(kv_a_norm_gain, l), w["q_heads"], *w["kv_heads"], row(q_norm_gain, l),
            row(k_norm_gain, l), dq, dk, dv)
        grads["q_norm_gain"][l], grads["k_norm_gain"][l] = g_qn[:, :QK_DIM], g_kn[:, :QK_DIM]
        dmz, grads["pool_w_group"][l], grads["pool_scale"][l], g_pool_up = _pool_bwd_a(
            proj, dy_pool, pool_w_group[l], row(pool_scale, l), w["pool_up"])
        d_uz = _pool_bwd_b(dmz)
        g_in = [_weight_grad("dw_in", hb, d) for d in (d_uz, d_cqkv, d_gz, d_kr)]
        dh, grads["norm_gain"][l] = _norm_proj_bwd(d_uz, d_cqkv, d_gz, d_kr, w["w_in"], h_l, row(norm_gain, l), dh)
        pending = [_unpack_w_in(g_in),
                   _unpack_cols("unpack_pool_up", g_pool_up[None], 128, BF16)[:, 0],
                   _unpack_q_heads(g_q),
                   _unpack_kv_heads(g_k, g_v),
                   _unpack_cols("unpack_mla_up", g_mla_up[None], 128, BF16)[:, 0],
                   _unpack_rows("unpack_out", g_out[None], D_MODEL // N_DEV)[:, 0]]

    grad_x = dh[HEAD_ROWS:][None]
    d_meta = dh[PAD_FRONT:HEAD_ROWS]

    small_shapes = [a.shape for a in (norm_gain, pool_w_group, pool_scale, q_a_norm_gain, kv_a_norm_gain, q_norm_gain, k_norm_gain)]
    small_grad = _pack_small([jnp.stack(grads[n], axis=0).reshape(s) for n, s in zip(SMALL, small_shapes)])
    meta_parts = _unpack_cols("unpack_meta", d_meta[None], 128, F32)[:, 0]
    last = _exchange("exchange_last", [small_grad], [meta_parts] + pending)
    small_buf = last[0]
    received[0] = last[2:]
    sharded = ["meta_tokens"] + names_big
    shard_bufs = {"meta_tokens": [last[1]]}
    for j, n in enumerate(names_big):
        shard_bufs[n] = [received[l][j] for l in range(DEPTH)]

    given = dict(meta_tokens=(meta_tokens, m_meta_tokens, v_meta_tokens), w_in=(w_in, m_w_in, v_w_in),
                 pool_w_up=(pool_w_up, m_pool_w_up, v_pool_w_up), w_q_b=(w_q_b, m_w_q_b, v_w_q_b),
                 w_kv_b=(w_kv_b, m_w_kv_b, v_w_kv_b), mla_w_up=(mla_w_up, m_mla_w_up, v_mla_w_up),
                 w_out=(w_out, m_w_out, v_w_out))
    result = {}
    for n in sharded:
        w, m, v = given[n]
        cols = w.shape[-1]
        res = _adamw("adamw_" + n, w.reshape(-1, cols), m.reshape(-1, cols), v.reshape(-1, cols), shard_bufs[n])
        result[n] = [r.reshape(w.shape) for r in res]
    small_w = (norm_gain, pool_w_group, pool_scale, q_a_norm_gain, kv_a_norm_gain, q_norm_gain, k_norm_gain)
    small_m = (m_norm_gain, m_pool_w_group, m_pool_scale, m_q_a_norm_gain, m_kv_a_norm_gain, m_q_norm_gain, m_k_norm_gain)
    small_v = (v_norm_gain, v_pool_w_group, v_pool_scale, v_q_a_norm_gain, v_kv_a_norm_gain, v_q_norm_gain, v_k_norm_gain)
    res = _adamw("adamw_small", _pack_small(small_w), _pack_small(small_m), _pack_small(small_v), [small_buf])
    small_res = [_unpack_small(r, small_shapes) for r in res]
    for j, n in enumerate(SMALL):
        result[n] = [small_res[kind][j] for kind in range(4)]

    order = ("meta_tokens", "norm_gain", "w_in", "pool_w_group", "pool_scale", "pool_w_up", "q_a_norm_gain",
             "kv_a_norm_gain", "w_q_b", "w_kv_b", "q_norm_gain", "k_norm_gain", "mla_w_up", "w_out")
    outs = [loss, grad_x]
    for kind in range(4):
        outs += [result[n][kind] for n in order]
    return tuple(outs)
```

```python
import math

import jax
import jax.numpy as jnp
from jax import lax
from jax.experimental import pallas as pl
from jax.experimental.pallas import tpu as pltpu

F32 = jnp.float32
BF16 = jnp.bfloat16

D_MODEL = 1024
DEPTH = 4
N_META = 16
PAD_FRONT = 112
HEAD_ROWS = PAD_FRONT + N_META
POOL_WIDTH = 512
POOL_WINDOWS = (2, 4, 8, 16)
POOL_GROUP_DIM = 128
HALO = 16
N_HEADS = 8
HEADS_PER_STEP = 4
HEADS_PER_STEP_FWD = 8
QK_NOPE = 64
QK_ROPE = 32
QK_DIM = 96
V_DIM = 64
MLA_WIDTH = 512
Q_RANK = 768
KV_RANK = 256
ROPE_THETA = 10000.0
NORM_EPS = 1e-6
MASK_VALUE = -1e30
D_IN = 4640
N_DEV = 8
IN_SHARD = D_IN // N_DEV

P_IN = 4736
IN_SEGMENTS = ((0, 2048, 0), (2048, 2080, 4608), (2080, 2592, 4096), (2592, 3616, 2048), (3616, 4640, 3072))
COL_UZ, COL_CQKV, COL_GATES, COL_ZMLA, COL_KR = 0, 1024, 2048, 4096, 4608

ADAM_LR = 0.001
ADAM_B1 = 0.9
ADAM_B2 = 0.999
ADAM_EPS = 1e-08
ADAM_WD = 0.01
ADAM_STEP = 10

VMEM_LIMIT = 56 * 1024 * 1024
MESH_AXES = ("x", "y", "c")


def _cp(*sem):
    return pltpu.CompilerParams(dimension_semantics=sem, vmem_limit_bytes=VMEM_LIMIT)


def _row_tile(n_rows):
    return 320 if n_rows % 320 == 0 else 128


def _attn_tile(n_rows):
    return 640 if (n_rows % 640 == 0 and n_rows > 640) else 128


def _mm(a, b):
    return jnp.dot(a.astype(BF16), b.astype(BF16), preferred_element_type=F32)


def _mm_nt(a, b):
    return lax.dot_general(a.astype(BF16), b.astype(BF16), (((1,), (1,)), ((), ())), preferred_element_type=F32)


def _mm_tn(a, b):
    return lax.dot_general(a.astype(BF16), b.astype(BF16), (((0,), (0,)), ((), ())), preferred_element_type=F32)


def _sigmoid(z):
    return 1.0 / (1.0 + jnp.exp(-z))


def _rms(x, g):
    inv = lax.rsqrt(jnp.mean(x * x, axis=-1, keepdims=True) + NORM_EPS)
    xh = x * inv
    return xh * g, xh, inv


def _rms_bwd(dy, xh, inv, g):
    dg = jnp.sum(dy * xh, axis=0, keepdims=True)
    dxh = dy * g
    dx = inv * (dxh - xh * jnp.mean(dxh * xh, axis=-1, keepdims=True))
    return dx, dg


def _rope(x, tab):
    return x * tab[:, 0:128] + pltpu.roll(x, 112, 1) * tab[:, 128:256] + pltpu.roll(x, 16, 1) * tab[:, 256:384]


def _rope_bwd(dy, tab):
    return dy * tab[:, 0:128] + pltpu.roll(dy * tab[:, 128:256], 16, 1) + pltpu.roll(dy * tab[:, 256:384], 112, 1)


def _head_rms(x, g):
    inv = lax.rsqrt(jnp.sum(x * x, axis=-1, keepdims=True) * (1.0 / QK_DIM) + NORM_EPS)
    xh = x * inv
    return xh * g, xh, inv


def _head_rms_bwd(dy, xh, inv, g):
    dg = jnp.sum(dy * xh, axis=0, keepdims=True)
    dxh = dy * g
    dx = inv * (dxh - xh * (jnp.sum(dxh * xh, axis=-1, keepdims=True) * (1.0 / QK_DIM)))
    return dx, dg


def _pad_gain(g_ref):
    return jnp.concatenate([g_ref[...], jnp.zeros((1, 128 - QK_DIM), F32)], axis=1)


class _Exchange:
    def __init__(self, gather_list, scatter_list):
        self.arrays = list(gather_list) + list(scatter_list)
        self.n_g, self.n = len(gather_list), len(self.arrays)
        self.out_shape = [jax.ShapeDtypeStruct((N_DEV,) + a.shape, a.dtype) for a in gather_list]
        self.out_shape += [jax.ShapeDtypeStruct(a.shape, a.dtype) for a in scatter_list]
        self.specs = [pl.BlockSpec(memory_space=pl.ANY)] * self.n
        self.sems = [pltpu.SemaphoreType.DMA((7 * self.n,)), pltpu.SemaphoreType.DMA((7 * self.n,)),
                     pltpu.SemaphoreType.DMA((self.n,))]

    def copies(self, ins, outs, sems):
        send_sems, recv_sems, local_sems = sems
        x, y, c = lax.axis_index("x"), lax.axis_index("y"), lax.axis_index("c")
        me = 4 * x + 2 * y + c

        def src(a, slot):
            return ins[a] if a < self.n_g else ins[a].at[slot]

        out = [pltpu.make_async_copy(src(a, me), outs[a].at[me], local_sems.at[a]) for a in range(self.n)]
        for k in range(1, N_DEV):
            px = 1 - x if k & 4 else x
            py = 1 - y if k & 2 else y
            pc = 1 - c if k & 1 else c
            peer = 4 * px + 2 * py + pc
            for a in range(self.n):
                out.append(pltpu.make_async_remote_copy(
                    src_ref=src(a, peer), dst_ref=outs[a].at[me],
                    send_sem=send_sems.at[a * 7 + k - 1], recv_sem=recv_sems.at[a * 7 + k - 1],
                    device_id=(px, py, pc), device_id_type=pl.DeviceIdType.MESH))
        return out


def _exchange(name, gather_list, scatter_list):
    ex = _Exchange(gather_list, scatter_list)
    n = ex.n

    def body(*refs):
        copies = ex.copies(refs[:n], refs[n:2 * n], refs[2 * n:])
        for cp in copies:
            cp.start()
        for cp in copies:
            cp.wait()

    return pl.pallas_call(
        body, name=name, out_shape=ex.out_shape, in_specs=ex.specs, out_specs=ex.specs, scratch_shapes=ex.sems,
        compiler_params=pltpu.CompilerParams(has_side_effects=True),
    )(*ex.arrays)


def _call_carrying(ex, body, name, out_shape, grid, in_specs, out_specs, scratch_shapes, semantics, args):
    if ex is None:
        return pl.pallas_call(body, name=name, out_shape=out_shape, grid=grid, in_specs=in_specs, out_specs=out_specs,
                              scratch_shapes=scratch_shapes, compiler_params=_cp(*semantics))(*args)
    n, n_in, n_out, n_scr = ex.n, len(in_specs), len(out_shape), len(scratch_shapes)

    def carrying(*refs):
        ins, refs = refs[:n_in], refs[n_in:]
        c_in, refs = refs[:n], refs[n:]
        outs, refs = refs[:n_out], refs[n_out:]
        c_out, refs = refs[:n], refs[n:]
        scratch, sems = refs[:n_scr], refs[n_scr:]
        ids = [pl.program_id(a) for a in range(len(grid))]
        first, last = ids[0] == 0, ids[0] == grid[0] - 1
        for a in range(1, len(grid)):
            first, last = first & (ids[a] == 0), last & (ids[a] == grid[a] - 1)
        copies = ex.copies(c_in, c_out, sems)

        @pl.when(first)
        def _():
            for cp in copies:
                cp.start()

        body(*ins, *outs, *scratch)

        @pl.when(last)
        def _():
            for cp in copies:
                cp.wait()

    return pl.pallas_call(
        carrying, name=name + "_x", out_shape=list(out_shape) + ex.out_shape, grid=grid,
        in_specs=list(in_specs) + ex.specs, out_specs=list(out_specs) + ex.specs,
        scratch_shapes=list(scratch_shapes) + ex.sems, compiler_params=_cp(*["arbitrary"] * len(grid)),
    )(*args, *ex.arrays)


def _in_pieces():
    runs = []
    for lo, hi, dst in sorted(IN_SEGMENTS, key=lambda s: s[2]):
        col = lo
        while col < hi:
            d = col // IN_SHARD
            end = min(hi, (d + 1) * IN_SHARD)
            runs.append((d, col - d * IN_SHARD, end - d * IN_SHARD))
            col = end
    return runs


def _cast_bf16(name, w):
    shape = w.shape
    w2 = w.reshape(-1, shape[-1])
    r, c = w2.shape
    tr = 512 if r % 512 == 0 else r

    def body(x_ref, o_ref):
        o_ref[...] = x_ref[...].astype(BF16)

    spec = pl.BlockSpec((tr, c), lambda i: (i, 0))
    out = pl.pallas_call(body, name=name, out_shape=jax.ShapeDtypeStruct((r, c), BF16), grid=(r // tr,),
                         in_specs=[spec], out_specs=spec, compiler_params=_cp("parallel"))(w2)
    return out.reshape(shape)


def _repack_w_in(gathered):
    tr = 256
    runs = _in_pieces()

    def body(x_ref, o_ref):
        parts = [x_ref[d, :, a:b].astype(F32) for d, a, b in runs]
        parts.append(jnp.zeros((tr, P_IN - D_IN), F32))
        o_ref[...] = jnp.concatenate(parts, axis=1).astype(BF16)

    return pl.pallas_call(
        body, name="repack_w_in", out_shape=jax.ShapeDtypeStruct((D_MODEL, P_IN), BF16), grid=(D_MODEL // tr,),
        in_specs=[pl.BlockSpec((N_DEV, tr, IN_SHARD), lambda i: (0, i, 0))],
        out_specs=pl.BlockSpec((tr, P_IN), lambda i: (i, 0)),
        compiler_params=_cp("parallel"),
    )(gathered)


def _unpack_w_in(parts):
    tr = 256
    runs = _in_pieces()
    widths = [p.shape[-1] for p in parts]

    def body(a_ref, b_ref, c_ref, d_ref, o_ref):
        full = jnp.concatenate([a_ref[...], b_ref[...], c_ref[...], d_ref[...]], axis=1)
        col = 0
        per_dev = [[] for _ in range(N_DEV)]
        for d, a, b in runs:
            per_dev[d].append((a, full[:, col:col + (b - a)]))
            col += b - a
        for d in range(N_DEV):
            pieces = [v for _, v in sorted(per_dev[d], key=lambda av: av[0])]
            o_ref[d] = (jnp.concatenate(pieces, axis=1) if len(pieces) > 1 else pieces[0]).astype(BF16)

    return pl.pallas_call(
        body, name="unpack_w_in", out_shape=jax.ShapeDtypeStruct((N_DEV, D_MODEL, IN_SHARD), BF16), grid=(D_MODEL // tr,),
        in_specs=[pl.BlockSpec((tr, w), lambda i: (i, 0)) for w in widths],
        out_specs=pl.BlockSpec((N_DEV, tr, IN_SHARD), lambda i: (0, i, 0)),
        compiler_params=_cp("parallel"),
    )(*parts)


def _repack_cols(name, gathered, dtype):
    _, depth, k, n = gathered.shape

    def body(x_ref, o_ref):
        o_ref[0] = jnp.concatenate([x_ref[d, 0] for d in range(N_DEV)], axis=1).astype(dtype)

    return pl.pallas_call(
        body, name=name, out_shape=jax.ShapeDtypeStruct((depth, k, N_DEV * n), dtype), grid=(depth,),
        in_specs=[pl.BlockSpec((N_DEV, 1, k, n), lambda l: (0, l, 0, 0))],
        out_specs=pl.BlockSpec((1, k, N_DEV * n), lambda l: (l, 0, 0)),
        compiler_params=_cp("parallel"),
    )(gathered)


def _unpack_cols(name, full, n, dtype):
    depth, k, _ = full.shape

    def body(x_ref, o_ref):
        for d in range(N_DEV):
            o_ref[d, 0] = x_ref[0, :, d * n:(d + 1) * n].astype(dtype)

    return pl.pallas_call(
        body, name=name, out_shape=jax.ShapeDtypeStruct((N_DEV, depth, k, n), dtype), grid=(depth,),
        in_specs=[pl.BlockSpec((1, k, N_DEV * n), lambda l: (l, 0, 0))],
        out_specs=pl.BlockSpec((N_DEV, 1, k, n), lambda l: (0, l, 0, 0)),
        compiler_params=_cp("parallel"),
    )(full)


def _repack_rows(name, gathered):
    _, depth, r, n = gathered.shape

    def body(x_ref, o_ref):
        for d in range(N_DEV):
            o_ref[0, d * r:(d + 1) * r, :] = x_ref[d, 0].astype(BF16)

    return pl.pallas_call(
        body, name=name, out_shape=jax.ShapeDtypeStruct((depth, N_DEV * r, n), BF16), grid=(depth,),
        in_specs=[pl.BlockSpec((N_DEV, 1, r, n), lambda l: (0, l, 0, 0))],
        out_specs=pl.BlockSpec((1, N_DEV * r, n), lambda l: (l, 0, 0)),
        compiler_params=_cp("parallel"),
    )(gathered)


def _unpack_rows(name, full, r):
    depth, _, n = full.shape

    def body(x_ref, o_ref):
        for d in range(N_DEV):
            o_ref[d, 0] = x_ref[0, d * r:(d + 1) * r, :].astype(BF16)

    return pl.pallas_call(
        body, name=name, out_shape=jax.ShapeDtypeStruct((N_DEV, depth, r, n), BF16), grid=(depth,),
        in_specs=[pl.BlockSpec((1, N_DEV * r, n), lambda l: (l, 0, 0))],
        out_specs=pl.BlockSpec((N_DEV, 1, r, n), lambda l: (0, l, 0, 0)),
        compiler_params=_cp("parallel"),
    )(full)


def _repack_q_heads(gathered):
    r = Q_RANK // N_DEV

    def body(x_ref, o_ref):
        z = jnp.zeros((r, 128 - QK_DIM), F32)
        for d in range(N_DEV):
            x = x_ref[d].astype(F32)
            parts = []
            for h in range(N_HEADS):
                parts += [x[:, QK_DIM * h:QK_DIM * (h + 1)], z]
            o_ref[d * r:(d + 1) * r, :] = jnp.concatenate(parts, axis=1).astype(BF16)

    return pl.pallas_call(body, name="repack_q_heads", out_shape=jax.ShapeDtypeStruct((Q_RANK, 1024), BF16),
                          compiler_params=_cp())(gathered)


def _unpack_q_heads(dwq):
    r = Q_RANK // N_DEV

    def body(x_ref, o_ref):
        for d in range(N_DEV):
            x = x_ref[d * r:(d + 1) * r, :]
            o_ref[d] = jnp.concatenate([x[:, 128 * h:128 * h + QK_DIM] for h in range(N_HEADS)], axis=1).astype(BF16)

    return pl.pallas_call(body, name="unpack_q_heads", out_shape=jax.ShapeDtypeStruct((N_DEV, r, Q_RANK), BF16),
                          compiler_params=_cp())(dwq)


def _repack_kv_heads(gathered):
    def body(x_ref, wk_ref, wv_ref):
        z = jnp.zeros((KV_RANK, 128 - QK_NOPE), F32)
        for h in range(N_HEADS):
            x = x_ref[h].astype(F32)
            wk_ref[:, 128 * h:128 * (h + 1)] = jnp.concatenate([x[:, :QK_NOPE], z], axis=1).astype(BF16)
            wv_ref[:, 128 * h:128 * (h + 1)] = jnp.concatenate([x[:, QK_NOPE:], z], axis=1).astype(BF16)

    sds = jax.ShapeDtypeStruct((KV_RANK, 1024), BF16)
    return pl.pallas_call(body, name="repack_kv_heads", out_shape=(sds, sds), compiler_params=_cp())(gathered)


def _unpack_kv_heads(dwk, dwv):
    def body(k_ref, v_ref, o_ref):
        for h in range(N_HEADS):
            o_ref[h] = jnp.concatenate([k_ref[:, 128 * h:128 * h + QK_NOPE], v_ref[:, 128 * h:128 * h + V_DIM]],
                                       axis=1).astype(BF16)

    return pl.pallas_call(body, name="unpack_kv_heads", out_shape=jax.ShapeDtypeStruct((N_DEV, KV_RANK, 128), BF16),
                          compiler_params=_cp())(dwk, dwv)


def _embed(x, meta_g):
    seq = x.shape[0]
    n_blk = seq // HEAD_ROWS + 1

    def body(x_ref, m_ref, o_ref):
        i = pl.program_id(0)

        @pl.when(i == 0)
        def _():
            meta = jnp.concatenate([m_ref[d] for d in range(N_DEV)], axis=1)
            o_ref[...] = jnp.concatenate([jnp.zeros((PAD_FRONT, D_MODEL), F32), meta], axis=0)

        @pl.when(i > 0)
        def _():
            o_ref[...] = x_ref[...]

    return pl.pallas_call(
        body, name="embed", out_shape=jax.ShapeDtypeStruct((seq + HEAD_ROWS, D_MODEL), F32), grid=(n_blk,),
        in_specs=[pl.BlockSpec((HEAD_ROWS, D_MODEL), lambda i: (jnp.maximum(i - 1, 0), 0)),
                  pl.BlockSpec((N_DEV, N_META, 128), lambda i: (0, 0, 0))],
        out_specs=pl.BlockSpec((HEAD_ROWS, D_MODEL), lambda i: (i, 0)),
        compiler_params=_cp("parallel"),
    )(x, meta_g)


def _rope_table(pos_col, freq_row):
    n_rows = pos_col.shape[0]
    tr = 128

    def body(p_ref, f_ref, o_ref):
        ang = p_ref[...] * f_ref[...]
        lane = lax.broadcasted_iota(jnp.int32, ang.shape, 1)
        cosv, sinv = jnp.cos(ang), jnp.sin(ang)
        half = QK_ROPE // 2
        o_ref[:, 0:128] = jnp.where(lane < QK_NOPE, 1.0, jnp.where(lane < QK_DIM, cosv, 0.0))
        o_ref[:, 128:256] = jnp.where((lane >= QK_NOPE) & (lane < QK_NOPE + half), -sinv, 0.0)
        o_ref[:, 256:384] = jnp.where((lane >= QK_NOPE + half) & (lane < QK_DIM), sinv, 0.0)

    return pl.pallas_call(
        body, name="rope_table", out_shape=jax.ShapeDtypeStruct((n_rows, 384), F32), grid=(n_rows // tr,),
        in_specs=[pl.BlockSpec((tr, 1), lambda i: (i, 0)), pl.BlockSpec((1, 128), lambda i: (0, 0))],
        out_specs=pl.BlockSpec((tr, 384), lambda i: (i, 0)),
        compiler_params=_cp("parallel"),
    )(pos_col, freq_row)


def _loss_head(h, target):
    n_rows = h.shape[0]
    n_blk = n_rows // HEAD_ROWS

    def body(h_ref, t_ref, loss_ref, dh_ref):
        i = pl.program_id(0)

        @pl.when(i == 0)
        def _():
            loss_ref[...] = jnp.zeros(loss_ref.shape, F32)
            dh_ref[...] = jnp.zeros(dh_ref.shape, F32)

        @pl.when(i > 0)
        def _():
            err = h_ref[...] - t_ref[...]
            dh_ref[...] = err * (1.0 / D_MODEL)
            loss_ref[...] += 0.5 * jnp.sum(jnp.mean(err * err, axis=-1, keepdims=True))

    return pl.pallas_call(
        body, name="loss_head",
        out_shape=(jax.ShapeDtypeStruct((8, 128), F32), jax.ShapeDtypeStruct((n_rows, D_MODEL), F32)),
        grid=(n_blk,),
        in_specs=[pl.BlockSpec((HEAD_ROWS, D_MODEL), lambda i: (i, 0)),
                  pl.BlockSpec((HEAD_ROWS, D_MODEL), lambda i: (jnp.maximum(i - 1, 0), 0))],
        out_specs=(pl.BlockSpec((8, 128), lambda i: (0, 0)), pl.BlockSpec((HEAD_ROWS, D_MODEL), lambda i: (i, 0))),
        compiler_params=_cp("arbitrary"),
    )(h, target)


PROJ_CHUNKS = (0, 1024, 2048, 3072, 4096, P_IN)


def _norm_proj(h_res, gain, w_in):
    n_rows = h_res.shape[0]
    tm = _row_tile(n_rows)

    def body(x_ref, g_ref, w_ref, hb_ref, p_ref):
        h, _, _ = _rms(x_ref[...], g_ref[...])
        hb = h.astype(BF16)
        hb_ref[...] = hb
        for c0, c1 in zip(PROJ_CHUNKS[:-1], PROJ_CHUNKS[1:]):
            p_ref[:, c0:c1] = jnp.dot(hb, w_ref[:, c0:c1], preferred_element_type=F32)

    return pl.pallas_call(
        body, name="norm_proj",
        out_shape=(jax.ShapeDtypeStruct((n_rows, D_MODEL), BF16), jax.ShapeDtypeStruct((n_rows, P_IN), F32)),
        grid=(n_rows // tm,),
        in_specs=[pl.BlockSpec((tm, D_MODEL), lambda i: (i, 0)), pl.BlockSpec((1, D_MODEL), lambda i: (0, 0)),
                  pl.BlockSpec((D_MODEL, P_IN), lambda i: (0, 0))],
        out_specs=(pl.BlockSpec((tm, D_MODEL), lambda i: (i, 0)), pl.BlockSpec((tm, P_IN), lambda i: (i, 0))),
        compiler_params=_cp("parallel"),
    )(h_res, gain, w_in)


def _pool_math(u, halo, z, wg_ref, scale, row0):
    tm = u.shape[0]
    ext = jnp.concatenate([halo, u], axis=0)
    t1 = (row0 + lax.broadcasted_iota(jnp.int32, (tm, 1), 0) - (PAD_FRONT - 1)).astype(F32)
    mixed, yg = [], []
    for g, w in enumerate(POOL_WINDOWS):
        a = ext[:, g * 128:(g + 1) * 128]
        k = 1
        while k < w:
            a = a + pltpu.roll(a, k, 0)
            k *= 2
        cnt = jnp.clip(t1, 1.0, float(w))
        mg = a[HALO:, :] / cnt - u[:, g * 128:(g + 1) * 128]
        mixed.append(mg)
        yg.append(_mm(mg, wg_ref[g]))
    mixed = jnp.concatenate(mixed, axis=1)
    yg = jnp.concatenate(yg, axis=1)
    ys = yg * scale
    sig = _sigmoid(z)
    return mixed, yg, ys, sig


def _halo_above(tm):
    return lambda i: (jnp.maximum(i * (tm // HALO) - 1, 0), 0)


def _pool_fwd(proj, wg, scale, w_up):
    n_rows = proj.shape[0]
    tm = _row_tile(n_rows)

    def body(uz_ref, halo_ref, wg_ref, sc_ref, wup_ref, y_ref):
        uz = uz_ref[...]
        u, z = uz[:, :POOL_WIDTH], uz[:, POOL_WIDTH:]
        _, _, ys, sig = _pool_math(u, halo_ref[...], z, wg_ref, sc_ref[...], pl.program_id(0) * tm)
        y_ref[...] = _mm(ys * (z * sig), wup_ref[...])

    return pl.pallas_call(
        body, name="pool_fwd", out_shape=jax.ShapeDtypeStruct((n_rows, D_MODEL), F32), grid=(n_rows // tm,),
        in_specs=[pl.BlockSpec((tm, 1024), lambda i: (i, 0)), pl.BlockSpec((HALO, POOL_WIDTH), _halo_above(tm)),
                  pl.BlockSpec((4, 128, 128), lambda i: (0, 0, 0)), pl.BlockSpec((1, POOL_WIDTH), lambda i: (0, 0)),
                  pl.BlockSpec((POOL_WIDTH, D_MODEL), lambda i: (0, 0))],
        out_specs=pl.BlockSpec((tm, D_MODEL), lambda i: (i, 0)),
        compiler_params=_cp("parallel"),
    )(proj, proj, wg, scale, w_up)


def _mla_pre_fwd(proj, tab, gqa, gkva, wq, wk, wv, gqn, gkn):
    n_rows = proj.shape[0]
    tm = _row_tile(n_rows)

    def body(c_ref, kr_ref, tab_ref, gqa_ref, gkva_ref, wq_ref, wk_ref, wv_ref, gqn_ref, gkn_ref, q_ref, k_ref, v_ref):
        c, tab = c_ref[...], tab_ref[...]
        cqn, _, _ = _rms(c[:, :Q_RANK], gqa_ref[...])
        ckvn, _, _ = _rms(c[:, Q_RANK:], gkva_ref[...])
        qp, kp, vp = _mm(cqn, wq_ref[...]), _mm(ckvn, wk_ref[...]), _mm(ckvn, wv_ref[...])
        kr = pltpu.roll(kr_ref[...], QK_NOPE, 1)
        gqn, gkn = _pad_gain(gqn_ref), _pad_gain(gkn_ref)
        lane = lax.broadcasted_iota(jnp.int32, (tm, 128), 1)
        one = jnp.where(lane == V_DIM, 1.0, 0.0)
        for h in range(N_HEADS):
            blk = slice(128 * h, 128 * (h + 1))
            qn, _, _ = _head_rms(qp[:, blk], gqn)
            q_ref[h] = _rope(qn, tab).astype(BF16)
            kn, _, _ = _head_rms(kp[:, blk] + kr, gkn)
            k_ref[h] = _rope(kn, tab).astype(BF16)
            v_ref[h] = (vp[:, blk] + one).astype(BF16)

    full = lambda *s: pl.BlockSpec(s, lambda i: (0,) * len(s))
    head = pl.BlockSpec((N_HEADS, tm, 128), lambda i: (0, i, 0))
    sds = jax.ShapeDtypeStruct((N_HEADS, n_rows, 128), BF16)
    return pl.pallas_call(
        body, name="mla_pre_fwd", out_shape=(sds, sds, sds), grid=(n_rows // tm,),
        in_specs=[pl.BlockSpec((tm, 1024), lambda i: (i, COL_CQKV // 1024)),
                  pl.BlockSpec((tm, 128), lambda i: (i, COL_KR // 128)),
                  pl.BlockSpec((tm, 384), lambda i: (i, 0)),
                  full(1, Q_RANK), full(1, KV_RANK), full(Q_RANK, 1024), full(KV_RANK, 1024), full(KV_RANK, 1024),
                  full(1, QK_DIM), full(1, QK_DIM)],
        out_specs=(head, head, head),
        compiler_params=_cp("parallel"),
    )(proj, proj, tab, gqa, gkva, wq, wk, wv, gqn, gkn)


def _attn_mask(row0, col0, n_r, n_c):
    row = row0 + lax.broadcasted_iota(jnp.int32, (n_r, n_c), 0)
    col = col0 + lax.broadcasted_iota(jnp.int32, (n_r, n_c), 1)
    return (col <= row) & (col >= PAD_FRONT)


ATTN_SCALE = 1.0 / math.sqrt(QK_DIM)
EXP2_SCALE = ATTN_SCALE * math.log2(math.e)
LOG2_E = math.log2(math.e)


def _row_chunks(t, size):
    return [(r, min(size, t - r)) for r in range(0, t, size)]


def _attn_fwd(q, k, v, ex=None):
    n_rows = q.shape[1]
    t = _attn_tile(n_rows)
    nb = n_rows // t

    def body(q_ref, k_ref, v_ref, o_ref, lse_ref, m_sc, acc_sc):
        qi, ki = pl.program_id(1), pl.program_id(2)

        @pl.when(ki == 0)
        def _():
            m_sc[...] = jnp.full(m_sc.shape, -jnp.inf, F32)
            acc_sc[...] = jnp.zeros(acc_sc.shape, F32)

        def update(masked):
            heads = range(hp)
            s = [_mm_nt(q_ref[h], k_ref[h]) for h in heads]
            if masked:
                mask = _attn_mask(qi * t, ki * t, t, t)
                s = [jnp.where(mask, sh, MASK_VALUE) for sh in s]
            p = []
            for h in heads:
                m_prev = m_sc[h]
                m_new = jnp.maximum(m_prev, jnp.max(s[h], axis=1, keepdims=True))
                alpha = jnp.exp2((m_prev - m_new) * EXP2_SCALE)
                p.append(jnp.exp2((s[h] - m_new) * EXP2_SCALE).astype(BF16))
                acc_sc[h] = alpha * acc_sc[h]
                m_sc[h] = m_new
            for h in heads:
                acc_sc[h] += _mm(p[h], v_ref[h])

        edge = (ki == qi) | (ki == 0)

        @pl.when((ki <= qi) & edge)
        def _():
            update(True)

        @pl.when((ki < qi) & (ki > 0))
        def _():
            update(False)

        @pl.when(ki == nb - 1)
        def _():
            acc = acc_sc[...]
            denom = acc[:, :, V_DIM:V_DIM + 1]
            o_ref[...] = acc[:, :, :V_DIM] / denom
            lse_ref[...] = m_sc[...] * ATTN_SCALE + jnp.log(denom)

    hp = HEADS_PER_STEP_FWD
    kv_idx = lambda h, qi, ki: (h, jnp.minimum(ki, qi), 0)
    return _call_carrying(
        ex, body, "attn_fwd",
        out_shape=(jax.ShapeDtypeStruct((N_HEADS, n_rows, V_DIM), F32), jax.ShapeDtypeStruct((N_HEADS, n_rows, 1), F32)),
        grid=(N_HEADS // hp, nb, nb),
        in_specs=[pl.BlockSpec((hp, t, 128), lambda h, qi, ki: (h, qi, 0)), pl.BlockSpec((hp, t, 128), kv_idx),
                  pl.BlockSpec((hp, t, 128), kv_idx)],
        out_specs=(pl.BlockSpec((hp, t, V_DIM), lambda h, qi, ki: (h, qi, 0)), pl.BlockSpec((hp, t, 1), lambda h, qi, ki: (h, qi, 0))),
        scratch_shapes=[pltpu.VMEM((hp, t, 1), F32), pltpu.VMEM((hp, t, 128), F32)],
        semantics=("parallel", "parallel", "arbitrary"), args=(q, k, v))


def _merge_math(o_ref, z, gates, y_pool, y_mla_fn):
    o_cat = jnp.concatenate([o_ref[h] for h in range(N_HEADS)], axis=1)
    sig_z = _sigmoid(z)
    a_mla = o_cat * (z * sig_z)
    y_mla = y_mla_fn(a_mla)
    sgp, sgm = _sigmoid(gates[:, :D_MODEL]), _sigmoid(gates[:, D_MODEL:])
    merged = sgp * y_pool + sgm * y_mla
    return o_cat, sig_z, a_mla, y_mla, sgp, sgm, merged


def _mla_post_fwd(h_res, proj, o, y_pool, w_mla_up, w_out):
    n_rows = h_res.shape[0]
    tm = _row_tile(n_rows)

    def body(h_ref, g_ref, z_ref, o_ref, yp_ref, wup_ref, wout_ref, ymla_ref, hn_ref):
        _, _, _, y_mla, _, _, merged = _merge_math(o_ref, z_ref[...], g_ref[...], yp_ref[...],
                                                   lambda a: _mm(a, wup_ref[...]))
        ymla_ref[...] = y_mla
        hn_ref[...] = h_ref[...] + _mm(merged, wout_ref[...])

    row = lambda w, c: pl.BlockSpec((tm, w), lambda i: (i, c))
    return pl.pallas_call(
        body, name="mla_post_fwd",
        out_shape=(jax.ShapeDtypeStruct((n_rows, D_MODEL), F32), jax.ShapeDtypeStruct((n_rows, D_MODEL), F32)),
        grid=(n_rows // tm,),
        in_specs=[row(D_MODEL, 0), row(2048, COL_GATES // 2048), row(MLA_WIDTH, COL_ZMLA // MLA_WIDTH),
                  pl.BlockSpec((N_HEADS, tm, V_DIM), lambda i: (0, i, 0)), row(D_MODEL, 0),
                  pl.BlockSpec((MLA_WIDTH, D_MODEL), lambda i: (0, 0)), pl.BlockSpec((D_MODEL, D_MODEL), lambda i: (0, 0))],
        out_specs=(row(D_MODEL, 0), row(D_MODEL, 0)),
        compiler_params=_cp("parallel"),
    )(h_res, proj, proj, o, y_pool, w_mla_up, w_out)


def _dsilu(z, sig):
    return sig * (1.0 + z * (1.0 - sig))


def _acc(ref, val, first):
    @pl.when(first)
    def _():
        ref[...] = val

    @pl.when(jnp.logical_not(first))
    def _():
        ref[...] += val


def _mla_post_bwd(dh, proj, o, y_pool, y_mla, w_mla_up, w_out):
    n_rows = dh.shape[0]
    tm = _row_tile(n_rows)

    def body(dh_ref, g_ref, z_ref, o_ref, yp_ref, ym_ref, wup_ref, wout_ref,
             dgz_ref, dyp_ref, do_ref, delta_ref, dwout_ref, dwup_ref):
        first = pl.program_id(0) == 0
        z, y_pool, y_mla = z_ref[...], yp_ref[...], ym_ref[...]
        dhv = dh_ref[...]
        dmerged = _mm_nt(dhv, wout_ref[...])
        o_cat, sig_z, a_mla, _, sgp, sgm, merged = _merge_math(o_ref, z, g_ref[...], y_pool, lambda a: y_mla)
        dy_mla = dmerged * sgm
        da = _mm_nt(dy_mla, wup_ref[...])
        dyp_ref[...] = dmerged * sgp
        dgz_ref[:, 0:D_MODEL] = dmerged * y_pool * (sgp * (1.0 - sgp))
        dgz_ref[:, D_MODEL:2 * D_MODEL] = dmerged * y_mla * (sgm * (1.0 - sgm))
        _acc(dwout_ref, _mm_tn(merged, dhv), first)
        do_cat = da * (z * sig_z)
        dgz_ref[:, 2 * D_MODEL:] = da * o_cat * _dsilu(z, sig_z)
        for h in range(N_HEADS):
            doh = do_cat[:, V_DIM * h:V_DIM * (h + 1)]
            do_ref[h] = doh
            delta_ref[h] = jnp.sum(doh * o_cat[:, V_DIM * h:V_DIM * (h + 1)], axis=1, keepdims=True)
        _acc(dwup_ref, _mm_tn(a_mla, dy_mla), first)

    row = lambda w, c: pl.BlockSpec((tm, w), lambda i: (i, c))
    head = lambda w: pl.BlockSpec((N_HEADS, tm, w), lambda i: (0, i, 0))
    const = lambda r, c: pl.BlockSpec((r, c), lambda i: (0, 0))
    return pl.pallas_call(
        body, name="mla_post_bwd",
        out_shape=(jax.ShapeDtypeStruct((n_rows, 2560), F32), jax.ShapeDtypeStruct((n_rows, D_MODEL), F32),
                   jax.ShapeDtypeStruct((N_HEADS, n_rows, V_DIM), F32), jax.ShapeDtypeStruct((N_HEADS, n_rows, 1), F32),
                   jax.ShapeDtypeStruct((D_MODEL, D_MODEL), F32), jax.ShapeDtypeStruct((MLA_WIDTH, D_MODEL), F32)),
        grid=(n_rows // tm,),
        in_specs=[row(D_MODEL, 0), row(2048, COL_GATES // 2048), row(MLA_WIDTH, COL_ZMLA // MLA_WIDTH), head(V_DIM),
                  row(D_MODEL, 0), row(D_MODEL, 0), const(MLA_WIDTH, D_MODEL), const(D_MODEL, D_MODEL)],
        out_specs=(row(2560, 0), row(D_MODEL, 0), head(V_DIM), head(1), const(D_MODEL, D_MODEL), const(MLA_WIDTH, D_MODEL)),
        compiler_params=_cp("arbitrary"),
    )(dh, proj, proj, o, y_pool, y_mla, w_mla_up, w_out)


def _attn_bwd(q, k, v, do, lse, delta, ex=None):
    n_rows = q.shape[1]
    t = _attn_tile(n_rows)
    nb = n_rows // t

    def body(q_ref, k_ref, v_ref, do_ref, lse_ref, delta_ref, dq_ref, dk_ref, dv_ref, dk_sc, dv_sc):
        ki, qi = pl.program_id(1), pl.program_id(2)

        @pl.when((ki == 0) & (qi == 0))
        def _():
            dq_ref[...] = jnp.zeros(dq_ref.shape, F32)

        @pl.when(qi == 0)
        def _():
            dk_sc[...] = jnp.zeros(dk_sc.shape, F32)
            dv_sc[...] = jnp.zeros(dv_sc.shape, F32)

        def step(masked):
            heads = range(HEADS_PER_STEP)
            st = [_mm_nt(k_ref[h], q_ref[h]) for h in heads]
            dpt = [_mm_nt(v_ref[h, :, 0:V_DIM], do_ref[h]) for h in heads]
            if masked:
                key = ki * t + lax.broadcasted_iota(jnp.int32, (t, t), 0)
                qry = qi * t + lax.broadcasted_iota(jnp.int32, (t, t), 1)
                mask = (key <= qry) & (key >= PAD_FRONT)
                st = [jnp.where(mask, sh, MASK_VALUE) for sh in st]
            pt = [jnp.exp2(st[h] * EXP2_SCALE - lse_ref[h] * LOG2_E) for h in heads]
            dst = [pt[h] * (dpt[h] - delta_ref[h]) for h in heads]
            rows = pl.ds(pl.multiple_of(qi * t, t), t)
            for h in heads:
                dv_sc[h] += _mm(pt[h], do_ref[h])
                dk_sc[h] += _mm(dst[h], q_ref[h])
                dq_ref[h, rows, :] += _mm_tn(dst[h], k_ref[h]) * ATTN_SCALE

        edge = (ki == qi) | (ki == 0)

        @pl.when((qi >= ki) & edge)
        def _():
            step(True)

        @pl.when((qi > ki) & (ki > 0))
        def _():
            step(False)

        @pl.when(qi == nb - 1)
        def _():
            dk_ref[...] = dk_sc[...] * ATTN_SCALE
            dv_ref[...] = dv_sc[...]

    hp = HEADS_PER_STEP
    q_idx = lambda h, ki, qi: (h, jnp.maximum(qi, ki), 0)
    row_idx = lambda h, ki, qi: (h, 0, jnp.maximum(qi, ki))
    k_idx = lambda h, ki, qi: (h, ki, 0)
    return _call_carrying(
        ex, body, "attn_bwd",
        out_shape=(jax.ShapeDtypeStruct((N_HEADS, n_rows, 128), F32), jax.ShapeDtypeStruct((N_HEADS, n_rows, 128), F32),
                   jax.ShapeDtypeStruct((N_HEADS, n_rows, V_DIM), F32)),
        grid=(N_HEADS // hp, nb, nb),
        in_specs=[pl.BlockSpec((hp, t, 128), q_idx), pl.BlockSpec((hp, t, 128), k_idx), pl.BlockSpec((hp, t, 128), k_idx),
                  pl.BlockSpec((hp, t, V_DIM), q_idx), pl.BlockSpec((hp, 1, t), row_idx), pl.BlockSpec((hp, 1, t), row_idx)],
        out_specs=(pl.BlockSpec((hp, n_rows, 128), lambda h, ki, qi: (h, 0, 0), pipeline_mode=pl.Buffered(1)),
                   pl.BlockSpec((hp, t, 128), k_idx),
                   pl.BlockSpec((hp, t, V_DIM), k_idx)),
        scratch_shapes=[pltpu.VMEM((hp, t, 128), F32), pltpu.VMEM((hp, t, V_DIM), F32)],
        semantics=("parallel", "arbitrary", "arbitrary"), args=(q, k, v, do, lse, delta))


def _mla_pre_bwd(proj, tab, gqa, gkva, wq, wk, wv, gqn, gkn, dq, dk, dv):
    n_rows = proj.shape[0]
    tm = _row_tile(n_rows)

    def body(c_ref, kr_ref, tab_ref, gqa_ref, gkva_ref, wq_ref, wk_ref, wv_ref, gqn_ref, gkn_ref, dq_ref, dk_ref, dv_ref,
             dc_ref, dkr_ref, dwq_ref, dwk_ref, dwv_ref, dgqa_ref, dgkva_ref, dgqn_ref, dgkn_ref):
        first = pl.program_id(0) == 0
        c, tab = c_ref[...], tab_ref[...]
        gqa, gkva = gqa_ref[...], gkva_ref[...]
        cqn, cq_h, cq_inv = _rms(c[:, :Q_RANK], gqa)
        ckvn, ckv_h, ckv_inv = _rms(c[:, Q_RANK:], gkva)
        qp, kp = _mm(cqn, wq_ref[...]), _mm(ckvn, wk_ref[...])
        kr = pltpu.roll(kr_ref[...], QK_NOPE, 1)
        gqn, gkn = _pad_gain(gqn_ref), _pad_gain(gkn_ref)
        dq_parts, dk_parts, dv_parts = [], [], []
        dkr = jnp.zeros((tm, 128), F32)
        dgqn = jnp.zeros((1, 128), F32)
        dgkn = jnp.zeros((1, 128), F32)
        zv = jnp.zeros((tm, 128 - V_DIM), F32)
        for h in range(N_HEADS):
            blk = slice(128 * h, 128 * (h + 1))
            _, xh, inv = _head_rms(qp[:, blk], gqn)
            dx, dg = _head_rms_bwd(_rope_bwd(dq_ref[h], tab), xh, inv, gqn)
            dq_parts.append(dx)
            dgqn = dgqn + dg
            _, xh, inv = _head_rms(kp[:, blk] + kr, gkn)
            dx, dg = _head_rms_bwd(_rope_bwd(dk_ref[h], tab), xh, inv, gkn)
            dk_parts.append(dx)
            dgkn = dgkn + dg
            dkr = dkr + dx
            dv_parts.append(jnp.concatenate([dv_ref[h], zv], axis=1))
        dqp = jnp.concatenate(dq_parts, axis=1)
        dkp = jnp.concatenate(dk_parts, axis=1)
        dvp = jnp.concatenate(dv_parts, axis=1)
        _acc(dwq_ref, _mm_tn(cqn, dqp), first)
        _acc(dwk_ref, _mm_tn(ckvn, dkp), first)
        _acc(dwv_ref, _mm_tn(ckvn, dvp), first)
        dcq, dg1 = _rms_bwd(_mm_nt(dqp, wq_ref[...]), cq_h, cq_inv, gqa)
        dckv, dg2 = _rms_bwd(_mm_nt(dkp, wk_ref[...]) + _mm_nt(dvp, wv_ref[...]), ckv_h, ckv_inv, gkva)
        _acc(dgqa_ref, dg1, first)
        _acc(dgkva_ref, dg2, first)
        _acc(dgqn_ref, dgqn, first)
        _acc(dgkn_ref, dgkn, first)
        dc_ref[...] = jnp.concatenate([dcq, dckv], axis=1)
        lane = lax.broadcasted_iota(jnp.int32, (tm, 128), 1)
        dkr_ref[...] = jnp.where(lane < QK_ROPE, pltpu.roll(dkr, 128 - QK_NOPE, 1), 0.0)

    full = lambda *s: pl.BlockSpec(s, lambda i: (0,) * len(s))
    head = lambda w: pl.BlockSpec((N_HEADS, tm, w), lambda i: (0, i, 0))
    return pl.pallas_call(
        body, name="mla_pre_bwd",
        out_shape=(jax.ShapeDtypeStruct((n_rows, 1024), F32), jax.ShapeDtypeStruct((n_rows, 128), F32),
                   jax.ShapeDtypeStruct((Q_RANK, 1024), F32), jax.ShapeDtypeStruct((KV_RANK, 1024), F32),
                   jax.ShapeDtypeStruct((KV_RANK, 1024), F32),
                   jax.ShapeDtypeStruct((1, Q_RANK), F32), jax.ShapeDtypeStruct((1, KV_RANK), F32),
                   jax.ShapeDtypeStruct((1, 128), F32), jax.ShapeDtypeStruct((1, 128), F32)),
        grid=(n_rows // tm,),
        in_specs=[pl.BlockSpec((tm, 1024), lambda i: (i, COL_CQKV // 1024)),
                  pl.BlockSpec((tm, 128), lambda i: (i, COL_KR // 128)),
                  pl.BlockSpec((tm, 384), lambda i: (i, 0)),
                  full(1, Q_RANK), full(1, KV_RANK), full(Q_RANK, 1024), full(KV_RANK, 1024), full(KV_RANK, 1024),
                  full(1, QK_DIM), full(1, QK_DIM), head(128), head(128), head(V_DIM)],
        out_specs=(pl.BlockSpec((tm, 1024), lambda i: (i, 0)), pl.BlockSpec((tm, 128), lambda i: (i, 0)),
                   full(Q_RANK, 1024), full(KV_RANK, 1024), full(KV_RANK, 1024), full(1, Q_RANK), full(1, KV_RANK),
                   full(1, 128), full(1, 128)),
        compiler_params=_cp("arbitrary"),
    )(proj, proj, tab, gqa, gkva, wq, wk, wv, gqn, gkn, dq, dk, dv)


def _pool_bwd_a(proj, dy_pool, wg, scale, w_up):
    n_rows = proj.shape[0]
    tm = _row_tile(n_rows)

    def body(uz_ref, halo_ref, dy_ref, wg_ref, sc_ref, wup_ref, dmz_ref, dwg_ref, dsc_ref, dwup_ref):
        first = pl.program_id(0) == 0
        uz = uz_ref[...]
        u, z = uz[:, :POOL_WIDTH], uz[:, POOL_WIDTH:]
        scale_v = sc_ref[...]
        mixed, yg, ys, sig = _pool_math(u, halo_ref[...], z, wg_ref, scale_v, pl.program_id(0) * tm)
        sp = z * sig
        dy = dy_ref[...]
        da = _mm_nt(dy, wup_ref[...])
        _acc(dwup_ref, _mm_tn(ys * sp, dy), first)
        dys = da * sp
        _acc(dsc_ref, jnp.sum(dys * yg, axis=0, keepdims=True), first)
        dyg = dys * scale_v
        for g in range(4):
            cols = slice(g * 128, (g + 1) * 128)
            dmz_ref[:, cols] = _mm_nt(dyg[:, cols], wg_ref[g])
            _acc(dwg_ref.at[g], _mm_tn(mixed[:, cols], dyg[:, cols]), first)
        dmz_ref[:, POOL_WIDTH:] = da * ys * _dsilu(z, sig)

    return pl.pallas_call(
        body, name="pool_bwd_a",
        out_shape=(jax.ShapeDtypeStruct((n_rows, 1024), F32), jax.ShapeDtypeStruct((4, 128, 128), F32),
                   jax.ShapeDtypeStruct((1, POOL_WIDTH), F32), jax.ShapeDtypeStruct((POOL_WIDTH, D_MODEL), F32)),
        grid=(n_rows // tm,),
        in_specs=[pl.BlockSpec((tm, 1024), lambda i: (i, 0)), pl.BlockSpec((HALO, POOL_WIDTH), _halo_above(tm)),
                  pl.BlockSpec((tm, D_MODEL), lambda i: (i, 0)),
                  pl.BlockSpec((4, 128, 128), lambda i: (0, 0, 0)), pl.BlockSpec((1, POOL_WIDTH), lambda i: (0, 0)),
                  pl.BlockSpec((POOL_WIDTH, D_MODEL), lambda i: (0, 0))],
        out_specs=(pl.BlockSpec((tm, 1024), lambda i: (i, 0)), pl.BlockSpec((4, 128, 128), lambda i: (0, 0, 0)),
                   pl.BlockSpec((1, POOL_WIDTH), lambda i: (0, 0)), pl.BlockSpec((POOL_WIDTH, D_MODEL), lambda i: (0, 0))),
        compiler_params=_cp("arbitrary"),
    )(proj, proj, dy_pool, wg, scale, w_up)


def _pool_bwd_b(dmz):
    n_rows = dmz.shape[0]
    tm = _row_tile(n_rows)
    n_tiles = n_rows // tm
    n_ext = tm + HALO

    def body(dmz_ref, halo_ref, o_ref):
        i = pl.program_id(0)
        v = dmz_ref[...]
        dm = v[:, :POOL_WIDTH]
        halo = jnp.where(i == n_tiles - 1, 0.0, halo_ref[...])
        ext = jnp.concatenate([dm, halo], axis=0)
        t1 = (i * tm + lax.broadcasted_iota(jnp.int32, (n_ext, 1), 0) - (PAD_FRONT - 1)).astype(F32)
        du = []
        for g, w in enumerate(POOL_WINDOWS):
            cols = slice(g * 128, (g + 1) * 128)
            a = ext[:, cols] / jnp.clip(t1, 1.0, float(w))
            k = 1
            while k < w:
                a = a + pltpu.roll(a, n_ext - k, 0)
                k *= 2
            du.append(a[:tm, :] - dm[:, cols])
        o_ref[...] = jnp.concatenate(du + [v[:, POOL_WIDTH:]], axis=1)

    last_halo = n_rows // HALO - 1
    return pl.pallas_call(
        body, name="pool_bwd_b", out_shape=jax.ShapeDtypeStruct((n_rows, 1024), F32), grid=(n_tiles,),
        in_specs=[pl.BlockSpec((tm, 1024), lambda i: (i, 0)),
                  pl.BlockSpec((HALO, POOL_WIDTH), lambda i: (jnp.minimum((i + 1) * (tm // HALO), last_halo), 0))],
        out_specs=pl.BlockSpec((tm, 1024), lambda i: (i, 0)),
        compiler_params=_cp("parallel"),
    )(dmz, dmz)


def _norm_proj_bwd(d_uz, d_cqkv, d_gz, d_kr, w_in, h_res, gain, dh_out):
    n_rows = h_res.shape[0]
    tm = _row_tile(n_rows)
    pieces = ((COL_UZ, 1024), (COL_CQKV, 1024), (COL_GATES, 2560), (COL_KR, 128))

    def body(a_ref, b_ref, c_ref, d_ref, w_ref, x_ref, g_ref, dho_ref, dhi_ref, dg_ref):
        i = pl.program_id(0)
        dh = None
        for ref, (c0, wd) in zip((a_ref, b_ref, c_ref, d_ref), pieces):
            part = _mm_nt(ref[...], w_ref[:, c0:c0 + wd])
            dh = part if dh is None else dh + part
        g = g_ref[...]
        _, xh, inv = _rms(x_ref[...], g)
        dx, dg = _rms_bwd(dh, xh, inv, g)
        _acc(dg_ref, dg, i == 0)
        row = i * tm + lax.broadcasted_iota(jnp.int32, (tm, 1), 0)
        dhi_ref[...] = jnp.where(row >= PAD_FRONT, dho_ref[...] + dx, 0.0)

    row = lambda w: pl.BlockSpec((tm, w), lambda i: (i, 0))
    return pl.pallas_call(
        body, name="norm_proj_bwd",
        out_shape=(jax.ShapeDtypeStruct((n_rows, D_MODEL), F32), jax.ShapeDtypeStruct((1, D_MODEL), F32)),
        grid=(n_rows // tm,),
        in_specs=[row(1024), row(1024), row(2560), row(128), pl.BlockSpec((D_MODEL, P_IN), lambda i: (0, 0)),
                  row(D_MODEL), pl.BlockSpec((1, D_MODEL), lambda i: (0, 0)), row(D_MODEL)],
        out_specs=(row(D_MODEL), pl.BlockSpec((1, D_MODEL), lambda i: (0, 0))),
        compiler_params=_cp("arbitrary"),
    )(d_uz, d_cqkv, d_gz, d_kr, w_in, h_res, gain, dh_out)


def _weight_grad(name, a, b):
    k_rows, m = a.shape
    n = b.shape[1]
    tk = next(t for t in (1664, 640, 128) if k_rows % t == 0)
    tn = next(t for t in (1280, 1024, 512, 128) if n % t == 0)

    def body(a_ref, b_ref, o_ref):
        _acc(o_ref, _mm_tn(a_ref[...], b_ref[...]), pl.program_id(1) == 0)

    return pl.pallas_call(
        body, name=name, out_shape=jax.ShapeDtypeStruct((m, n), F32), grid=(n // tn, k_rows // tk),
        in_specs=[pl.BlockSpec((tk, m), lambda j, k: (k, 0)), pl.BlockSpec((tk, tn), lambda j, k: (k, j))],
        out_specs=pl.BlockSpec((m, tn), lambda j, k: (0, j)),
        compiler_params=_cp("parallel", "arbitrary"),
    )(a, b)


def _adamw(name, w, m, v, gbufs):
    r, c = w.shape
    depth = len(gbufs)
    r_l = r // depth
    tr = r_l
    if r_l * c * 4 > (1 << 20):
        tr = 256 if r_l % 256 == 0 else 128
    assert r_l * depth == r and r_l % tr == 0, (name, r, depth, tr)
    n_t = r_l // tr

    def body(w_ref, m_ref, v_ref, *refs):
        g_refs, (go_ref, d_ref, mo_ref, vo_ref) = refs[:depth], refs[depth:]

        def update(g_ref):
            g = g_ref[0].astype(F32)
            for s in range(1, N_DEV):
                g = g + g_ref[s].astype(F32)
            go_ref[...] = g
            m_new = ADAM_B1 * m_ref[...] + (1.0 - ADAM_B1) * g
            v_new = ADAM_B2 * v_ref[...] + (1.0 - ADAM_B2) * (g * g)
            mo_ref[...] = m_new
            vo_ref[...] = v_new
            m_hat = m_new / (1.0 - ADAM_B1 ** ADAM_STEP)
            v_hat = v_new / (1.0 - ADAM_B2 ** ADAM_STEP)
            d_ref[...] = -ADAM_LR * (m_hat / (jnp.sqrt(v_hat) + ADAM_EPS) + ADAM_WD * w_ref[...])

        for j in range(depth):
            pl.when(pl.program_id(0) == j)(lambda j=j: update(g_refs[j]))

    def g_spec(j):
        return pl.BlockSpec((N_DEV, tr, c), lambda l, i: (0, jnp.where(l == j, i, jnp.where(l < j, 0, n_t - 1)), 0))

    spec = pl.BlockSpec((tr, c), lambda l, i: (l * n_t + i, 0))
    sds = jax.ShapeDtypeStruct((r, c), F32)
    return pl.pallas_call(
        body, name=name, out_shape=(sds, sds, sds, sds), grid=(depth, n_t),
        in_specs=[spec, spec, spec] + [g_spec(j) for j in range(depth)],
        out_specs=(spec, spec, spec, spec),
        compiler_params=_cp("arbitrary", "arbitrary"),
    )(w, m, v, *gbufs)


SMALL = ("norm_gain", "pool_w_group", "pool_scale", "q_a_norm_gain", "kv_a_norm_gain", "q_norm_gain", "k_norm_gain")


def _pack_small(parts):
    rows = []
    for p in parts:
        flat = p.reshape(-1)
        pad = (-flat.shape[0]) % 1024
        if pad:
            flat = jnp.concatenate([flat, jnp.zeros((pad,), F32)])
        rows.append(flat.reshape(-1, 128))
    n_rows = sum(r.shape[0] for r in rows)
    tail = (-n_rows) % 256
    if tail:
        rows.append(jnp.zeros((tail, 128), F32))
    return jnp.concatenate(rows, axis=0)


def _unpack_small(packed, shapes):
    out, r0 = [], 0
    for shp in shapes:
        n = math.prod(shp)
        n_rows = (n + 1023) // 1024 * 8
        out.append(packed[r0:r0 + n_rows].reshape(-1)[:n].reshape(shp))
        r0 += n_rows
    return out


def kernel(x, positions, meta_tokens, norm_gain, w_in, pool_w_group, pool_scale, pool_w_up, q_a_norm_gain, kv_a_norm_gain, w_q_b, w_kv_b, q_norm_gain, k_norm_gain, mla_w_up, w_out, loss_target, m_meta_tokens, m_norm_gain, m_w_in, m_pool_w_group, m_pool_scale, m_pool_w_up, m_q_a_norm_gain, m_kv_a_norm_gain, m_w_q_b, m_w_kv_b, m_q_norm_gain, m_k_norm_gain, m_mla_w_up, m_w_out, v_meta_tokens, v_norm_gain, v_w_in, v_pool_w_group, v_pool_scale, v_pool_w_up, v_q_a_norm_gain, v_kv_a_norm_gain, v_w_q_b, v_w_kv_b, v_q_norm_gain, v_k_norm_gain, v_mla_w_up, v_w_out):
    x2, target = x[0], loss_target[0]

    big = dict(w_in=w_in, pool_w_up=pool_w_up, w_q_b=w_q_b, w_kv_b=w_kv_b, mla_w_up=mla_w_up, w_out=w_out)
    names_big = list(big)
    shards = {n: _cast_bf16("cast_" + n, w) for n, w in big.items()}

    def layer_shards(l):
        return [shards[n][l] for n in names_big]

    def repack(gathered):
        g_in, g_pup, g_qb, g_kvb, g_mup, g_out = gathered
        return dict(w_in=_repack_w_in(g_in),
                    pool_up=_repack_cols("repack_pool_up", g_pup[:, None], BF16)[0],
                    q_heads=_repack_q_heads(g_qb),
                    kv_heads=_repack_kv_heads(g_kvb),
                    mla_up=_repack_cols("repack_mla_up", g_mup[:, None], BF16)[0],
                    out=_repack_rows("repack_out", g_out[:, None])[0])

    first = _exchange("gather_first", [meta_tokens] + layer_shards(0), [])
    weights = [repack(first[1:])]

    h = _embed(x2, first[0])
    pos = jnp.concatenate([jnp.zeros((PAD_FRONT,), jnp.int32), jnp.arange(N_META, dtype=jnp.int32),
                           positions[0] + N_META]).astype(F32).reshape(-1, 1)
    half = QK_ROPE // 2
    inv_freq = ROPE_THETA ** (-jnp.arange(half, dtype=F32) / half)
    tab = _rope_table(pos, jnp.tile(inv_freq, 128 // half).reshape(1, 128))

    row = lambda a, l: a[l].reshape(1, -1)
    saved = []
    for l in range(DEPTH):
        w = weights[l]
        hb, proj = _norm_proj(h, row(norm_gain, l), w["w_in"])
        y_pool = _pool_fwd(proj, pool_w_group[l], row(pool_scale, l), w["pool_up"])
        q, k, v = _mla_pre_fwd(proj, tab, row(q_a_norm_gain, l), row(kv_a_norm_gain, l), w["q_heads"], *w["kv_heads"],
                               row(q_norm_gain, l), row(k_norm_gain, l))
        ex = _Exchange(layer_shards(l + 1), []) if l + 1 < DEPTH else None
        res = _attn_fwd(q, k, v, ex)
        o, lse = res[0], res[1]
        if ex is not None:
            weights.append(repack(res[2:]))
        y_mla, h_next = _mla_post_fwd(h, proj, o, y_pool, w["mla_up"], w["out"])
        saved.append((h, hb, proj, y_pool, q, k, v, o, lse, y_mla))
        h = h_next

    loss_part, dh = _loss_head(h, target)
    loss = lax.psum(loss_part[0, 0], MESH_AXES)

    grads = {n: [None] * DEPTH for n in SMALL}
    pending = None
    received = [None] * DEPTH
    for l in reversed(range(DEPTH)):
        w = weights[l]
        h_l, hb, proj, y_pool, q, k, v, o, lse, y_mla = saved[l]
        d_gz, dy_pool, do, delta, g_out, g_mla_up = _mla_post_bwd(dh, proj, o, y_pool, y_mla, w["mla_up"], w["out"])
        ex = _Exchange([], pending) if pending is not None else None
        as_rows = lambda a: a.reshape(N_HEADS, 1, -1)
        res = _attn_bwd(q, k, v, do, as_rows(lse), as_rows(delta), ex)
        dq, dk, dv = res[0], res[1], res[2]
        if ex is not None:
            received[l + 1] = res[3:]
        (d_cqkv, d_kr, g_q, g_k, g_v, grads["q_a_norm_gain"][l], grads["kv_a_norm_gain"][l], g_qn, g_kn) = _mla_pre_bwd(
            proj, tab, row(q_a_norm_gain, l), row(kv_a_norm_gain, l), w["q_heads"], *w["kv_heads"], row(q_norm_gain, l),
            row(k_norm_gain, l), dq, dk, dv)
        grads["q_norm_gain"][l], grads["k_norm_gain"][l] = g_qn[:, :QK_DIM], g_kn[:, :QK_DIM]
        dmz, grads["pool_w_group"][l], grads["pool_scale"][l], g_pool_up = _pool_bwd_a(
            proj, dy_pool, pool_w_group[l], row(pool_scale, l), w["pool_up"])
        d_uz = _pool_bwd_b(dmz)
        g_in = [_weight_grad("dw_in", hb, d) for d in (d_uz, d_cqkv, d_gz, d_kr)]
        dh, grads["norm_gain"][l] = _norm_proj_bwd(d_uz, d_cqkv, d_gz, d_kr, w["w_in"], h_l, row(norm_gain, l), dh)
        pending = [_unpack_w_in(g_in),
                   _unpack_cols("unpack_pool_up", g_pool_up[None], 128, BF16)[:, 0],
                   _unpack_q_heads(g_q),
                   _unpack_kv_heads(g_k, g_v),
                   _unpack_cols("unpack_mla_up", g_mla_up[None], 128, BF16)[:, 0],
                   _unpack_rows("unpack_out", g_out[None], D_MODEL // N_DEV)[:, 0]]

    grad_x = dh[HEAD_ROWS:][None]
    d_meta = dh[PAD_FRONT:HEAD_ROWS]

    small_shapes = [a.shape for a in (norm_gain, pool_w_group, pool_scale, q_a_norm_gain, kv_a_norm_gain, q_norm_gain, k_norm_gain)]
    small_grad = _pack_small([jnp.stack(grads[n], axis=0).reshape(s) for n, s in zip(SMALL, small_shapes)])
    meta_parts = _unpack_cols("unpack_meta", d_meta[None], 128, F32)[:, 0]
    last = _exchange("exchange_last", [small_grad], [meta_parts] + pending)
    small_buf = last[0]
    received[0] = last[2:]
    sharded = ["meta_tokens"] + names_big
    shard_bufs = {"meta_tokens": [last[1]]}
    for j, n in enumerate(names_big):
        shard_bufs[n] = [received[l][j] for l in range(DEPTH)]

    given = dict(meta_tokens=(meta_tokens, m_meta_tokens, v_meta_tokens), w_in=(w_in, m_w_in, v_w_in),
                 pool_w_up=(pool_w_up, m_pool_w_up, v_pool_w_up), w_q_b=(w_q_b, m_w_q_b, v_w_q_b),
                 w_kv_b=(w_kv_b, m_w_kv_b, v_w_kv_b), mla_w_up=(mla_w_up, m_mla_w_up, v_mla_w_up),
                 w_out=(w_out, m_w_out, v_w_out))
    result = {}
    for n in sharded:
        w, m, v = given[n]
        cols = w.shape[-1]
        res = _adamw("adamw_" + n, w.reshape(-1, cols), m.reshape(-1, cols), v.reshape(-1, cols), shard_bufs[n])
        result[n] = [r.reshape(w.shape) for r in res]
    small_w = (norm_gain, pool_w_group, pool_scale, q_a_norm_gain, kv_a_norm_gain, q_norm_gain, k_norm_gain)
    small_m = (m_norm_gain, m_pool_w_group, m_pool_scale, m_q_a_norm_gain, m_kv_a_norm_gain, m_q_norm_gain, m_k_norm_gain)
    small_v = (v_norm_gain, v_pool_w_group, v_pool_scale, v_q_a_norm_gain, v_kv_a_norm_gain, v_q_norm_gain, v_k_norm_gain)
    res = _adamw("adamw_small", _pack_small(small_w), _pack_small(small_m), _pack_small(small_v), [small_buf])
    small_res = [_unpack_small(r, small_shapes) for r in res]
    for j, n in enumerate(SMALL):
        result[n] = [small_res[kind][j] for kind in range(4)]

    order = ("meta_tokens", "norm_gain", "w_in", "pool_w_group", "pool_scale", "pool_w_up", "q_a_norm_gain",
             "kv_a_norm_gain", "w_q_b", "w_kv_b", "q_norm_gain", "k_norm_gain", "mla_w_up", "w_out")
    outs = [loss, grad_x]
    for kind in range(4):
        outs += [result[n][kind] for n in order]
    return tuple(outs)
```

```python
import math

import jax
import jax.numpy as jnp
from jax import lax
from jax.experimental import pallas as pl
from jax.experimental.pallas import tpu as pltpu

F32 = jnp.float32
BF16 = jnp.bfloat16

D_MODEL = 1024
DEPTH = 4
N_META = 16
PAD_FRONT = 112
HEAD_ROWS = PAD_FRONT + N_META
POOL_WIDTH = 512
POOL_WINDOWS = (2, 4, 8, 16)
POOL_GROUP_DIM = 128
HALO = 16
N_HEADS = 8
HEADS_PER_STEP = 4
HEADS_PER_STEP_FWD = 8
QK_NOPE = 64
QK_ROPE = 32
QK_DIM = 96
V_DIM = 64
MLA_WIDTH = 512
Q_RANK = 768
KV_RANK = 256
ROPE_THETA = 10000.0
NORM_EPS = 1e-6
MASK_VALUE = -1e30
D_IN = 4640
N_DEV = 8
IN_SHARD = D_IN // N_DEV

P_IN = 4736
IN_SEGMENTS = ((0, 2048, 0), (2048, 2080, 4608), (2080, 2592, 4096), (2592, 3616, 2048), (3616, 4640, 3072))
COL_UZ, COL_CQKV, COL_GATES, COL_ZMLA, COL_KR = 0, 1024, 2048, 4096, 4608

ADAM_LR = 0.001
ADAM_B1 = 0.9
ADAM_B2 = 0.999
ADAM_EPS = 1e-08
ADAM_WD = 0.01
ADAM_STEP = 10

VMEM_LIMIT = 56 * 1024 * 1024
MESH_AXES = ("x", "y", "c")


def _cp(*sem):
    return pltpu.CompilerParams(dimension_semantics=sem, vmem_limit_bytes=VMEM_LIMIT)


def _row_tile(n_rows):
    return 320 if n_rows % 320 == 0 else 128


def _attn_tile(n_rows):
    return 640 if (n_rows % 640 == 0 and n_rows > 640) else 128


def _mm(a, b):
    return jnp.dot(a.astype(BF16), b.astype(BF16), preferred_element_type=F32)


def _mm_nt(a, b):
    return lax.dot_general(a.astype(BF16), b.astype(BF16), (((1,), (1,)), ((), ())), preferred_element_type=F32)


def _mm_tn(a, b):
    return lax.dot_general(a.astype(BF16), b.astype(BF16), (((0,), (0,)), ((), ())), preferred_element_type=F32)


def _sigmoid(z):
    return 1.0 / (1.0 + jnp.exp(-z))


def _rms(x, g):
    inv = lax.rsqrt(jnp.mean(x * x, axis=-1, keepdims=True) + NORM_EPS)
    xh = x * inv
    return xh * g, xh, inv


def _rms_bwd(dy, xh, inv, g):
    dg = jnp.sum(dy * xh, axis=0, keepdims=True)
    dxh = dy * g
    dx = inv * (dxh - xh * jnp.mean(dxh * xh, axis=-1, keepdims=True))
    return dx, dg


def _rope(x, tab):
    return x * tab[:, 0:128] + pltpu.roll(x, 112, 1) * tab[:, 128:256] + pltpu.roll(x, 16, 1) * tab[:, 256:384]


def _rope_bwd(dy, tab):
    return dy * tab[:, 0:128] + pltpu.roll(dy * tab[:, 128:256], 16, 1) + pltpu.roll(dy * tab[:, 256:384], 112, 1)


def _head_rms(x, g):
    inv = lax.rsqrt(jnp.sum(x * x, axis=-1, keepdims=True) * (1.0 / QK_DIM) + NORM_EPS)
    xh = x * inv
    return xh * g, xh, inv


def _head_rms_bwd(dy, xh, inv, g):
    dg = jnp.sum(dy * xh, axis=0, keepdims=True)
    dxh = dy * g
    dx = inv * (dxh - xh * (jnp.sum(dxh * xh, axis=-1, keepdims=True) * (1.0 / QK_DIM)))
    return dx, dg


def _pad_gain(g_ref):
    return jnp.concatenate([g_ref[...], jnp.zeros((1, 128 - QK_DIM), F32)], axis=1)


class _Exchange:
    def __init__(self, gather_list, scatter_list):
        self.arrays = list(gather_list) + list(scatter_list)
        self.n_g, self.n = len(gather_list), len(self.arrays)
        self.out_shape = [jax.ShapeDtypeStruct((N_DEV,) + a.shape, a.dtype) for a in gather_list]
        self.out_shape += [jax.ShapeDtypeStruct(a.shape, a.dtype) for a in scatter_list]
        self.specs = [pl.BlockSpec(memory_space=pl.ANY)] * self.n
        self.sems = [pltpu.SemaphoreType.DMA((7 * self.n,)), pltpu.SemaphoreType.DMA((7 * self.n,)),
                     pltpu.SemaphoreType.DMA((self.n,))]

    def copies(self, ins, outs, sems):
        send_sems, recv_sems, local_sems = sems
        x, y, c = lax.axis_index("x"), lax.axis_index("y"), lax.axis_index("c")
        me = 4 * x + 2 * y + c

        def src(a, slot):
            return ins[a] if a < self.n_g else ins[a].at[slot]

        out = [pltpu.make_async_copy(src(a, me), outs[a].at[me], local_sems.at[a]) for a in range(self.n)]
        for k in range(1, N_DEV):
            px = 1 - x if k & 4 else x
            py = 1 - y if k & 2 else y
            pc = 1 - c if k & 1 else c
            peer = 4 * px + 2 * py + pc
            for a in range(self.n):
                out.append(pltpu.make_async_remote_copy(
                    src_ref=src(a, peer), dst_ref=outs[a].at[me],
                    send_sem=send_sems.at[a * 7 + k - 1], recv_sem=recv_sems.at[a * 7 + k - 1],
                    device_id=(px, py, pc), device_id_type=pl.DeviceIdType.MESH))
        return out


def _exchange(name, gather_list, scatter_list):
    ex = _Exchange(gather_list, scatter_list)
    n = ex.n

    def body(*refs):
        copies = ex.copies(refs[:n], refs[n:2 * n], refs[2 * n:])
        for cp in copies:
            cp.start()
        for cp in copies:
            cp.wait()

    return pl.pallas_call(
        body, name=name, out_shape=ex.out_shape, in_specs=ex.specs, out_specs=ex.specs, scratch_shapes=ex.sems,
        compiler_params=pltpu.CompilerParams(has_side_effects=True),
    )(*ex.arrays)


def _call_carrying(ex, body, name, out_shape, grid, in_specs, out_specs, scratch_shapes, semantics, args):
    if ex is None:
        return pl.pallas_call(body, name=name, out_shape=out_shape, grid=grid, in_specs=in_specs, out_specs=out_specs,
                              scratch_shapes=scratch_shapes, compiler_params=_cp(*semantics))(*args)
    n, n_in, n_out, n_scr = ex.n, len(in_specs), len(out_shape), len(scratch_shapes)

    def carrying(*refs):
        ins, refs = refs[:n_in], refs[n_in:]
        c_in, refs = refs[:n], refs[n:]
        outs, refs = refs[:n_out], refs[n_out:]
        c_out, refs = refs[:n], refs[n:]
        scratch, sems = refs[:n_scr], refs[n_scr:]
        ids = [pl.program_id(a) for a in range(len(grid))]
        first, last = ids[0] == 0, ids[0] == grid[0] - 1
        for a in range(1, len(grid)):
            first, last = first & (ids[a] == 0), last & (ids[a] == grid[a] - 1)
        copies = ex.copies(c_in, c_out, sems)

        @pl.when(first)
        def _():
            for cp in copies:
                cp.start()

        body(*ins, *outs, *scratch)

        @pl.when(last)
        def _():
            for cp in copies:
                cp.wait()

    return pl.pallas_call(
        carrying, name=name + "_x", out_shape=list(out_shape) + ex.out_shape, grid=grid,
        in_specs=list(in_specs) + ex.specs, out_specs=list(out_specs) + ex.specs,
        scratch_shapes=list(scratch_shapes) + ex.sems, compiler_params=_cp(*["arbitrary"] * len(grid)),
    )(*args, *ex.arrays)


def _in_pieces():
    runs = []
    for lo, hi, dst in sorted(IN_SEGMENTS, key=lambda s: s[2]):
        col = lo
        while col < hi:
            d = col // IN_SHARD
            end = min(hi, (d + 1) * IN_SHARD)
            runs.append((d, col - d * IN_SHARD, end - d * IN_SHARD))
            col = end
    return runs


def _cast_bf16(name, w):
    shape = w.shape
    w2 = w.reshape(-1, shape[-1])
    r, c = w2.shape
    tr = 512 if r % 512 == 0 else r

    def body(x_ref, o_ref):
        o_ref[...] = x_ref[...].astype(BF16)

    spec = pl.BlockSpec((tr, c), lambda i: (i, 0))
    out = pl.pallas_call(body, name=name, out_shape=jax.ShapeDtypeStruct((r, c), BF16), grid=(r // tr,),
                         in_specs=[spec], out_specs=spec, compiler_params=_cp("parallel"))(w2)
    return out.reshape(shape)


def _repack_w_in(gathered):
    tr = 256
    runs = _in_pieces()

    def body(x_ref, o_ref):
        parts = [x_ref[d, :, a:b].astype(F32) for d, a, b in runs]
        parts.append(jnp.zeros((tr, P_IN - D_IN), F32))
        o_ref[...] = jnp.concatenate(parts, axis=1).astype(BF16)

    return pl.pallas_call(
        body, name="repack_w_in", out_shape=jax.ShapeDtypeStruct((D_MODEL, P_IN), BF16), grid=(D_MODEL // tr,),
        in_specs=[pl.BlockSpec((N_DEV, tr, IN_SHARD), lambda i: (0, i, 0))],
        out_specs=pl.BlockSpec((tr, P_IN), lambda i: (i, 0)),
        compiler_params=_cp("parallel"),
    )(gathered)


def _unpack_w_in(parts):
    tr = 256
    runs = _in_pieces()
    widths = [p.shape[-1] for p in parts]

    def body(a_ref, b_ref, c_ref, d_ref, o_ref):
        full = jnp.concatenate([a_ref[...], b_ref[...], c_ref[...], d_ref[...]], axis=1)
        col = 0
        per_dev = [[] for _ in range(N_DEV)]
        for d, a, b in runs:
            per_dev[d].append((a, full[:, col:col + (b - a)]))
            col += b - a
        for d in range(N_DEV):
            pieces = [v for _, v in sorted(per_dev[d], key=lambda av: av[0])]
            o_ref[d] = (jnp.concatenate(pieces, axis=1) if len(pieces) > 1 else pieces[0]).astype(BF16)

    return pl.pallas_call(
        body, name="unpack_w_in", out_shape=jax.ShapeDtypeStruct((N_DEV, D_MODEL, IN_SHARD), BF16), grid=(D_MODEL // tr,),
        in_specs=[pl.BlockSpec((tr, w), lambda i: (i, 0)) for w in widths],
        out_specs=pl.BlockSpec((N_DEV, tr, IN_SHARD), lambda i: (0, i, 0)),
        compiler_params=_cp("parallel"),
    )(*parts)


def _repack_cols(name, gathered, dtype):
    _, depth, k, n = gathered.shape

    def body(x_ref, o_ref):
        o_ref[0] = jnp.concatenate([x_ref[d, 0] for d in range(N_DEV)], axis=1).astype(dtype)

    return pl.pallas_call(
        body, name=name, out_shape=jax.ShapeDtypeStruct((depth, k, N_DEV * n), dtype), grid=(depth,),
        in_specs=[pl.BlockSpec((N_DEV, 1, k, n), lambda l: (0, l, 0, 0))],
        out_specs=pl.BlockSpec((1, k, N_DEV * n), lambda l: (l, 0, 0)),
        compiler_params=_cp("parallel"),
    )(gathered)


def _unpack_cols(name, full, n, dtype):
    depth, k, _ = full.shape

    def body(x_ref, o_ref):
        for d in range(N_DEV):
            o_ref[d, 0] = x_ref[0, :, d * n:(d + 1) * n].astype(dtype)

    return pl.pallas_call(
        body, name=name, out_shape=jax.ShapeDtypeStruct((N_DEV, depth, k, n), dtype), grid=(depth,),
        in_specs=[pl.BlockSpec((1, k, N_DEV * n), lambda l: (l, 0, 0))],
        out_specs=pl.BlockSpec((N_DEV, 1, k, n), lambda l: (0, l, 0, 0)),
        compiler_params=_cp("parallel"),
    )(full)


def _repack_rows(name, gathered):
    _, depth, r, n = gathered.shape

    def body(x_ref, o_ref):
        for d in range(N_DEV):
            o_ref[0, d * r:(d + 1) * r, :] = x_ref[d, 0].astype(BF16)

    return pl.pallas_call(
        body, name=name, out_shape=jax.ShapeDtypeStruct((depth, N_DEV * r, n), BF16), grid=(depth,),
        in_specs=[pl.BlockSpec((N_DEV, 1, r, n), lambda l: (0, l, 0, 0))],
        out_specs=pl.BlockSpec((1, N_DEV * r, n), lambda l: (l, 0, 0)),
        compiler_params=_cp("parallel"),
    )(gathered)


def _unpack_rows(name, full, r):
    depth, _, n = full.shape

    def body(x_ref, o_ref):
        for d in range(N_DEV):
            o_ref[d, 0] = x_ref[0, d * r:(d + 1) * r, :].astype(BF16)

    return pl.pallas_call(
        body, name=name, out_shape=jax.ShapeDtypeStruct((N_DEV, depth, r, n), BF16), grid=(depth,),
        in_specs=[pl.BlockSpec((1, N_DEV * r, n), lambda l: (l, 0, 0))],
        out_specs=pl.BlockSpec((N_DEV, 1, r, n), lambda l: (0, l, 0, 0)),
        compiler_params=_cp("parallel"),
    )(full)


def _repack_q_heads(gathered):
    r = Q_RANK // N_DEV

    def body(x_ref, o_ref):
        z = jnp.zeros((r, 128 - QK_DIM), F32)
        for d in range(N_DEV):
            x = x_ref[d].astype(F32)
            parts = []
            for h in range(N_HEADS):
                parts += [x[:, QK_DIM * h:QK_DIM * (h + 1)], z]
            o_ref[d * r:(d + 1) * r, :] = jnp.concatenate(parts, axis=1).astype(BF16)

    return pl.pallas_call(body, name="repack_q_heads", out_shape=jax.ShapeDtypeStruct((Q_RANK, 1024), BF16),
                          compiler_params=_cp())(gathered)


def _unpack_q_heads(dwq):
    r = Q_RANK // N_DEV

    def body(x_ref, o_ref):
        for d in range(N_DEV):
            x = x_ref[d * r:(d + 1) * r, :]
            o_ref[d] = jnp.concatenate([x[:, 128 * h:128 * h + QK_DIM] for h in range(N_HEADS)], axis=1).astype(BF16)

    return pl.pallas_call(body, name="unpack_q_heads", out_shape=jax.ShapeDtypeStruct((N_DEV, r, Q_RANK), BF16),
                          compiler_params=_cp())(dwq)


def _repack_kv_heads(gathered):
    def body(x_ref, wk_ref, wv_ref):
        z = jnp.zeros((KV_RANK, 128 - QK_NOPE), F32)
        for h in range(N_HEADS):
            x = x_ref[h].astype(F32)
            wk_ref[:, 128 * h:128 * (h + 1)] = jnp.concatenate([x[:, :QK_NOPE], z], axis=1).astype(BF16)
            wv_ref[:, 128 * h:128 * (h + 1)] = jnp.concatenate([x[:, QK_NOPE:], z], axis=1).astype(BF16)

    sds = jax.ShapeDtypeStruct((KV_RANK, 1024), BF16)
    return pl.pallas_call(body, name="repack_kv_heads", out_shape=(sds, sds), compiler_params=_cp())(gathered)


def _unpack_kv_heads(dwk, dwv):
    def body(k_ref, v_ref, o_ref):
        for h in range(N_HEADS):
            o_ref[h] = jnp.concatenate([k_ref[:, 128 * h:128 * h + QK_NOPE], v_ref[:, 128 * h:128 * h + V_DIM]],
                                       axis=1).astype(BF16)

    return pl.pallas_call(body, name="unpack_kv_heads", out_shape=jax.ShapeDtypeStruct((N_DEV, KV_RANK, 128), BF16),
                          compiler_params=_cp())(dwk, dwv)


def _embed(x, meta_g):
    seq = x.shape[0]
    n_blk = seq // HEAD_ROWS + 1

    def body(x_ref, m_ref, o_ref):
        i = pl.program_id(0)

        @pl.when(i == 0)
        def _():
            meta = jnp.concatenate([m_ref[d] for d in range(N_DEV)], axis=1)
            o_ref[...] = jnp.concatenate([jnp.zeros((PAD_FRONT, D_MODEL), F32), meta], axis=0)

        @pl.when(i > 0)
        def _():
            o_ref[...] = x_ref[...]

    return pl.pallas_call(
        body, name="embed", out_shape=jax.ShapeDtypeStruct((seq + HEAD_ROWS, D_MODEL), F32), grid=(n_blk,),
        in_specs=[pl.BlockSpec((HEAD_ROWS, D_MODEL), lambda i: (jnp.maximum(i - 1, 0), 0)),
                  pl.BlockSpec((N_DEV, N_META, 128), lambda i: (0, 0, 0))],
        out_specs=pl.BlockSpec((HEAD_ROWS, D_MODEL), lambda i: (i, 0)),
        compiler_params=_cp("parallel"),
    )(x, meta_g)


def _rope_table(pos_col, freq_row):
    n_rows = pos_col.shape[0]
    tr = 128

    def body(p_ref, f_ref, o_ref):
        ang = p_ref[...] * f_ref[...]
        lane = lax.broadcasted_iota(jnp.int32, ang.shape, 1)
        cosv, sinv = jnp.cos(ang), jnp.sin(ang)
        half = QK_ROPE // 2
        o_ref[:, 0:128] = jnp.where(lane < QK_NOPE, 1.0, jnp.where(lane < QK_DIM, cosv, 0.0))
        o_ref[:, 128:256] = jnp.where((lane >= QK_NOPE) & (lane < QK_NOPE + half), -sinv, 0.0)
        o_ref[:, 256:384] = jnp.where((lane >= QK_NOPE + half) & (lane < QK_DIM), sinv, 0.0)

    return pl.pallas_call(
        body, name="rope_table", out_shape=jax.ShapeDtypeStruct((n_rows, 384), F32), grid=(n_rows // tr,),
        in_specs=[pl.BlockSpec((tr, 1), lambda i: (i, 0)), pl.BlockSpec((1, 128), lambda i: (0, 0))],
        out_specs=pl.BlockSpec((tr, 384), lambda i: (i, 0)),
        compiler_params=_cp("parallel"),
    )(pos_col, freq_row)


def _loss_head(h, target):
    n_rows = h.shape[0]
    n_blk = n_rows // HEAD_ROWS

    def body(h_ref, t_ref, loss_ref, dh_ref):
        i = pl.program_id(0)

        @pl.when(i == 0)
        def _():
            loss_ref[...] = jnp.zeros(loss_ref.shape, F32)
            dh_ref[...] = jnp.zeros(dh_ref.shape, F32)

        @pl.when(i > 0)
        def _():
            err = h_ref[...] - t_ref[...]
            dh_ref[...] = err * (1.0 / D_MODEL)
            loss_ref[...] += 0.5 * jnp.sum(jnp.mean(err * err, axis=-1, keepdims=True))

    return pl.pallas_call(
        body, name="loss_head",
        out_shape=(jax.ShapeDtypeStruct((8, 128), F32), jax.ShapeDtypeStruct((n_rows, D_MODEL), F32)),
        grid=(n_blk,),
        in_specs=[pl.BlockSpec((HEAD_ROWS, D_MODEL), lambda i: (i, 0)),
                  pl.BlockSpec((HEAD_ROWS, D_MODEL), lambda i: (jnp.maximum(i - 1, 0), 0))],
        out_specs=(pl.BlockSpec((8, 128), lambda i: (0, 0)), pl.BlockSpec((HEAD_ROWS, D_MODEL), lambda i: (i, 0))),
        compiler_params=_cp("arbitrary"),
    )(h, target)


PROJ_CHUNKS = (0, 1024, 2048, 3072, 4096, P_IN)


def _norm_proj(h_res, gain, w_in):
    n_rows = h_res.shape[0]
    tm = _row_tile(n_rows)

    def body(x_ref, g_ref, w_ref, hb_ref, p_ref):
        h, _, _ = _rms(x_ref[...], g_ref[...])
        hb = h.astype(BF16)
        hb_ref[...] = hb
        for c0, c1 in zip(PROJ_CHUNKS[:-1], PROJ_CHUNKS[1:]):
            p_ref[:, c0:c1] = jnp.dot(hb, w_ref[:, c0:c1], preferred_element_type=F32)

    return pl.pallas_call(
        body, name="norm_proj",
        out_shape=(jax.ShapeDtypeStruct((n_rows, D_MODEL), BF16), jax.ShapeDtypeStruct((n_rows, P_IN), F32)),
        grid=(n_rows // tm,),
        in_specs=[pl.BlockSpec((tm, D_MODEL), lambda i: (i, 0)), pl.BlockSpec((1, D_MODEL), lambda i: (0, 0)),
                  pl.BlockSpec((D_MODEL, P_IN), lambda i: (0, 0))],
        out_specs=(pl.BlockSpec((tm, D_MODEL), lambda i: (i, 0)), pl.BlockSpec((tm, P_IN), lambda i: (i, 0))),
        compiler_params=_cp("parallel"),
    )(h_res, gain, w_in)


def _pool_math(u, halo, z, wg_ref, scale, row0):
    tm = u.shape[0]
    ext = jnp.concatenate([halo, u], axis=0)
    t1 = (row0 + lax.broadcasted_iota(jnp.int32, (tm, 1), 0) - (PAD_FRONT - 1)).astype(F32)
    mixed, yg = [], []
    for g, w in enumerate(POOL_WINDOWS):
        a = ext[:, g * 128:(g + 1) * 128]
        k = 1
        while k < w:
            a = a + pltpu.roll(a, k, 0)
            k *= 2
        cnt = jnp.clip(t1, 1.0, float(w))
        mg = a[HALO:, :] / cnt - u[:, g * 128:(g + 1) * 128]
        mixed.append(mg)
        yg.append(_mm(mg, wg_ref[g]))
    mixed = jnp.concatenate(mixed, axis=1)
    yg = jnp.concatenate(yg, axis=1)
    ys = yg * scale
    sig = _sigmoid(z)
    return mixed, yg, ys, sig


def _halo_above(tm):
    return lambda i: (jnp.maximum(i * (tm // HALO) - 1, 0), 0)


def _pool_fwd(proj, wg, scale, w_up):
    n_rows = proj.shape[0]
    tm = _row_tile(n_rows)

    def body(uz_ref, halo_ref, wg_ref, sc_ref, wup_ref, y_ref):
        uz = uz_ref[...]
        u, z = uz[:, :POOL_WIDTH], uz[:, POOL_WIDTH:]
        _, _, ys, sig = _pool_math(u, halo_ref[...], z, wg_ref, sc_ref[...], pl.program_id(0) * tm)
        y_ref[...] = _mm(ys * (z * sig), wup_ref[...])

    return pl.pallas_call(
        body, name="pool_fwd", out_shape=jax.ShapeDtypeStruct((n_rows, D_MODEL), F32), grid=(n_rows // tm,),
        in_specs=[pl.BlockSpec((tm, 1024), lambda i: (i, 0)), pl.BlockSpec((HALO, POOL_WIDTH), _halo_above(tm)),
                  pl.BlockSpec((4, 128, 128), lambda i: (0, 0, 0)), pl.BlockSpec((1, POOL_WIDTH), lambda i: (0, 0)),
                  pl.BlockSpec((POOL_WIDTH, D_MODEL), lambda i: (0, 0))],
        out_specs=pl.BlockSpec((tm, D_MODEL), lambda i: (i, 0)),
        compiler_params=_cp("parallel"),
    )(proj, proj, wg, scale, w_up)


def _mla_pre_fwd(proj, tab, gqa, gkva, wq, wk, wv, gqn, gkn):
    n_rows = proj.shape[0]
    tm = _row_tile(n_rows)

    def body(c_ref, kr_ref, tab_ref, gqa_ref, gkva_ref, wq_ref, wk_ref, wv_ref, gqn_ref, gkn_ref, q_ref, k_ref, v_ref):
        c, tab = c_ref[...], tab_ref[...]
        cqn, _, _ = _rms(c[:, :Q_RANK], gqa_ref[...])
        ckvn, _, _ = _rms(c[:, Q_RANK:], gkva_ref[...])
        qp, kp, vp = _mm(cqn, wq_ref[...]), _mm(ckvn, wk_ref[...]), _mm(ckvn, wv_ref[...])
        kr = pltpu.roll(kr_ref[...], QK_NOPE, 1)
        gqn, gkn = _pad_gain(gqn_ref), _pad_gain(gkn_ref)
        lane = lax.broadcasted_iota(jnp.int32, (tm, 128), 1)
        one = jnp.where(lane == V_DIM, 1.0, 0.0)
        for h in range(N_HEADS):
            blk = slice(128 * h, 128 * (h + 1))
            qn, _, _ = _head_rms(qp[:, blk], gqn)
            q_ref[h] = _rope(qn, tab).astype(BF16)
            kn, _, _ = _head_rms(kp[:, blk] + kr, gkn)
            k_ref[h] = _rope(kn, tab).astype(BF16)
            v_ref[h] = (vp[:, blk] + one).astype(BF16)

    full = lambda *s: pl.BlockSpec(s, lambda i: (0,) * len(s))
    head = pl.BlockSpec((N_HEADS, tm, 128), lambda i: (0, i, 0))
    sds = jax.ShapeDtypeStruct((N_HEADS, n_rows, 128), BF16)
    return pl.pallas_call(
        body, name="mla_pre_fwd", out_shape=(sds, sds, sds), grid=(n_rows // tm,),
        in_specs=[pl.BlockSpec((tm, 1024), lambda i: (i, COL_CQKV // 1024)),
                  pl.BlockSpec((tm, 128), lambda i: (i, COL_KR // 128)),
                  pl.BlockSpec((tm, 384), lambda i: (i, 0)),
                  full(1, Q_RANK), full(1, KV_RANK), full(Q_RANK, 1024), full(KV_RANK, 1024), full(KV_RANK, 1024),
                  full(1, QK_DIM), full(1, QK_DIM)],
        out_specs=(head, head, head),
        compiler_params=_cp("parallel"),
    )(proj, proj, tab, gqa, gkva, wq, wk, wv, gqn, gkn)


def _attn_mask(row0, col0, n_r, n_c):
    row = row0 + lax.broadcasted_iota(jnp.int32, (n_r, n_c), 0)
    col = col0 + lax.broadcasted_iota(jnp.int32, (n_r, n_c), 1)
    return (col <= row) & (col >= PAD_FRONT)


ATTN_SCALE = 1.0 / math.sqrt(QK_DIM)
EXP2_SCALE = ATTN_SCALE * math.log2(math.e)
LOG2_E = math.log2(math.e)


def _row_chunks(t, size):
    return [(r, min(size, t - r)) for r in range(0, t, size)]


def _attn_fwd(q, k, v, ex=None):
    n_rows = q.shape[1]
    t = _attn_tile(n_rows)
    nb = n_rows // t

    def body(q_ref, k_ref, v_ref, o_ref, lse_ref, m_sc, acc_sc):
        qi, ki = pl.program_id(1), pl.program_id(2)

        @pl.when(ki == 0)
        def _():
            m_sc[...] = jnp.full(m_sc.shape, -jnp.inf, F32)
            acc_sc[...] = jnp.zeros(acc_sc.shape, F32)

        def update(masked):
            heads = range(hp)
            s = [_mm_nt(q_ref[h], k_ref[h]) for h in heads]
            if masked:
                mask = _attn_mask(qi * t, ki * t, t, t)
                s = [jnp.where(mask, sh, MASK_VALUE) for sh in s]
            p = []
            for h in heads:
                m_prev = m_sc[h]
                m_new = jnp.maximum(m_prev, jnp.max(s[h], axis=1, keepdims=True))
                alpha = jnp.exp2((m_prev - m_new) * EXP2_SCALE)
                p.append(jnp.exp2((s[h] - m_new) * EXP2_SCALE).astype(BF16))
                acc_sc[h] = alpha * acc_sc[h]
                m_sc[h] = m_new
            for h in heads:
                acc_sc[h] += _mm(p[h], v_ref[h])

        edge = (ki == qi) | (ki == 0)

        @pl.when((ki <= qi) & edge)
        def _():
            update(True)

        @pl.when((ki < qi) & (ki > 0))
        def _():
            update(False)

        @pl.when(ki == nb - 1)
        def _():
            acc = acc_sc[...]
            denom = acc[:, :, V_DIM:V_DIM + 1]
            o_ref[...] = acc[:, :, :V_DIM] / denom
            lse = m_sc[...] * ATTN_SCALE + jnp.log(denom)
            lane = lax.broadcasted_iota(jnp.int32, (t, 128), 1)
            wide = jnp.zeros((t, 128), F32)
            for h in range(hp):
                wide = jnp.where(lane == h, lse[h], wide)
            rows = wide.T
            for h in range(hp):
                lse_ref[h] = rows[h:h + 1, :]

    hp = HEADS_PER_STEP_FWD
    kv_idx = lambda h, qi, ki: (h, jnp.minimum(ki, qi), 0)
    return _call_carrying(
        ex, body, "attn_fwd",
        out_shape=(jax.ShapeDtypeStruct((N_HEADS, n_rows, V_DIM), F32), jax.ShapeDtypeStruct((N_HEADS, 1, n_rows), F32)),
        grid=(N_HEADS // hp, nb, nb),
        in_specs=[pl.BlockSpec((hp, t, 128), lambda h, qi, ki: (h, qi, 0)), pl.BlockSpec((hp, t, 128), kv_idx),
                  pl.BlockSpec((hp, t, 128), kv_idx)],
        out_specs=(pl.BlockSpec((hp, t, V_DIM), lambda h, qi, ki: (h, qi, 0)), pl.BlockSpec((hp, 1, t), lambda h, qi, ki: (h, 0, qi))),
        scratch_shapes=[pltpu.VMEM((hp, t, 1), F32), pltpu.VMEM((hp, t, 128), F32)],
        semantics=("parallel", "parallel", "arbitrary"), args=(q, k, v))


def _merge_math(o_ref, z, gates, y_pool, y_mla_fn):
    o_cat = jnp.concatenate([o_ref[h] for h in range(N_HEADS)], axis=1)
    sig_z = _sigmoid(z)
    a_mla = o_cat * (z * sig_z)
    y_mla = y_mla_fn(a_mla)
    sgp, sgm = _sigmoid(gates[:, :D_MODEL]), _sigmoid(gates[:, D_MODEL:])
    merged = sgp * y_pool + sgm * y_mla
    return o_cat, sig_z, a_mla, y_mla, sgp, sgm, merged


def _mla_post_fwd(h_res, proj, o, y_pool, w_mla_up, w_out):
    n_rows = h_res.shape[0]
    tm = _row_tile(n_rows)

    def body(h_ref, g_ref, z_ref, o_ref, yp_ref, wup_ref, wout_ref, ymla_ref, hn_ref):
        _, _, _, y_mla, _, _, merged = _merge_math(o_ref, z_ref[...], g_ref[...], yp_ref[...],
                                                   lambda a: _mm(a, wup_ref[...]))
        ymla_ref[...] = y_mla
        hn_ref[...] = h_ref[...] + _mm(merged, wout_ref[...])

    row = lambda w, c: pl.BlockSpec((tm, w), lambda i: (i, c))
    return pl.pallas_call(
        body, name="mla_post_fwd",
        out_shape=(jax.ShapeDtypeStruct((n_rows, D_MODEL), F32), jax.ShapeDtypeStruct((n_rows, D_MODEL), F32)),
        grid=(n_rows // tm,),
        in_specs=[row(D_MODEL, 0), row(2048, COL_GATES // 2048), row(MLA_WIDTH, COL_ZMLA // MLA_WIDTH),
                  pl.BlockSpec((N_HEADS, tm, V_DIM), lambda i: (0, i, 0)), row(D_MODEL, 0),
                  pl.BlockSpec((MLA_WIDTH, D_MODEL), lambda i: (0, 0)), pl.BlockSpec((D_MODEL, D_MODEL), lambda i: (0, 0))],
        out_specs=(row(D_MODEL, 0), row(D_MODEL, 0)),
        compiler_params=_cp("parallel"),
    )(h_res, proj, proj, o, y_pool, w_mla_up, w_out)


def _dsilu(z, sig):
    return sig * (1.0 + z * (1.0 - sig))


def _acc(ref, val, first):
    @pl.when(first)
    def _():
        ref[...] = val

    @pl.when(jnp.logical_not(first))
    def _():
        ref[...] += val


def _mla_post_bwd(dh, proj, o, y_pool, y_mla, w_mla_up, w_out):
    n_rows = dh.shape[0]
    tm = _row_tile(n_rows)

    def body(dh_ref, g_ref, z_ref, o_ref, yp_ref, ym_ref, wup_ref, wout_ref,
             dgz_ref, dyp_ref, do_ref, dwout_ref, dwup_ref):
        first = pl.program_id(0) == 0
        z, y_pool, y_mla = z_ref[...], yp_ref[...], ym_ref[...]
        dhv = dh_ref[...]
        dmerged = _mm_nt(dhv, wout_ref[...])
        o_cat, sig_z, a_mla, _, sgp, sgm, merged = _merge_math(o_ref, z, g_ref[...], y_pool, lambda a: y_mla)
        dy_mla = dmerged * sgm
        da = _mm_nt(dy_mla, wup_ref[...])
        dyp_ref[...] = dmerged * sgp
        dgz_ref[:, 0:D_MODEL] = dmerged * y_pool * (sgp * (1.0 - sgp))
        dgz_ref[:, D_MODEL:2 * D_MODEL] = dmerged * y_mla * (sgm * (1.0 - sgm))
        _acc(dwout_ref, _mm_tn(merged, dhv), first)
        do_cat = da * (z * sig_z)
        dgz_ref[:, 2 * D_MODEL:] = da * o_cat * _dsilu(z, sig_z)
        for h in range(N_HEADS):
            do_ref[h] = do_cat[:, V_DIM * h:V_DIM * (h + 1)]
        _acc(dwup_ref, _mm_tn(a_mla, dy_mla), first)

    row = lambda w, c: pl.BlockSpec((tm, w), lambda i: (i, c))
    head = lambda w: pl.BlockSpec((N_HEADS, tm, w), lambda i: (0, i, 0))
    const = lambda r, c: pl.BlockSpec((r, c), lambda i: (0, 0))
    return pl.pallas_call(
        body, name="mla_post_bwd",
        out_shape=(jax.ShapeDtypeStruct((n_rows, 2560), F32), jax.ShapeDtypeStruct((n_rows, D_MODEL), F32),
                   jax.ShapeDtypeStruct((N_HEADS, n_rows, V_DIM), F32),
                   jax.ShapeDtypeStruct((D_MODEL, D_MODEL), F32), jax.ShapeDtypeStruct((MLA_WIDTH, D_MODEL), F32)),
        grid=(n_rows // tm,),
        in_specs=[row(D_MODEL, 0), row(2048, COL_GATES // 2048), row(MLA_WIDTH, COL_ZMLA // MLA_WIDTH), head(V_DIM),
                  row(D_MODEL, 0), row(D_MODEL, 0), const(MLA_WIDTH, D_MODEL), const(D_MODEL, D_MODEL)],
        out_specs=(row(2560, 0), row(D_MODEL, 0), head(V_DIM), const(D_MODEL, D_MODEL), const(MLA_WIDTH, D_MODEL)),
        compiler_params=_cp("arbitrary"),
    )(dh, proj, proj, o, y_pool, y_mla, w_mla_up, w_out)


def _attn_bwd(q, k, v, do, lse, o, ex=None):
    n_rows = q.shape[1]
    t = _attn_tile(n_rows)
    nb = n_rows // t

    def body(q_ref, k_ref, v_ref, do_ref, lse_ref, o_ref, dq_ref, dk_ref, dv_ref, dk_sc, dv_sc, delta_sc):
        ki, qi = pl.program_id(1), pl.program_id(2)

        @pl.when((ki == 0) & (qi == 0))
        def _():
            dq_ref[...] = jnp.zeros(dq_ref.shape, F32)

        @pl.when(ki == 0)
        def _():
            pad = jnp.zeros((t, 128 - V_DIM), F32)
            for h in range(HEADS_PER_STEP):
                wide = jnp.concatenate([do_ref[h] * o_ref[h], pad], axis=1)
                delta_sc[h, qi] = jnp.sum(wide.T, axis=0, keepdims=True)

        @pl.when(qi == 0)
        def _():
            dk_sc[...] = jnp.zeros(dk_sc.shape, F32)
            dv_sc[...] = jnp.zeros(dv_sc.shape, F32)

        def step(masked):
            heads = range(HEADS_PER_STEP)
            st = [_mm_nt(k_ref[h], q_ref[h]) for h in heads]
            dpt = [_mm_nt(v_ref[h, :, 0:V_DIM], do_ref[h]) for h in heads]
            if masked:
                key = ki * t + lax.broadcasted_iota(jnp.int32, (t, t), 0)
                qry = qi * t + lax.broadcasted_iota(jnp.int32, (t, t), 1)
                mask = (key <= qry) & (key >= PAD_FRONT)
                st = [jnp.where(mask, sh, MASK_VALUE) for sh in st]
            pt = [jnp.exp2(st[h] * EXP2_SCALE - lse_ref[h] * LOG2_E) for h in heads]
            dst = [pt[h] * (dpt[h] - delta_sc[h, qi]) for h in heads]
            rows = pl.ds(pl.multiple_of(qi * t, t), t)
            for h in heads:
                dv_sc[h] += _mm(pt[h], do_ref[h])
                dk_sc[h] += _mm(dst[h], q_ref[h])
                dq_ref[h, rows, :] += _mm_tn(dst[h], k_ref[h]) * ATTN_SCALE

        edge = (ki == qi) | (ki == 0)

        @pl.when((qi >= ki) & edge)
        def _():
            step(True)

        @pl.when((qi > ki) & (ki > 0))
        def _():
            step(False)

        @pl.when(qi == nb - 1)
        def _():
            dk_ref[...] = dk_sc[...] * ATTN_SCALE
            dv_ref[...] = dv_sc[...]

    hp = HEADS_PER_STEP
    q_idx = lambda h, ki, qi: (h, jnp.maximum(qi, ki), 0)
    row_idx = lambda h, ki, qi: (h, 0, jnp.maximum(qi, ki))
    o_idx = lambda h, ki, qi: (h, jnp.where(ki == 0, qi, 0), 0)
    k_idx = lambda h, ki, qi: (h, ki, 0)
    return _call_carrying(
        ex, body, "attn_bwd",
        out_shape=(jax.ShapeDtypeStruct((N_HEADS, n_rows, 128), F32), jax.ShapeDtypeStruct((N_HEADS, n_rows, 128), F32),
                   jax.ShapeDtypeStruct((N_HEADS, n_rows, V_DIM), F32)),
        grid=(N_HEADS // hp, nb, nb),
        in_specs=[pl.BlockSpec((hp, t, 128), q_idx), pl.BlockSpec((hp, t, 128), k_idx), pl.BlockSpec((hp, t, 128), k_idx),
                  pl.BlockSpec((hp, t, V_DIM), q_idx), pl.BlockSpec((hp, 1, t), row_idx), pl.BlockSpec((hp, t, V_DIM), o_idx)],
        out_specs=(pl.BlockSpec((hp, n_rows, 128), lambda h, ki, qi: (h, 0, 0), pipeline_mode=pl.Buffered(1)),
                   pl.BlockSpec((hp, t, 128), k_idx),
                   pl.BlockSpec((hp, t, V_DIM), k_idx)),
        scratch_shapes=[pltpu.VMEM((hp, t, 128), F32), pltpu.VMEM((hp, t, V_DIM), F32), pltpu.VMEM((hp, nb, 1, t), F32)],
        semantics=("parallel", "arbitrary", "arbitrary"), args=(q, k, v, do, lse, o))


def _mla_pre_bwd(proj, tab, gqa, gkva, wq, wk, wv, gqn, gkn, dq, dk, dv):
    n_rows = proj.shape[0]
    tm = _row_tile(n_rows)

    def body(c_ref, kr_ref, tab_ref, gqa_ref, gkva_ref, wq_ref, wk_ref, wv_ref, gqn_ref, gkn_ref, dq_ref, dk_ref, dv_ref,
             dc_ref, dkr_ref, dwq_ref, dwk_ref, dwv_ref, dgqa_ref, dgkva_ref, dgqn_ref, dgkn_ref):
        first = pl.program_id(0) == 0
        c, tab = c_ref[...], tab_ref[...]
        gqa, gkva = gqa_ref[...], gkva_ref[...]
        cqn, cq_h, cq_inv = _rms(c[:, :Q_RANK], gqa)
        ckvn, ckv_h, ckv_inv = _rms(c[:, Q_RANK:], gkva)
        qp, kp = _mm(cqn, wq_ref[...]), _mm(ckvn, wk_ref[...])
        kr = pltpu.roll(kr_ref[...], QK_NOPE, 1)
        gqn, gkn = _pad_gain(gqn_ref), _pad_gain(gkn_ref)
        dq_parts, dk_parts, dv_parts = [], [], []
        dkr = jnp.zeros((tm, 128), F32)
        dgqn = jnp.zeros((1, 128), F32)
        dgkn = jnp.zeros((1, 128), F32)
        zv = jnp.zeros((tm, 128 - V_DIM), F32)
        for h in range(N_HEADS):
            blk = slice(128 * h, 128 * (h + 1))
            _, xh, inv = _head_rms(qp[:, blk], gqn)
            dx, dg = _head_rms_bwd(_rope_bwd(dq_ref[h], tab), xh, inv, gqn)
            dq_parts.append(dx)
            dgqn = dgqn + dg
            _, xh, inv = _head_rms(kp[:, blk] + kr, gkn)
            dx, dg = _head_rms_bwd(_rope_bwd(dk_ref[h], tab), xh, inv, gkn)
            dk_parts.append(dx)
            dgkn = dgkn + dg
            dkr = dkr + dx
            dv_parts.append(jnp.concatenate([dv_ref[h], zv], axis=1))
        dqp = jnp.concatenate(dq_parts, axis=1)
        dkp = jnp.concatenate(dk_parts, axis=1)
        dvp = jnp.concatenate(dv_parts, axis=1)
        _acc(dwq_ref, _mm_tn(cqn, dqp), first)
        _acc(dwk_ref, _mm_tn(ckvn, dkp), first)
        _acc(dwv_ref, _mm_tn(ckvn, dvp), first)
        dcq, dg1 = _rms_bwd(_mm_nt(dqp, wq_ref[...]), cq_h, cq_inv, gqa)
        dckv, dg2 = _rms_bwd(_mm_nt(dkp, wk_ref[...]) + _mm_nt(dvp, wv_ref[...]), ckv_h, ckv_inv, gkva)
        _acc(dgqa_ref, dg1, first)
        _acc(dgkva_ref, dg2, first)
        _acc(dgqn_ref, dgqn, first)
        _acc(dgkn_ref, dgkn, first)
        dc_ref[...] = jnp.concatenate([dcq, dckv], axis=1)
        lane = lax.broadcasted_iota(jnp.int32, (tm, 128), 1)
        dkr_ref[...] = jnp.where(lane < QK_ROPE, pltpu.roll(dkr, 128 - QK_NOPE, 1), 0.0)

    full = lambda *s: pl.BlockSpec(s, lambda i: (0,) * len(s))
    head = lambda w: pl.BlockSpec((N_HEADS, tm, w), lambda i: (0, i, 0))
    return pl.pallas_call(
        body, name="mla_pre_bwd",
        out_shape=(jax.ShapeDtypeStruct((n_rows, 1024), F32), jax.ShapeDtypeStruct((n_rows, 128), F32),
                   jax.ShapeDtypeStruct((Q_RANK, 1024), F32), jax.ShapeDtypeStruct((KV_RANK, 1024), F32),
                   jax.ShapeDtypeStruct((KV_RANK, 1024), F32),
                   jax.ShapeDtypeStruct((1, Q_RANK), F32), jax.ShapeDtypeStruct((1, KV_RANK), F32),
                   jax.ShapeDtypeStruct((1, 128), F32), jax.ShapeDtypeStruct((1, 128), F32)),
        grid=(n_rows // tm,),
        in_specs=[pl.BlockSpec((tm, 1024), lambda i: (i, COL_CQKV // 1024)),
                  pl.BlockSpec((tm, 128), lambda i: (i, COL_KR // 128)),
                  pl.BlockSpec((tm, 384), lambda i: (i, 0)),
                  full(1, Q_RANK), full(1, KV_RANK), full(Q_RANK, 1024), full(KV_RANK, 1024), full(KV_RANK, 1024),
                  full(1, QK_DIM), full(1, QK_DIM), head(128), head(128), head(V_DIM)],
        out_specs=(pl.BlockSpec((tm, 1024), lambda i: (i, 0)), pl.BlockSpec((tm, 128), lambda i: (i, 0)),
                   full(Q_RANK, 1024), full(KV_RANK, 1024), full(KV_RANK, 1024), full(1, Q_RANK), full(1, KV_RANK),
                   full(1, 128), full(1, 128)),
        compiler_params=_cp("arbitrary"),
    )(proj, proj, tab, gqa, gkva, wq, wk, wv, gqn, gkn, dq, dk, dv)


def _pool_bwd_a(proj, dy_pool, wg, scale, w_up):
    n_rows = proj.shape[0]
    tm = _row_tile(n_rows)

    def body(uz_ref, halo_ref, dy_ref, wg_ref, sc_ref, wup_ref, dmz_ref, dwg_ref, dsc_ref, dwup_ref):
        first = pl.program_id(0) == 0
        uz = uz_ref[...]
        u, z = uz[:, :POOL_WIDTH], uz[:, POOL_WIDTH:]
        scale_v = sc_ref[...]
        mixed, yg, ys, sig = _pool_math(u, halo_ref[...], z, wg_ref, scale_v, pl.program_id(0) * tm)
        sp = z * sig
        dy = dy_ref[...]
        da = _mm_nt(dy, wup_ref[...])
        _acc(dwup_ref, _mm_tn(ys * sp, dy), first)
        dys = da * sp
        _acc(dsc_ref, jnp.sum(dys * yg, axis=0, keepdims=True), first)
        dyg = dys * scale_v
        for g in range(4):
            cols = slice(g * 128, (g + 1) * 128)
            dmz_ref[:, cols] = _mm_nt(dyg[:, cols], wg_ref[g])
            _acc(dwg_ref.at[g], _mm_tn(mixed[:, cols], dyg[:, cols]), first)
        dmz_ref[:, POOL_WIDTH:] = da * ys * _dsilu(z, sig)

    return pl.pallas_call(
        body, name="pool_bwd_a",
        out_shape=(jax.ShapeDtypeStruct((n_rows, 1024), F32), jax.ShapeDtypeStruct((4, 128, 128), F32),
                   jax.ShapeDtypeStruct((1, POOL_WIDTH), F32), jax.ShapeDtypeStruct((POOL_WIDTH, D_MODEL), F32)),
        grid=(n_rows // tm,),
        in_specs=[pl.BlockSpec((tm, 1024), lambda i: (i, 0)), pl.BlockSpec((HALO, POOL_WIDTH), _halo_above(tm)),
                  pl.BlockSpec((tm, D_MODEL), lambda i: (i, 0)),
                  pl.BlockSpec((4, 128, 128), lambda i: (0, 0, 0)), pl.BlockSpec((1, POOL_WIDTH), lambda i: (0, 0)),
                  pl.BlockSpec((POOL_WIDTH, D_MODEL), lambda i: (0, 0))],
        out_specs=(pl.BlockSpec((tm, 1024), lambda i: (i, 0)), pl.BlockSpec((4, 128, 128), lambda i: (0, 0, 0)),
                   pl.BlockSpec((1, POOL_WIDTH), lambda i: (0, 0)), pl.BlockSpec((POOL_WIDTH, D_MODEL), lambda i: (0, 0))),
        compiler_params=_cp("arbitrary"),
    )(proj, proj, dy_pool, wg, scale, w_up)


def _pool_bwd_b(dmz):
    n_rows = dmz.shape[0]
    tm = _row_tile(n_rows)
    n_tiles = n_rows // tm
    n_ext = tm + HALO

    def body(dmz_ref, halo_ref, o_ref):
        i = pl.program_id(0)
        v = dmz_ref[...]
        dm = v[:, :POOL_WIDTH]
        halo = jnp.where(i == n_tiles - 1, 0.0, halo_ref[...])
        ext = jnp.concatenate([dm, halo], axis=0)
        t1 = (i * tm + lax.broadcasted_iota(jnp.int32, (n_ext, 1), 0) - (PAD_FRONT - 1)).astype(F32)
        du = []
        for g, w in enumerate(POOL_WINDOWS):
            cols = slice(g * 128, (g + 1) * 128)
            a = ext[:, cols] / jnp.clip(t1, 1.0, float(w))
            k = 1
            while k < w:
                a = a + pltpu.roll(a, n_ext - k, 0)
                k *= 2
            du.append(a[:tm, :] - dm[:, cols])
        o_ref[...] = jnp.concatenate(du + [v[:, POOL_WIDTH:]], axis=1)

    last_halo = n_rows // HALO - 1
    return pl.pallas_call(
        body, name="pool_bwd_b", out_shape=jax.ShapeDtypeStruct((n_rows, 1024), F32), grid=(n_tiles,),
        in_specs=[pl.BlockSpec((tm, 1024), lambda i: (i, 0)),
                  pl.BlockSpec((HALO, POOL_WIDTH), lambda i: (jnp.minimum((i + 1) * (tm // HALO), last_halo), 0))],
        out_specs=pl.BlockSpec((tm, 1024), lambda i: (i, 0)),
        compiler_params=_cp("parallel"),
    )(dmz, dmz)


def _norm_proj_bwd(d_uz, d_cqkv, d_gz, d_kr, w_in, h_res, gain, dh_out):
    n_rows = h_res.shape[0]
    tm = _row_tile(n_rows)
    pieces = ((COL_UZ, 1024), (COL_CQKV, 1024), (COL_GATES, 2560), (COL_KR, 128))

    def body(a_ref, b_ref, c_ref, d_ref, w_ref, x_ref, g_ref, dho_ref, dhi_ref, dg_ref):
        i = pl.program_id(0)
        dh = None
        for ref, (c0, wd) in zip((a_ref, b_ref, c_ref, d_ref), pieces):
            part = _mm_nt(ref[...], w_ref[:, c0:c0 + wd])
            dh = part if dh is None else dh + part
        g = g_ref[...]
        _, xh, inv = _rms(x_ref[...], g)
        dx, dg = _rms_bwd(dh, xh, inv, g)
        _acc(dg_ref, dg, i == 0)
        row = i * tm + lax.broadcasted_iota(jnp.int32, (tm, 1), 0)
        dhi_ref[...] = jnp.where(row >= PAD_FRONT, dho_ref[...] + dx, 0.0)

    row = lambda w: pl.BlockSpec((tm, w), lambda i: (i, 0))
    return pl.pallas_call(
        body, name="norm_proj_bwd",
        out_shape=(jax.ShapeDtypeStruct((n_rows, D_MODEL), F32), jax.ShapeDtypeStruct((1, D_MODEL), F32)),
        grid=(n_rows // tm,),
        in_specs=[row(1024), row(1024), row(2560), row(128), pl.BlockSpec((D_MODEL, P_IN), lambda i: (0, 0)),
                  row(D_MODEL), pl.BlockSpec((1, D_MODEL), lambda i: (0, 0)), row(D_MODEL)],
        out_specs=(row(D_MODEL), pl.BlockSpec((1, D_MODEL), lambda i: (0, 0))),
        compiler_params=_cp("arbitrary"),
    )(d_uz, d_cqkv, d_gz, d_kr, w_in, h_res, gain, dh_out)


def _weight_grad(name, a, b):
    k_rows, m = a.shape
    n = b.shape[1]
    tk = next(t for t in (1664, 640, 128) if k_rows % t == 0)
    tn = next(t for t in (1280, 1024, 512, 128) if n % t == 0)

    def body(a_ref, b_ref, o_ref):
        _acc(o_ref, _mm_tn(a_ref[...], b_ref[...]), pl.program_id(1) == 0)

    return pl.pallas_call(
        body, name=name, out_shape=jax.ShapeDtypeStruct((m, n), F32), grid=(n // tn, k_rows // tk),
        in_specs=[pl.BlockSpec((tk, m), lambda j, k: (k, 0)), pl.BlockSpec((tk, tn), lambda j, k: (k, j))],
        out_specs=pl.BlockSpec((m, tn), lambda j, k: (0, j)),
        compiler_params=_cp("parallel", "arbitrary"),
    )(a, b)


def _adamw(name, w, m, v, gbufs):
    r, c = w.shape
    depth = len(gbufs)
    r_l = r // depth
    tr = r_l
    if r_l * c * 4 > (1 << 20):
        tr = 256 if r_l % 256 == 0 else 128
    assert r_l * depth == r and r_l % tr == 0, (name, r, depth, tr)
    n_t = r_l // tr

    def body(w_ref, m_ref, v_ref, *refs):
        g_refs, (go_ref, d_ref, mo_ref, vo_ref) = refs[:depth], refs[depth:]

        def update(g_ref):
            g = g_ref[0].astype(F32)
            for s in range(1, N_DEV):
                g = g + g_ref[s].astype(F32)
            go_ref[...] = g
            m_new = ADAM_B1 * m_ref[...] + (1.0 - ADAM_B1) * g
            v_new = ADAM_B2 * v_ref[...] + (1.0 - ADAM_B2) * (g * g)
            mo_ref[...] = m_new
            vo_ref[...] = v_new
            m_hat = m_new / (1.0 - ADAM_B1 ** ADAM_STEP)
            v_hat = v_new / (1.0 - ADAM_B2 ** ADAM_STEP)
            d_ref[...] = -ADAM_LR * (m_hat / (jnp.sqrt(v_hat) + ADAM_EPS) + ADAM_WD * w_ref[...])

        for j in range(depth):
            pl.when(pl.program_id(0) == j)(lambda j=j: update(g_refs[j]))

    def g_spec(j):
        return pl.BlockSpec((N_DEV, tr, c), lambda l, i: (0, jnp.where(l == j, i, jnp.where(l < j, 0, n_t - 1)), 0))

    spec = pl.BlockSpec((tr, c), lambda l, i: (l * n_t + i, 0))
    sds = jax.ShapeDtypeStruct((r, c), F32)
    return pl.pallas_call(
        body, name=name, out_shape=(sds, sds, sds, sds), grid=(depth, n_t),
        in_specs=[spec, spec, spec] + [g_spec(j) for j in range(depth)],
        out_specs=(spec, spec, spec, spec),
        compiler_params=_cp("arbitrary", "arbitrary"),
    )(w, m, v, *gbufs)


SMALL = ("norm_gain", "pool_w_group", "pool_scale", "q_a_norm_gain", "kv_a_norm_gain", "q_norm_gain", "k_norm_gain")


def _pack_small(parts):
    rows = []
    for p in parts:
        flat = p.reshape(-1)
        pad = (-flat.shape[0]) % 1024
        if pad:
            flat = jnp.concatenate([flat, jnp.zeros((pad,), F32)])
        rows.append(flat.reshape(-1, 128))
    n_rows = sum(r.shape[0] for r in rows)
    tail = (-n_rows) % 256
    if tail:
        rows.append(jnp.zeros((tail, 128), F32))
    return jnp.concatenate(rows, axis=0)


def _unpack_small(packed, shapes):
    out, r0 = [], 0
    for shp in shapes:
        n = math.prod(shp)
        n_rows = (n + 1023) // 1024 * 8
        out.append(packed[r0:r0 + n_rows].reshape(-1)[:n].reshape(shp))
        r0 += n_rows
    return out


def kernel(x, positions, meta_tokens, norm_gain, w_in, pool_w_group, pool_scale, pool_w_up, q_a_norm_gain, kv_a_norm_gain, w_q_b, w_kv_b, q_norm_gain, k_norm_gain, mla_w_up, w_out, loss_target, m_meta_tokens, m_norm_gain, m_w_in, m_pool_w_group, m_pool_scale, m_pool_w_up, m_q_a_norm_gain, m_kv_a_norm_gain, m_w_q_b, m_w_kv_b, m_q_norm_gain, m_k_norm_gain, m_mla_w_up, m_w_out, v_meta_tokens, v_norm_gain, v_w_in, v_pool_w_group, v_pool_scale, v_pool_w_up, v_q_a_norm_gain, v_kv_a_norm_gain, v_w_q_b, v_w_kv_b, v_q_norm_gain, v_k_norm_gain, v_mla_w_up, v_w_out):
    x2, target = x[0], loss_target[0]

    big = dict(w_in=w_in, pool_w_up=pool_w_up, w_q_b=w_q_b, w_kv_b=w_kv_b, mla_w_up=mla_w_up, w_out=w_out)
    names_big = list(big)
    shards = {n: _cast_bf16("cast_" + n, w) for n, w in big.items()}

    def layer_shards(l):
        return [shards[n][l] for n in names_big]

    def repack(gathered):
        g_in, g_pup, g_qb, g_kvb, g_mup, g_out = gathered
        return dict(w_in=_repack_w_in(g_in),
                    pool_up=_repack_cols("repack_pool_up", g_pup[:, None], BF16)[0],
                    q_heads=_repack_q_heads(g_qb),
                    kv_heads=_repack_kv_heads(g_kvb),
                    mla_up=_repack_cols("repack_mla_up", g_mup[:, None], BF16)[0],
                    out=_repack_rows("repack_out", g_out[:, None])[0])

    first = _exchange("gather_first", [meta_tokens] + layer_shards(0), [])
    weights = [repack(first[1:])]

    h = _embed(x2, first[0])
    pos = jnp.concatenate([jnp.zeros((PAD_FRONT,), jnp.int32), jnp.arange(N_META, dtype=jnp.int32),
                           positions[0] + N_META]).astype(F32).reshape(-1, 1)
    half = QK_ROPE // 2
    inv_freq = ROPE_THETA ** (-jnp.arange(half, dtype=F32) / half)
    tab = _rope_table(pos, jnp.tile(inv_freq, 128 // half).reshape(1, 128))

    row = lambda a, l: a[l].reshape(1, -1)
    saved = []
    for l in range(DEPTH):
        w = weights[l]
        hb, proj = _norm_proj(h, row(norm_gain, l), w["w_in"])
        y_pool = _pool_fwd(proj, pool_w_group[l], row(pool_scale, l), w["pool_up"])
        q, k, v = _mla_pre_fwd(proj, tab, row(q_a_norm_gain, l), row(kv_a_norm_gain, l), w["q_heads"], *w["kv_heads"],
                               row(q_norm_gain, l), row(k_norm_gain, l))
        ex = _Exchange(layer_shards(l + 1), []) if l + 1 < DEPTH else None
        res = _attn_fwd(q, k, v, ex)
        o, lse = res[0], res[1]
        if ex is not None:
            weights.append(repack(res[2:]))
        y_mla, h_next = _mla_post_fwd(h, proj, o, y_pool, w["mla_up"], w["out"])
        saved.append((h, hb, proj, y_pool, q, k, v, o, lse, y_mla))
        h = h_next

    loss_part, dh = _loss_head(h, target)
    loss = lax.psum(loss_part[0, 0], MESH_AXES)

    grads = {n: [None] * DEPTH for n in SMALL}
    pending = None
    received = [None] * DEPTH
    for l in reversed(range(DEPTH)):
        w = weights[l]
        h_l, hb, proj, y_pool, q, k, v, o, lse, y_mla = saved[l]
        d_gz, dy_pool, do, g_out, g_mla_up = _mla_post_bwd(dh, proj, o, y_pool, y_mla, w["mla_up"], w["out"])
        ex = _Exchange([], pending) if pending is not None else None
        res = _attn_bwd(q, k, v, do, lse, o, ex)
        dq, dk, dv = res[0], res[1], res[2]
        if ex is not None:
            received[l + 1] = res[3:]
        (d_cqkv, d_kr, g_q, g_k, g_v, grads["q_a_norm_gain"][l], grads["kv_a_norm_gain"][l], g_qn, g_kn) = _mla_pre_bwd(
            proj, tab, row(q_a_norm_gain, l), row(kv_a_norm_gain, l), w["q_heads"], *w["kv_heads"], row(q_norm_gain, l),
            row(k_norm_gain, l), dq, dk, dv)
        grads["q_norm_gain"][l], grads["k_norm_gain"][l] = g_qn[:, :QK_DIM], g_kn[:, :QK_DIM]
        dmz, grads["pool_w_group"][l], grads["pool_scale"][l], g_pool_up = _pool_bwd_a(
            proj, dy_pool, pool_w_group[l], row(pool_scale, l), w["pool_up"])
        d_uz = _pool_bwd_b(dmz)
        g_in = [_weight_grad("dw_in", hb, d) for d in (d_uz, d_cqkv, d_gz, d_kr)]
        dh, grads["norm_gain"][l] = _norm_proj_bwd(d_uz, d_cqkv, d_gz, d_kr, w["w_in"], h_l, row(norm_gain, l), dh)
        pending = [_unpack_w_in(g_in),
                   _unpack_cols("unpack_pool_up", g_pool_up[None], 128, BF16)[:, 0],
                   _unpack_q_heads(g_q),
                   _unpack_kv_heads(g_k, g_v),
                   _unpack_cols("unpack_mla_up", g_mla_up[None], 128, BF16)[:, 0],
                   _unpack_rows("unpack_out", g_out[None], D_MODEL // N_DEV)[:, 0]]

    grad_x = dh[HEAD_ROWS:][None]
    d_meta = dh[PAD_FRONT:HEAD_ROWS]

    small_shapes = [a.shape for a in (norm_gain, pool_w_group, pool_scale, q_a_norm_gain, kv_a_norm_gain, q_norm_gain, k_norm_gain)]
    small_grad = _pack_small([jnp.stack(grads[n], axis=0).reshape(s) for n, s in zip(SMALL, small_shapes)])
    meta_parts = _unpack_cols("unpack_meta", d_meta[None], 128, F32)[:, 0]
    last = _exchange("exchange_last", [small_grad], [meta_parts] + pending)
    small_buf = last[0]
    received[0] = last[2:]
    sharded = ["meta_tokens"] + names_big
    shard_bufs = {"meta_tokens": [last[1]]}
    for j, n in enumerate(names_big):
        shard_bufs[n] = [received[l][j] for l in range(DEPTH)]

    given = dict(meta_tokens=(meta_tokens, m_meta_tokens, v_meta_tokens), w_in=(w_in, m_w_in, v_w_in),
                 pool_w_up=(pool_w_up, m_pool_w_up, v_pool_w_up), w_q_b=(w_q_b, m_w_q_b, v_w_q_b),
                 w_kv_b=(w_kv_b, m_w_kv_b, v_w_kv_b), mla_w_up=(mla_w_up, m_mla_w_up, v_mla_w_up),
                 w_out=(w_out, m_w_out, v_w_out))
    result = {}
    for n in sharded:
        w, m, v = given[n]
        cols = w.shape[-1]
        res = _adamw("adamw_" + n, w.reshape(-1, cols), m.reshape(-1, cols), v.reshape(-1, cols), shard_bufs[n])
        result[n] = [r.reshape(w.shape) for r in res]
    small_w = (norm_gain, pool_w_group, pool_scale, q_a_norm_gain, kv_a_norm_gain, q_norm_gain, k_norm_gain)
    small_m = (m_norm_gain, m_pool_w_group, m_pool_scale, m_q_a_norm_gain, m_kv_a_norm_gain, m_q_norm_gain, m_k_norm_gain)
    small_v = (v_norm_gain, v_pool_w_group, v_pool_scale, v_q_a_norm_gain, v_kv_a_norm_gain, v_q_norm_gain, v_k_norm_gain)
    res = _adamw("adamw_small", _pack_small(small_w), _pack_small(small_m), _pack_small(small_v), [small_buf])
    small_res = [_unpack_small(r, small_shapes) for r in res]
    for j, n in enumerate(SMALL):
        result[n] = [small_res[kind][j] for kind in range(4)]

    order = ("meta_tokens", "norm_gain", "w_in", "pool_w_group", "pool_scale", "pool_w_up", "q_a_norm_gain",
             "kv_a_norm_gain", "w_q_b", "w_kv_b", "q_norm_gain", "k_norm_gain", "mla_w_up", "w_out")
    outs = [loss, grad_x]
    for kind in range(4):
        outs += [result[n][kind] for n in order]
    return tuple(outs)
```

```python
import math

import jax
import jax.numpy as jnp
from jax import lax
from jax.experimental import pallas as pl
from jax.experimental.pallas import tpu as pltpu

F32 = jnp.float32
BF16 = jnp.bfloat16

D_MODEL = 1024
DEPTH = 4
N_META = 16
PAD_FRONT = 112
HEAD_ROWS = PAD_FRONT + N_META
POOL_WIDTH = 512
POOL_WINDOWS = (2, 4, 8, 16)
POOL_GROUP_DIM = 128
HALO = 16
N_HEADS = 8
HEADS_PER_STEP = 4
HEADS_PER_STEP_FWD = 8
QK_NOPE = 64
QK_ROPE = 32
QK_DIM = 96
V_DIM = 64
MLA_WIDTH = 512
Q_RANK = 768
KV_RANK = 256
ROPE_THETA = 10000.0
NORM_EPS = 1e-6
MASK_VALUE = -1e30
D_IN = 4640
N_DEV = 8
IN_SHARD = D_IN // N_DEV

P_IN = 4736
IN_SEGMENTS = ((0, 2048, 0), (2048, 2080, 4608), (2080, 2592, 4096), (2592, 3616, 2048), (3616, 4640, 3072))
COL_UZ, COL_CQKV, COL_GATES, COL_ZMLA, COL_KR = 0, 1024, 2048, 4096, 4608

ADAM_LR = 0.001
ADAM_B1 = 0.9
ADAM_B2 = 0.999
ADAM_EPS = 1e-08
ADAM_WD = 0.01
ADAM_STEP = 10

VMEM_LIMIT = 56 * 1024 * 1024
MESH_AXES = ("x", "y", "c")


def _cp(*sem):
    return pltpu.CompilerParams(dimension_semantics=sem, vmem_limit_bytes=VMEM_LIMIT)


def _row_tile(n_rows):
    return 320 if n_rows % 320 == 0 else 128


def _attn_tile(n_rows):
    return 640 if (n_rows % 640 == 0 and n_rows > 640) else 128


def _mm(a, b):
    return jnp.dot(a.astype(BF16), b.astype(BF16), preferred_element_type=F32)


def _mm_nt(a, b):
    return lax.dot_general(a.astype(BF16), b.astype(BF16), (((1,), (1,)), ((), ())), preferred_element_type=F32)


def _mm_tn(a, b):
    return lax.dot_general(a.astype(BF16), b.astype(BF16), (((0,), (0,)), ((), ())), preferred_element_type=F32)


def _sigmoid(z):
    return 1.0 / (1.0 + jnp.exp(-z))


def _rms(x, g):
    inv = lax.rsqrt(jnp.mean(x * x, axis=-1, keepdims=True) + NORM_EPS)
    xh = x * inv
    return xh * g, xh, inv


def _rms_bwd(dy, xh, inv, g):
    dg = jnp.sum(dy * xh, axis=0, keepdims=True)
    dxh = dy * g
    dx = inv * (dxh - xh * jnp.mean(dxh * xh, axis=-1, keepdims=True))
    return dx, dg


def _rope(x, tab):
    return x * tab[:, 0:128] + pltpu.roll(x, 112, 1) * tab[:, 128:256] + pltpu.roll(x, 16, 1) * tab[:, 256:384]


def _rope_bwd(dy, tab):
    return dy * tab[:, 0:128] + pltpu.roll(dy * tab[:, 128:256], 16, 1) + pltpu.roll(dy * tab[:, 256:384], 112, 1)


def _head_rms(x, g):
    inv = lax.rsqrt(jnp.sum(x * x, axis=-1, keepdims=True) * (1.0 / QK_DIM) + NORM_EPS)
    xh = x * inv
    return xh * g, xh, inv


def _head_rms_bwd(dy, xh, inv, g):
    dg = jnp.sum(dy * xh, axis=0, keepdims=True)
    dxh = dy * g
    dx = inv * (dxh - xh * (jnp.sum(dxh * xh, axis=-1, keepdims=True) * (1.0 / QK_DIM)))
    return dx, dg


def _pad_gain(g_ref):
    return jnp.concatenate([g_ref[...], jnp.zeros((1, 128 - QK_DIM), F32)], axis=1)


class _Exchange:
    def __init__(self, gather_list, scatter_list):
        self.arrays = list(gather_list) + list(scatter_list)
        self.n_g, self.n = len(gather_list), len(self.arrays)
        self.out_shape = [jax.ShapeDtypeStruct((N_DEV,) + a.shape, a.dtype) for a in gather_list]
        self.out_shape += [jax.ShapeDtypeStruct(a.shape, a.dtype) for a in scatter_list]
        self.specs = [pl.BlockSpec(memory_space=pl.ANY)] * self.n
        self.sems = [pltpu.SemaphoreType.DMA((7 * self.n,)), pltpu.SemaphoreType.DMA((7 * self.n,)),
                     pltpu.SemaphoreType.DMA((self.n,))]

    def copies(self, ins, outs, sems):
        send_sems, recv_sems, local_sems = sems
        x, y, c = lax.axis_index("x"), lax.axis_index("y"), lax.axis_index("c")
        me = 4 * x + 2 * y + c

        def src(a, slot):
            return ins[a] if a < self.n_g else ins[a].at[slot]

        out = [pltpu.make_async_copy(src(a, me), outs[a].at[me], local_sems.at[a]) for a in range(self.n)]
        for k in range(1, N_DEV):
            px = 1 - x if k & 4 else x
            py = 1 - y if k & 2 else y
            pc = 1 - c if k & 1 else c
            peer = 4 * px + 2 * py + pc
            for a in range(self.n):
                out.append(pltpu.make_async_remote_copy(
                    src_ref=src(a, peer), dst_ref=outs[a].at[me],
                    send_sem=send_sems.at[a * 7 + k - 1], recv_sem=recv_sems.at[a * 7 + k - 1],
                    device_id=(px, py, pc), device_id_type=pl.DeviceIdType.MESH))
        return out


def _exchange(name, gather_list, scatter_list):
    ex = _Exchange(gather_list, scatter_list)
    n = ex.n

    def body(*refs):
        copies = ex.copies(refs[:n], refs[n:2 * n], refs[2 * n:])
        for cp in copies:
            cp.start()
        for cp in copies:
            cp.wait()

    return pl.pallas_call(
        body, name=name, out_shape=ex.out_shape, in_specs=ex.specs, out_specs=ex.specs, scratch_shapes=ex.sems,
        compiler_params=pltpu.CompilerParams(has_side_effects=True),
    )(*ex.arrays)


def _call_carrying(ex, body, name, out_shape, grid, in_specs, out_specs, scratch_shapes, semantics, args):
    if ex is None:
        return pl.pallas_call(body, name=name, out_shape=out_shape, grid=grid, in_specs=in_specs, out_specs=out_specs,
                              scratch_shapes=scratch_shapes, compiler_params=_cp(*semantics))(*args)
    n, n_in, n_out, n_scr = ex.n, len(in_specs), len(out_shape), len(scratch_shapes)

    def carrying(*refs):
        ins, refs = refs[:n_in], refs[n_in:]
        c_in, refs = refs[:n], refs[n:]
        outs, refs = refs[:n_out], refs[n_out:]
        c_out, refs = refs[:n], refs[n:]
        scratch, sems = refs[:n_scr], refs[n_scr:]
        ids = [pl.program_id(a) for a in range(len(grid))]
        first, last = ids[0] == 0, ids[0] == grid[0] - 1
        for a in range(1, len(grid)):
            first, last = first & (ids[a] == 0), last & (ids[a] == grid[a] - 1)
        copies = ex.copies(c_in, c_out, sems)

        @pl.when(first)
        def _():
            for cp in copies:
                cp.start()

        body(*ins, *outs, *scratch)

        @pl.when(last)
        def _():
            for cp in copies:
                cp.wait()

    return pl.pallas_call(
        carrying, name=name + "_x", out_shape=list(out_shape) + ex.out_shape, grid=grid,
        in_specs=list(in_specs) + ex.specs, out_specs=list(out_specs) + ex.specs,
        scratch_shapes=list(scratch_shapes) + ex.sems, compiler_params=_cp(*["arbitrary"] * len(grid)),
    )(*args, *ex.arrays)


def _in_pieces():
    runs = []
    for lo, hi, dst in sorted(IN_SEGMENTS, key=lambda s: s[2]):
        col = lo
        while col < hi:
            d = col // IN_SHARD
            end = min(hi, (d + 1) * IN_SHARD)
            runs.append((d, col - d * IN_SHARD, end - d * IN_SHARD))
            col = end
    return runs


def _cast_bf16(name, w):
    shape = w.shape
    w2 = w.reshape(-1, shape[-1])
    r, c = w2.shape
    tr = 512 if r % 512 == 0 else r

    def body(x_ref, o_ref):
        o_ref[...] = x_ref[...].astype(BF16)

    spec = pl.BlockSpec((tr, c), lambda i: (i, 0))
    out = pl.pallas_call(body, name=name, out_shape=jax.ShapeDtypeStruct((r, c), BF16), grid=(r // tr,),
                         in_specs=[spec], out_specs=spec, compiler_params=_cp("parallel"))(w2)
    return out.reshape(shape)


def _repack_w_in(gathered):
    tr = 256
    runs = _in_pieces()

    def body(x_ref, o_ref):
        parts = [x_ref[d, :, a:b].astype(F32) for d, a, b in runs]
        parts.append(jnp.zeros((tr, P_IN - D_IN), F32))
        o_ref[...] = jnp.concatenate(parts, axis=1).astype(BF16)

    return pl.pallas_call(
        body, name="repack_w_in", out_shape=jax.ShapeDtypeStruct((D_MODEL, P_IN), BF16), grid=(D_MODEL // tr,),
        in_specs=[pl.BlockSpec((N_DEV, tr, IN_SHARD), lambda i: (0, i, 0))],
        out_specs=pl.BlockSpec((tr, P_IN), lambda i: (i, 0)),
        compiler_params=_cp("parallel"),
    )(gathered)


def _unpack_w_in(parts):
    tr = 256
    runs = _in_pieces()
    widths = [p.shape[-1] for p in parts]

    def body(a_ref, b_ref, c_ref, d_ref, o_ref):
        full = jnp.concatenate([a_ref[...], b_ref[...], c_ref[...], d_ref[...]], axis=1)
        col = 0
        per_dev = [[] for _ in range(N_DEV)]
        for d, a, b in runs:
            per_dev[d].append((a, full[:, col:col + (b - a)]))
            col += b - a
        for d in range(N_DEV):
            pieces = [v for _, v in sorted(per_dev[d], key=lambda av: av[0])]
            o_ref[d] = (jnp.concatenate(pieces, axis=1) if len(pieces) > 1 else pieces[0]).astype(BF16)

    return pl.pallas_call(
        body, name="unpack_w_in", out_shape=jax.ShapeDtypeStruct((N_DEV, D_MODEL, IN_SHARD), BF16), grid=(D_MODEL // tr,),
        in_specs=[pl.BlockSpec((tr, w), lambda i: (i, 0)) for w in widths],
        out_specs=pl.BlockSpec((N_DEV, tr, IN_SHARD), lambda i: (0, i, 0)),
        compiler_params=_cp("parallel"),
    )(*parts)


def _repack_cols(name, gathered, dtype):
    _, depth, k, n = gathered.shape

    def body(x_ref, o_ref):
        o_ref[0] = jnp.concatenate([x_ref[d, 0] for d in range(N_DEV)], axis=1).astype(dtype)

    return pl.pallas_call(
        body, name=name, out_shape=jax.ShapeDtypeStruct((depth, k, N_DEV * n), dtype), grid=(depth,),
        in_specs=[pl.BlockSpec((N_DEV, 1, k, n), lambda l: (0, l, 0, 0))],
        out_specs=pl.BlockSpec((1, k, N_DEV * n), lambda l: (l, 0, 0)),
        compiler_params=_cp("parallel"),
    )(gathered)


def _unpack_cols(name, full, n, dtype):
    depth, k, _ = full.shape

    def body(x_ref, o_ref):
        for d in range(N_DEV):
            o_ref[d, 0] = x_ref[0, :, d * n:(d + 1) * n].astype(dtype)

    return pl.pallas_call(
        body, name=name, out_shape=jax.ShapeDtypeStruct((N_DEV, depth, k, n), dtype), grid=(depth,),
        in_specs=[pl.BlockSpec((1, k, N_DEV * n), lambda l: (l, 0, 0))],
        out_specs=pl.BlockSpec((N_DEV, 1, k, n), lambda l: (0, l, 0, 0)),
        compiler_params=_cp("parallel"),
    )(full)


def _repack_rows(name, gathered):
    _, depth, r, n = gathered.shape

    def body(x_ref, o_ref):
        for d in range(N_DEV):
            o_ref[0, d * r:(d + 1) * r, :] = x_ref[d, 0].astype(BF16)

    return pl.pallas_call(
        body, name=name, out_shape=jax.ShapeDtypeStruct((depth, N_DEV * r, n), BF16), grid=(depth,),
        in_specs=[pl.BlockSpec((N_DEV, 1, r, n), lambda l: (0, l, 0, 0))],
        out_specs=pl.BlockSpec((1, N_DEV * r, n), lambda l: (l, 0, 0)),
        compiler_params=_cp("parallel"),
    )(gathered)


def _unpack_rows(name, full, r):
    depth, _, n = full.shape

    def body(x_ref, o_ref):
        for d in range(N_DEV):
            o_ref[d, 0] = x_ref[0, d * r:(d + 1) * r, :].astype(BF16)

    return pl.pallas_call(
        body, name=name, out_shape=jax.ShapeDtypeStruct((N_DEV, depth, r, n), BF16), grid=(depth,),
        in_specs=[pl.BlockSpec((1, N_DEV * r, n), lambda l: (l, 0, 0))],
        out_specs=pl.BlockSpec((N_DEV, 1, r, n), lambda l: (0, l, 0, 0)),
        compiler_params=_cp("parallel"),
    )(full)


def _repack_q_heads(gathered):
    r = Q_RANK // N_DEV

    def body(x_ref, o_ref):
        z = jnp.zeros((r, 128 - QK_DIM), F32)
        for d in range(N_DEV):
            x = x_ref[d].astype(F32)
            parts = []
            for h in range(N_HEADS):
                parts += [x[:, QK_DIM * h:QK_DIM * (h + 1)], z]
            o_ref[d * r:(d + 1) * r, :] = jnp.concatenate(parts, axis=1).astype(BF16)

    return pl.pallas_call(body, name="repack_q_heads", out_shape=jax.ShapeDtypeStruct((Q_RANK, 1024), BF16),
                          compiler_params=_cp())(gathered)


def _unpack_q_heads(dwq):
    r = Q_RANK // N_DEV

    def body(x_ref, o_ref):
        for d in range(N_DEV):
            x = x_ref[d * r:(d + 1) * r, :]
            o_ref[d] = jnp.concatenate([x[:, 128 * h:128 * h + QK_DIM] for h in range(N_HEADS)], axis=1).astype(BF16)

    return pl.pallas_call(body, name="unpack_q_heads", out_shape=jax.ShapeDtypeStruct((N_DEV, r, Q_RANK), BF16),
                          compiler_params=_cp())(dwq)


def _repack_kv_heads(gathered):
    def body(x_ref, wk_ref, wv_ref):
        z = jnp.zeros((KV_RANK, 128 - QK_NOPE), F32)
        for h in range(N_HEADS):
            x = x_ref[h].astype(F32)
            wk_ref[:, 128 * h:128 * (h + 1)] = jnp.concatenate([x[:, :QK_NOPE], z], axis=1).astype(BF16)
            wv_ref[:, 128 * h:128 * (h + 1)] = jnp.concatenate([x[:, QK_NOPE:], z], axis=1).astype(BF16)

    sds = jax.ShapeDtypeStruct((KV_RANK, 1024), BF16)
    return pl.pallas_call(body, name="repack_kv_heads", out_shape=(sds, sds), compiler_params=_cp())(gathered)


def _unpack_kv_heads(dwk, dwv):
    def body(k_ref, v_ref, o_ref):
        for h in range(N_HEADS):
            o_ref[h] = jnp.concatenate([k_ref[:, 128 * h:128 * h + QK_NOPE], v_ref[:, 128 * h:128 * h + V_DIM]],
                                       axis=1).astype(BF16)

    return pl.pallas_call(body, name="unpack_kv_heads", out_shape=jax.ShapeDtypeStruct((N_DEV, KV_RANK, 128), BF16),
                          compiler_params=_cp())(dwk, dwv)


def _embed(x, meta_g):
    seq = x.shape[0]
    n_blk = seq // HEAD_ROWS + 1

    def body(x_ref, m_ref, o_ref):
        i = pl.program_id(0)

        @pl.when(i == 0)
        def _():
            meta = jnp.concatenate([m_ref[d] for d in range(N_DEV)], axis=1)
            o_ref[...] = jnp.concatenate([jnp.zeros((PAD_FRONT, D_MODEL), F32), meta], axis=0)

        @pl.when(i > 0)
        def _():
            o_ref[...] = x_ref[...]

    return pl.pallas_call(
        body, name="embed", out_shape=jax.ShapeDtypeStruct((seq + HEAD_ROWS, D_MODEL), F32), grid=(n_blk,),
        in_specs=[pl.BlockSpec((HEAD_ROWS, D_MODEL), lambda i: (jnp.maximum(i - 1, 0), 0)),
                  pl.BlockSpec((N_DEV, N_META, 128), lambda i: (0, 0, 0))],
        out_specs=pl.BlockSpec((HEAD_ROWS, D_MODEL), lambda i: (i, 0)),
        compiler_params=_cp("parallel"),
    )(x, meta_g)


def _rope_table(pos_col, freq_row):
    n_rows = pos_col.shape[0]
    tr = 128

    def body(p_ref, f_ref, o_ref):
        ang = p_ref[...] * f_ref[...]
        lane = lax.broadcasted_iota(jnp.int32, ang.shape, 1)
        cosv, sinv = jnp.cos(ang), jnp.sin(ang)
        half = QK_ROPE // 2
        o_ref[:, 0:128] = jnp.where(lane < QK_NOPE, 1.0, jnp.where(lane < QK_DIM, cosv, 0.0))
        o_ref[:, 128:256] = jnp.where((lane >= QK_NOPE) & (lane < QK_NOPE + half), -sinv, 0.0)
        o_ref[:, 256:384] = jnp.where((lane >= QK_NOPE + half) & (lane < QK_DIM), sinv, 0.0)

    return pl.pallas_call(
        body, name="rope_table", out_shape=jax.ShapeDtypeStruct((n_rows, 384), F32), grid=(n_rows // tr,),
        in_specs=[pl.BlockSpec((tr, 1), lambda i: (i, 0)), pl.BlockSpec((1, 128), lambda i: (0, 0))],
        out_specs=pl.BlockSpec((tr, 384), lambda i: (i, 0)),
        compiler_params=_cp("parallel"),
    )(pos_col, freq_row)


def _loss_head(h, target):
    n_rows = h.shape[0]
    n_blk = n_rows // HEAD_ROWS

    def body(h_ref, t_ref, loss_ref, dh_ref):
        i = pl.program_id(0)

        @pl.when(i == 0)
        def _():
            loss_ref[...] = jnp.zeros(loss_ref.shape, F32)
            dh_ref[...] = jnp.zeros(dh_ref.shape, F32)

        @pl.when(i > 0)
        def _():
            err = h_ref[...] - t_ref[...]
            dh_ref[...] = err * (1.0 / D_MODEL)
            loss_ref[...] += 0.5 * jnp.sum(jnp.mean(err * err, axis=-1, keepdims=True))

    return pl.pallas_call(
        body, name="loss_head",
        out_shape=(jax.ShapeDtypeStruct((8, 128), F32), jax.ShapeDtypeStruct((n_rows, D_MODEL), F32)),
        grid=(n_blk,),
        in_specs=[pl.BlockSpec((HEAD_ROWS, D_MODEL), lambda i: (i, 0)),
                  pl.BlockSpec((HEAD_ROWS, D_MODEL), lambda i: (jnp.maximum(i - 1, 0), 0))],
        out_specs=(pl.BlockSpec((8, 128), lambda i: (0, 0)), pl.BlockSpec((HEAD_ROWS, D_MODEL), lambda i: (i, 0))),
        compiler_params=_cp("arbitrary"),
    )(h, target)


PROJ_CHUNKS = (0, 1024, 2048, 3072, 4096, P_IN)


def _norm_proj(h_res, gain, w_in, ex=None):
    n_rows = h_res.shape[0]
    tm = _row_tile(n_rows)

    def body(x_ref, g_ref, w_ref, hb_ref, p_ref):
        h, _, _ = _rms(x_ref[...], g_ref[...])
        hb = h.astype(BF16)
        hb_ref[...] = hb
        for c0, c1 in zip(PROJ_CHUNKS[:-1], PROJ_CHUNKS[1:]):
            p_ref[:, c0:c1] = jnp.dot(hb, w_ref[:, c0:c1], preferred_element_type=F32)

    return _call_carrying(
        ex, body, "norm_proj",
        out_shape=(jax.ShapeDtypeStruct((n_rows, D_MODEL), BF16), jax.ShapeDtypeStruct((n_rows, P_IN), F32)),
        grid=(n_rows // tm,),
        in_specs=[pl.BlockSpec((tm, D_MODEL), lambda i: (i, 0)), pl.BlockSpec((1, D_MODEL), lambda i: (0, 0)),
                  pl.BlockSpec((D_MODEL, P_IN), lambda i: (0, 0))],
        out_specs=(pl.BlockSpec((tm, D_MODEL), lambda i: (i, 0)), pl.BlockSpec((tm, P_IN), lambda i: (i, 0))),
        scratch_shapes=[], semantics=("parallel",), args=(h_res, gain, w_in))


def _pool_math(u, halo, z, wg_ref, scale, row0):
    tm = u.shape[0]
    ext = jnp.concatenate([halo, u], axis=0)
    t1 = (row0 + lax.broadcasted_iota(jnp.int32, (tm, 1), 0) - (PAD_FRONT - 1)).astype(F32)
    mixed, yg = [], []
    for g, w in enumerate(POOL_WINDOWS):
        a = ext[:, g * 128:(g + 1) * 128]
        k = 1
        while k < w:
            a = a + pltpu.roll(a, k, 0)
            k *= 2
        cnt = jnp.clip(t1, 1.0, float(w))
        mg = a[HALO:, :] / cnt - u[:, g * 128:(g + 1) * 128]
        mixed.append(mg)
        yg.append(_mm(mg, wg_ref[g]))
    mixed = jnp.concatenate(mixed, axis=1)
    yg = jnp.concatenate(yg, axis=1)
    ys = yg * scale
    sig = _sigmoid(z)
    return mixed, yg, ys, sig


def _halo_above(tm):
    return lambda i: (jnp.maximum(i * (tm // HALO) - 1, 0), 0)


def _pool_fwd(proj, wg, scale, w_up):
    n_rows = proj.shape[0]
    tm = _row_tile(n_rows)

    def body(uz_ref, halo_ref, wg_ref, sc_ref, wup_ref, y_ref):
        uz = uz_ref[...]
        u, z = uz[:, :POOL_WIDTH], uz[:, POOL_WIDTH:]
        _, _, ys, sig = _pool_math(u, halo_ref[...], z, wg_ref, sc_ref[...], pl.program_id(0) * tm)
        y_ref[...] = _mm(ys * (z * sig), wup_ref[...])

    return pl.pallas_call(
        body, name="pool_fwd", out_shape=jax.ShapeDtypeStruct((n_rows, D_MODEL), F32), grid=(n_rows // tm,),
        in_specs=[pl.BlockSpec((tm, 1024), lambda i: (i, 0)), pl.BlockSpec((HALO, POOL_WIDTH), _halo_above(tm)),
                  pl.BlockSpec((4, 128, 128), lambda i: (0, 0, 0)), pl.BlockSpec((1, POOL_WIDTH), lambda i: (0, 0)),
                  pl.BlockSpec((POOL_WIDTH, D_MODEL), lambda i: (0, 0))],
        out_specs=pl.BlockSpec((tm, D_MODEL), lambda i: (i, 0)),
        compiler_params=_cp("parallel"),
    )(proj, proj, wg, scale, w_up)


def _mla_pre_fwd(proj, tab, gqa, gkva, wq, wk, wv, gqn, gkn):
    n_rows = proj.shape[0]
    tm = _row_tile(n_rows)

    def body(c_ref, kr_ref, tab_ref, gqa_ref, gkva_ref, wq_ref, wk_ref, wv_ref, gqn_ref, gkn_ref, q_ref, k_ref, v_ref):
        c, tab = c_ref[...], tab_ref[...]
        cqn, _, _ = _rms(c[:, :Q_RANK], gqa_ref[...])
        ckvn, _, _ = _rms(c[:, Q_RANK:], gkva_ref[...])
        qp, kp, vp = _mm(cqn, wq_ref[...]), _mm(ckvn, wk_ref[...]), _mm(ckvn, wv_ref[...])
        kr = pltpu.roll(kr_ref[...], QK_NOPE, 1)
        gqn, gkn = _pad_gain(gqn_ref), _pad_gain(gkn_ref)
        lane = lax.broadcasted_iota(jnp.int32, (tm, 128), 1)
        one = jnp.where(lane == V_DIM, 1.0, 0.0)
        for h in range(N_HEADS):
            blk = slice(128 * h, 128 * (h + 1))
            qn, _, _ = _head_rms(qp[:, blk], gqn)
            q_ref[h] = _rope(qn, tab).astype(BF16)
            kn, _, _ = _head_rms(kp[:, blk] + kr, gkn)
            k_ref[h] = _rope(kn, tab).astype(BF16)
            v_ref[h] = (vp[:, blk] + one).astype(BF16)

    full = lambda *s: pl.BlockSpec(s, lambda i: (0,) * len(s))
    head = pl.BlockSpec((N_HEADS, tm, 128), lambda i: (0, i, 0))
    sds = jax.ShapeDtypeStruct((N_HEADS, n_rows, 128), BF16)
    return pl.pallas_call(
        body, name="mla_pre_fwd", out_shape=(sds, sds, sds), grid=(n_rows // tm,),
        in_specs=[pl.BlockSpec((tm, 1024), lambda i: (i, COL_CQKV // 1024)),
                  pl.BlockSpec((tm, 128), lambda i: (i, COL_KR // 128)),
                  pl.BlockSpec((tm, 384), lambda i: (i, 0)),
                  full(1, Q_RANK), full(1, KV_RANK), full(Q_RANK, 1024), full(KV_RANK, 1024), full(KV_RANK, 1024),
                  full(1, QK_DIM), full(1, QK_DIM)],
        out_specs=(head, head, head),
        compiler_params=_cp("parallel"),
    )(proj, proj, tab, gqa, gkva, wq, wk, wv, gqn, gkn)


def _attn_mask(row0, col0, n_r, n_c):
    row = row0 + lax.broadcasted_iota(jnp.int32, (n_r, n_c), 0)
    col = col0 + lax.broadcasted_iota(jnp.int32, (n_r, n_c), 1)
    return (col <= row) & (col >= PAD_FRONT)


ATTN_SCALE = 1.0 / math.sqrt(QK_DIM)
EXP2_SCALE = ATTN_SCALE * math.log2(math.e)
LOG2_E = math.log2(math.e)


def _row_chunks(t, size):
    return [(r, min(size, t - r)) for r in range(0, t, size)]


def _attn_fwd(q, k, v, ex=None):
    n_rows = q.shape[1]
    t = _attn_tile(n_rows)
    nb = n_rows // t

    def body(q_ref, k_ref, v_ref, o_ref, lse_ref, m_sc, acc_sc):
        qi, ki = pl.program_id(1), pl.program_id(2)

        @pl.when(ki == 0)
        def _():
            m_sc[...] = jnp.full(m_sc.shape, -jnp.inf, F32)
            acc_sc[...] = jnp.zeros(acc_sc.shape, F32)

        def update(masked):
            heads = range(hp)
            s = [_mm_nt(q_ref[h], k_ref[h]) for h in heads]
            if masked:
                mask = _attn_mask(qi * t, ki * t, t, t)
                s = [jnp.where(mask, sh, MASK_VALUE) for sh in s]
            p = []
            for h in heads:
                m_prev = m_sc[h]
                m_new = jnp.maximum(m_prev, jnp.max(s[h], axis=1, keepdims=True))
                alpha = jnp.exp2((m_prev - m_new) * EXP2_SCALE)
                p.append(jnp.exp2((s[h] - m_new) * EXP2_SCALE).astype(BF16))
                acc_sc[h] = alpha * acc_sc[h]
                m_sc[h] = m_new
            for h in heads:
                acc_sc[h] += _mm(p[h], v_ref[h])

        edge = (ki == qi) | (ki == 0)

        @pl.when((ki <= qi) & edge)
        def _():
            update(True)

        @pl.when((ki < qi) & (ki > 0))
        def _():
            update(False)

        @pl.when(ki == nb - 1)
        def _():
            acc = acc_sc[...]
            denom = acc[:, :, V_DIM:V_DIM + 1]
            o_ref[...] = acc[:, :, :V_DIM] / denom
            lse = m_sc[...] * ATTN_SCALE + jnp.log(denom)
            lane = lax.broadcasted_iota(jnp.int32, (t, 128), 1)
            wide = jnp.zeros((t, 128), F32)
            for h in range(hp):
                wide = jnp.where(lane == h, lse[h], wide)
            rows = wide.T
            for h in range(hp):
                lse_ref[h] = rows[h:h + 1, :]

    hp = HEADS_PER_STEP_FWD
    kv_idx = lambda h, qi, ki: (h, jnp.minimum(ki, qi), 0)
    return _call_carrying(
        ex, body, "attn_fwd",
        out_shape=(jax.ShapeDtypeStruct((N_HEADS, n_rows, V_DIM), F32), jax.ShapeDtypeStruct((N_HEADS, 1, n_rows), F32)),
        grid=(N_HEADS // hp, nb, nb),
        in_specs=[pl.BlockSpec((hp, t, 128), lambda h, qi, ki: (h, qi, 0)), pl.BlockSpec((hp, t, 128), kv_idx),
                  pl.BlockSpec((hp, t, 128), kv_idx)],
        out_specs=(pl.BlockSpec((hp, t, V_DIM), lambda h, qi, ki: (h, qi, 0)), pl.BlockSpec((hp, 1, t), lambda h, qi, ki: (h, 0, qi))),
        scratch_shapes=[pltpu.VMEM((hp, t, 1), F32), pltpu.VMEM((hp, t, 128), F32)],
        semantics=("parallel", "parallel", "arbitrary"), args=(q, k, v))


def _merge_math(o_ref, z, gates, y_pool, y_mla_fn):
    o_cat = jnp.concatenate([o_ref[h] for h in range(N_HEADS)], axis=1)
    sig_z = _sigmoid(z)
    a_mla = o_cat * (z * sig_z)
    y_mla = y_mla_fn(a_mla)
    sgp, sgm = _sigmoid(gates[:, :D_MODEL]), _sigmoid(gates[:, D_MODEL:])
    merged = sgp * y_pool + sgm * y_mla
    return o_cat, sig_z, a_mla, y_mla, sgp, sgm, merged


def _mla_post_fwd(h_res, proj, o, y_pool, w_mla_up, w_out):
    n_rows = h_res.shape[0]
    tm = _row_tile(n_rows)

    def body(h_ref, g_ref, z_ref, o_ref, yp_ref, wup_ref, wout_ref, ymla_ref, hn_ref):
        _, _, _, y_mla, _, _, merged = _merge_math(o_ref, z_ref[...], g_ref[...], yp_ref[...],
                                                   lambda a: _mm(a, wup_ref[...]))
        ymla_ref[...] = y_mla
        hn_ref[...] = h_ref[...] + _mm(merged, wout_ref[...])

    row = lambda w, c: pl.BlockSpec((tm, w), lambda i: (i, c))
    return pl.pallas_call(
        body, name="mla_post_fwd",
        out_shape=(jax.ShapeDtypeStruct((n_rows, D_MODEL), F32), jax.ShapeDtypeStruct((n_rows, D_MODEL), F32)),
        grid=(n_rows // tm,),
        in_specs=[row(D_MODEL, 0), row(2048, COL_GATES // 2048), row(MLA_WIDTH, COL_ZMLA // MLA_WIDTH),
                  pl.BlockSpec((N_HEADS, tm, V_DIM), lambda i: (0, i, 0)), row(D_MODEL, 0),
                  pl.BlockSpec((MLA_WIDTH, D_MODEL), lambda i: (0, 0)), pl.BlockSpec((D_MODEL, D_MODEL), lambda i: (0, 0))],
        out_specs=(row(D_MODEL, 0), row(D_MODEL, 0)),
        compiler_params=_cp("parallel"),
    )(h_res, proj, proj, o, y_pool, w_mla_up, w_out)


def _dsilu(z, sig):
    return sig * (1.0 + z * (1.0 - sig))


def _acc(ref, val, first):
    @pl.when(first)
    def _():
        ref[...] = val

    @pl.when(jnp.logical_not(first))
    def _():
        ref[...] += val


def _mla_post_bwd(dh, proj, o, y_pool, y_mla, w_mla_up, w_out):
    n_rows = dh.shape[0]
    tm = _row_tile(n_rows)

    def body(dh_ref, g_ref, z_ref, o_ref, yp_ref, ym_ref, wup_ref, wout_ref,
             dgz_ref, dyp_ref, do_ref, dwout_ref, dwup_ref):
        first = pl.program_id(0) == 0
        z, y_pool, y_mla = z_ref[...], yp_ref[...], ym_ref[...]
        dhv = dh_ref[...]
        dmerged = _mm_nt(dhv, wout_ref[...])
        o_cat, sig_z, a_mla, _, sgp, sgm, merged = _merge_math(o_ref, z, g_ref[...], y_pool, lambda a: y_mla)
        dy_mla = dmerged * sgm
        da = _mm_nt(dy_mla, wup_ref[...])
        dyp_ref[...] = dmerged * sgp
        dgz_ref[:, 0:D_MODEL] = dmerged * y_pool * (sgp * (1.0 - sgp))
        dgz_ref[:, D_MODEL:2 * D_MODEL] = dmerged * y_mla * (sgm * (1.0 - sgm))
        _acc(dwout_ref, _mm_tn(merged, dhv), first)
        do_cat = da * (z * sig_z)
        dgz_ref[:, 2 * D_MODEL:] = da * o_cat * _dsilu(z, sig_z)
        for h in range(N_HEADS):
            do_ref[h] = do_cat[:, V_DIM * h:V_DIM * (h + 1)]
        _acc(dwup_ref, _mm_tn(a_mla, dy_mla), first)

    row = lambda w, c: pl.BlockSpec((tm, w), lambda i: (i, c))
    head = lambda w: pl.BlockSpec((N_HEADS, tm, w), lambda i: (0, i, 0))
    const = lambda r, c: pl.BlockSpec((r, c), lambda i: (0, 0))
    return pl.pallas_call(
        body, name="mla_post_bwd",
        out_shape=(jax.ShapeDtypeStruct((n_rows, 2560), F32), jax.ShapeDtypeStruct((n_rows, D_MODEL), F32),
                   jax.ShapeDtypeStruct((N_HEADS, n_rows, V_DIM), F32),
                   jax.ShapeDtypeStruct((D_MODEL, D_MODEL), F32), jax.ShapeDtypeStruct((MLA_WIDTH, D_MODEL), F32)),
        grid=(n_rows // tm,),
        in_specs=[row(D_MODEL, 0), row(2048, COL_GATES // 2048), row(MLA_WIDTH, COL_ZMLA // MLA_WIDTH), head(V_DIM),
                  row(D_MODEL, 0), row(D_MODEL, 0), const(MLA_WIDTH, D_MODEL), const(D_MODEL, D_MODEL)],
        out_specs=(row(2560, 0), row(D_MODEL, 0), head(V_DIM), const(D_MODEL, D_MODEL), const(MLA_WIDTH, D_MODEL)),
        compiler_params=_cp("arbitrary"),
    )(dh, proj, proj, o, y_pool, y_mla, w_mla_up, w_out)


def _attn_bwd(q, k, v, do, lse, o, ex=None):
    n_rows = q.shape[1]
    t = _attn_tile(n_rows)
    nb = n_rows // t

    def body(q_ref, k_ref, v_ref, do_ref, lse_ref, o_ref, dq_ref, dk_ref, dv_ref, dk_sc, dv_sc, delta_sc):
        ki, qi = pl.program_id(1), pl.program_id(2)

        @pl.when((ki == 0) & (qi == 0))
        def _():
            dq_ref[...] = jnp.zeros(dq_ref.shape, F32)

        @pl.when(ki == 0)
        def _():
            pad = jnp.zeros((t, 128 - V_DIM), F32)
            for h in range(HEADS_PER_STEP):
                wide = jnp.concatenate([do_ref[h] * o_ref[h], pad], axis=1)
                delta_sc[h, qi] = jnp.sum(wide.T, axis=0, keepdims=True)

        @pl.when(qi == 0)
        def _():
            dk_sc[...] = jnp.zeros(dk_sc.shape, F32)
            dv_sc[...] = jnp.zeros(dv_sc.shape, F32)

        def step(masked):
            heads = range(HEADS_PER_STEP)
            st = [_mm_nt(k_ref[h], q_ref[h]) for h in heads]
            dpt = [_mm_nt(v_ref[h, :, 0:V_DIM], do_ref[h]) for h in heads]
            if masked:
                key = ki * t + lax.broadcasted_iota(jnp.int32, (t, t), 0)
                qry = qi * t + lax.broadcasted_iota(jnp.int32, (t, t), 1)
                mask = (key <= qry) & (key >= PAD_FRONT)
                st = [jnp.where(mask, sh, MASK_VALUE) for sh in st]
            pt = [jnp.exp2(st[h] * EXP2_SCALE - lse_ref[h] * LOG2_E) for h in heads]
            dst = [pt[h] * (dpt[h] - delta_sc[h, qi]) for h in heads]
            rows = pl.ds(pl.multiple_of(qi * t, t), t)
            for h in heads:
                dv_sc[h] += _mm(pt[h], do_ref[h])
                dk_sc[h] += _mm(dst[h], q_ref[h])
                dq_ref[h, rows, :] += _mm_tn(dst[h], k_ref[h]) * ATTN_SCALE

        edge = (ki == qi) | (ki == 0)

        @pl.when((qi >= ki) & edge)
        def _():
            step(True)

        @pl.when((qi > ki) & (ki > 0))
        def _():
            step(False)

        @pl.when(qi == nb - 1)
        def _():
            dk_ref[...] = dk_sc[...] * ATTN_SCALE
            dv_ref[...] = dv_sc[...]

    hp = HEADS_PER_STEP
    q_idx = lambda h, ki, qi: (h, jnp.maximum(qi, ki), 0)
    row_idx = lambda h, ki, qi: (h, 0, jnp.maximum(qi, ki))
    o_idx = lambda h, ki, qi: (h, jnp.where(ki == 0, qi, 0), 0)
    k_idx = lambda h, ki, qi: (h, ki, 0)
    return _call_carrying(
        ex, body, "attn_bwd",
        out_shape=(jax.ShapeDtypeStruct((N_HEADS, n_rows, 128), F32), jax.ShapeDtypeStruct((N_HEADS, n_rows, 128), F32),
                   jax.ShapeDtypeStruct((N_HEADS, n_rows, V_DIM), F32)),
        grid=(N_HEADS // hp, nb, nb),
        in_specs=[pl.BlockSpec((hp, t, 128), q_idx), pl.BlockSpec((hp, t, 128), k_idx), pl.BlockSpec((hp, t, 128), k_idx),
                  pl.BlockSpec((hp, t, V_DIM), q_idx), pl.BlockSpec((hp, 1, t), row_idx), pl.BlockSpec((hp, t, V_DIM), o_idx)],
        out_specs=(pl.BlockSpec((hp, n_rows, 128), lambda h, ki, qi: (h, 0, 0), pipeline_mode=pl.Buffered(1)),
                   pl.BlockSpec((hp, t, 128), k_idx),
                   pl.BlockSpec((hp, t, V_DIM), k_idx)),
        scratch_shapes=[pltpu.VMEM((hp, t, 128), F32), pltpu.VMEM((hp, t, V_DIM), F32), pltpu.VMEM((hp, nb, 1, t), F32)],
        semantics=("parallel", "arbitrary", "arbitrary"), args=(q, k, v, do, lse, o))


def _mla_pre_bwd(proj, tab, gqa, gkva, wq, wk, wv, gqn, gkn, dq, dk, dv):
    n_rows = proj.shape[0]
    tm = _row_tile(n_rows)

    def body(c_ref, kr_ref, tab_ref, gqa_ref, gkva_ref, wq_ref, wk_ref, wv_ref, gqn_ref, gkn_ref, dq_ref, dk_ref, dv_ref,
             dc_ref, dkr_ref, dwq_ref, dwk_ref, dwv_ref, dgqa_ref, dgkva_ref, dgqn_ref, dgkn_ref):
        first = pl.program_id(0) == 0
        c, tab = c_ref[...], tab_ref[...]
        gqa, gkva = gqa_ref[...], gkva_ref[...]
        cqn, cq_h, cq_inv = _rms(c[:, :Q_RANK], gqa)
        ckvn, ckv_h, ckv_inv = _rms(c[:, Q_RANK:], gkva)
        qp, kp = _mm(cqn, wq_ref[...]), _mm(ckvn, wk_ref[...])
        kr = pltpu.roll(kr_ref[...], QK_NOPE, 1)
        gqn, gkn = _pad_gain(gqn_ref), _pad_gain(gkn_ref)
        dq_parts, dk_parts, dv_parts = [], [], []
        dkr = jnp.zeros((tm, 128), F32)
        dgqn = jnp.zeros((1, 128), F32)
        dgkn = jnp.zeros((1, 128), F32)
        zv = jnp.zeros((tm, 128 - V_DIM), F32)
        for h in range(N_HEADS):
            blk = slice(128 * h, 128 * (h + 1))
            _, xh, inv = _head_rms(qp[:, blk], gqn)
            dx, dg = _head_rms_bwd(_rope_bwd(dq_ref[h], tab), xh, inv, gqn)
            dq_parts.append(dx)
            dgqn = dgqn + dg
            _, xh, inv = _head_rms(kp[:, blk] + kr, gkn)
            dx, dg = _head_rms_bwd(_rope_bwd(dk_ref[h], tab), xh, inv, gkn)
            dk_parts.append(dx)
            dgkn = dgkn + dg
            dkr = dkr + dx
            dv_parts.append(jnp.concatenate([dv_ref[h], zv], axis=1))
        dqp = jnp.concatenate(dq_parts, axis=1)
        dkp = jnp.concatenate(dk_parts, axis=1)
        dvp = jnp.concatenate(dv_parts, axis=1)
        _acc(dwq_ref, _mm_tn(cqn, dqp), first)
        _acc(dwk_ref, _mm_tn(ckvn, dkp), first)
        _acc(dwv_ref, _mm_tn(ckvn, dvp), first)
        dcq, dg1 = _rms_bwd(_mm_nt(dqp, wq_ref[...]), cq_h, cq_inv, gqa)
        dckv, dg2 = _rms_bwd(_mm_nt(dkp, wk_ref[...]) + _mm_nt(dvp, wv_ref[...]), ckv_h, ckv_inv, gkva)
        _acc(dgqa_ref, dg1, first)
        _acc(dgkva_ref, dg2, first)
        _acc(dgqn_ref, dgqn, first)
        _acc(dgkn_ref, dgkn, first)
        dc_ref[...] = jnp.concatenate([dcq, dckv], axis=1)
        lane = lax.broadcasted_iota(jnp.int32, (tm, 128), 1)
        dkr_ref[...] = jnp.where(lane < QK_ROPE, pltpu.roll(dkr, 128 - QK_NOPE, 1), 0.0)

    full = lambda *s: pl.BlockSpec(s, lambda i: (0,) * len(s))
    head = lambda w: pl.BlockSpec((N_HEADS, tm, w), lambda i: (0, i, 0))
    return pl.pallas_call(
        body, name="mla_pre_bwd",
        out_shape=(jax.ShapeDtypeStruct((n_rows, 1024), F32), jax.ShapeDtypeStruct((n_rows, 128), F32),
                   jax.ShapeDtypeStruct((Q_RANK, 1024), F32), jax.ShapeDtypeStruct((KV_RANK, 1024), F32),
                   jax.ShapeDtypeStruct((KV_RANK, 1024), F32),
                   jax.ShapeDtypeStruct((1, Q_RANK), F32), jax.ShapeDtypeStruct((1, KV_RANK), F32),
                   jax.ShapeDtypeStruct((1, 128), F32), jax.ShapeDtypeStruct((1, 128), F32)),
        grid=(n_rows // tm,),
        in_specs=[pl.BlockSpec((tm, 1024), lambda i: (i, COL_CQKV // 1024)),
                  pl.BlockSpec((tm, 128), lambda i: (i, COL_KR // 128)),
                  pl.BlockSpec((tm, 384), lambda i: (i, 0)),
                  full(1, Q_RANK), full(1, KV_RANK), full(Q_RANK, 1024), full(KV_RANK, 1024), full(KV_RANK, 1024),
                  full(1, QK_DIM), full(1, QK_DIM), head(128), head(128), head(V_DIM)],
        out_specs=(pl.BlockSpec((tm, 1024), lambda i: (i, 0)), pl.BlockSpec((tm, 128), lambda i: (i, 0)),
                   full(Q_RANK, 1024), full(KV_RANK, 1024), full(KV_RANK, 1024), full(1, Q_RANK), full(1, KV_RANK),
                   full(1, 128), full(1, 128)),
        compiler_params=_cp("arbitrary"),
    )(proj, proj, tab, gqa, gkva, wq, wk, wv, gqn, gkn, dq, dk, dv)


def _pool_bwd_a(proj, dy_pool, wg, scale, w_up):
    n_rows = proj.shape[0]
    tm = _row_tile(n_rows)

    def body(uz_ref, halo_ref, dy_ref, wg_ref, sc_ref, wup_ref, dmz_ref, dwg_ref, dsc_ref, dwup_ref):
        first = pl.program_id(0) == 0
        uz = uz_ref[...]
        u, z = uz[:, :POOL_WIDTH], uz[:, POOL_WIDTH:]
        scale_v = sc_ref[...]
        mixed, yg, ys, sig = _pool_math(u, halo_ref[...], z, wg_ref, scale_v, pl.program_id(0) * tm)
        sp = z * sig
        dy = dy_ref[...]
        da = _mm_nt(dy, wup_ref[...])
        _acc(dwup_ref, _mm_tn(ys * sp, dy), first)
        dys = da * sp
        _acc(dsc_ref, jnp.sum(dys * yg, axis=0, keepdims=True), first)
        dyg = dys * scale_v
        for g in range(4):
            cols = slice(g * 128, (g + 1) * 128)
            dmz_ref[:, cols] = _mm_nt(dyg[:, cols], wg_ref[g])
            _acc(dwg_ref.at[g], _mm_tn(mixed[:, cols], dyg[:, cols]), first)
        dmz_ref[:, POOL_WIDTH:] = da * ys * _dsilu(z, sig)

    return pl.pallas_call(
        body, name="pool_bwd_a",
        out_shape=(jax.ShapeDtypeStruct((n_rows, 1024), F32), jax.ShapeDtypeStruct((4, 128, 128), F32),
                   jax.ShapeDtypeStruct((1, POOL_WIDTH), F32), jax.ShapeDtypeStruct((POOL_WIDTH, D_MODEL), F32)),
        grid=(n_rows // tm,),
        in_specs=[pl.BlockSpec((tm, 1024), lambda i: (i, 0)), pl.BlockSpec((HALO, POOL_WIDTH), _halo_above(tm)),
                  pl.BlockSpec((tm, D_MODEL), lambda i: (i, 0)),
                  pl.BlockSpec((4, 128, 128), lambda i: (0, 0, 0)), pl.BlockSpec((1, POOL_WIDTH), lambda i: (0, 0)),
                  pl.BlockSpec((POOL_WIDTH, D_MODEL), lambda i: (0, 0))],
        out_specs=(pl.BlockSpec((tm, 1024), lambda i: (i, 0)), pl.BlockSpec((4, 128, 128), lambda i: (0, 0, 0)),
                   pl.BlockSpec((1, POOL_WIDTH), lambda i: (0, 0)), pl.BlockSpec((POOL_WIDTH, D_MODEL), lambda i: (0, 0))),
        compiler_params=_cp("arbitrary"),
    )(proj, proj, dy_pool, wg, scale, w_up)


def _pool_bwd_b(dmz):
    n_rows = dmz.shape[0]
    tm = _row_tile(n_rows)
    n_tiles = n_rows // tm
    n_ext = tm + HALO

    def body(dmz_ref, halo_ref, o_ref):
        i = pl.program_id(0)
        v = dmz_ref[...]
        dm = v[:, :POOL_WIDTH]
        halo = jnp.where(i == n_tiles - 1, 0.0, halo_ref[...])
        ext = jnp.concatenate([dm, halo], axis=0)
        t1 = (i * tm + lax.broadcasted_iota(jnp.int32, (n_ext, 1), 0) - (PAD_FRONT - 1)).astype(F32)
        du = []
        for g, w in enumerate(POOL_WINDOWS):
            cols = slice(g * 128, (g + 1) * 128)
            a = ext[:, cols] / jnp.clip(t1, 1.0, float(w))
            k = 1
            while k < w:
                a = a + pltpu.roll(a, n_ext - k, 0)
                k *= 2
            du.append(a[:tm, :] - dm[:, cols])
        o_ref[...] = jnp.concatenate(du + [v[:, POOL_WIDTH:]], axis=1)

    last_halo = n_rows // HALO - 1
    return pl.pallas_call(
        body, name="pool_bwd_b", out_shape=jax.ShapeDtypeStruct((n_rows, 1024), F32), grid=(n_tiles,),
        in_specs=[pl.BlockSpec((tm, 1024), lambda i: (i, 0)),
                  pl.BlockSpec((HALO, POOL_WIDTH), lambda i: (jnp.minimum((i + 1) * (tm // HALO), last_halo), 0))],
        out_specs=pl.BlockSpec((tm, 1024), lambda i: (i, 0)),
        compiler_params=_cp("parallel"),
    )(dmz, dmz)


def _norm_proj_bwd(d_uz, d_cqkv, d_gz, d_kr, w_in, h_res, gain, dh_out):
    n_rows = h_res.shape[0]
    tm = _row_tile(n_rows)
    pieces = ((COL_UZ, 1024), (COL_CQKV, 1024), (COL_GATES, 2560), (COL_KR, 128))

    def body(a_ref, b_ref, c_ref, d_ref, w_ref, x_ref, g_ref, dho_ref, dhi_ref, dg_ref):
        i = pl.program_id(0)
        dh = None
        for ref, (c0, wd) in zip((a_ref, b_ref, c_ref, d_ref), pieces):
            part = _mm_nt(ref[...], w_ref[:, c0:c0 + wd])
            dh = part if dh is None else dh + part
        g = g_ref[...]
        _, xh, inv = _rms(x_ref[...], g)
        dx, dg = _rms_bwd(dh, xh, inv, g)
        _acc(dg_ref, dg, i == 0)
        row = i * tm + lax.broadcasted_iota(jnp.int32, (tm, 1), 0)
        dhi_ref[...] = jnp.where(row >= PAD_FRONT, dho_ref[...] + dx, 0.0)

    row = lambda w: pl.BlockSpec((tm, w), lambda i: (i, 0))
    return pl.pallas_call(
        body, name="norm_proj_bwd",
        out_shape=(jax.ShapeDtypeStruct((n_rows, D_MODEL), F32), jax.ShapeDtypeStruct((1, D_MODEL), F32)),
        grid=(n_rows // tm,),
        in_specs=[row(1024), row(1024), row(2560), row(128), pl.BlockSpec((D_MODEL, P_IN), lambda i: (0, 0)),
                  row(D_MODEL), pl.BlockSpec((1, D_MODEL), lambda i: (0, 0)), row(D_MODEL)],
        out_specs=(row(D_MODEL), pl.BlockSpec((1, D_MODEL), lambda i: (0, 0))),
        compiler_params=_cp("arbitrary"),
    )(d_uz, d_cqkv, d_gz, d_kr, w_in, h_res, gain, dh_out)


def _weight_grad(name, a, b):
    k_rows, m = a.shape
    n = b.shape[1]
    tk = next(t for t in (1664, 640, 128) if k_rows % t == 0)
    tn = next(t for t in (1280, 1024, 512, 128) if n % t == 0)

    def body(a_ref, b_ref, o_ref):
        _acc(o_ref, _mm_tn(a_ref[...], b_ref[...]), pl.program_id(1) == 0)

    return pl.pallas_call(
        body, name=name, out_shape=jax.ShapeDtypeStruct((m, n), F32), grid=(n // tn, k_rows // tk),
        in_specs=[pl.BlockSpec((tk, m), lambda j, k: (k, 0)), pl.BlockSpec((tk, tn), lambda j, k: (k, j))],
        out_specs=pl.BlockSpec((m, tn), lambda j, k: (0, j)),
        compiler_params=_cp("parallel", "arbitrary"),
    )(a, b)


def _adamw(name, w, m, v, gbufs):
    r, c = w.shape
    depth = len(gbufs)
    r_l = r // depth
    tr = r_l
    if r_l * c * 4 > (1 << 20):
        tr = 256 if r_l % 256 == 0 else 128
    assert r_l * depth == r and r_l % tr == 0, (name, r, depth, tr)
    n_t = r_l // tr

    def body(w_ref, m_ref, v_ref, *refs):
        g_refs, (go_ref, d_ref, mo_ref, vo_ref) = refs[:depth], refs[depth:]

        def update(g_ref):
            g = g_ref[0].astype(F32)
            for s in range(1, N_DEV):
                g = g + g_ref[s].astype(F32)
            go_ref[...] = g
            m_new = ADAM_B1 * m_ref[...] + (1.0 - ADAM_B1) * g
            v_new = ADAM_B2 * v_ref[...] + (1.0 - ADAM_B2) * (g * g)
            mo_ref[...] = m_new
            vo_ref[...] = v_new
            m_hat = m_new / (1.0 - ADAM_B1 ** ADAM_STEP)
            v_hat = v_new / (1.0 - ADAM_B2 ** ADAM_STEP)
            d_ref[...] = -ADAM_LR * (m_hat / (jnp.sqrt(v_hat) + ADAM_EPS) + ADAM_WD * w_ref[...])

        for j in range(depth):
            pl.when(pl.program_id(0) == j)(lambda j=j: update(g_refs[j]))

    def g_spec(j):
        return pl.BlockSpec((N_DEV, tr, c), lambda l, i: (0, jnp.where(l == j, i, jnp.where(l < j, 0, n_t - 1)), 0))

    spec = pl.BlockSpec((tr, c), lambda l, i: (l * n_t + i, 0))
    sds = jax.ShapeDtypeStruct((r, c), F32)
    return pl.pallas_call(
        body, name=name, out_shape=(sds, sds, sds, sds), grid=(depth, n_t),
        in_specs=[spec, spec, spec] + [g_spec(j) for j in range(depth)],
        out_specs=(spec, spec, spec, spec),
        compiler_params=_cp("arbitrary", "arbitrary"),
    )(w, m, v, *gbufs)


SMALL = ("norm_gain", "pool_w_group", "pool_scale", "q_a_norm_gain", "kv_a_norm_gain", "q_norm_gain", "k_norm_gain")


def _pack_small(parts):
    rows = []
    for p in parts:
        flat = p.reshape(-1)
        pad = (-flat.shape[0]) % 1024
        if pad:
            flat = jnp.concatenate([flat, jnp.zeros((pad,), F32)])
        rows.append(flat.reshape(-1, 128))
    n_rows = sum(r.shape[0] for r in rows)
    tail = (-n_rows) % 256
    if tail:
        rows.append(jnp.zeros((tail, 128), F32))
    return jnp.concatenate(rows, axis=0)


def _unpack_small(packed, shapes):
    out, r0 = [], 0
    for shp in shapes:
        n = math.prod(shp)
        n_rows = (n + 1023) // 1024 * 8
        out.append(packed[r0:r0 + n_rows].reshape(-1)[:n].reshape(shp))
        r0 += n_rows
    return out


def kernel(x, positions, meta_tokens, norm_gain, w_in, pool_w_group, pool_scale, pool_w_up, q_a_norm_gain, kv_a_norm_gain, w_q_b, w_kv_b, q_norm_gain, k_norm_gain, mla_w_up, w_out, loss_target, m_meta_tokens, m_norm_gain, m_w_in, m_pool_w_group, m_pool_scale, m_pool_w_up, m_q_a_norm_gain, m_kv_a_norm_gain, m_w_q_b, m_w_kv_b, m_q_norm_gain, m_k_norm_gain, m_mla_w_up, m_w_out, v_meta_tokens, v_norm_gain, v_w_in, v_pool_w_group, v_pool_scale, v_pool_w_up, v_q_a_norm_gain, v_kv_a_norm_gain, v_w_q_b, v_w_kv_b, v_q_norm_gain, v_k_norm_gain, v_mla_w_up, v_w_out):
    x2, target = x[0], loss_target[0]

    big = dict(w_in=w_in, pool_w_up=pool_w_up, w_q_b=w_q_b, w_kv_b=w_kv_b, mla_w_up=mla_w_up, w_out=w_out)
    names_big = list(big)
    shards = {n: _cast_bf16("cast_" + n, w) for n, w in big.items()}

    def layer_shards(l):
        return [shards[n][l] for n in names_big]

    def repack_rest(gathered):
        g_pup, g_qb, g_kvb, g_mup, g_out = gathered
        return dict(pool_up=_repack_cols("repack_pool_up", g_pup[:, None], BF16)[0],
                    q_heads=_repack_q_heads(g_qb),
                    kv_heads=_repack_kv_heads(g_kvb),
                    mla_up=_repack_cols("repack_mla_up", g_mup[:, None], BF16)[0],
                    out=_repack_rows("repack_out", g_out[:, None])[0])

    def repack(gathered):
        return dict(w_in=_repack_w_in(gathered[0]), **repack_rest(gathered[1:]))

    first = _exchange("gather_first", [meta_tokens, shards["w_in"][0]], [])
    weights = [dict(w_in=_repack_w_in(first[1]))]

    h = _embed(x2, first[0])
    pos = jnp.concatenate([jnp.zeros((PAD_FRONT,), jnp.int32), jnp.arange(N_META, dtype=jnp.int32),
                           positions[0] + N_META]).astype(F32).reshape(-1, 1)
    half = QK_ROPE // 2
    inv_freq = ROPE_THETA ** (-jnp.arange(half, dtype=F32) / half)
    tab = _rope_table(pos, jnp.tile(inv_freq, 128 // half).reshape(1, 128))

    row = lambda a, l: a[l].reshape(1, -1)
    saved = []
    for l in range(DEPTH):
        w = weights[l]
        if l == 0:
            res = _norm_proj(h, row(norm_gain, l), w["w_in"], _Exchange(layer_shards(0)[1:], []))
            hb, proj = res[0], res[1]
            w.update(repack_rest(res[2:]))
        else:
            hb, proj = _norm_proj(h, row(norm_gain, l), w["w_in"])
        y_pool = _pool_fwd(proj, pool_w_group[l], row(pool_scale, l), w["pool_up"])
        q, k, v = _mla_pre_fwd(proj, tab, row(q_a_norm_gain, l), row(kv_a_norm_gain, l), w["q_heads"], *w["kv_heads"],
                               row(q_norm_gain, l), row(k_norm_gain, l))
        ex = _Exchange(layer_shards(l + 1), []) if l + 1 < DEPTH else None
        res = _attn_fwd(q, k, v, ex)
        o, lse = res[0], res[1]
        if ex is not None:
            weights.append(repack(res[2:]))
        y_mla, h_next = _mla_post_fwd(h, proj, o, y_pool, w["mla_up"], w["out"])
        saved.append((h, hb, proj, y_pool, q, k, v, o, lse, y_mla))
        h = h_next

    loss_part, dh = _loss_head(h, target)
    loss = lax.psum(loss_part[0, 0], MESH_AXES)

    grads = {n: [None] * DEPTH for n in SMALL}
    pending = None
    received = [None] * DEPTH
    for l in reversed(range(DEPTH)):
        w = weights[l]
        h_l, hb, proj, y_pool, q, k, v, o, lse, y_mla = saved[l]
        d_gz, dy_pool, do, g_out, g_mla_up = _mla_post_bwd(dh, proj, o, y_pool, y_mla, w["mla_up"], w["out"])
        early = []
        if l == 0:
            early = [_unpack_cols("unpack_mla_up", g_mla_up[None], 128, BF16)[:, 0],
                     _unpack_rows("unpack_out", g_out[None], D_MODEL // N_DEV)[:, 0]]
        sends = (pending or []) + early
        ex = _Exchange([], sends) if sends else None
        res = _attn_bwd(q, k, v, do, lse, o, ex)
        dq, dk, dv = res[0], res[1], res[2]
        n_pending = len(pending or [])
        if n_pending:
            received[l + 1] = res[3:3 + n_pending]
        early_received = list(res[3 + n_pending:])
        (d_cqkv, d_kr, g_q, g_k, g_v, grads["q_a_norm_gain"][l], grads["kv_a_norm_gain"][l], g_qn, g_kn) = _mla_pre_bwd(
            proj, tab, row(q_a_norm_gain, l), row(kv_a_norm_gain, l), w["q_heads"], *w["kv_heads"], row(q_norm_gain, l),
            row(k_norm_gain, l), dq, dk, dv)
        grads["q_norm_gain"][l], grads["k_norm_gain"][l] = g_qn[:, :QK_DIM], g_kn[:, :QK_DIM]
        dmz, grads["pool_w_group"][l], grads["pool_scale"][l], g_pool_up = _pool_bwd_a(
            proj, dy_pool, pool_w_group[l], row(pool_scale, l), w["pool_up"])
        d_uz = _pool_bwd_b(dmz)
        g_in = [_weight_grad("dw_in", hb, d) for d in (d_uz, d_cqkv, d_gz, d_kr)]
        dh, grads["norm_gain"][l] = _norm_proj_bwd(d_uz, d_cqkv, d_gz, d_kr, w["w_in"], h_l, row(norm_gain, l), dh)
        pending = [_unpack_w_in(g_in),
                   _unpack_cols("unpack_pool_up", g_pool_up[None], 128, BF16)[:, 0],
                   _unpack_q_heads(g_q),
                   _unpack_kv_heads(g_k, g_v)]
        if l > 0:
            pending += [_unpack_cols("unpack_mla_up", g_mla_up[None], 128, BF16)[:, 0],
                        _unpack_rows("unpack_out", g_out[None], D_MODEL // N_DEV)[:, 0]]

    grad_x = dh[HEAD_ROWS:][None]
    d_meta = dh[PAD_FRONT:HEAD_ROWS]

    small_shapes = [a.shape for a in (norm_gain, pool_w_group, pool_scale, q_a_norm_gain, kv_a_norm_gain, q_norm_gain, k_norm_gain)]
    small_grad = _pack_small([jnp.stack(grads[n], axis=0).reshape(s) for n, s in zip(SMALL, small_shapes)])
    meta_parts = _unpack_cols("unpack_meta", d_meta[None], 128, F32)[:, 0]
    last = _exchange("exchange_last", [small_grad], [meta_parts] + pending)
    small_buf = last[0]
    received[0] = list(last[2:]) + early_received
    sharded = ["meta_tokens"] + names_big
    shard_bufs = {"meta_tokens": [last[1]]}
    for j, n in enumerate(names_big):
        shard_bufs[n] = [received[l][j] for l in range(DEPTH)]

    given = dict(meta_tokens=(meta_tokens, m_meta_tokens, v_meta_tokens), w_in=(w_in, m_w_in, v_w_in),
                 pool_w_up=(pool_w_up, m_pool_w_up, v_pool_w_up), w_q_b=(w_q_b, m_w_q_b, v_w_q_b),
                 w_kv_b=(w_kv_b, m_w_kv_b, v_w_kv_b), mla_w_up=(mla_w_up, m_mla_w_up, v_mla_w_up),
                 w_out=(w_out, m_w_out, v_w_out))
    result = {}
    for n in sharded:
        w, m, v = given[n]
        cols = w.shape[-1]
        res = _adamw("adamw_" + n, w.reshape(-1, cols), m.reshape(-1, cols), v.reshape(-1, cols), shard_bufs[n])
        result[n] = [r.reshape(w.shape) for r in res]
    small_w = (norm_gain, pool_w_group, pool_scale, q_a_norm_gain, kv_a_norm_gain, q_norm_gain, k_norm_gain)
    small_m = (m_norm_gain, m_pool_w_group, m_pool_scale, m_q_a_norm_gain, m_kv_a_norm_gain, m_q_norm_gain, m_k_norm_gain)
    small_v = (v_norm_gain, v_pool_w_group, v_pool_scale, v_q_a_norm_gain, v_kv_a_norm_gain, v_q_norm_gain, v_k_norm_gain)
    res = _adamw("adamw_small", _pack_small(small_w), _pack_small(small_m), _pack_small(small_v), [small_buf])
    small_res = [_unpack_small(r, small_shapes) for r in res]
    for j, n in enumerate(SMALL):
        result[n] = [small_res[kind][j] for kind in range(4)]

    order = ("meta_tokens", "norm_gain", "w_in", "pool_w_group", "pool_scale", "pool_w_up", "q_a_norm_gain",
             "kv_a_norm_gain", "w_q_b", "w_kv_b", "q_norm_gain", "k_norm_gain", "mla_w_up", "w_out")
    outs = [loss, grad_x]
    for kind in range(4):
        outs += [result[n][kind] for n in order]
    return tuple(outs)
```

```python
import math

import jax
import jax.numpy as jnp
from jax import lax
from jax.experimental import pallas as pl
from jax.experimental.pallas import tpu as pltpu

F32 = jnp.float32
BF16 = jnp.bfloat16

D_MODEL = 1024
DEPTH = 4
N_META = 16
PAD_FRONT = 112
HEAD_ROWS = PAD_FRONT + N_META
POOL_WIDTH = 512
POOL_WINDOWS = (2, 4, 8, 16)
POOL_GROUP_DIM = 128
HALO = 16
N_HEADS = 8
HEADS_PER_STEP = 4
HEADS_PER_STEP_FWD = 8
QK_NOPE = 64
QK_ROPE = 32
QK_DIM = 96
V_DIM = 64
MLA_WIDTH = 512
Q_RANK = 768
KV_RANK = 256
ROPE_THETA = 10000.0
NORM_EPS = 1e-6
MASK_VALUE = -1e30
D_IN = 4640
N_DEV = 8
IN_SHARD = D_IN // N_DEV

P_IN = 4736
IN_SEGMENTS = ((0, 2048, 0), (2048, 2080, 4608), (2080, 2592, 4096), (2592, 3616, 2048), (3616, 4640, 3072))
COL_UZ, COL_CQKV, COL_GATES, COL_ZMLA, COL_KR = 0, 1024, 2048, 4096, 4608

ADAM_LR = 0.001
ADAM_B1 = 0.9
ADAM_B2 = 0.999
ADAM_EPS = 1e-08
ADAM_WD = 0.01
ADAM_STEP = 10

VMEM_LIMIT = 56 * 1024 * 1024
MESH_AXES = ("x", "y", "c")


def _cp(*sem):
    return pltpu.CompilerParams(dimension_semantics=sem, vmem_limit_bytes=VMEM_LIMIT)


def _row_tile(n_rows):
    return 320 if n_rows % 320 == 0 else 128


def _attn_tile(n_rows):
    return 640 if (n_rows % 640 == 0 and n_rows > 640) else 128


def _mm(a, b):
    return jnp.dot(a.astype(BF16), b.astype(BF16), preferred_element_type=F32)


def _mm_nt(a, b):
    return lax.dot_general(a.astype(BF16), b.astype(BF16), (((1,), (1,)), ((), ())), preferred_element_type=F32)


def _mm_tn(a, b):
    return lax.dot_general(a.astype(BF16), b.astype(BF16), (((0,), (0,)), ((), ())), preferred_element_type=F32)


def _sigmoid(z):
    return 1.0 / (1.0 + jnp.exp(-z))


def _rms(x, g):
    inv = lax.rsqrt(jnp.mean(x * x, axis=-1, keepdims=True) + NORM_EPS)
    xh = x * inv
    return xh * g, xh, inv


def _rms_bwd(dy, xh, inv, g):
    dg = jnp.sum(dy * xh, axis=0, keepdims=True)
    dxh = dy * g
    dx = inv * (dxh - xh * jnp.mean(dxh * xh, axis=-1, keepdims=True))
    return dx, dg


def _rope(x, tab):
    return x * tab[:, 0:128] + pltpu.roll(x, 112, 1) * tab[:, 128:256] + pltpu.roll(x, 16, 1) * tab[:, 256:384]


def _rope_bwd(dy, tab):
    return dy * tab[:, 0:128] + pltpu.roll(dy * tab[:, 128:256], 16, 1) + pltpu.roll(dy * tab[:, 256:384], 112, 1)


def _head_rms(x, g):
    inv = lax.rsqrt(jnp.sum(x * x, axis=-1, keepdims=True) * (1.0 / QK_DIM) + NORM_EPS)
    xh = x * inv
    return xh * g, xh, inv


def _head_rms_bwd(dy, xh, inv, g):
    dg = jnp.sum(dy * xh, axis=0, keepdims=True)
    dxh = dy * g
    dx = inv * (dxh - xh * (jnp.sum(dxh * xh, axis=-1, keepdims=True) * (1.0 / QK_DIM)))
    return dx, dg


def _pad_gain(g_ref):
    return jnp.concatenate([g_ref[...], jnp.zeros((1, 128 - QK_DIM), F32)], axis=1)


class _Exchange:
    def __init__(self, gather_list, scatter_list):
        self.arrays = list(gather_list) + list(scatter_list)
        self.n_g, self.n = len(gather_list), len(self.arrays)
        self.out_shape = [jax.ShapeDtypeStruct((N_DEV,) + a.shape, a.dtype) for a in gather_list]
        self.out_shape += [jax.ShapeDtypeStruct(a.shape, a.dtype) for a in scatter_list]
        self.specs = [pl.BlockSpec(memory_space=pl.ANY)] * self.n
        self.sems = [pltpu.SemaphoreType.DMA((7 * self.n,)), pltpu.SemaphoreType.DMA((7 * self.n,)),
                     pltpu.SemaphoreType.DMA((self.n,))]

    def copies(self, ins, outs, sems):
        send_sems, recv_sems, local_sems = sems
        x, y, c = lax.axis_index("x"), lax.axis_index("y"), lax.axis_index("c")
        me = 4 * x + 2 * y + c

        def src(a, slot):
            return ins[a] if a < self.n_g else ins[a].at[slot]

        out = [pltpu.make_async_copy(src(a, me), outs[a].at[me], local_sems.at[a]) for a in range(self.n)]
        for k in range(1, N_DEV):
            px = 1 - x if k & 4 else x
            py = 1 - y if k & 2 else y
            pc = 1 - c if k & 1 else c
            peer = 4 * px + 2 * py + pc
            for a in range(self.n):
                out.append(pltpu.make_async_remote_copy(
                    src_ref=src(a, peer), dst_ref=outs[a].at[me],
                    send_sem=send_sems.at[a * 7 + k - 1], recv_sem=recv_sems.at[a * 7 + k - 1],
                    device_id=(px, py, pc), device_id_type=pl.DeviceIdType.MESH))
        return out


def _exchange(name, gather_list, scatter_list):
    ex = _Exchange(gather_list, scatter_list)
    n = ex.n

    def body(*refs):
        copies = ex.copies(refs[:n], refs[n:2 * n], refs[2 * n:])
        for cp in copies:
            cp.start()
        for cp in copies:
            cp.wait()

    return pl.pallas_call(
        body, name=name, out_shape=ex.out_shape, in_specs=ex.specs, out_specs=ex.specs, scratch_shapes=ex.sems,
        compiler_params=pltpu.CompilerParams(has_side_effects=True),
    )(*ex.arrays)


def _call_carrying(ex, body, name, out_shape, grid, in_specs, out_specs, scratch_shapes, semantics, args):
    if ex is None:
        return pl.pallas_call(body, name=name, out_shape=out_shape, grid=grid, in_specs=in_specs, out_specs=out_specs,
                              scratch_shapes=scratch_shapes, compiler_params=_cp(*semantics))(*args)
    n, n_in, n_out, n_scr = ex.n, len(in_specs), len(out_shape), len(scratch_shapes)

    def carrying(*refs):
        ins, refs = refs[:n_in], refs[n_in:]
        c_in, refs = refs[:n], refs[n:]
        outs, refs = refs[:n_out], refs[n_out:]
        c_out, refs = refs[:n], refs[n:]
        scratch, sems = refs[:n_scr], refs[n_scr:]
        ids = [pl.program_id(a) for a in range(len(grid))]
        first, last = ids[0] == 0, ids[0] == grid[0] - 1
        for a in range(1, len(grid)):
            first, last = first & (ids[a] == 0), last & (ids[a] == grid[a] - 1)
        copies = ex.copies(c_in, c_out, sems)

        @pl.when(first)
        def _():
            for cp in copies:
                cp.start()

        body(*ins, *outs, *scratch)

        @pl.when(last)
        def _():
            for cp in copies:
                cp.wait()

    return pl.pallas_call(
        carrying, name=name + "_x", out_shape=list(out_shape) + ex.out_shape, grid=grid,
        in_specs=list(in_specs) + ex.specs, out_specs=list(out_specs) + ex.specs,
        scratch_shapes=list(scratch_shapes) + ex.sems, compiler_params=_cp(*["arbitrary"] * len(grid)),
    )(*args, *ex.arrays)


def _in_pieces():
    runs = []
    for lo, hi, dst in sorted(IN_SEGMENTS, key=lambda s: s[2]):
        col = lo
        while col < hi:
            d = col // IN_SHARD
            end = min(hi, (d + 1) * IN_SHARD)
            runs.append((d, col - d * IN_SHARD, end - d * IN_SHARD))
            col = end
    return runs


def _cast_bf16(name, w):
    shape = w.shape
    w2 = w.reshape(-1, shape[-1])
    r, c = w2.shape
    tr = 512 if r % 512 == 0 else r

    def body(x_ref, o_ref):
        o_ref[...] = x_ref[...].astype(BF16)

    spec = pl.BlockSpec((tr, c), lambda i: (i, 0))
    out = pl.pallas_call(body, name=name, out_shape=jax.ShapeDtypeStruct((r, c), BF16), grid=(r // tr,),
                         in_specs=[spec], out_specs=spec, compiler_params=_cp("parallel"))(w2)
    return out.reshape(shape)


def _repack_w_in(gathered):
    tr = 256
    runs = _in_pieces()

    def body(x_ref, o_ref):
        parts = [x_ref[d, :, a:b].astype(F32) for d, a, b in runs]
        parts.append(jnp.zeros((tr, P_IN - D_IN), F32))
        o_ref[...] = jnp.concatenate(parts, axis=1).astype(BF16)

    return pl.pallas_call(
        body, name="repack_w_in", out_shape=jax.ShapeDtypeStruct((D_MODEL, P_IN), BF16), grid=(D_MODEL // tr,),
        in_specs=[pl.BlockSpec((N_DEV, tr, IN_SHARD), lambda i: (0, i, 0))],
        out_specs=pl.BlockSpec((tr, P_IN), lambda i: (i, 0)),
        compiler_params=_cp("parallel"),
    )(gathered)


def _unpack_w_in(parts):
    tr = 256
    runs = _in_pieces()
    widths = [p.shape[-1] for p in parts]

    def body(a_ref, b_ref, c_ref, d_ref, o_ref):
        full = jnp.concatenate([a_ref[...], b_ref[...], c_ref[...], d_ref[...]], axis=1)
        col = 0
        per_dev = [[] for _ in range(N_DEV)]
        for d, a, b in runs:
            per_dev[d].append((a, full[:, col:col + (b - a)]))
            col += b - a
        for d in range(N_DEV):
            pieces = [v for _, v in sorted(per_dev[d], key=lambda av: av[0])]
            o_ref[d] = (jnp.concatenate(pieces, axis=1) if len(pieces) > 1 else pieces[0]).astype(BF16)

    return pl.pallas_call(
        body, name="unpack_w_in", out_shape=jax.ShapeDtypeStruct((N_DEV, D_MODEL, IN_SHARD), BF16), grid=(D_MODEL // tr,),
        in_specs=[pl.BlockSpec((tr, w), lambda i: (i, 0)) for w in widths],
        out_specs=pl.BlockSpec((N_DEV, tr, IN_SHARD), lambda i: (0, i, 0)),
        compiler_params=_cp("parallel"),
    )(*parts)


def _repack_cols(name, gathered, dtype):
    _, depth, k, n = gathered.shape

    def body(x_ref, o_ref):
        o_ref[0] = jnp.concatenate([x_ref[d, 0] for d in range(N_DEV)], axis=1).astype(dtype)

    return pl.pallas_call(
        body, name=name, out_shape=jax.ShapeDtypeStruct((depth, k, N_DEV * n), dtype), grid=(depth,),
        in_specs=[pl.BlockSpec((N_DEV, 1, k, n), lambda l: (0, l, 0, 0))],
        out_specs=pl.BlockSpec((1, k, N_DEV * n), lambda l: (l, 0, 0)),
        compiler_params=_cp("parallel"),
    )(gathered)


def _unpack_cols(name, full, n, dtype):
    depth, k, _ = full.shape

    def body(x_ref, o_ref):
        for d in range(N_DEV):
            o_ref[d, 0] = x_ref[0, :, d * n:(d + 1) * n].astype(dtype)

    return pl.pallas_call(
        body, name=name, out_shape=jax.ShapeDtypeStruct((N_DEV, depth, k, n), dtype), grid=(depth,),
        in_specs=[pl.BlockSpec((1, k, N_DEV * n), lambda l: (l, 0, 0))],
        out_specs=pl.BlockSpec((N_DEV, 1, k, n), lambda l: (0, l, 0, 0)),
        compiler_params=_cp("parallel"),
    )(full)


def _repack_rows(name, gathered):
    _, depth, r, n = gathered.shape

    def body(x_ref, o_ref):
        for d in range(N_DEV):
            o_ref[0, d * r:(d + 1) * r, :] = x_ref[d, 0].astype(BF16)

    return pl.pallas_call(
        body, name=name, out_shape=jax.ShapeDtypeStruct((depth, N_DEV * r, n), BF16), grid=(depth,),
        in_specs=[pl.BlockSpec((N_DEV, 1, r, n), lambda l: (0, l, 0, 0))],
        out_specs=pl.BlockSpec((1, N_DEV * r, n), lambda l: (l, 0, 0)),
        compiler_params=_cp("parallel"),
    )(gathered)


def _unpack_rows(name, full, r):
    depth, _, n = full.shape

    def body(x_ref, o_ref):
        for d in range(N_DEV):
            o_ref[d, 0] = x_ref[0, d * r:(d + 1) * r, :].astype(BF16)

    return pl.pallas_call(
        body, name=name, out_shape=jax.ShapeDtypeStruct((N_DEV, depth, r, n), BF16), grid=(depth,),
        in_specs=[pl.BlockSpec((1, N_DEV * r, n), lambda l: (l, 0, 0))],
        out_specs=pl.BlockSpec((N_DEV, 1, r, n), lambda l: (0, l, 0, 0)),
        compiler_params=_cp("parallel"),
    )(full)


def _repack_q_heads(gathered):
    r = Q_RANK // N_DEV

    def body(x_ref, o_ref):
        z = jnp.zeros((r, 128 - QK_DIM), F32)
        for d in range(N_DEV):
            x = x_ref[d].astype(F32)
            parts = []
            for h in range(N_HEADS):
                parts += [x[:, QK_DIM * h:QK_DIM * (h + 1)], z]
            o_ref[d * r:(d + 1) * r, :] = jnp.concatenate(parts, axis=1).astype(BF16)

    return pl.pallas_call(body, name="repack_q_heads", out_shape=jax.ShapeDtypeStruct((Q_RANK, 1024), BF16),
                          compiler_params=_cp())(gathered)


def _unpack_q_heads(dwq):
    r = Q_RANK // N_DEV

    def body(x_ref, o_ref):
        for d in range(N_DEV):
            x = x_ref[d * r:(d + 1) * r, :]
            o_ref[d] = jnp.concatenate([x[:, 128 * h:128 * h + QK_DIM] for h in range(N_HEADS)], axis=1).astype(BF16)

    return pl.pallas_call(body, name="unpack_q_heads", out_shape=jax.ShapeDtypeStruct((N_DEV, r, Q_RANK), BF16),
                          compiler_params=_cp())(dwq)


def _repack_kv_heads(gathered):
    def body(x_ref, wk_ref, wv_ref):
        z = jnp.zeros((KV_RANK, 128 - QK_NOPE), F32)
        for h in range(N_HEADS):
            x = x_ref[h].astype(F32)
            wk_ref[:, 128 * h:128 * (h + 1)] = jnp.concatenate([x[:, :QK_NOPE], z], axis=1).astype(BF16)
            wv_ref[:, 128 * h:128 * (h + 1)] = jnp.concatenate([x[:, QK_NOPE:], z], axis=1).astype(BF16)

    sds = jax.ShapeDtypeStruct((KV_RANK, 1024), BF16)
    return pl.pallas_call(body, name="repack_kv_heads", out_shape=(sds, sds), compiler_params=_cp())(gathered)


def _unpack_kv_heads(dwk, dwv):
    def body(k_ref, v_ref, o_ref):
        for h in range(N_HEADS):
            o_ref[h] = jnp.concatenate([k_ref[:, 128 * h:128 * h + QK_NOPE], v_ref[:, 128 * h:128 * h + V_DIM]],
                                       axis=1).astype(BF16)

    return pl.pallas_call(body, name="unpack_kv_heads", out_shape=jax.ShapeDtypeStruct((N_DEV, KV_RANK, 128), BF16),
                          compiler_params=_cp())(dwk, dwv)


def _embed(x, meta_g):
    seq = x.shape[0]
    n_blk = seq // HEAD_ROWS + 1

    def body(x_ref, m_ref, o_ref):
        i = pl.program_id(0)

        @pl.when(i == 0)
        def _():
            meta = jnp.concatenate([m_ref[d] for d in range(N_DEV)], axis=1)
            o_ref[...] = jnp.concatenate([jnp.zeros((PAD_FRONT, D_MODEL), F32), meta], axis=0)

        @pl.when(i > 0)
        def _():
            o_ref[...] = x_ref[...]

    return pl.pallas_call(
        body, name="embed", out_shape=jax.ShapeDtypeStruct((seq + HEAD_ROWS, D_MODEL), F32), grid=(n_blk,),
        in_specs=[pl.BlockSpec((HEAD_ROWS, D_MODEL), lambda i: (jnp.maximum(i - 1, 0), 0)),
                  pl.BlockSpec((N_DEV, N_META, 128), lambda i: (0, 0, 0))],
        out_specs=pl.BlockSpec((HEAD_ROWS, D_MODEL), lambda i: (i, 0)),
        compiler_params=_cp("parallel"),
    )(x, meta_g)


def _rope_table(pos_col, freq_row):
    n_rows = pos_col.shape[0]
    tr = 128

    def body(p_ref, f_ref, o_ref):
        ang = p_ref[...] * f_ref[...]
        lane = lax.broadcasted_iota(jnp.int32, ang.shape, 1)
        cosv, sinv = jnp.cos(ang), jnp.sin(ang)
        half = QK_ROPE // 2
        o_ref[:, 0:128] = jnp.where(lane < QK_NOPE, 1.0, jnp.where(lane < QK_DIM, cosv, 0.0))
        o_ref[:, 128:256] = jnp.where((lane >= QK_NOPE) & (lane < QK_NOPE + half), -sinv, 0.0)
        o_ref[:, 256:384] = jnp.where((lane >= QK_NOPE + half) & (lane < QK_DIM), sinv, 0.0)

    return pl.pallas_call(
        body, name="rope_table", out_shape=jax.ShapeDtypeStruct((n_rows, 384), F32), grid=(n_rows // tr,),
        in_specs=[pl.BlockSpec((tr, 1), lambda i: (i, 0)), pl.BlockSpec((1, 128), lambda i: (0, 0))],
        out_specs=pl.BlockSpec((tr, 384), lambda i: (i, 0)),
        compiler_params=_cp("parallel"),
    )(pos_col, freq_row)


def _loss_head(h, target):
    n_rows = h.shape[0]
    n_blk = n_rows // HEAD_ROWS

    def body(h_ref, t_ref, loss_ref, dh_ref):
        i = pl.program_id(0)

        @pl.when(i == 0)
        def _():
            loss_ref[...] = jnp.zeros(loss_ref.shape, F32)
            dh_ref[...] = jnp.zeros(dh_ref.shape, F32)

        @pl.when(i > 0)
        def _():
            err = h_ref[...] - t_ref[...]
            dh_ref[...] = err * (1.0 / D_MODEL)
            loss_ref[...] += 0.5 * jnp.sum(jnp.mean(err * err, axis=-1, keepdims=True))

    return pl.pallas_call(
        body, name="loss_head",
        out_shape=(jax.ShapeDtypeStruct((8, 128), F32), jax.ShapeDtypeStruct((n_rows, D_MODEL), F32)),
        grid=(n_blk,),
        in_specs=[pl.BlockSpec((HEAD_ROWS, D_MODEL), lambda i: (i, 0)),
                  pl.BlockSpec((HEAD_ROWS, D_MODEL), lambda i: (jnp.maximum(i - 1, 0), 0))],
        out_specs=(pl.BlockSpec((8, 128), lambda i: (0, 0)), pl.BlockSpec((HEAD_ROWS, D_MODEL), lambda i: (i, 0))),
        compiler_params=_cp("arbitrary"),
    )(h, target)


PROJ_CHUNKS = (0, 1024, 2048, 3072, 4096, P_IN)


def _norm_proj(h_res, gain, w_in, ex=None):
    n_rows = h_res.shape[0]
    tm = _row_tile(n_rows)

    def body(x_ref, g_ref, w_ref, hb_ref, p_ref):
        h, _, _ = _rms(x_ref[...], g_ref[...])
        hb = h.astype(BF16)
        hb_ref[...] = hb
        for c0, c1 in zip(PROJ_CHUNKS[:-1], PROJ_CHUNKS[1:]):
            p_ref[:, c0:c1] = jnp.dot(hb, w_ref[:, c0:c1], preferred_element_type=F32)

    return _call_carrying(
        ex, body, "norm_proj",
        out_shape=(jax.ShapeDtypeStruct((n_rows, D_MODEL), BF16), jax.ShapeDtypeStruct((n_rows, P_IN), F32)),
        grid=(n_rows // tm,),
        in_specs=[pl.BlockSpec((tm, D_MODEL), lambda i: (i, 0)), pl.BlockSpec((1, D_MODEL), lambda i: (0, 0)),
                  pl.BlockSpec((D_MODEL, P_IN), lambda i: (0, 0))],
        out_specs=(pl.BlockSpec((tm, D_MODEL), lambda i: (i, 0)), pl.BlockSpec((tm, P_IN), lambda i: (i, 0))),
        scratch_shapes=[], semantics=("parallel",), args=(h_res, gain, w_in))


def _pool_math(u, halo, z, wg_ref, scale, row0):
    tm = u.shape[0]
    ext = jnp.concatenate([halo, u], axis=0)
    t1 = (row0 + lax.broadcasted_iota(jnp.int32, (tm, 1), 0) - (PAD_FRONT - 1)).astype(F32)
    mixed, yg = [], []
    for g, w in enumerate(POOL_WINDOWS):
        a = ext[:, g * 128:(g + 1) * 128]
        k = 1
        while k < w:
            a = a + pltpu.roll(a, k, 0)
            k *= 2
        cnt = jnp.clip(t1, 1.0, float(w))
        mg = a[HALO:, :] / cnt - u[:, g * 128:(g + 1) * 128]
        mixed.append(mg)
        yg.append(_mm(mg, wg_ref[g]))
    mixed = jnp.concatenate(mixed, axis=1)
    yg = jnp.concatenate(yg, axis=1)
    ys = yg * scale
    sig = _sigmoid(z)
    return mixed, yg, ys, sig


def _halo_above(tm):
    return lambda i: (jnp.maximum(i * (tm // HALO) - 1, 0), 0)


def _pool_fwd(proj, wg, scale, w_up):
    n_rows = proj.shape[0]
    tm = _row_tile(n_rows)

    def body(uz_ref, halo_ref, wg_ref, sc_ref, wup_ref, y_ref):
        uz = uz_ref[...]
        u, z = uz[:, :POOL_WIDTH], uz[:, POOL_WIDTH:]
        _, _, ys, sig = _pool_math(u, halo_ref[...], z, wg_ref, sc_ref[...], pl.program_id(0) * tm)
        y_ref[...] = _mm(ys * (z * sig), wup_ref[...])

    return pl.pallas_call(
        body, name="pool_fwd", out_shape=jax.ShapeDtypeStruct((n_rows, D_MODEL), F32), grid=(n_rows // tm,),
        in_specs=[pl.BlockSpec((tm, 1024), lambda i: (i, 0)), pl.BlockSpec((HALO, POOL_WIDTH), _halo_above(tm)),
                  pl.BlockSpec((4, 128, 128), lambda i: (0, 0, 0)), pl.BlockSpec((1, POOL_WIDTH), lambda i: (0, 0)),
                  pl.BlockSpec((POOL_WIDTH, D_MODEL), lambda i: (0, 0))],
        out_specs=pl.BlockSpec((tm, D_MODEL), lambda i: (i, 0)),
        compiler_params=_cp("parallel"),
    )(proj, proj, wg, scale, w_up)


def _mla_pre_fwd(proj, tab, gqa, gkva, wq, wk, wv, gqn, gkn):
    n_rows = proj.shape[0]
    tm = _row_tile(n_rows)

    def body(c_ref, kr_ref, tab_ref, gqa_ref, gkva_ref, wq_ref, wk_ref, wv_ref, gqn_ref, gkn_ref, q_ref, k_ref, v_ref):
        c, tab = c_ref[...], tab_ref[...]
        cqn, _, _ = _rms(c[:, :Q_RANK], gqa_ref[...])
        ckvn, _, _ = _rms(c[:, Q_RANK:], gkva_ref[...])
        qp, kp, vp = _mm(cqn, wq_ref[...]), _mm(ckvn, wk_ref[...]), _mm(ckvn, wv_ref[...])
        kr = pltpu.roll(kr_ref[...], QK_NOPE, 1)
        gqn, gkn = _pad_gain(gqn_ref), _pad_gain(gkn_ref)
        lane = lax.broadcasted_iota(jnp.int32, (tm, 128), 1)
        one = jnp.where(lane == V_DIM, 1.0, 0.0)
        for h in range(N_HEADS):
            blk = slice(128 * h, 128 * (h + 1))
            qn, _, _ = _head_rms(qp[:, blk], gqn)
            q_ref[h] = _rope(qn, tab).astype(BF16)
            kn, _, _ = _head_rms(kp[:, blk] + kr, gkn)
            k_ref[h] = _rope(kn, tab).astype(BF16)
            v_ref[h] = (vp[:, blk] + one).astype(BF16)

    full = lambda *s: pl.BlockSpec(s, lambda i: (0,) * len(s))
    head = pl.BlockSpec((N_HEADS, tm, 128), lambda i: (0, i, 0))
    sds = jax.ShapeDtypeStruct((N_HEADS, n_rows, 128), BF16)
    return pl.pallas_call(
        body, name="mla_pre_fwd", out_shape=(sds, sds, sds), grid=(n_rows // tm,),
        in_specs=[pl.BlockSpec((tm, 1024), lambda i: (i, COL_CQKV // 1024)),
                  pl.BlockSpec((tm, 128), lambda i: (i, COL_KR // 128)),
                  pl.BlockSpec((tm, 384), lambda i: (i, 0)),
                  full(1, Q_RANK), full(1, KV_RANK), full(Q_RANK, 1024), full(KV_RANK, 1024), full(KV_RANK, 1024),
                  full(1, QK_DIM), full(1, QK_DIM)],
        out_specs=(head, head, head),
        compiler_params=_cp("parallel"),
    )(proj, proj, tab, gqa, gkva, wq, wk, wv, gqn, gkn)


def _attn_mask(row0, col0, n_r, n_c):
    row = row0 + lax.broadcasted_iota(jnp.int32, (n_r, n_c), 0)
    col = col0 + lax.broadcasted_iota(jnp.int32, (n_r, n_c), 1)
    return (col <= row) & (col >= PAD_FRONT)


ATTN_SCALE = 1.0 / math.sqrt(QK_DIM)
EXP2_SCALE = ATTN_SCALE * math.log2(math.e)
LOG2_E = math.log2(math.e)


def _row_chunks(t, size):
    return [(r, min(size, t - r)) for r in range(0, t, size)]


def _attn_fwd(q, k, v, ex=None):
    n_rows = q.shape[1]
    t = _attn_tile(n_rows)
    nb = n_rows // t

    def body(q_ref, k_ref, v_ref, o_ref, lse_ref, m_sc, acc_sc):
        qi, ki = pl.program_id(1), pl.program_id(2)

        @pl.when(ki == 0)
        def _():
            m_sc[...] = jnp.full(m_sc.shape, -jnp.inf, F32)
            acc_sc[...] = jnp.zeros(acc_sc.shape, F32)

        def update(masked):
            heads = range(hp)
            s = [_mm_nt(q_ref[h], k_ref[h]) for h in heads]
            if masked:
                mask = _attn_mask(qi * t, ki * t, t, t)
                s = [jnp.where(mask, sh, MASK_VALUE) for sh in s]
            p = []
            for h in heads:
                m_prev = m_sc[h]
                m_new = jnp.maximum(m_prev, jnp.max(s[h], axis=1, keepdims=True))
                alpha = jnp.exp2((m_prev - m_new) * EXP2_SCALE)
                p.append(jnp.exp2((s[h] - m_new) * EXP2_SCALE).astype(BF16))
                acc_sc[h] = alpha * acc_sc[h]
                m_sc[h] = m_new
            for h in heads:
                acc_sc[h] += _mm(p[h], v_ref[h])

        edge = (ki == qi) | (ki == 0)

        @pl.when((ki <= qi) & edge)
        def _():
            update(True)

        @pl.when((ki < qi) & (ki > 0))
        def _():
            update(False)

        @pl.when(ki == nb - 1)
        def _():
            acc = acc_sc[...]
            denom = acc[:, :, V_DIM:V_DIM + 1]
            o_ref[...] = acc[:, :, :V_DIM] / denom
            lse = m_sc[...] * ATTN_SCALE + jnp.log(denom)
            lane = lax.broadcasted_iota(jnp.int32, (t, 128), 1)
            wide = jnp.zeros((t, 128), F32)
            for h in range(hp):
                wide = jnp.where(lane == h, lse[h], wide)
            rows = wide.T
            for h in range(hp):
                lse_ref[h] = rows[h:h + 1, :]

    hp = HEADS_PER_STEP_FWD
    kv_idx = lambda h, qi, ki: (h, jnp.minimum(ki, qi), 0)
    return _call_carrying(
        ex, body, "attn_fwd",
        out_shape=(jax.ShapeDtypeStruct((N_HEADS, n_rows, V_DIM), F32), jax.ShapeDtypeStruct((N_HEADS, 1, n_rows), F32)),
        grid=(N_HEADS // hp, nb, nb),
        in_specs=[pl.BlockSpec((hp, t, 128), lambda h, qi, ki: (h, qi, 0)), pl.BlockSpec((hp, t, 128), kv_idx),
                  pl.BlockSpec((hp, t, 128), kv_idx)],
        out_specs=(pl.BlockSpec((hp, t, V_DIM), lambda h, qi, ki: (h, qi, 0)), pl.BlockSpec((hp, 1, t), lambda h, qi, ki: (h, 0, qi))),
        scratch_shapes=[pltpu.VMEM((hp, t, 1), F32), pltpu.VMEM((hp, t, 128), F32)],
        semantics=("parallel", "parallel", "arbitrary"), args=(q, k, v))


def _merge_math(o_ref, z, gates, y_pool, y_mla_fn):
    o_cat = jnp.concatenate([o_ref[h] for h in range(N_HEADS)], axis=1)
    sig_z = _sigmoid(z)
    a_mla = o_cat * (z * sig_z)
    y_mla = y_mla_fn(a_mla)
    sgp, sgm = _sigmoid(gates[:, :D_MODEL]), _sigmoid(gates[:, D_MODEL:])
    merged = sgp * y_pool + sgm * y_mla
    return o_cat, sig_z, a_mla, y_mla, sgp, sgm, merged


def _mla_post_fwd(h_res, proj, o, y_pool, w_mla_up, w_out):
    n_rows = h_res.shape[0]
    tm = _row_tile(n_rows)

    def body(h_ref, g_ref, z_ref, o_ref, yp_ref, wup_ref, wout_ref, ymla_ref, hn_ref):
        _, _, _, y_mla, _, _, merged = _merge_math(o_ref, z_ref[...], g_ref[...], yp_ref[...],
                                                   lambda a: _mm(a, wup_ref[...]))
        ymla_ref[...] = y_mla
        hn_ref[...] = h_ref[...] + _mm(merged, wout_ref[...])

    row = lambda w, c: pl.BlockSpec((tm, w), lambda i: (i, c))
    return pl.pallas_call(
        body, name="mla_post_fwd",
        out_shape=(jax.ShapeDtypeStruct((n_rows, D_MODEL), F32), jax.ShapeDtypeStruct((n_rows, D_MODEL), F32)),
        grid=(n_rows // tm,),
        in_specs=[row(D_MODEL, 0), row(2048, COL_GATES // 2048), row(MLA_WIDTH, COL_ZMLA // MLA_WIDTH),
                  pl.BlockSpec((N_HEADS, tm, V_DIM), lambda i: (0, i, 0)), row(D_MODEL, 0),
                  pl.BlockSpec((MLA_WIDTH, D_MODEL), lambda i: (0, 0)), pl.BlockSpec((D_MODEL, D_MODEL), lambda i: (0, 0))],
        out_specs=(row(D_MODEL, 0), row(D_MODEL, 0)),
        compiler_params=_cp("parallel"),
    )(h_res, proj, proj, o, y_pool, w_mla_up, w_out)


def _dsilu(z, sig):
    return sig * (1.0 + z * (1.0 - sig))


def _acc(ref, val, first):
    @pl.when(first)
    def _():
        ref[...] = val

    @pl.when(jnp.logical_not(first))
    def _():
        ref[...] += val


def _mla_post_bwd(dh, proj, o, y_pool, y_mla, w_mla_up, w_out):
    n_rows = dh.shape[0]
    tm = _row_tile(n_rows)

    def body(dh_ref, g_ref, z_ref, o_ref, yp_ref, ym_ref, wup_ref, wout_ref,
             dgz_ref, dyp_ref, do_ref, dwout_ref, dwup_ref):
        first = pl.program_id(0) == 0
        z, y_pool, y_mla = z_ref[...], yp_ref[...], ym_ref[...]
        dhv = dh_ref[...]
        dmerged = _mm_nt(dhv, wout_ref[...])
        o_cat, sig_z, a_mla, _, sgp, sgm, merged = _merge_math(o_ref, z, g_ref[...], y_pool, lambda a: y_mla)
        dy_mla = dmerged * sgm
        da = _mm_nt(dy_mla, wup_ref[...])
        dyp_ref[...] = (dmerged * sgp).astype(BF16)
        dgz_ref[:, 0:D_MODEL] = (dmerged * y_pool * (sgp * (1.0 - sgp))).astype(BF16)
        dgz_ref[:, D_MODEL:2 * D_MODEL] = (dmerged * y_mla * (sgm * (1.0 - sgm))).astype(BF16)
        _acc(dwout_ref, _mm_tn(merged, dhv), first)
        do_cat = da * (z * sig_z)
        dgz_ref[:, 2 * D_MODEL:] = (da * o_cat * _dsilu(z, sig_z)).astype(BF16)
        for h in range(N_HEADS):
            do_ref[h] = do_cat[:, V_DIM * h:V_DIM * (h + 1)]
        _acc(dwup_ref, _mm_tn(a_mla, dy_mla), first)

    row = lambda w, c: pl.BlockSpec((tm, w), lambda i: (i, c))
    head = lambda w: pl.BlockSpec((N_HEADS, tm, w), lambda i: (0, i, 0))
    const = lambda r, c: pl.BlockSpec((r, c), lambda i: (0, 0))
    return pl.pallas_call(
        body, name="mla_post_bwd",
        out_shape=(jax.ShapeDtypeStruct((n_rows, 2560), BF16), jax.ShapeDtypeStruct((n_rows, D_MODEL), BF16),
                   jax.ShapeDtypeStruct((N_HEADS, n_rows, V_DIM), F32),
                   jax.ShapeDtypeStruct((D_MODEL, D_MODEL), F32), jax.ShapeDtypeStruct((MLA_WIDTH, D_MODEL), F32)),
        grid=(n_rows // tm,),
        in_specs=[row(D_MODEL, 0), row(2048, COL_GATES // 2048), row(MLA_WIDTH, COL_ZMLA // MLA_WIDTH), head(V_DIM),
                  row(D_MODEL, 0), row(D_MODEL, 0), const(MLA_WIDTH, D_MODEL), const(D_MODEL, D_MODEL)],
        out_specs=(row(2560, 0), row(D_MODEL, 0), head(V_DIM), const(D_MODEL, D_MODEL), const(MLA_WIDTH, D_MODEL)),
        compiler_params=_cp("arbitrary"),
    )(dh, proj, proj, o, y_pool, y_mla, w_mla_up, w_out)


def _attn_bwd(q, k, v, do, lse, o, ex=None):
    n_rows = q.shape[1]
    t = _attn_tile(n_rows)
    nb = n_rows // t

    def body(q_ref, k_ref, v_ref, do_ref, lse_ref, o_ref, dq_ref, dk_ref, dv_ref, dk_sc, dv_sc, delta_sc):
        ki, qi = pl.program_id(1), pl.program_id(2)

        @pl.when((ki == 0) & (qi == 0))
        def _():
            dq_ref[...] = jnp.zeros(dq_ref.shape, F32)

        @pl.when(ki == 0)
        def _():
            pad = jnp.zeros((t, 128 - V_DIM), F32)
            for h in range(HEADS_PER_STEP):
                wide = jnp.concatenate([do_ref[h] * o_ref[h], pad], axis=1)
                delta_sc[h, qi] = jnp.sum(wide.T, axis=0, keepdims=True)

        @pl.when(qi == 0)
        def _():
            dk_sc[...] = jnp.zeros(dk_sc.shape, F32)
            dv_sc[...] = jnp.zeros(dv_sc.shape, F32)

        def step(masked):
            heads = range(HEADS_PER_STEP)
            st = [_mm_nt(k_ref[h], q_ref[h]) for h in heads]
            dpt = [_mm_nt(v_ref[h, :, 0:V_DIM], do_ref[h]) for h in heads]
            if masked:
                key = ki * t + lax.broadcasted_iota(jnp.int32, (t, t), 0)
                qry = qi * t + lax.broadcasted_iota(jnp.int32, (t, t), 1)
                mask = (key <= qry) & (key >= PAD_FRONT)
                st = [jnp.where(mask, sh, MASK_VALUE) for sh in st]
            pt = [jnp.exp2(st[h] * EXP2_SCALE - lse_ref[h] * LOG2_E) for h in heads]
            dst = [pt[h] * (dpt[h] - delta_sc[h, qi]) for h in heads]
            rows = pl.ds(pl.multiple_of(qi * t, t), t)
            for h in heads:
                dv_sc[h] += _mm(pt[h], do_ref[h])
                dk_sc[h] += _mm(dst[h], q_ref[h])
                dq_ref[h, rows, :] += _mm_tn(dst[h], k_ref[h]) * ATTN_SCALE

        edge = (ki == qi) | (ki == 0)

        @pl.when((qi >= ki) & edge)
        def _():
            step(True)

        @pl.when((qi > ki) & (ki > 0))
        def _():
            step(False)

        @pl.when(qi == nb - 1)
        def _():
            dk_ref[...] = dk_sc[...] * ATTN_SCALE
            dv_ref[...] = dv_sc[...]

    hp = HEADS_PER_STEP
    q_idx = lambda h, ki, qi: (h, jnp.maximum(qi, ki), 0)
    row_idx = lambda h, ki, qi: (h, 0, jnp.maximum(qi, ki))
    o_idx = lambda h, ki, qi: (h, jnp.where(ki == 0, qi, 0), 0)
    k_idx = lambda h, ki, qi: (h, ki, 0)
    return _call_carrying(
        ex, body, "attn_bwd",
        out_shape=(jax.ShapeDtypeStruct((N_HEADS, n_rows, 128), F32), jax.ShapeDtypeStruct((N_HEADS, n_rows, 128), F32),
                   jax.ShapeDtypeStruct((N_HEADS, n_rows, V_DIM), F32)),
        grid=(N_HEADS // hp, nb, nb),
        in_specs=[pl.BlockSpec((hp, t, 128), q_idx), pl.BlockSpec((hp, t, 128), k_idx), pl.BlockSpec((hp, t, 128), k_idx),
                  pl.BlockSpec((hp, t, V_DIM), q_idx), pl.BlockSpec((hp, 1, t), row_idx), pl.BlockSpec((hp, t, V_DIM), o_idx)],
        out_specs=(pl.BlockSpec((hp, n_rows, 128), lambda h, ki, qi: (h, 0, 0), pipeline_mode=pl.Buffered(1)),
                   pl.BlockSpec((hp, t, 128), k_idx),
                   pl.BlockSpec((hp, t, V_DIM), k_idx)),
        scratch_shapes=[pltpu.VMEM((hp, t, 128), F32), pltpu.VMEM((hp, t, V_DIM), F32), pltpu.VMEM((hp, nb, 1, t), F32)],
        semantics=("parallel", "arbitrary", "arbitrary"), args=(q, k, v, do, lse, o))


def _mla_pre_bwd(proj, tab, gqa, gkva, wq, wk, wv, gqn, gkn, dq, dk, dv):
    n_rows = proj.shape[0]
    tm = _row_tile(n_rows)

    def body(c_ref, kr_ref, tab_ref, gqa_ref, gkva_ref, wq_ref, wk_ref, wv_ref, gqn_ref, gkn_ref, dq_ref, dk_ref, dv_ref,
             dc_ref, dkr_ref, dwq_ref, dwk_ref, dwv_ref, dgqa_ref, dgkva_ref, dgqn_ref, dgkn_ref):
        first = pl.program_id(0) == 0
        c, tab = c_ref[...], tab_ref[...]
        gqa, gkva = gqa_ref[...], gkva_ref[...]
        cqn, cq_h, cq_inv = _rms(c[:, :Q_RANK], gqa)
        ckvn, ckv_h, ckv_inv = _rms(c[:, Q_RANK:], gkva)
        qp, kp = _mm(cqn, wq_ref[...]), _mm(ckvn, wk_ref[...])
        kr = pltpu.roll(kr_ref[...], QK_NOPE, 1)
        gqn, gkn = _pad_gain(gqn_ref), _pad_gain(gkn_ref)
        dq_parts, dk_parts, dv_parts = [], [], []
        dkr = jnp.zeros((tm, 128), F32)
        dgqn = jnp.zeros((1, 128), F32)
        dgkn = jnp.zeros((1, 128), F32)
        zv = jnp.zeros((tm, 128 - V_DIM), F32)
        for h in range(N_HEADS):
            blk = slice(128 * h, 128 * (h + 1))
            _, xh, inv = _head_rms(qp[:, blk], gqn)
            dx, dg = _head_rms_bwd(_rope_bwd(dq_ref[h], tab), xh, inv, gqn)
            dq_parts.append(dx)
            dgqn = dgqn + dg
            _, xh, inv = _head_rms(kp[:, blk] + kr, gkn)
            dx, dg = _head_rms_bwd(_rope_bwd(dk_ref[h], tab), xh, inv, gkn)
            dk_parts.append(dx)
            dgkn = dgkn + dg
            dkr = dkr + dx
            dv_parts.append(jnp.concatenate([dv_ref[h], zv], axis=1))
        dqp = jnp.concatenate(dq_parts, axis=1)
        dkp = jnp.concatenate(dk_parts, axis=1)
        dvp = jnp.concatenate(dv_parts, axis=1)
        _acc(dwq_ref, _mm_tn(cqn, dqp), first)
        _acc(dwk_ref, _mm_tn(ckvn, dkp), first)
        _acc(dwv_ref, _mm_tn(ckvn, dvp), first)
        dcq, dg1 = _rms_bwd(_mm_nt(dqp, wq_ref[...]), cq_h, cq_inv, gqa)
        dckv, dg2 = _rms_bwd(_mm_nt(dkp, wk_ref[...]) + _mm_nt(dvp, wv_ref[...]), ckv_h, ckv_inv, gkva)
        _acc(dgqa_ref, dg1, first)
        _acc(dgkva_ref, dg2, first)
        _acc(dgqn_ref, dgqn, first)
        _acc(dgkn_ref, dgkn, first)
        dc_ref[...] = jnp.concatenate([dcq, dckv], axis=1).astype(BF16)
        lane = lax.broadcasted_iota(jnp.int32, (tm, 128), 1)
        dkr_ref[...] = jnp.where(lane < QK_ROPE, pltpu.roll(dkr, 128 - QK_NOPE, 1), 0.0).astype(BF16)

    full = lambda *s: pl.BlockSpec(s, lambda i: (0,) * len(s))
    head = lambda w: pl.BlockSpec((N_HEADS, tm, w), lambda i: (0, i, 0))
    return pl.pallas_call(
        body, name="mla_pre_bwd",
        out_shape=(jax.ShapeDtypeStruct((n_rows, 1024), BF16), jax.ShapeDtypeStruct((n_rows, 128), BF16),
                   jax.ShapeDtypeStruct((Q_RANK, 1024), F32), jax.ShapeDtypeStruct((KV_RANK, 1024), F32),
                   jax.ShapeDtypeStruct((KV_RANK, 1024), F32),
                   jax.ShapeDtypeStruct((1, Q_RANK), F32), jax.ShapeDtypeStruct((1, KV_RANK), F32),
                   jax.ShapeDtypeStruct((1, 128), F32), jax.ShapeDtypeStruct((1, 128), F32)),
        grid=(n_rows // tm,),
        in_specs=[pl.BlockSpec((tm, 1024), lambda i: (i, COL_CQKV // 1024)),
                  pl.BlockSpec((tm, 128), lambda i: (i, COL_KR // 128)),
                  pl.BlockSpec((tm, 384), lambda i: (i, 0)),
                  full(1, Q_RANK), full(1, KV_RANK), full(Q_RANK, 1024), full(KV_RANK, 1024), full(KV_RANK, 1024),
                  full(1, QK_DIM), full(1, QK_DIM), head(128), head(128), head(V_DIM)],
        out_specs=(pl.BlockSpec((tm, 1024), lambda i: (i, 0)), pl.BlockSpec((tm, 128), lambda i: (i, 0)),
                   full(Q_RANK, 1024), full(KV_RANK, 1024), full(KV_RANK, 1024), full(1, Q_RANK), full(1, KV_RANK),
                   full(1, 128), full(1, 128)),
        compiler_params=_cp("arbitrary"),
    )(proj, proj, tab, gqa, gkva, wq, wk, wv, gqn, gkn, dq, dk, dv)


def _pool_bwd_a(proj, dy_pool, wg, scale, w_up):
    n_rows = proj.shape[0]
    tm = _row_tile(n_rows)

    def body(uz_ref, halo_ref, dy_ref, wg_ref, sc_ref, wup_ref, dmz_ref, dwg_ref, dsc_ref, dwup_ref):
        first = pl.program_id(0) == 0
        uz = uz_ref[...]
        u, z = uz[:, :POOL_WIDTH], uz[:, POOL_WIDTH:]
        scale_v = sc_ref[...]
        mixed, yg, ys, sig = _pool_math(u, halo_ref[...], z, wg_ref, scale_v, pl.program_id(0) * tm)
        sp = z * sig
        dy = dy_ref[...]
        da = _mm_nt(dy, wup_ref[...])
        _acc(dwup_ref, _mm_tn(ys * sp, dy), first)
        dys = da * sp
        _acc(dsc_ref, jnp.sum(dys * yg, axis=0, keepdims=True), first)
        dyg = dys * scale_v
        for g in range(4):
            cols = slice(g * 128, (g + 1) * 128)
            dmz_ref[:, cols] = _mm_nt(dyg[:, cols], wg_ref[g])
            _acc(dwg_ref.at[g], _mm_tn(mixed[:, cols], dyg[:, cols]), first)
        dmz_ref[:, POOL_WIDTH:] = da * ys * _dsilu(z, sig)

    return pl.pallas_call(
        body, name="pool_bwd_a",
        out_shape=(jax.ShapeDtypeStruct((n_rows, 1024), F32), jax.ShapeDtypeStruct((4, 128, 128), F32),
                   jax.ShapeDtypeStruct((1, POOL_WIDTH), F32), jax.ShapeDtypeStruct((POOL_WIDTH, D_MODEL), F32)),
        grid=(n_rows // tm,),
        in_specs=[pl.BlockSpec((tm, 1024), lambda i: (i, 0)), pl.BlockSpec((HALO, POOL_WIDTH), _halo_above(tm)),
                  pl.BlockSpec((tm, D_MODEL), lambda i: (i, 0)),
                  pl.BlockSpec((4, 128, 128), lambda i: (0, 0, 0)), pl.BlockSpec((1, POOL_WIDTH), lambda i: (0, 0)),
                  pl.BlockSpec((POOL_WIDTH, D_MODEL), lambda i: (0, 0))],
        out_specs=(pl.BlockSpec((tm, 1024), lambda i: (i, 0)), pl.BlockSpec((4, 128, 128), lambda i: (0, 0, 0)),
                   pl.BlockSpec((1, POOL_WIDTH), lambda i: (0, 0)), pl.BlockSpec((POOL_WIDTH, D_MODEL), lambda i: (0, 0))),
        compiler_params=_cp("arbitrary"),
    )(proj, proj, dy_pool, wg, scale, w_up)


def _pool_bwd_b(dmz):
    n_rows = dmz.shape[0]
    tm = _row_tile(n_rows)
    n_tiles = n_rows // tm
    n_ext = tm + HALO

    def body(dmz_ref, halo_ref, o_ref):
        i = pl.program_id(0)
        v = dmz_ref[...]
        dm = v[:, :POOL_WIDTH]
        halo = jnp.where(i == n_tiles - 1, 0.0, halo_ref[...])
        ext = jnp.concatenate([dm, halo], axis=0)
        t1 = (i * tm + lax.broadcasted_iota(jnp.int32, (n_ext, 1), 0) - (PAD_FRONT - 1)).astype(F32)
        du = []
        for g, w in enumerate(POOL_WINDOWS):
            cols = slice(g * 128, (g + 1) * 128)
            a = ext[:, cols] / jnp.clip(t1, 1.0, float(w))
            k = 1
            while k < w:
                a = a + pltpu.roll(a, n_ext - k, 0)
                k *= 2
            du.append(a[:tm, :] - dm[:, cols])
        o_ref[...] = jnp.concatenate(du + [v[:, POOL_WIDTH:]], axis=1).astype(BF16)

    last_halo = n_rows // HALO - 1
    return pl.pallas_call(
        body, name="pool_bwd_b", out_shape=jax.ShapeDtypeStruct((n_rows, 1024), BF16), grid=(n_tiles,),
        in_specs=[pl.BlockSpec((tm, 1024), lambda i: (i, 0)),
                  pl.BlockSpec((HALO, POOL_WIDTH), lambda i: (jnp.minimum((i + 1) * (tm // HALO), last_halo), 0))],
        out_specs=pl.BlockSpec((tm, 1024), lambda i: (i, 0)),
        compiler_params=_cp("parallel"),
    )(dmz, dmz)


def _norm_proj_bwd(d_uz, d_cqkv, d_gz, d_kr, w_in, h_res, gain, dh_out):
    n_rows = h_res.shape[0]
    tm = _row_tile(n_rows)
    pieces = ((COL_UZ, 1024), (COL_CQKV, 1024), (COL_GATES, 2560), (COL_KR, 128))

    def body(a_ref, b_ref, c_ref, d_ref, w_ref, x_ref, g_ref, dho_ref, dhi_ref, dg_ref):
        i = pl.program_id(0)
        dh = None
        for ref, (c0, wd) in zip((a_ref, b_ref, c_ref, d_ref), pieces):
            part = _mm_nt(ref[...], w_ref[:, c0:c0 + wd])
            dh = part if dh is None else dh + part
        g = g_ref[...]
        _, xh, inv = _rms(x_ref[...], g)
        dx, dg = _rms_bwd(dh, xh, inv, g)
        _acc(dg_ref, dg, i == 0)
        row = i * tm + lax.broadcasted_iota(jnp.int32, (tm, 1), 0)
        dhi_ref[...] = jnp.where(row >= PAD_FRONT, dho_ref[...] + dx, 0.0)

    row = lambda w: pl.BlockSpec((tm, w), lambda i: (i, 0))
    return pl.pallas_call(
        body, name="norm_proj_bwd",
        out_shape=(jax.ShapeDtypeStruct((n_rows, D_MODEL), F32), jax.ShapeDtypeStruct((1, D_MODEL), F32)),
        grid=(n_rows // tm,),
        in_specs=[row(1024), row(1024), row(2560), row(128), pl.BlockSpec((D_MODEL, P_IN), lambda i: (0, 0)),
                  row(D_MODEL), pl.BlockSpec((1, D_MODEL), lambda i: (0, 0)), row(D_MODEL)],
        out_specs=(row(D_MODEL), pl.BlockSpec((1, D_MODEL), lambda i: (0, 0))),
        compiler_params=_cp("arbitrary"),
    )(d_uz, d_cqkv, d_gz, d_kr, w_in, h_res, gain, dh_out)


def _weight_grad(name, a, b):
    k_rows, m = a.shape
    n = b.shape[1]
    tk = next(t for t in (1664, 640, 128) if k_rows % t == 0)
    tn = next(t for t in (1280, 1024, 512, 128) if n % t == 0)

    def body(a_ref, b_ref, o_ref):
        _acc(o_ref, _mm_tn(a_ref[...], b_ref[...]), pl.program_id(1) == 0)

    return pl.pallas_call(
        body, name=name, out_shape=jax.ShapeDtypeStruct((m, n), F32), grid=(n // tn, k_rows // tk),
        in_specs=[pl.BlockSpec((tk, m), lambda j, k: (k, 0)), pl.BlockSpec((tk, tn), lambda j, k: (k, j))],
        out_specs=pl.BlockSpec((m, tn), lambda j, k: (0, j)),
        compiler_params=_cp("parallel", "arbitrary"),
    )(a, b)


def _adamw(name, w, m, v, gbufs):
    r, c = w.shape
    depth = len(gbufs)
    r_l = r // depth
    tr = r_l
    if r_l * c * 4 > (1 << 20):
        tr = 256 if r_l % 256 == 0 else 128
    assert r_l * depth == r and r_l % tr == 0, (name, r, depth, tr)
    n_t = r_l // tr

    def body(w_ref, m_ref, v_ref, *refs):
        g_refs, (go_ref, d_ref, mo_ref, vo_ref) = refs[:depth], refs[depth:]

        def update(g_ref):
            g = g_ref[0].astype(F32)
            for s in range(1, N_DEV):
                g = g + g_ref[s].astype(F32)
            go_ref[...] = g
            m_new = ADAM_B1 * m_ref[...] + (1.0 - ADAM_B1) * g
            v_new = ADAM_B2 * v_ref[...] + (1.0 - ADAM_B2) * (g * g)
            mo_ref[...] = m_new
            vo_ref[...] = v_new
            m_hat = m_new / (1.0 - ADAM_B1 ** ADAM_STEP)
            v_hat = v_new / (1.0 - ADAM_B2 ** ADAM_STEP)
            d_ref[...] = -ADAM_LR * (m_hat / (jnp.sqrt(v_hat) + ADAM_EPS) + ADAM_WD * w_ref[...])

        for j in range(depth):
            pl.when(pl.program_id(0) == j)(lambda j=j: update(g_refs[j]))

    def g_spec(j):
        return pl.BlockSpec((N_DEV, tr, c), lambda l, i: (0, jnp.where(l == j, i, jnp.where(l < j, 0, n_t - 1)), 0))

    spec = pl.BlockSpec((tr, c), lambda l, i: (l * n_t + i, 0))
    sds = jax.ShapeDtypeStruct((r, c), F32)
    return pl.pallas_call(
        body, name=name, out_shape=(sds, sds, sds, sds), grid=(depth, n_t),
        in_specs=[spec, spec, spec] + [g_spec(j) for j in range(depth)],
        out_specs=(spec, spec, spec, spec),
        compiler_params=_cp("arbitrary", "arbitrary"),
    )(w, m, v, *gbufs)


SMALL = ("norm_gain", "pool_w_group", "pool_scale", "q_a_norm_gain", "kv_a_norm_gain", "q_norm_gain", "k_norm_gain")


def _pack_small(parts):
    rows = []
    for p in parts:
        flat = p.reshape(-1)
        pad = (-flat.shape[0]) % 1024
        if pad:
            flat = jnp.concatenate([flat, jnp.zeros((pad,), F32)])
        rows.append(flat.reshape(-1, 128))
    n_rows = sum(r.shape[0] for r in rows)
    tail = (-n_rows) % 256
    if tail:
        rows.append(jnp.zeros((tail, 128), F32))
    return jnp.concatenate(rows, axis=0)


def _unpack_small(packed, shapes):
    out, r0 = [], 0
    for shp in shapes:
        n = math.prod(shp)
        n_rows = (n + 1023) // 1024 * 8
        out.append(packed[r0:r0 + n_rows].reshape(-1)[:n].reshape(shp))
        r0 += n_rows
    return out


def kernel(x, positions, meta_tokens, norm_gain, w_in, pool_w_group, pool_scale, pool_w_up, q_a_norm_gain, kv_a_norm_gain, w_q_b, w_kv_b, q_norm_gain, k_norm_gain, mla_w_up, w_out, loss_target, m_meta_tokens, m_norm_gain, m_w_in, m_pool_w_group, m_pool_scale, m_pool_w_up, m_q_a_norm_gain, m_kv_a_norm_gain, m_w_q_b, m_w_kv_b, m_q_norm_gain, m_k_norm_gain, m_mla_w_up, m_w_out, v_meta_tokens, v_norm_gain, v_w_in, v_pool_w_group, v_pool_scale, v_pool_w_up, v_q_a_norm_gain, v_kv_a_norm_gain, v_w_q_b, v_w_kv_b, v_q_norm_gain, v_k_norm_gain, v_mla_w_up, v_w_out):
    x2, target = x[0], loss_target[0]

    big = dict(w_in=w_in, pool_w_up=pool_w_up, w_q_b=w_q_b, w_kv_b=w_kv_b, mla_w_up=mla_w_up, w_out=w_out)
    names_big = list(big)
    shards = {n: _cast_bf16("cast_" + n, w) for n, w in big.items()}

    def layer_shards(l):
        return [shards[n][l] for n in names_big]

    def repack_rest(gathered):
        g_pup, g_qb, g_kvb, g_mup, g_out = gathered
        return dict(pool_up=_repack_cols("repack_pool_up", g_pup[:, None], BF16)[0],
                    q_heads=_repack_q_heads(g_qb),
                    kv_heads=_repack_kv_heads(g_kvb),
                    mla_up=_repack_cols("repack_mla_up", g_mup[:, None], BF16)[0],
                    out=_repack_rows("repack_out", g_out[:, None])[0])

    def repack(gathered):
        return dict(w_in=_repack_w_in(gathered[0]), **repack_rest(gathered[1:]))

    first = _exchange("gather_first", [meta_tokens, shards["w_in"][0]], [])
    weights = [dict(w_in=_repack_w_in(first[1]))]

    h = _embed(x2, first[0])
    pos = jnp.concatenate([jnp.zeros((PAD_FRONT,), jnp.int32), jnp.arange(N_META, dtype=jnp.int32),
                           positions[0] + N_META]).astype(F32).reshape(-1, 1)
    half = QK_ROPE // 2
    inv_freq = ROPE_THETA ** (-jnp.arange(half, dtype=F32) / half)
    tab = _rope_table(pos, jnp.tile(inv_freq, 128 // half).reshape(1, 128))

    row = lambda a, l: a[l].reshape(1, -1)
    saved = []
    for l in range(DEPTH):
        w = weights[l]
        if l == 0:
            res = _norm_proj(h, row(norm_gain, l), w["w_in"], _Exchange(layer_shards(0)[1:], []))
            hb, proj = res[0], res[1]
            w.update(repack_rest(res[2:]))
        else:
            hb, proj = _norm_proj(h, row(norm_gain, l), w["w_in"])
        y_pool = _pool_fwd(proj, pool_w_group[l], row(pool_scale, l), w["pool_up"])
        q, k, v = _mla_pre_fwd(proj, tab, row(q_a_norm_gain, l), row(kv_a_norm_gain, l), w["q_heads"], *w["kv_heads"],
                               row(q_norm_gain, l), row(k_norm_gain, l))
        ex = _Exchange(layer_shards(l + 1), []) if l + 1 < DEPTH else None
        res = _attn_fwd(q, k, v, ex)
        o, lse = res[0], res[1]
        if ex is not None:
            weights.append(repack(res[2:]))
        y_mla, h_next = _mla_post_fwd(h, proj, o, y_pool, w["mla_up"], w["out"])
        saved.append((h, hb, proj, y_pool, q, k, v, o, lse, y_mla))
        h = h_next

    loss_part, dh = _loss_head(h, target)
    loss = lax.psum(loss_part[0, 0], MESH_AXES)

    grads = {n: [None] * DEPTH for n in SMALL}
    pending = None
    received = [None] * DEPTH
    for l in reversed(range(DEPTH)):
        w = weights[l]
        h_l, hb, proj, y_pool, q, k, v, o, lse, y_mla = saved[l]
        d_gz, dy_pool, do, g_out, g_mla_up = _mla_post_bwd(dh, proj, o, y_pool, y_mla, w["mla_up"], w["out"])
        early = []
        if l == 0:
            early = [_unpack_cols("unpack_mla_up", g_mla_up[None], 128, BF16)[:, 0],
                     _unpack_rows("unpack_out", g_out[None], D_MODEL // N_DEV)[:, 0]]
        sends = (pending or []) + early
        ex = _Exchange([], sends) if sends else None
        res = _attn_bwd(q, k, v, do, lse, o, ex)
        dq, dk, dv = res[0], res[1], res[2]
        n_pending = len(pending or [])
        if n_pending:
            received[l + 1] = res[3:3 + n_pending]
        early_received = list(res[3 + n_pending:])
        (d_cqkv, d_kr, g_q, g_k, g_v, grads["q_a_norm_gain"][l], grads["kv_a_norm_gain"][l], g_qn, g_kn) = _mla_pre_bwd(
            proj, tab, row(q_a_norm_gain, l), row(kv_a_norm_gain, l), w["q_heads"], *w["kv_heads"], row(q_norm_gain, l),
            row(k_norm_gain, l), dq, dk, dv)
        grads["q_norm_gain"][l], grads["k_norm_gain"][l] = g_qn[:, :QK_DIM], g_kn[:, :QK_DIM]
        dmz, grads["pool_w_group"][l], grads["pool_scale"][l], g_pool_up = _pool_bwd_a(
            proj, dy_pool, pool_w_group[l], row(pool_scale, l), w["pool_up"])
        d_uz = _pool_bwd_b(dmz)
        g_in = [_weight_grad("dw_in", hb, d) for d in (d_uz, d_cqkv, d_gz, d_kr)]
        dh, grads["norm_gain"][l] = _norm_proj_bwd(d_uz, d_cqkv, d_gz, d_kr, w["w_in"], h_l, row(norm_gain, l), dh)
        pending = [_unpack_w_in(g_in),
                   _unpack_cols("unpack_pool_up", g_pool_up[None], 128, BF16)[:, 0],
                   _unpack_q_heads(g_q),
                   _unpack_kv_heads(g_k, g_v)]
        if l > 0:
            pending += [_unpack_cols("unpack_mla_up", g_mla_up[None], 128, BF16)[:, 0],
                        _unpack_rows("unpack_out", g_out[None], D_MODEL // N_DEV)[:, 0]]

    grad_x = dh[HEAD_ROWS:][None]
    d_meta = dh[PAD_FRONT:HEAD_ROWS]

    small_shapes = [a.shape for a in (norm_gain, pool_w_group, pool_scale, q_a_norm_gain, kv_a_norm_gain, q_norm_gain, k_norm_gain)]
    small_grad = _pack_small([jnp.stack(grads[n], axis=0).reshape(s) for n, s in zip(SMALL, small_shapes)])
    meta_parts = _unpack_cols("unpack_meta", d_meta[None], 128, F32)[:, 0]
    last = _exchange("exchange_last", [small_grad], [meta_parts] + pending)
    small_buf = last[0]
    received[0] = list(last[2:]) + early_received
    sharded = ["meta_tokens"] + names_big
    shard_bufs = {"meta_tokens": [last[1]]}
    for j, n in enumerate(names_big):
        shard_bufs[n] = [received[l][j] for l in range(DEPTH)]

    given = dict(meta_tokens=(meta_tokens, m_meta_tokens, v_meta_tokens), w_in=(w_in, m_w_in, v_w_in),
                 pool_w_up=(pool_w_up, m_pool_w_up, v_pool_w_up), w_q_b=(w_q_b, m_w_q_b, v_w_q_b),
                 w_kv_b=(w_kv_b, m_w_kv_b, v_w_kv_b), mla_w_up=(mla_w_up, m_mla_w_up, v_mla_w_up),
                 w_out=(w_out, m_w_out, v_w_out))
    result = {}
    for n in sharded:
        w, m, v = given[n]
        cols = w.shape[-1]
        res = _adamw("adamw_" + n, w.reshape(-1, cols), m.reshape(-1, cols), v.reshape(-1, cols), shard_bufs[n])
        result[n] = [r.reshape(w.shape) for r in res]
    small_w = (norm_gain, pool_w_group, pool_scale, q_a_norm_gain, kv_a_norm_gain, q_norm_gain, k_norm_gain)
    small_m = (m_norm_gain, m_pool_w_group, m_pool_scale, m_q_a_norm_gain, m_kv_a_norm_gain, m_q_norm_gain, m_k_norm_gain)
    small_v = (v_norm_gain, v_pool_w_group, v_pool_scale, v_q_a_norm_gain, v_kv_a_norm_gain, v_q_norm_gain, v_k_norm_gain)
    res = _adamw("adamw_small", _pack_small(small_w), _pack_small(small_m), _pack_small(small_v), [small_buf])
    small_res = [_unpack_small(r, small_shapes) for r in res]
    for j, n in enumerate(SMALL):
        result[n] = [small_res[kind][j] for kind in range(4)]

    order = ("meta_tokens", "norm_gain", "w_in", "pool_w_group", "pool_scale", "pool_w_up", "q_a_norm_gain",
             "kv_a_norm_gain", "w_q_b", "w_kv_b", "q_norm_gain", "k_norm_gain", "mla_w_up", "w_out")
    outs = [loss, grad_x]
    for kind in range(4):
        outs += [result[n][kind] for n in order]
    return tuple(outs)
```
